```python
import jax
import jax.numpy as jnp
from jax import lax
import numpy as np

D_MODEL = 1024
BATCH = 16
SEQ = 256
DEPTH = 2
DEC_BATCH = 8
DEC_SEQ = 1024
PAST_LEN = 256

GRID_W = 64
HEAD_DIM = 64
N_Q_HEADS = 8
N_KV_HEADS = 4
ATTN_WIDTH = N_Q_HEADS * HEAD_DIM
KV_WIDTH = N_KV_HEADS * HEAD_DIM
Q_BLOCK = 128
ROPE_THETA = 10000.0
M_HEADS = 4
M_DK = 64
M_DV = 64
M_WIDTH = M_HEADS * M_DV
M_CHUNK = 64
M_CONV = 3
F_GROUPS = 4
F_GROUP_CH = 64
F_WIDTH = F_GROUPS * F_GROUP_CH
FF_HIDDEN = -(-8 * D_MODEL // (3 * 256)) * 256
EPS = 1e-6
IN_SIZES = (ATTN_WIDTH, KV_WIDTH, KV_WIDTH, M_HEADS * M_DK, M_HEADS * M_DK, M_WIDTH, M_WIDTH, 4 * M_HEADS, F_WIDTH, 3 * D_MODEL)
IN_COLS = sum(IN_SIZES)

kernel_name = 'hybrid_mlstm_gqa_fourier_diffusion_step'


def _rmsnorm(x, g):
    xf = x.astype(jnp.float32)
    y = xf * lax.rsqrt(jnp.mean(xf * xf, axis=-1, keepdims=True) + EPS)
    return (y * g.astype(jnp.float32)).astype(x.dtype)


def _modulation(cvec, w_ada, b_ada):
    mod = jax.nn.silu(cvec) @ w_ada + b_ada
    return jnp.split(mod, 6, axis=-1)


def _rope_1d(x, pos):
    n = x.shape[-1] // 2
    inv = 1.0 / (ROPE_THETA ** (jnp.arange(n, dtype=jnp.float32) / n))
    ang = pos.astype(jnp.float32)[:, None] * inv[None, :]
    cos = jnp.cos(ang)[:, None, :]
    sin = jnp.sin(ang)[:, None, :]
    xf = x.astype(jnp.float32)
    x1, x2 = xf[..., :n], xf[..., n:]
    return jnp.concatenate([x1 * cos - x2 * sin, x2 * cos + x1 * sin], axis=-1).astype(x.dtype)


def _rope_2d(x, row, col):
    half = x.shape[-1] // 2
    return jnp.concatenate([_rope_1d(x[..., :half], row), _rope_1d(x[..., half:], col)], axis=-1)


def _attend(q, k, v):
    B, Tq, Hq, hd = q.shape
    G = Hq // N_KV_HEADS
    nb = Tq // Q_BLOCK
    qb = q.reshape(B, nb, Q_BLOCK, N_KV_HEADS, G, hd).transpose(1, 0, 2, 3, 4, 5)
    scale = hd ** -0.5

    def block(qi):
        s = jnp.einsum('bqhgd,bkhd->bhgqk', qi, k, preferred_element_type=jnp.float32) * scale
        p = jax.nn.softmax(s, axis=-1).astype(v.dtype)
        return jnp.einsum('bhgqk,bkhd->bqhgd', p, v)

    o = lax.map(block, qb)
    return o.transpose(1, 0, 2, 3, 4, 5).reshape(B, Tq, Hq * hd)


def _short_conv(x, w):
    T = x.shape[1]
    pad = M_CONV // 2
    xp = jnp.pad(x, ((0, 0), (pad, pad), (0, 0)))
    return sum(xp[:, j:j + T] * w[j] for j in range(M_CONV))


def _fourier(u):
    B, T, _ = u.shape
    ug = u.reshape(B, T, F_GROUPS, F_GROUP_CH).astype(jnp.float32)
    y = jnp.fft.fft2(ug, axes=(1, 3), norm='ortho').real
    return y.reshape(B, T, F_WIDTH).astype(u.dtype)


def _mlstm_scan(q, k, v, li, lf, C0, n0, m0):
    B, T, H, _ = q.shape
    nc = T // M_CHUNK
    tril = jnp.tril(jnp.ones((M_CHUNK, M_CHUNK), dtype=bool))

    def chunks(a):
        return a.reshape((B, nc, M_CHUNK) + a.shape[2:]).swapaxes(0, 1)

    def step(carry, inp):
        C, n, m = carry
        qc, kc, vc, lic, lfc = inp
        b = jnp.cumsum(lfc, axis=1).swapaxes(1, 2)
        ig = lic.swapaxes(1, 2)
        inter = b + m[..., None]
        dmat = b[..., :, None] - b[..., None, :] + ig[..., None, :]
        dmat = jnp.where(tril, dmat, -jnp.inf)
        mj = jnp.maximum(inter, dmat.max(axis=-1))
        w = jnp.exp(dmat - mj[..., None])
        wi = jnp.exp(inter - mj)
        s = jnp.einsum('blhd,bshd->bhls', qc, kc) * w
        num = jnp.einsum('bhls,bshe->bhle', s, vc) + wi[..., None] * jnp.einsum('blhd,bhde->bhle', qc, C)
        den = s.sum(axis=-1) + wi * jnp.einsum('blhd,bhd->bhl', qc, n)
        h = num / jnp.maximum(jnp.abs(den), jnp.exp(-mj))[..., None]
        bl = b[..., -1]
        g = bl[..., None] - b + ig
        m_new = jnp.maximum(bl + m, g.max(axis=-1))
        wk = jnp.exp(g - m_new[..., None])
        decay = jnp.exp(bl + m - m_new)
        C_new = decay[..., None, None] * C + jnp.einsum('bhs,bshd,bshe->bhde', wk, kc, vc)
        n_new = decay[..., None] * n + jnp.einsum('bhs,bshd->bhd', wk, kc)
        return (C_new, n_new, m_new), h.swapaxes(1, 2)

    (C, n, m), h = lax.scan(step, (C0, n0, m0), (chunks(q), chunks(k), chunks(v), chunks(li), chunks(lf)))
    h = h.swapaxes(0, 1).reshape(B, T, H, v.shape[-1])
    return h, (C, n, m)


def _mlstm_bidir(q, k, v, gates, st_fw, st_bw):
    li_f, lf_f, li_b, lf_b = jnp.split(gates, 4, axis=-1)
    lf_f = jax.nn.log_sigmoid(lf_f)
    lf_b = jax.nn.log_sigmoid(lf_b)
    h_f, s_f = _mlstm_scan(q, k, v, li_f, lf_f, *st_fw)
    fl = lambda a: jnp.flip(a, axis=1)
    h_b, s_b = _mlstm_scan(fl(q), fl(k), fl(v), fl(li_b), fl(lf_b), *st_bw)
    return h_f + fl(h_b), s_f, s_b


def _layer(x, mod, pos, ctx, norm1_g, w_in, q_norm_g, k_norm_g, conv_w, gate_b, m_norm_g,
           w_pa, w_pm, w_pf, w_out, norm2_g, w_ffn_in, w_ffn_out):
    shift1, scale1, gate1, shift2, scale2, gate2 = mod
    B, T, _ = x.shape
    f32 = jnp.float32
    h = _rmsnorm(x, norm1_g) * (1 + scale1) + shift1
    z = h @ w_in
    qa, ka, va, qm, km, vm, om, gm, fu, bg = jnp.split(z, np.cumsum(IN_SIZES)[:-1].tolist(), axis=-1)
    qa = _rmsnorm(qa.reshape(B, T, N_Q_HEADS, HEAD_DIM), q_norm_g)
    ka = _rmsnorm(ka.reshape(B, T, N_KV_HEADS, HEAD_DIM), k_norm_g)
    va = va.reshape(B, T, N_KV_HEADS, HEAD_DIM)
    if ctx is None:
        keys, vals = ka, va
        zero_state = (jnp.zeros((B, M_HEADS, M_DK, M_DV), f32), jnp.zeros((B, M_HEADS, M_DK), f32),
                      jnp.zeros((B, M_HEADS), f32))
        st_fw, st_bw = zero_state, zero_state
    else:
        row, col = pos
        qa = _rope_2d(qa, row, col)
        keys = jnp.concatenate([ctx[0].astype(ka.dtype), _rope_2d(ka, row, col)], axis=1)
        vals = jnp.concatenate([ctx[1].astype(va.dtype), va], axis=1)
        st_fw, st_bw = ctx[2], ctx[3]
    attn = _attend(qa, keys, vals)
    qk = jax.nn.silu(_short_conv(jnp.concatenate([qm, km], axis=-1), conv_w))
    qm, km = jnp.split(qk, 2, axis=-1)
    qm = qm.reshape(B, T, M_HEADS, M_DK).astype(f32)
    km = km.reshape(B, T, M_HEADS, M_DK).astype(f32) * (M_DK ** -0.5)
    vm = vm.reshape(B, T, M_HEADS, M_DV).astype(f32)
    gm = gm.astype(f32) + gate_b.astype(f32)
    hm, st_f, st_b = _mlstm_bidir(qm, km, vm, gm, st_fw, st_bw)
    hm = _rmsnorm(hm, m_norm_g.reshape(M_HEADS, M_DV)).reshape(B, T, M_WIDTH).astype(x.dtype)
    hm = jax.nn.sigmoid(om) * hm
    fo = _fourier(fu)
    g_a, g_m, g_f = jnp.split(jax.nn.sigmoid(bg), 3, axis=-1)
    merged = g_a * (attn @ w_pa) + g_m * (hm @ w_pm) + g_f * (fo @ w_pf)
    x = x + gate1 * (merged @ w_out)
    h2 = _rmsnorm(x, norm2_g) * (1 + scale2) + shift2
    fg, fv = jnp.split(h2 @ w_ffn_in, 2, axis=-1)
    x = x + gate2 * ((jax.nn.silu(fg) * fv) @ w_ffn_out)
    return x, ka, va, st_f, st_b


def setup_inputs(seed: int = 0) -> dict:
    key = jax.random.key(seed)
    ks = jax.random.split(key, 32)
    f32 = jnp.float32

    def nrm(k, shape, s):
        return s * jax.random.normal(k, shape, f32)

    D = D_MODEL
    f_bias = jnp.linspace(3.0, 6.0, M_HEADS, dtype=f32)
    zh = jnp.zeros((M_HEADS,), f32)
    gate_base = jnp.concatenate([zh, f_bias, zh, f_bias])
    return {
        'x_prompt': nrm(ks[0], (BATCH, SEQ, D), 1.0),
        'x_sample': nrm(ks[1], (DEC_BATCH, DEC_SEQ, D), 1.0),
        'cache_k': nrm(ks[2], (DEC_BATCH, DEPTH, PAST_LEN, N_KV_HEADS, HEAD_DIM), 1.0),
        'cache_v': nrm(ks[3], (DEC_BATCH, DEPTH, PAST_LEN, N_KV_HEADS, HEAD_DIM), 1.0),
        'state_C': nrm(ks[4], (DEC_BATCH, DEPTH, 2, M_HEADS, M_DK, M_DV), 0.1),
        'state_n': nrm(ks[5], (DEC_BATCH, DEPTH, 2, M_HEADS, M_DK), 0.5),
        'state_m': 1.0 + nrm(ks[6], (DEC_BATCH, DEPTH, 2, M_HEADS), 0.5),
        'c': nrm(ks[7], (DEC_BATCH, D), 1.0),
        'c_ctx': nrm(ks[8], (D,), 1.0),
        'w_ada': nrm(ks[9], (DEPTH, D, 6 * D), D ** -0.5),
        'b_ada': nrm(ks[10], (DEPTH, 6 * D), 0.01),
        'norm1_g': 1.0 + nrm(ks[11], (DEPTH, D), 0.02),
        'w_in': nrm(ks[12], (DEPTH, D, IN_COLS), D ** -0.5),
        'q_norm_g': 1.0 + nrm(ks[13], (DEPTH, HEAD_DIM), 0.02),
        'k_norm_g': 1.0 + nrm(ks[14], (DEPTH, HEAD_DIM), 0.02),
        'm_conv_w': nrm(ks[15], (DEPTH, M_CONV, 2 * M_HEADS * M_DK), M_CONV ** -0.5),
        'm_gate_b': gate_base + nrm(ks[16], (DEPTH, 4 * M_HEADS), 0.1),
        'm_norm_g': 1.0 + nrm(ks[17], (DEPTH, M_WIDTH), 0.02),
        'w_proj_attn': nrm(ks[18], (DEPTH, ATTN_WIDTH, D), ATTN_WIDTH ** -0.5),
        'w_proj_mlstm': nrm(ks[19], (DEPTH, M_WIDTH, D), M_WIDTH ** -0.5),
        'w_proj_fourier': nrm(ks[20], (DEPTH, F_WIDTH, D), F_WIDTH ** -0.5),
        'w_out': nrm(ks[21], (DEPTH, D, D), D ** -0.5),
        'norm2_g': 1.0 + nrm(ks[22], (DEPTH, D), 0.02),
        'w_ffn_in': nrm(ks[23], (DEPTH, D, 2 * FF_HIDDEN), D ** -0.5),
        'w_ffn_out': nrm(ks[24], (DEPTH, FF_HIDDEN, D), FF_HIDDEN ** -0.5),
    }


def reference(x_prompt, x_sample, cache_k, cache_v, state_C, state_n, state_m, c, c_ctx,
              w_ada, b_ada, norm1_g, w_in, q_norm_g, k_norm_g, m_conv_w, m_gate_b, m_norm_g,
              w_proj_attn, w_proj_mlstm, w_proj_fourier, w_out, norm2_g, w_ffn_in, w_ffn_out):
    f32 = jnp.float32
    t_lat = x_sample.shape[1]
    rows = t_lat // GRID_W
    row = jnp.repeat(jnp.arange(rows, dtype=jnp.int32), GRID_W)
    col = jnp.tile(jnp.arange(GRID_W, dtype=jnp.int32), rows)
    xp = x_prompt
    xs = x_sample
    ks_, vs_, Cs, ns, ms = [], [], [], [], []
    for l in range(DEPTH):
        wts = (norm1_g[l], w_in[l], q_norm_g[l], k_norm_g[l], m_conv_w[l], m_gate_b[l], m_norm_g[l],
               w_proj_attn[l], w_proj_mlstm[l], w_proj_fourier[l], w_out[l], norm2_g[l],
               w_ffn_in[l], w_ffn_out[l])
        mod_ctx = _modulation(c_ctx, w_ada[l], b_ada[l])
        xp, k_l, v_l, st_f, st_b = _layer(xp, mod_ctx, None, None, *wts)
        ks_.append(k_l)
        vs_.append(v_l)
        Cs.append(jnp.stack([st_f[0], st_b[0]], axis=1))
        ns.append(jnp.stack([st_f[1], st_b[1]], axis=1))
        ms.append(jnp.stack([st_f[2], st_b[2]], axis=1))
        mod_lat = [m[:, None, :] for m in _modulation(c, w_ada[l], b_ada[l])]
        ctx_fw = (state_C[:, l, 0].astype(f32), state_n[:, l, 0].astype(f32), state_m[:, l, 0].astype(f32))
        ctx_bw = (state_C[:, l, 1].astype(f32), state_n[:, l, 1].astype(f32), state_m[:, l, 1].astype(f32))
        ctx = (cache_k[:, l], cache_v[:, l], ctx_fw, ctx_bw)
        xs = _layer(xs, mod_lat, (row, col), ctx, *wts)[0]
    new_k = jnp.stack(ks_, axis=1)
    new_v = jnp.stack(vs_, axis=1)
    new_C = jnp.stack(Cs, axis=1)
    new_n = jnp.stack(ns, axis=1)
    new_m = jnp.stack(ms, axis=1)
    return (xp, xs, new_k, new_v, new_C, new_n, new_m)
```

```python
import functools

import numpy as np
import jax
import jax.numpy as jnp
from jax import lax
from jax.experimental import pallas as pl
from jax.experimental.pallas import tpu as pltpu

D_MODEL = 1024
DEPTH = 2
GRID_W = 64
HEAD_DIM = 64
N_Q_HEADS = 8
N_KV_HEADS = 4
ATTN_WIDTH = N_Q_HEADS * HEAD_DIM
KV_WIDTH = N_KV_HEADS * HEAD_DIM
ROPE_THETA = 10000.0
M_HEADS = 4
M_DK = 64
M_DV = 64
M_WIDTH = M_HEADS * M_DV
F_GROUPS = 4
F_GROUP_CH = 64
F_WIDTH = F_GROUPS * F_GROUP_CH
FF_HIDDEN = -(-8 * D_MODEL // (3 * 256)) * 256
EPS = 1e-6
N_GATES = 4 * M_HEADS
IN_SIZES = (ATTN_WIDTH, KV_WIDTH, KV_WIDTH, M_HEADS * M_DK, M_HEADS * M_DK, M_WIDTH, M_WIDTH,
            N_GATES, F_WIDTH, 3 * D_MODEL)
IN_OFFS = tuple(int(v) for v in np.cumsum((0,) + IN_SIZES))

LANES = 128
GATE_PAD = LANES
N_MOD = 6
MOD_ROWS = 16
M_CHUNK_K = 256
NEG_BIG = -1e30
VMEM_LIMIT = 56 * 1024 * 1024

F32 = jnp.float32
BF16 = jnp.bfloat16


def _cparams(n_axes):
    return pltpu.CompilerParams(dimension_semantics=("arbitrary",) * n_axes,
                                vmem_limit_bytes=VMEM_LIMIT)


def _mm(a, b):
    return jnp.dot(a, b, preferred_element_type=F32)


def _mm_nt(a, b):
    return lax.dot_general(a, b, (((1,), (1,)), ((), ())), preferred_element_type=F32)


def _split3(x):
    hi = x.astype(BF16)
    r = x - hi.astype(F32)
    mid = r.astype(BF16)
    lo = (r - mid.astype(F32)).astype(BF16)
    return hi, mid, lo


def _mm_left_f32(x, m_bf16):
    hi, mid, lo = _split3(x)
    return _mm(hi, m_bf16) + _mm(mid, m_bf16) + _mm(lo, m_bf16)


def _mm_right_f32(m_bf16, x):
    hi, mid, lo = _split3(x)
    return _mm(m_bf16, hi) + _mm(m_bf16, mid) + _mm(m_bf16, lo)


def _sigmoid(x):
    return 1.0 / (1.0 + jnp.exp(-x))


def _silu(x):
    return x * _sigmoid(x)


def _log_sigmoid(x):
    return jnp.minimum(x, 0.0) - jnp.log(1.0 + jnp.exp(-jnp.abs(x)))


def _normmod(x, g, scale, shift):
    ms = jnp.mean(x * x, axis=-1, keepdims=True)
    return (x * lax.rsqrt(ms + EPS)) * g * (1.0 + scale) + shift


def _group_inv_rms(x, bd):
    x2 = x * x
    hi = x2.astype(BF16)
    lo = (x2 - hi.astype(F32)).astype(BF16)
    ss = _mm(hi, bd) + _mm(lo, bd)
    return lax.rsqrt(ss * (1.0 / HEAD_DIM) + EPS)


def _rope(x, cos, sin_signed):
    w = x.shape[1]
    lane = lax.broadcasted_iota(jnp.int32, x.shape, 1)
    up = pltpu.roll(x, w - 16, axis=1)
    dn = pltpu.roll(x, 16, axis=1)
    partner = jnp.where((lane & 31) < 16, up, dn)
    return x * cos + partner * sin_signed


def _mod_kernel(c_ref, w_ref, b_ref, o_ref):
    s = _silu(c_ref[...]).astype(BF16)
    o_ref[...] = _mm(s, w_ref[...].astype(BF16)) + b_ref[...]


def _modulation(cvec, w_ada, b_ada):
    tn = 512
    n_out = N_MOD * D_MODEL
    return pl.pallas_call(
        _mod_kernel,
        grid=(DEPTH, n_out // tn),
        in_specs=[
            pl.BlockSpec((MOD_ROWS, D_MODEL), lambda l, j: (0, 0)),
            pl.BlockSpec((None, D_MODEL, tn), lambda l, j: (l, 0, j)),
            pl.BlockSpec((None, 1, tn), lambda l, j: (l, 0, j)),
        ],
        out_specs=pl.BlockSpec((None, MOD_ROWS, tn), lambda l, j: (l, 0, j)),
        out_shape=jax.ShapeDtypeStruct((DEPTH, MOD_ROWS, n_out), F32),
        compiler_params=_cparams(2),
        name="modulation",
    )(cvec, w_ada, b_ada.reshape(DEPTH, 1, n_out))


MIX_COLS = 2 * D_MODEL + F_WIDTH + GATE_PAD


def _in_kernel(x_ref, mod_ref, g_ref, w_ref, za_ref, zm_ref, fu_ref, gt_ref):
    h = _normmod(x_ref[...], g_ref[...], mod_ref[1:2, :], mod_ref[0:1, :])
    z = _mm(h.astype(BF16), w_ref[...])
    za_ref[...] = z[:, :D_MODEL]
    zm_ref[...] = z[:, D_MODEL:2 * D_MODEL]
    fu_ref[...] = z[:, 2 * D_MODEL:2 * D_MODEL + F_WIDTH]
    gt_ref[...] = z[:, 2 * D_MODEL + F_WIDTH:]


def _in_proj(x, mod, g, w_mix, sample_of_tile, tm):
    rows = x.shape[0]
    row_spec = lambda width: pl.BlockSpec((tm, width), lambda i: (i, 0))
    return pl.pallas_call(
        _in_kernel,
        grid=(rows // tm,),
        in_specs=[
            row_spec(D_MODEL),
            pl.BlockSpec((None, N_MOD, D_MODEL), lambda i: (sample_of_tile(i), 0, 0)),
            pl.BlockSpec((1, D_MODEL), lambda i: (0, 0)),
            pl.BlockSpec((D_MODEL, MIX_COLS), lambda i: (0, 0)),
        ],
        out_specs=[row_spec(D_MODEL), row_spec(D_MODEL), row_spec(F_WIDTH), row_spec(GATE_PAD)],
        out_shape=[
            jax.ShapeDtypeStruct((rows, D_MODEL), F32),
            jax.ShapeDtypeStruct((rows, D_MODEL), F32),
            jax.ShapeDtypeStruct((rows, F_WIDTH), F32),
            jax.ShapeDtypeStruct((rows, GATE_PAD), F32),
        ],
        compiler_params=_cparams(1),
        name="in_proj",
    )(x, mod, g, w_mix)


def _attn_kernel(*refs, tq, t_new, t_past, latent):
    if latent:
        (q_ref, k_ref, v_ref, ck_ref, cv_ref, cosq_ref, sinq_ref, cosk_ref, sink_ref,
         qg_ref, kg_ref, bdq_ref, bdk_ref, o_ref, ks_ref, vs_ref) = refs
    else:
        (q_ref, k_ref, v_ref, qg_ref, kg_ref, bdq_ref, bdk_ref,
         o_ref, nk_ref, nv_ref, ks_ref, vs_ref) = refs

    @pl.when(pl.program_id(1) == 0)
    def _():
        k = k_ref[...]
        kn = k * _group_inv_rms(k, bdk_ref[...]) * kg_ref[...]
        v = v_ref[...]
        if latent:
            kn = _rope(kn, cosk_ref[...], sink_ref[...])
            ck = ck_ref[...]
            cv = cv_ref[...]
        else:
            nk_ref[...] = kn
            nv_ref[...] = v
        for h in range(N_KV_HEADS):
            sl = slice(HEAD_DIM * h, HEAD_DIM * (h + 1))
            if latent:
                ks_ref[h, :t_past, :] = ck[:, sl].astype(BF16)
                vs_ref[h, :t_past, :] = cv[:, sl].astype(BF16)
            ks_ref[h, t_past:, :] = kn[:, sl].astype(BF16)
            vs_ref[h, t_past:, :] = v[:, sl].astype(BF16)

    q = q_ref[...]
    qn = q * _group_inv_rms(q, bdq_ref[...]) * qg_ref[...]
    if latent:
        qn = _rope(qn, cosq_ref[...], sinq_ref[...])
    qn = qn * (HEAD_DIM ** -0.5)
    group = N_Q_HEADS // N_KV_HEADS
    outs = []
    for h in range(N_KV_HEADS):
        qh = jnp.concatenate(
            [qn[:, HEAD_DIM * (group * h + g):HEAD_DIM * (group * h + g + 1)] for g in range(group)],
            axis=0).astype(BF16)
        s = _mm_nt(qh, ks_ref[h])
        m = jnp.max(s, axis=-1, keepdims=True)
        p = jnp.exp(s - m)
        l = jnp.sum(p, axis=-1, keepdims=True)
        o = _mm(p.astype(BF16), vs_ref[h]) / l
        outs.extend(o[tq * g:tq * (g + 1)] for g in range(group))
    o_ref[...] = jnp.concatenate(outs, axis=1)


def _block_diag_ones(width):
    idx = np.arange(width) // HEAD_DIM
    return jnp.asarray(idx[:, None] == idx[None, :], dtype=BF16)


def _attention(za, qg, kg, *, batch, t_new, latent, cache_k=None, cache_v=None, layer=0, rope_tabs=None):
    tq = 256
    nq = t_new // tq
    t_past = cache_k.shape[2] if latent else 0
    t_keys = t_past + t_new
    kcol = ATTN_WIDTH // KV_WIDTH
    bdq = _block_diag_ones(ATTN_WIDTH)
    bdk = _block_diag_ones(KV_WIDTH)
    const = lambda shape: pl.BlockSpec(shape, lambda b, i: (0,) * len(shape))
    in_specs = [
        pl.BlockSpec((tq, ATTN_WIDTH), lambda b, i: (b * nq + i, 0)),
        pl.BlockSpec((t_new, KV_WIDTH), lambda b, i: (b, kcol)),
        pl.BlockSpec((t_new, KV_WIDTH), lambda b, i: (b, kcol + 1)),
    ]
    args = [za, za, za]
    if latent:
        cos_t, sin_t = rope_tabs
        in_specs += [
            pl.BlockSpec((None, None, t_past, KV_WIDTH), lambda b, i: (b, layer, 0, 0)),
            pl.BlockSpec((None, None, t_past, KV_WIDTH), lambda b, i: (b, layer, 0, 0)),
            pl.BlockSpec((tq, ATTN_WIDTH), lambda b, i: (i, 0)),
            pl.BlockSpec((tq, ATTN_WIDTH), lambda b, i: (i, 0)),
            pl.BlockSpec((t_new, KV_WIDTH), lambda b, i: (0, 0)),
            pl.BlockSpec((t_new, KV_WIDTH), lambda b, i: (0, 0)),
        ]
        args += [cache_k, cache_v, cos_t, sin_t, cos_t, sin_t]
    in_specs += [const((1, ATTN_WIDTH)), const((1, KV_WIDTH)),
                 const((ATTN_WIDTH, ATTN_WIDTH)), const((KV_WIDTH, KV_WIDTH))]
    args += [qg, kg, bdq, bdk]
    out_specs = [pl.BlockSpec((tq, ATTN_WIDTH), lambda b, i: (b * nq + i, 0))]
    out_shape = [jax.ShapeDtypeStruct((batch * t_new, ATTN_WIDTH), F32)]
    if not latent:
        out_specs += [pl.BlockSpec((t_new, KV_WIDTH), lambda b, i: (b, 0))] * 2
        out_shape += [jax.ShapeDtypeStruct((batch * t_new, KV_WIDTH), F32)] * 2
    return pl.pallas_call(
        functools.partial(_attn_kernel, tq=tq, t_new=t_new, t_past=t_past, latent=latent),
        grid=(batch, nq),
        in_specs=in_specs,
        out_specs=out_specs,
        out_shape=out_shape,
        scratch_shapes=[pltpu.VMEM((N_KV_HEADS, t_keys, HEAD_DIM), BF16),
                        pltpu.VMEM((N_KV_HEADS, t_keys, HEAD_DIM), BF16)],
        compiler_params=_cparams(2),
        name="attention_latent" if latent else "attention_context",
    )(*args)


def _mlstm_unit(qc, kc, ktc, vaug, bcol, brow, irow, mask, last, m, caug):
    dmat = jnp.where(mask, bcol - brow + irow, NEG_BIG)
    inter = bcol + m
    mj = jnp.maximum(inter, jnp.max(dmat, axis=-1, keepdims=True))
    w = jnp.exp(dmat - mj)
    wi = jnp.exp(inter - mj)
    s = _mm_nt(qc, kc) * w
    nd = _mm(s.astype(BF16), vaug) + wi * _mm(qc, caug.astype(BF16))
    num = nd[:, :M_DV]
    den = nd[:, M_DV:M_DV + 1]
    h = num / jnp.maximum(jnp.abs(den), jnp.exp(-mj))
    bl = brow[:, last:last + 1]
    grow = bl - brow + irow
    m_new = jnp.maximum(bl + m, jnp.max(grow, axis=-1, keepdims=True))
    wk = jnp.exp(grow - m_new)
    decay = jnp.exp(bl + m - m_new)
    caug_new = decay * caug + _mm((ktc * wk).astype(BF16), vaug)
    return h, m_new, caug_new


def _mlstm_kernel(*refs, t, chunk, has_state, emit_state):
    refs = list(refs)
    zm_ref, gt_ref, cw_ref, gb_ref, mg_ref, bd_ref = refs[:6]
    refs = refs[6:]
    if has_state:
        c0_ref, m0_ref = refs[:2]
        refs = refs[2:]
    o_ref = refs[0]
    refs = refs[1:]
    if emit_state:
        cs_ref, ms_ref = refs[:2]
        refs = refs[2:]
    hs_ref = refs[0]

    width = M_HEADS * M_DK
    z = zm_ref[...]
    x = z[:, :2 * width]
    row = lax.broadcasted_iota(jnp.int32, (t, 1), 0)
    x_prev = jnp.where(row == 0, 0.0, pltpu.roll(x, 1, axis=0))
    x_next = jnp.where(row == t - 1, 0.0, pltpu.roll(x, t - 1, axis=0))
    cw = cw_ref[...]
    qk = _silu(x_prev * cw[0:1, :] + x * cw[1:2, :] + x_next * cw[2:3, :])
    q = qk[:, :width]
    k = qk[:, width:] * (M_DK ** -0.5)
    v = z[:, 2 * width:2 * width + M_WIDTH]
    om = z[:, 2 * width + M_WIDTH:]
    kt = k.T

    gates = gt_ref[...] + gb_ref[...]
    lane = lax.broadcasted_iota(jnp.int32, (1, GATE_PAD), 1)
    is_forget = (lane & M_HEADS) != 0
    gates = jnp.where(is_forget, _log_sigmoid(gates), gates)
    gates_t = gates.T

    ri = lax.broadcasted_iota(jnp.int32, (chunk, chunk), 0)
    ci = lax.broadcasted_iota(jnp.int32, (chunk, chunk), 1)
    lower = ci <= ri
    upper = ci >= ri
    tri_l = lower.astype(BF16)
    tri_u = upper.astype(BF16)
    ones_col = (lax.broadcasted_iota(jnp.int32, (t, LANES - M_DV), 1) == 0).astype(F32)

    q_b = q.astype(BF16)
    k_b = k.astype(BF16)
    vaug = [jnp.concatenate([v[:, M_DV * h:M_DV * (h + 1)], ones_col], axis=1).astype(BF16)
            for h in range(M_HEADS)]

    n_chunks = t // chunk
    for direction in range(2):
        order = range(n_chunks) if direction == 0 else range(n_chunks - 1, -1, -1)
        state = []
        for h in range(M_HEADS):
            u = direction * M_HEADS + h
            if has_state:
                state.append((m0_ref[u:u + 1, 0:1], c0_ref[u]))
            else:
                state.append((jnp.zeros((1, 1), F32), jnp.zeros((M_DK, LANES), F32)))
        for c in order:
            rs = slice(chunk * c, chunk * (c + 1))
            gc = gates[rs, :]
            gtc = gates_t[:N_GATES, rs]
            if direction == 0:
                cum_col = _mm_right_f32(tri_l, gc)
                cum_row = _mm_left_f32(gtc, tri_u)
                mask, last = lower, chunk - 1
            else:
                cum_col = _mm_right_f32(tri_u, gc)
                cum_row = _mm_left_f32(gtc, tri_l)
                mask, last = upper, 0
            outs = []
            for h in range(M_HEADS):
                ig = 2 * direction * M_HEADS + h
                fg = ig + M_HEADS
                hs = slice(M_DK * h, M_DK * (h + 1))
                m, caug = state[h]
                hout, m, caug = _mlstm_unit(
                    q_b[rs, hs], k_b[rs, hs], kt[hs, rs], vaug[h][rs, :],
                    cum_col[:, fg:fg + 1], cum_row[fg:fg + 1, :], gtc[ig:ig + 1, :],
                    mask, last, m, caug)
                state[h] = (m, caug)
                outs.append(hout)
            hcat = jnp.concatenate(outs, axis=1)
            if direction == 0:
                hs_ref[rs, :] = hcat
            else:
                hs_ref[rs, :] += hcat
        if emit_state:
            for h in range(M_HEADS):
                u = direction * M_HEADS + h
                m, caug = state[h]
                cs_ref[u] = caug
                ms_ref[u:u + 1, :] = jnp.broadcast_to(m, (1, LANES))

    hsum = hs_ref[...]
    hn = hsum * _group_inv_rms(hsum, bd_ref[...]) * mg_ref[...]
    o_ref[...] = _sigmoid(om) * hn


def _mlstm(zm, gt, conv_w, gate_b, m_norm_g, *, batch, t, caug0=None, m0=None):
    has_state = caug0 is not None
    emit_state = not has_state
    chunk = min(M_CHUNK_K, t)
    n_units = 2 * M_HEADS
    const = lambda shape: pl.BlockSpec(shape, lambda b: (0,) * len(shape))
    in_specs = [
        pl.BlockSpec((t, D_MODEL), lambda b: (b, 0)),
        pl.BlockSpec((t, GATE_PAD), lambda b: (b, 0)),
        const((3, 2 * M_HEADS * M_DK)),
        const((1, GATE_PAD)),
        const((1, M_WIDTH)),
        const((M_WIDTH, M_WIDTH)),
    ]
    args = [zm, gt, conv_w, gate_b, m_norm_g, _block_diag_ones(M_WIDTH)]
    if has_state:
        in_specs += [pl.BlockSpec((None, n_units, M_DK, LANES), lambda b: (b, 0, 0, 0)),
                     pl.BlockSpec((None, n_units, LANES), lambda b: (b, 0, 0))]
        args += [caug0, m0]
    out_specs = [pl.BlockSpec((t, M_WIDTH), lambda b: (b, 0))]
    out_shape = [jax.ShapeDtypeStruct((batch * t, M_WIDTH), F32)]
    if emit_state:
        out_specs += [pl.BlockSpec((None, n_units, M_DK, LANES), lambda b: (b, 0, 0, 0)),
                      pl.BlockSpec((None, n_units, LANES), lambda b: (b, 0, 0))]
        out_shape += [jax.ShapeDtypeStruct((batch, n_units, M_DK, LANES), F32),
                      jax.ShapeDtypeStruct((batch, n_units, LANES), F32)]
    return pl.pallas_call(
        functools.partial(_mlstm_kernel, t=t, chunk=chunk, has_state=has_state, emit_state=emit_state),
        grid=(batch,),
        in_specs=in_specs,
        out_specs=out_specs,
        out_shape=out_shape,
        scratch_shapes=[pltpu.VMEM((t, M_WIDTH), F32)],
        compiler_params=_cparams(1),
        name="mlstm_latent" if has_state else "mlstm_context",
    )(*args)


def _fourier_kernel(u_ref, cs_ref, ct_ref, st_ref, o_ref):
    a = _mm(u_ref[...].astype(BF16), cs_ref[...])
    ac = a[:, :F_WIDTH].astype(BF16)
    asn = a[:, F_WIDTH:].astype(BF16)
    o_ref[...] = _mm(ct_ref[...], ac) - _mm(st_ref[...], asn)


def _dft_tables(t):
    kt = (np.arange(t)[:, None] * np.arange(t)[None, :]) % t
    ang_t = 2.0 * np.pi * kt.astype(np.float64) / t
    ct = np.cos(ang_t) / np.sqrt(t)
    st = np.sin(ang_t) / np.sqrt(t)
    c = F_GROUP_CH
    kc = (np.arange(c)[:, None] * np.arange(c)[None, :]) % c
    ang_c = 2.0 * np.pi * kc.astype(np.float64) / c
    eye = np.eye(F_GROUPS)
    cc = np.kron(eye, np.cos(ang_c) / np.sqrt(c))
    sc = np.kron(eye, np.sin(ang_c) / np.sqrt(c))
    cs = np.concatenate([cc, sc], axis=1)
    to_dev = lambda a: jnp.asarray(a.astype(np.float32)).astype(BF16)
    return to_dev(cs), to_dev(ct), to_dev(st)


def _fourier(fu, *, batch, t):
    cs, ct, st = _dft_tables(t)
    const = lambda shape: pl.BlockSpec(shape, lambda b: (0,) * len(shape))
    return pl.pallas_call(
        _fourier_kernel,
        grid=(batch,),
        in_specs=[pl.BlockSpec((t, F_WIDTH), lambda b: (b, 0)),
                  const((F_WIDTH, 2 * F_WIDTH)), const((t, t)), const((t, t))],
        out_specs=pl.BlockSpec((t, F_WIDTH), lambda b: (b, 0)),
        out_shape=jax.ShapeDtypeStruct((batch * t, F_WIDTH), F32),
        compiler_params=_cparams(1),
        name="fourier",
    )(fu, cs, ct, st)


def _merge_kernel(x_ref, mod_ref, g_ref, a_ref, hm_ref, fo_ref,
                  wbg_ref, wpa_ref, wpm_ref, wpf_ref, wo_ref, o_ref):
    x = x_ref[...]
    h = _normmod(x, g_ref[...], mod_ref[1:2, :], mod_ref[0:1, :]).astype(BF16)

    def branch(j, y_ref, w_ref):
        gate = _sigmoid(_mm(h, wbg_ref[:, D_MODEL * j:D_MODEL * (j + 1)]))
        return gate * _mm(y_ref[...].astype(BF16), w_ref[...])

    merged = branch(0, a_ref, wpa_ref) + branch(1, hm_ref, wpm_ref) + branch(2, fo_ref, wpf_ref)
    o_ref[...] = x + mod_ref[2:3, :] * _mm(merged.astype(BF16), wo_ref[...])


def _merge(x, mod, g, attn, hm, fo, w_bg, w_pa, w_pm, w_pf, w_out, sample_of_tile, tm):
    rows = x.shape[0]
    row_spec = lambda width: pl.BlockSpec((tm, width), lambda i: (i, 0))
    const = lambda shape: pl.BlockSpec(shape, lambda i: (0,) * len(shape))
    return pl.pallas_call(
        _merge_kernel,
        grid=(rows // tm,),
        in_specs=[
            row_spec(D_MODEL),
            pl.BlockSpec((None, N_MOD, D_MODEL), lambda i: (sample_of_tile(i), 0, 0)),
            const((1, D_MODEL)),
            row_spec(ATTN_WIDTH), row_spec(M_WIDTH), row_spec(F_WIDTH),
            const((D_MODEL, 3 * D_MODEL)), const((ATTN_WIDTH, D_MODEL)),
            const((M_WIDTH, D_MODEL)), const((F_WIDTH, D_MODEL)), const((D_MODEL, D_MODEL)),
        ],
        out_specs=row_spec(D_MODEL),
        out_shape=jax.ShapeDtypeStruct((rows, D_MODEL), F32),
        compiler_params=_cparams(1),
        name="merge",
    )(x, mod, g, attn, hm, fo, w_bg, w_pa, w_pm, w_pf, w_out)


def _ffn_kernel(x_ref, mod_ref, g_ref, win_ref, wout_ref, o_ref):
    x = x_ref[...]
    h = _normmod(x, g_ref[...], mod_ref[4:5, :], mod_ref[3:4, :]).astype(BF16)
    u = _mm(h, win_ref[...])
    a = (_silu(u[:, :FF_HIDDEN]) * u[:, FF_HIDDEN:]).astype(BF16)
    o_ref[...] = x + mod_ref[5:6, :] * _mm(a, wout_ref[...])


def _ffn(x, mod, g, w_in, w_out, sample_of_tile, tm):
    rows = x.shape[0]
    row_spec = pl.BlockSpec((tm, D_MODEL), lambda i: (i, 0))
    const = lambda shape: pl.BlockSpec(shape, lambda i: (0,) * len(shape))
    return pl.pallas_call(
        _ffn_kernel,
        grid=(rows // tm,),
        in_specs=[
            row_spec,
            pl.BlockSpec((None, N_MOD, D_MODEL), lambda i: (sample_of_tile(i), 0, 0)),
            const((1, D_MODEL)),
            const((D_MODEL, 2 * FF_HIDDEN)), const((FF_HIDDEN, D_MODEL)),
        ],
        out_specs=row_spec,
        out_shape=jax.ShapeDtypeStruct((rows, D_MODEL), F32),
        compiler_params=_cparams(1),
        name="ffn",
    )(x, mod, g, w_in, w_out)


def _rope_tables(t):
    n = HEAD_DIM // 4
    inv = 1.0 / (ROPE_THETA ** (np.arange(n, dtype=np.float64) / n))
    pos = np.arange(t)
    ang_r = (pos // GRID_W)[:, None] * inv[None, :]
    ang_c = (pos % GRID_W)[:, None] * inv[None, :]
    cos = np.concatenate([np.cos(ang_r)] * 2 + [np.cos(ang_c)] * 2, axis=1)
    sin = np.concatenate([-np.sin(ang_r), np.sin(ang_r), -np.sin(ang_c), np.sin(ang_c)], axis=1)
    tile = lambda a: jnp.asarray(np.tile(a, (1, N_Q_HEADS)).astype(np.float32))
    return tile(cos), tile(sin)


def _layer(x, mod, sample_of_tile, wts, *, batch, t, latent, layer, ctx=None, rope_tabs=None):
    (norm1_g, w_mix, w_bg, qg, kg, conv_w, gate_b, m_norm_g, w_pa, w_pm, w_pf, w_out,
     norm2_g, w_ffn_in, w_ffn_out) = wts
    tm = 512
    za, zm, fu, gt = _in_proj(x, mod, norm1_g, w_mix, sample_of_tile(tm), tm)
    extra = ()
    if latent:
        cache_k, cache_v, caug0, m0 = ctx
        (attn,) = _attention(za, qg, kg, batch=batch, t_new=t, latent=True, cache_k=cache_k,
                             cache_v=cache_v, layer=layer, rope_tabs=rope_tabs)
        (hm,) = _mlstm(zm, gt, conv_w, gate_b, m_norm_g, batch=batch, t=t, caug0=caug0, m0=m0)
    else:
        attn, new_k, new_v = _attention(za, qg, kg, batch=batch, t_new=t, latent=False)
        hm, cs, ms = _mlstm(zm, gt, conv_w, gate_b, m_norm_g, batch=batch, t=t)
        extra = (new_k, new_v, cs, ms)
    fo = _fourier(fu, batch=batch, t=t)
    x = _merge(x, mod, norm1_g, attn, hm, fo, w_bg, w_pa, w_pm, w_pf, w_out, sample_of_tile(tm), tm)
    tm_ffn = 256
    x = _ffn(x, mod, norm2_g, w_ffn_in, w_ffn_out, sample_of_tile(tm_ffn), tm_ffn)
    return x, extra


def kernel(x_prompt, x_sample, cache_k, cache_v, state_C, state_n, state_m, c, c_ctx, w_ada, b_ada,
           norm1_g, w_in, q_norm_g, k_norm_g, m_conv_w, m_gate_b, m_norm_g, w_proj_attn,
           w_proj_mlstm, w_proj_fourier, w_out, norm2_g, w_ffn_in, w_ffn_out):
    n_ctx, t_ctx, _ = x_prompt.shape
    n_lat, t_lat, _ = x_sample.shape
    t_past = cache_k.shape[2]
    n_units = 2 * M_HEADS

    cvec = jnp.concatenate([c_ctx[None, :], c], axis=0)
    cvec = jnp.pad(cvec, ((0, MOD_ROWS - cvec.shape[0]), (0, 0)))
    mod = _modulation(cvec, w_ada, b_ada).reshape(DEPTH, MOD_ROWS, N_MOD, D_MODEL)

    rope_tabs = _rope_tables(t_lat)
    ck = cache_k.reshape(n_lat, DEPTH, t_past, KV_WIDTH)
    cv = cache_v.reshape(n_lat, DEPTH, t_past, KV_WIDTH)

    ctx_sample = lambda tm: (lambda i: 0)
    lat_sample = lambda tm: (lambda i: 1 + i // (t_lat // tm))

    xp = x_prompt.reshape(n_ctx * t_ctx, D_MODEL)
    xs = x_sample.reshape(n_lat * t_lat, D_MODEL)
    ks, vs, cs_all, ms_all = [], [], [], []
    o = IN_OFFS
    for l in range(DEPTH):
        w = w_in[l]
        w_mix = jnp.concatenate(
            [w[:, o[0]:o[7]], w[:, o[8]:o[9]], w[:, o[7]:o[8]],
             jnp.zeros((D_MODEL, GATE_PAD - N_GATES), w.dtype)], axis=1).astype(BF16)
        w_bg = w[:, o[9]:o[10]].astype(BF16)
        tile_row = lambda g, reps: jnp.tile(g, reps)[None, :]
        wts = (norm1_g[l][None, :], w_mix, w_bg,
               tile_row(q_norm_g[l], N_Q_HEADS), tile_row(k_norm_g[l], N_KV_HEADS),
               m_conv_w[l], jnp.pad(m_gate_b[l], (0, GATE_PAD - N_GATES))[None, :],
               m_norm_g[l][None, :],
               w_proj_attn[l].astype(BF16), w_proj_mlstm[l].astype(BF16),
               w_proj_fourier[l].astype(BF16), w_out[l].astype(BF16),
               norm2_g[l][None, :], w_ffn_in[l].astype(BF16), w_ffn_out[l].astype(BF16))

        xp, (k_l, v_l, cs, ms) = _layer(xp, mod[l], ctx_sample, wts, batch=n_ctx, t=t_ctx,
                                        latent=False, layer=l)
        ks.append(k_l.reshape(n_ctx, t_ctx, N_KV_HEADS, HEAD_DIM))
        vs.append(v_l.reshape(n_ctx, t_ctx, N_KV_HEADS, HEAD_DIM))
        cs_all.append(cs)
        ms_all.append(ms)

        c0 = state_C[:, l].astype(F32).reshape(n_lat, n_units, M_DK, M_DV)
        n0 = state_n[:, l].astype(F32).reshape(n_lat, n_units, M_DK, 1)
        caug0 = jnp.concatenate(
            [c0, n0, jnp.zeros((n_lat, n_units, M_DK, LANES - M_DV - 1), F32)], axis=-1)
        m0 = jnp.broadcast_to(state_m[:, l].astype(F32).reshape(n_lat, n_units, 1),
                              (n_lat, n_units, LANES))
        xs, _ = _layer(xs, mod[l], lat_sample, wts, batch=n_lat, t=t_lat, latent=True, layer=l,
                       ctx=(ck, cv, caug0, m0), rope_tabs=rope_tabs)

    cs_all = jnp.stack(cs_all, axis=1)
    ms_all = jnp.stack(ms_all, axis=1)
    new_c = cs_all[..., :M_DV].reshape(n_ctx, DEPTH, 2, M_HEADS, M_DK, M_DV)
    new_n = cs_all[..., M_DV].reshape(n_ctx, DEPTH, 2, M_HEADS, M_DK)
    new_m = ms_all[..., 0].reshape(n_ctx, DEPTH, 2, M_HEADS)
    return (xp.reshape(n_ctx, t_ctx, D_MODEL), xs.reshape(n_lat, t_lat, D_MODEL),
            jnp.stack(ks, axis=1), jnp.stack(vs, axis=1), new_c, new_n, new_m)
```

```python
import functools

import numpy as np
import jax
import jax.numpy as jnp
from jax import lax
from jax.experimental import pallas as pl
from jax.experimental.pallas import tpu as pltpu

D_MODEL = 1024
DEPTH = 2
GRID_W = 64
HEAD_DIM = 64
N_Q_HEADS = 8
N_KV_HEADS = 4
ATTN_WIDTH = N_Q_HEADS * HEAD_DIM
KV_WIDTH = N_KV_HEADS * HEAD_DIM
ROPE_THETA = 10000.0
M_HEADS = 4
M_DK = 64
M_DV = 64
M_WIDTH = M_HEADS * M_DV
F_GROUPS = 4
F_GROUP_CH = 64
F_WIDTH = F_GROUPS * F_GROUP_CH
FF_HIDDEN = -(-8 * D_MODEL // (3 * 256)) * 256
EPS = 1e-6
N_GATES = 4 * M_HEADS
IN_SIZES = (ATTN_WIDTH, KV_WIDTH, KV_WIDTH, M_HEADS * M_DK, M_HEADS * M_DK, M_WIDTH, M_WIDTH,
            N_GATES, F_WIDTH, 3 * D_MODEL)
IN_OFFS = tuple(int(v) for v in np.cumsum((0,) + IN_SIZES))

LANES = 128
GATE_PAD = LANES
N_MOD = 6
MOD_ROWS = 16
M_CHUNK_K = 128
NEG_BIG = -1e30
VMEM_LIMIT = 56 * 1024 * 1024

F32 = jnp.float32
BF16 = jnp.bfloat16


def _cparams(n_axes):
    return pltpu.CompilerParams(dimension_semantics=("arbitrary",) * n_axes,
                                vmem_limit_bytes=VMEM_LIMIT)


def _mm(a, b):
    return jnp.dot(a, b, preferred_element_type=F32)


def _mm_nt(a, b):
    return lax.dot_general(a, b, (((1,), (1,)), ((), ())), preferred_element_type=F32)


def _split3(x):
    hi = x.astype(BF16)
    r = x - hi.astype(F32)
    mid = r.astype(BF16)
    lo = (r - mid.astype(F32)).astype(BF16)
    return hi, mid, lo


def _mm_left_f32(x, m_bf16):
    hi, mid, lo = _split3(x)
    return _mm(hi, m_bf16) + _mm(mid, m_bf16) + _mm(lo, m_bf16)


def _mm_right_f32(m_bf16, x):
    hi, mid, lo = _split3(x)
    return _mm(m_bf16, hi) + _mm(m_bf16, mid) + _mm(m_bf16, lo)


def _sigmoid(x):
    return 1.0 / (1.0 + jnp.exp(-x))


def _silu(x):
    return x * _sigmoid(x)


def _log_sigmoid(x):
    return jnp.minimum(x, 0.0) - jnp.log(1.0 + jnp.exp(-jnp.abs(x)))


def _normmod(x, g, scale, shift):
    ms = jnp.mean(x * x, axis=-1, keepdims=True)
    return (x * lax.rsqrt(ms + EPS)) * g * (1.0 + scale) + shift


def _group_inv_rms(x, bd):
    x2 = x * x
    hi = x2.astype(BF16)
    lo = (x2 - hi.astype(F32)).astype(BF16)
    ss = _mm(hi, bd) + _mm(lo, bd)
    return lax.rsqrt(ss * (1.0 / HEAD_DIM) + EPS)


def _rope(x, cos, sin_signed):
    w = x.shape[1]
    lane = lax.broadcasted_iota(jnp.int32, x.shape, 1)
    up = pltpu.roll(x, w - 16, axis=1)
    dn = pltpu.roll(x, 16, axis=1)
    partner = jnp.where((lane & 31) < 16, up, dn)
    return x * cos + partner * sin_signed


def _mod_kernel(c_ref, w_ref, b_ref, o_ref):
    s = _silu(c_ref[...]).astype(BF16)
    o_ref[...] = _mm(s, w_ref[...].astype(BF16)) + b_ref[...]


def _modulation(cvec, w_ada, b_ada):
    tn = 512
    n_out = N_MOD * D_MODEL
    return pl.pallas_call(
        _mod_kernel,
        grid=(DEPTH, n_out // tn),
        in_specs=[
            pl.BlockSpec((MOD_ROWS, D_MODEL), lambda l, j: (0, 0)),
            pl.BlockSpec((None, D_MODEL, tn), lambda l, j: (l, 0, j)),
            pl.BlockSpec((None, 1, tn), lambda l, j: (l, 0, j)),
        ],
        out_specs=pl.BlockSpec((None, MOD_ROWS, tn), lambda l, j: (l, 0, j)),
        out_shape=jax.ShapeDtypeStruct((DEPTH, MOD_ROWS, n_out), F32),
        compiler_params=_cparams(2),
        name="modulation",
    )(cvec, w_ada, b_ada.reshape(DEPTH, 1, n_out))


MIX_COLS = 2 * D_MODEL + F_WIDTH + GATE_PAD


def _in_kernel(x_ref, mod_ref, g_ref, w_ref, za_ref, zm_ref, fu_ref, gt_ref):
    h = _normmod(x_ref[...], g_ref[...], mod_ref[1:2, :], mod_ref[0:1, :])
    z = _mm(h.astype(BF16), w_ref[...])
    za_ref[...] = z[:, :D_MODEL]
    zm_ref[...] = z[:, D_MODEL:2 * D_MODEL]
    fu_ref[...] = z[:, 2 * D_MODEL:2 * D_MODEL + F_WIDTH]
    gt_ref[...] = z[:, 2 * D_MODEL + F_WIDTH:]


def _in_proj(x, mod, g, w_mix, sample_of_tile, tm):
    rows = x.shape[0]
    row_spec = lambda width: pl.BlockSpec((tm, width), lambda i: (i, 0))
    return pl.pallas_call(
        _in_kernel,
        grid=(rows // tm,),
        in_specs=[
            row_spec(D_MODEL),
            pl.BlockSpec((None, N_MOD, D_MODEL), lambda i: (sample_of_tile(i), 0, 0)),
            pl.BlockSpec((1, D_MODEL), lambda i: (0, 0)),
            pl.BlockSpec((D_MODEL, MIX_COLS), lambda i: (0, 0)),
        ],
        out_specs=[row_spec(D_MODEL), row_spec(D_MODEL), row_spec(F_WIDTH), row_spec(GATE_PAD)],
        out_shape=[
            jax.ShapeDtypeStruct((rows, D_MODEL), F32),
            jax.ShapeDtypeStruct((rows, D_MODEL), F32),
            jax.ShapeDtypeStruct((rows, F_WIDTH), F32),
            jax.ShapeDtypeStruct((rows, GATE_PAD), F32),
        ],
        compiler_params=_cparams(1),
        name="in_proj",
    )(x, mod, g, w_mix)


def _attn_kernel(*refs, tq, t_new, t_past, latent):
    if latent:
        (q_ref, k_ref, v_ref, ck_ref, cv_ref, cosq_ref, sinq_ref, cosk_ref, sink_ref,
         qg_ref, kg_ref, bdq_ref, bdk_ref, o_ref, ks_ref, vs_ref) = refs
    else:
        (q_ref, k_ref, v_ref, qg_ref, kg_ref, bdq_ref, bdk_ref,
         o_ref, nk_ref, nv_ref, ks_ref, vs_ref) = refs

    @pl.when(pl.program_id(1) == 0)
    def _():
        k = k_ref[...]
        kn = k * _group_inv_rms(k, bdk_ref[...]) * kg_ref[...]
        v = v_ref[...]
        if latent:
            kn = _rope(kn, cosk_ref[...], sink_ref[...])
            ck = ck_ref[...]
            cv = cv_ref[...]
        else:
            nk_ref[...] = kn
            nv_ref[...] = v
        for h in range(N_KV_HEADS):
            sl = slice(HEAD_DIM * h, HEAD_DIM * (h + 1))
            if latent:
                ks_ref[h, :t_past, :] = ck[:, sl].astype(BF16)
                vs_ref[h, :t_past, :] = cv[:, sl].astype(BF16)
            ks_ref[h, t_past:, :] = kn[:, sl].astype(BF16)
            vs_ref[h, t_past:, :] = v[:, sl].astype(BF16)

    q = q_ref[...]
    qn = q * _group_inv_rms(q, bdq_ref[...]) * qg_ref[...]
    if latent:
        qn = _rope(qn, cosq_ref[...], sinq_ref[...])
    qn = qn * (HEAD_DIM ** -0.5)
    group = N_Q_HEADS // N_KV_HEADS
    outs = []
    for h in range(N_KV_HEADS):
        qh = jnp.concatenate(
            [qn[:, HEAD_DIM * (group * h + g):HEAD_DIM * (group * h + g + 1)] for g in range(group)],
            axis=0).astype(BF16)
        s = _mm_nt(qh, ks_ref[h])
        m = jnp.max(s, axis=-1, keepdims=True)
        p = jnp.exp(s - m)
        l = jnp.sum(p, axis=-1, keepdims=True)
        o = _mm(p.astype(BF16), vs_ref[h]) / l
        outs.extend(o[tq * g:tq * (g + 1)] for g in range(group))
    o_ref[...] = jnp.concatenate(outs, axis=1)


def _block_diag_ones(width):
    idx = np.arange(width) // HEAD_DIM
    return jnp.asarray(idx[:, None] == idx[None, :], dtype=BF16)


def _attention(za, qg, kg, *, batch, t_new, latent, cache_k=None, cache_v=None, layer=0, rope_tabs=None):
    tq = 256
    nq = t_new // tq
    t_past = cache_k.shape[2] if latent else 0
    t_keys = t_past + t_new
    kcol = ATTN_WIDTH // KV_WIDTH
    bdq = _block_diag_ones(ATTN_WIDTH)
    bdk = _block_diag_ones(KV_WIDTH)
    const = lambda shape: pl.BlockSpec(shape, lambda b, i: (0,) * len(shape))
    in_specs = [
        pl.BlockSpec((tq, ATTN_WIDTH), lambda b, i: (b * nq + i, 0)),
        pl.BlockSpec((t_new, KV_WIDTH), lambda b, i: (b, kcol)),
        pl.BlockSpec((t_new, KV_WIDTH), lambda b, i: (b, kcol + 1)),
    ]
    args = [za, za, za]
    if latent:
        cos_t, sin_t = rope_tabs
        in_specs += [
            pl.BlockSpec((None, None, t_past, KV_WIDTH), lambda b, i: (b, layer, 0, 0)),
            pl.BlockSpec((None, None, t_past, KV_WIDTH), lambda b, i: (b, layer, 0, 0)),
            pl.BlockSpec((tq, ATTN_WIDTH), lambda b, i: (i, 0)),
            pl.BlockSpec((tq, ATTN_WIDTH), lambda b, i: (i, 0)),
            pl.BlockSpec((t_new, KV_WIDTH), lambda b, i: (0, 0)),
            pl.BlockSpec((t_new, KV_WIDTH), lambda b, i: (0, 0)),
        ]
        args += [cache_k, cache_v, cos_t, sin_t, cos_t, sin_t]
    in_specs += [const((1, ATTN_WIDTH)), const((1, KV_WIDTH)),
                 const((ATTN_WIDTH, ATTN_WIDTH)), const((KV_WIDTH, KV_WIDTH))]
    args += [qg, kg, bdq, bdk]
    out_specs = [pl.BlockSpec((tq, ATTN_WIDTH), lambda b, i: (b * nq + i, 0))]
    out_shape = [jax.ShapeDtypeStruct((batch * t_new, ATTN_WIDTH), F32)]
    if not latent:
        out_specs += [pl.BlockSpec((t_new, KV_WIDTH), lambda b, i: (b, 0))] * 2
        out_shape += [jax.ShapeDtypeStruct((batch * t_new, KV_WIDTH), F32)] * 2
    return pl.pallas_call(
        functools.partial(_attn_kernel, tq=tq, t_new=t_new, t_past=t_past, latent=latent),
        grid=(batch, nq),
        in_specs=in_specs,
        out_specs=out_specs,
        out_shape=out_shape,
        scratch_shapes=[pltpu.VMEM((N_KV_HEADS, t_keys, HEAD_DIM), BF16),
                        pltpu.VMEM((N_KV_HEADS, t_keys, HEAD_DIM), BF16)],
        compiler_params=_cparams(2),
        name="attention_latent" if latent else "attention_context",
    )(*args)


def _running_max_lanes(x, reverse):
    n = x.shape[1]
    lane = lax.broadcasted_iota(jnp.int32, x.shape, 1)
    k = 1
    while k < n:
        if reverse:
            cand = jnp.where(lane < n - k, pltpu.roll(x, n - k, axis=1), NEG_BIG)
        else:
            cand = jnp.where(lane >= k, pltpu.roll(x, k, axis=1), NEG_BIG)
        x = jnp.maximum(x, cand)
        k *= 2
    return x


def _mlstm_kernel(*refs, t, chunk, has_state, emit_state):
    refs = list(refs)
    zm_ref, gt_ref, cw_ref, gb_ref, mg_ref, bd_ref = refs[:6]
    refs = refs[6:]
    if has_state:
        c0_ref, m0_ref = refs[:2]
        refs = refs[2:]
    o_ref = refs[0]
    refs = refs[1:]
    if emit_state:
        cs_ref, ms_ref = refs[:2]
        refs = refs[2:]
    hst_ref = refs[0]

    width = M_HEADS * M_DK
    z = zm_ref[...]
    x = z[:, :2 * width]
    row = lax.broadcasted_iota(jnp.int32, (t, 1), 0)
    x_prev = jnp.where(row == 0, 0.0, pltpu.roll(x, 1, axis=0))
    x_next = jnp.where(row == t - 1, 0.0, pltpu.roll(x, t - 1, axis=0))
    cw = cw_ref[...]
    qk = _silu(x_prev * cw[0:1, :] + x * cw[1:2, :] + x_next * cw[2:3, :])
    q = qk[:, :width]
    k = qk[:, width:] * (M_DK ** -0.5)
    v = z[:, 2 * width:2 * width + M_WIDTH]
    om = z[:, 2 * width + M_WIDTH:]
    heads = [slice(M_DK * h, M_DK * (h + 1)) for h in range(M_HEADS)]
    q_rows = [q[:, hs].astype(BF16) for hs in heads]
    k_rows = [k[:, hs].astype(BF16) for hs in heads]
    qt = q.T.astype(BF16)
    vt = v.T
    ones_r = jnp.ones((LANES - M_DV, t), F32)
    vaug_t = [jnp.concatenate([vt[hs, :], ones_r], axis=0) for hs in heads]
    vaug_tb = [a.astype(BF16) for a in vaug_t]

    gates = gt_ref[...] + gb_ref[...]
    lane = lax.broadcasted_iota(jnp.int32, (1, GATE_PAD), 1)
    is_forget = (lane & M_HEADS) != 0
    gates = jnp.where(is_forget, _log_sigmoid(gates), gates)
    gates_t = gates.T[:N_GATES, :]

    ri = lax.broadcasted_iota(jnp.int32, (chunk, chunk), 0)
    ci = lax.broadcasted_iota(jnp.int32, (chunk, chunk), 1)
    row_le_col = ri <= ci
    row_ge_col = ri >= ci
    le_b = row_le_col.astype(BF16)
    ge_b = row_ge_col.astype(BF16)
    eye_b = (ri == ci).astype(BF16)
    ones_b = jnp.ones((chunk, chunk), BF16)

    if has_state:
        m_state = [m0_ref[M_HEADS * d:M_HEADS * (d + 1), 0:1] for d in range(2)]
        c_state = [[c0_ref[M_HEADS * d + h] for h in range(M_HEADS)] for d in range(2)]
    else:
        m_state = [jnp.zeros((M_HEADS, 1), F32) for _ in range(2)]
        c_state = [[jnp.zeros((LANES, M_DK), F32) for _ in range(M_HEADS)] for _ in range(2)]

    n_chunks = t // chunk
    written = set()
    for j in range(n_chunks):
        for direction in range(2):
            fwd = direction == 0
            c = j if fwd else n_chunks - 1 - j
            rs = slice(chunk * c, chunk * (c + 1))
            last = chunk - 1 if fwd else 0
            valid = row_le_col if fwd else row_ge_col
            r0 = (2 * direction + 1) * M_HEADS
            gtc = gates_t[:, rs]
            i_al = pltpu.roll(gtc, M_HEADS, axis=0)
            cum = _mm_left_f32(gtc, le_b if fwd else ge_b)
            u = i_al - cum
            pm = _running_max_lanes(u, reverse=not fwd)
            b4, u4, pm4 = (a[r0:r0 + M_HEADS, :] for a in (cum, u, pm))
            bl = b4[:, last:last + 1]
            pml = pm4[:, last:last + 1]
            m = m_state[direction]
            mx = jnp.maximum(m, pm4)
            f_loc = jnp.exp(pm4 - mx)
            f_int = jnp.exp(m - mx)
            floor = jnp.exp(-(b4 + mx))
            wk = jnp.exp(u4 - pml)
            m_new = bl + jnp.maximum(m, pml)
            decay = jnp.exp(bl + m - m_new)
            gain = jnp.exp(bl + pml - m_new)
            m_state[direction] = m_new

            ucol, st = [], []
            for h in range(M_HEADS):
                ucol.append(sum(_mm(eye_b * part, ones_b) for part in _split3(u4[h:h + 1, :])))
                st.append(_mm_nt(k_rows[h][rs, :], q_rows[h][rs, :]))
            s_loc, x_loc = [], []
            for h in range(M_HEADS):
                w = jnp.exp(jnp.where(valid, ucol[h] - pm4[h:h + 1, :], NEG_BIG))
                s_loc.append(_mm(vaug_tb[h][:, rs], (st[h] * w).astype(BF16)))
                x_loc.append(_mm((vaug_t[h][:, rs] * wk[h:h + 1, :]).astype(BF16), k_rows[h][rs, :]))
            for h in range(M_HEADS):
                caug = c_state[direction][h]
                inter = _mm(caug.astype(BF16), qt[heads[h], rs])
                nd = f_loc[h:h + 1, :] * s_loc[h] + f_int[h:h + 1, :] * inter
                ht = nd[:M_DV, :] / jnp.maximum(jnp.abs(nd[M_DV:, :]), floor[h:h + 1, :])
                c_state[direction][h] = decay[h:h + 1, :] * caug + gain[h:h + 1, :] * x_loc[h]
                if c in written:
                    hst_ref[heads[h], rs] += ht
                else:
                    hst_ref[heads[h], rs] = ht
            written.add(c)

    if emit_state:
        for d in range(2):
            ms_ref[M_HEADS * d:M_HEADS * (d + 1), :] = jnp.broadcast_to(m_state[d], (M_HEADS, LANES))
            for h in range(M_HEADS):
                cs_ref[M_HEADS * d + h] = c_state[d][h]

    hsum = hst_ref[...].T
    hn = hsum * _group_inv_rms(hsum, bd_ref[...]) * mg_ref[...]
    o_ref[...] = _sigmoid(om) * hn


def _mlstm(zm, gt, conv_w, gate_b, m_norm_g, *, batch, t, caug0=None, m0=None):
    has_state = caug0 is not None
    emit_state = not has_state
    chunk = min(M_CHUNK_K, t)
    n_units = 2 * M_HEADS
    const = lambda shape: pl.BlockSpec(shape, lambda b: (0,) * len(shape))
    in_specs = [
        pl.BlockSpec((t, D_MODEL), lambda b: (b, 0)),
        pl.BlockSpec((t, GATE_PAD), lambda b: (b, 0)),
        const((3, 2 * M_HEADS * M_DK)),
        const((1, GATE_PAD)),
        const((1, M_WIDTH)),
        const((M_WIDTH, M_WIDTH)),
    ]
    args = [zm, gt, conv_w, gate_b, m_norm_g, _block_diag_ones(M_WIDTH)]
    if has_state:
        in_specs += [pl.BlockSpec((None, n_units, LANES, M_DK), lambda b: (b, 0, 0, 0)),
                     pl.BlockSpec((None, n_units, LANES), lambda b: (b, 0, 0))]
        args += [caug0, m0]
    out_specs = [pl.BlockSpec((t, M_WIDTH), lambda b: (b, 0))]
    out_shape = [jax.ShapeDtypeStruct((batch * t, M_WIDTH), F32)]
    if emit_state:
        out_specs += [pl.BlockSpec((None, n_units, LANES, M_DK), lambda b: (b, 0, 0, 0)),
                      pl.BlockSpec((None, n_units, LANES), lambda b: (b, 0, 0))]
        out_shape += [jax.ShapeDtypeStruct((batch, n_units, LANES, M_DK), F32),
                      jax.ShapeDtypeStruct((batch, n_units, LANES), F32)]
    return pl.pallas_call(
        functools.partial(_mlstm_kernel, t=t, chunk=chunk, has_state=has_state, emit_state=emit_state),
        grid=(batch,),
        in_specs=in_specs,
        out_specs=out_specs,
        out_shape=out_shape,
        scratch_shapes=[pltpu.VMEM((M_WIDTH, t), F32)],
        compiler_params=_cparams(1),
        name="mlstm_latent" if has_state else "mlstm_context",
    )(*args)


def _fourier_kernel(u_ref, cs_ref, ct_ref, st_ref, o_ref):
    a = _mm(u_ref[...].astype(BF16), cs_ref[...])
    ac = a[:, :F_WIDTH].astype(BF16)
    asn = a[:, F_WIDTH:].astype(BF16)
    o_ref[...] = _mm(ct_ref[...], ac) - _mm(st_ref[...], asn)


def _dft_tables(t):
    kt = (np.arange(t)[:, None] * np.arange(t)[None, :]) % t
    ang_t = 2.0 * np.pi * kt.astype(np.float64) / t
    ct = np.cos(ang_t) / np.sqrt(t)
    st = np.sin(ang_t) / np.sqrt(t)
    c = F_GROUP_CH
    kc = (np.arange(c)[:, None] * np.arange(c)[None, :]) % c
    ang_c = 2.0 * np.pi * kc.astype(np.float64) / c
    eye = np.eye(F_GROUPS)
    cc = np.kron(eye, np.cos(ang_c) / np.sqrt(c))
    sc = np.kron(eye, np.sin(ang_c) / np.sqrt(c))
    cs = np.concatenate([cc, sc], axis=1)
    to_dev = lambda a: jnp.asarray(a.astype(np.float32)).astype(BF16)
    return to_dev(cs), to_dev(ct), to_dev(st)


def _fourier(fu, *, batch, t):
    cs, ct, st = _dft_tables(t)
    const = lambda shape: pl.BlockSpec(shape, lambda b: (0,) * len(shape))
    return pl.pallas_call(
        _fourier_kernel,
        grid=(batch,),
        in_specs=[pl.BlockSpec((t, F_WIDTH), lambda b: (b, 0)),
                  const((F_WIDTH, 2 * F_WIDTH)), const((t, t)), const((t, t))],
        out_specs=pl.BlockSpec((t, F_WIDTH), lambda b: (b, 0)),
        out_shape=jax.ShapeDtypeStruct((batch * t, F_WIDTH), F32),
        compiler_params=_cparams(1),
        name="fourier",
    )(fu, cs, ct, st)


def _merge_kernel(x_ref, mod_ref, g_ref, a_ref, hm_ref, fo_ref,
                  wbg_ref, wpa_ref, wpm_ref, wpf_ref, wo_ref, o_ref):
    x = x_ref[...]
    h = _normmod(x, g_ref[...], mod_ref[1:2, :], mod_ref[0:1, :]).astype(BF16)

    def branch(j, y_ref, w_ref):
        gate = _sigmoid(_mm(h, wbg_ref[:, D_MODEL * j:D_MODEL * (j + 1)]))
        return gate * _mm(y_ref[...].astype(BF16), w_ref[...])

    merged = branch(0, a_ref, wpa_ref) + branch(1, hm_ref, wpm_ref) + branch(2, fo_ref, wpf_ref)
    o_ref[...] = x + mod_ref[2:3, :] * _mm(merged.astype(BF16), wo_ref[...])


def _merge(x, mod, g, attn, hm, fo, w_bg, w_pa, w_pm, w_pf, w_out, sample_of_tile, tm):
    rows = x.shape[0]
    row_spec = lambda width: pl.BlockSpec((tm, width), lambda i: (i, 0))
    const = lambda shape: pl.BlockSpec(shape, lambda i: (0,) * len(shape))
    return pl.pallas_call(
        _merge_kernel,
        grid=(rows // tm,),
        in_specs=[
            row_spec(D_MODEL),
            pl.BlockSpec((None, N_MOD, D_MODEL), lambda i: (sample_of_tile(i), 0, 0)),
            const((1, D_MODEL)),
            row_spec(ATTN_WIDTH), row_spec(M_WIDTH), row_spec(F_WIDTH),
            const((D_MODEL, 3 * D_MODEL)), const((ATTN_WIDTH, D_MODEL)),
            const((M_WIDTH, D_MODEL)), const((F_WIDTH, D_MODEL)), const((D_MODEL, D_MODEL)),
        ],
        out_specs=row_spec(D_MODEL),
        out_shape=jax.ShapeDtypeStruct((rows, D_MODEL), F32),
        compiler_params=_cparams(1),
        name="merge",
    )(x, mod, g, attn, hm, fo, w_bg, w_pa, w_pm, w_pf, w_out)


def _ffn_kernel(x_ref, mod_ref, g_ref, win_ref, wout_ref, o_ref):
    x = x_ref[...]
    h = _normmod(x, g_ref[...], mod_ref[4:5, :], mod_ref[3:4, :]).astype(BF16)
    u = _mm(h, win_ref[...])
    a = (_silu(u[:, :FF_HIDDEN]) * u[:, FF_HIDDEN:]).astype(BF16)
    o_ref[...] = x + mod_ref[5:6, :] * _mm(a, wout_ref[...])


def _ffn(x, mod, g, w_in, w_out, sample_of_tile, tm):
    rows = x.shape[0]
    row_spec = pl.BlockSpec((tm, D_MODEL), lambda i: (i, 0))
    const = lambda shape: pl.BlockSpec(shape, lambda i: (0,) * len(shape))
    return pl.pallas_call(
        _ffn_kernel,
        grid=(rows // tm,),
        in_specs=[
            row_spec,
            pl.BlockSpec((None, N_MOD, D_MODEL), lambda i: (sample_of_tile(i), 0, 0)),
            const((1, D_MODEL)),
            const((D_MODEL, 2 * FF_HIDDEN)), const((FF_HIDDEN, D_MODEL)),
        ],
        out_specs=row_spec,
        out_shape=jax.ShapeDtypeStruct((rows, D_MODEL), F32),
        compiler_params=_cparams(1),
        name="ffn",
    )(x, mod, g, w_in, w_out)


def _rope_tables(t):
    n = HEAD_DIM // 4
    inv = 1.0 / (ROPE_THETA ** (np.arange(n, dtype=np.float64) / n))
    pos = np.arange(t)
    ang_r = (pos // GRID_W)[:, None] * inv[None, :]
    ang_c = (pos % GRID_W)[:, None] * inv[None, :]
    cos = np.concatenate([np.cos(ang_r)] * 2 + [np.cos(ang_c)] * 2, axis=1)
    sin = np.concatenate([-np.sin(ang_r), np.sin(ang_r), -np.sin(ang_c), np.sin(ang_c)], axis=1)
    tile = lambda a: jnp.asarray(np.tile(a, (1, N_Q_HEADS)).astype(np.float32))
    return tile(cos), tile(sin)


def _layer(x, mod, sample_of_tile, wts, *, batch, t, latent, layer, ctx=None, rope_tabs=None):
    (norm1_g, w_mix, w_bg, qg, kg, conv_w, gate_b, m_norm_g, w_pa, w_pm, w_pf, w_out,
     norm2_g, w_ffn_in, w_ffn_out) = wts
    tm = 512
    za, zm, fu, gt = _in_proj(x, mod, norm1_g, w_mix, sample_of_tile(tm), tm)
    extra = ()
    if latent:
        cache_k, cache_v, caug0, m0 = ctx
        (attn,) = _attention(za, qg, kg, batch=batch, t_new=t, latent=True, cache_k=cache_k,
                             cache_v=cache_v, layer=layer, rope_tabs=rope_tabs)
        (hm,) = _mlstm(zm, gt, conv_w, gate_b, m_norm_g, batch=batch, t=t, caug0=caug0, m0=m0)
    else:
        attn, new_k, new_v = _attention(za, qg, kg, batch=batch, t_new=t, latent=False)
        hm, cs, ms = _mlstm(zm, gt, conv_w, gate_b, m_norm_g, batch=batch, t=t)
        extra = (new_k, new_v, cs, ms)
    fo = _fourier(fu, batch=batch, t=t)
    x = _merge(x, mod, norm1_g, attn, hm, fo, w_bg, w_pa, w_pm, w_pf, w_out, sample_of_tile(tm), tm)
    tm_ffn = 256
    x = _ffn(x, mod, norm2_g, w_ffn_in, w_ffn_out, sample_of_tile(tm_ffn), tm_ffn)
    return x, extra


def kernel(x_prompt, x_sample, cache_k, cache_v, state_C, state_n, state_m, c, c_ctx, w_ada, b_ada,
           norm1_g, w_in, q_norm_g, k_norm_g, m_conv_w, m_gate_b, m_norm_g, w_proj_attn,
           w_proj_mlstm, w_proj_fourier, w_out, norm2_g, w_ffn_in, w_ffn_out):
    n_ctx, t_ctx, _ = x_prompt.shape
    n_lat, t_lat, _ = x_sample.shape
    t_past = cache_k.shape[2]
    n_units = 2 * M_HEADS

    cvec = jnp.concatenate([c_ctx[None, :], c], axis=0)
    cvec = jnp.pad(cvec, ((0, MOD_ROWS - cvec.shape[0]), (0, 0)))
    mod = _modulation(cvec, w_ada, b_ada).reshape(DEPTH, MOD_ROWS, N_MOD, D_MODEL)

    rope_tabs = _rope_tables(t_lat)
    ck = cache_k.reshape(n_lat, DEPTH, t_past, KV_WIDTH)
    cv = cache_v.reshape(n_lat, DEPTH, t_past, KV_WIDTH)

    ctx_sample = lambda tm: (lambda i: 0)
    lat_sample = lambda tm: (lambda i: 1 + i // (t_lat // tm))

    xp = x_prompt.reshape(n_ctx * t_ctx, D_MODEL)
    xs = x_sample.reshape(n_lat * t_lat, D_MODEL)
    ks, vs, cs_all, ms_all = [], [], [], []
    o = IN_OFFS
    for l in range(DEPTH):
        w = w_in[l]
        w_mix = jnp.concatenate(
            [w[:, o[0]:o[7]], w[:, o[8]:o[9]], w[:, o[7]:o[8]],
             jnp.zeros((D_MODEL, GATE_PAD - N_GATES), w.dtype)], axis=1).astype(BF16)
        w_bg = w[:, o[9]:o[10]].astype(BF16)
        tile_row = lambda g, reps: jnp.tile(g, reps)[None, :]
        wts = (norm1_g[l][None, :], w_mix, w_bg,
               tile_row(q_norm_g[l], N_Q_HEADS), tile_row(k_norm_g[l], N_KV_HEADS),
               m_conv_w[l], jnp.pad(m_gate_b[l], (0, GATE_PAD - N_GATES))[None, :],
               m_norm_g[l][None, :],
               w_proj_attn[l].astype(BF16), w_proj_mlstm[l].astype(BF16),
               w_proj_fourier[l].astype(BF16), w_out[l].astype(BF16),
               norm2_g[l][None, :], w_ffn_in[l].astype(BF16), w_ffn_out[l].astype(BF16))

        xp, (k_l, v_l, cs, ms) = _layer(xp, mod[l], ctx_sample, wts, batch=n_ctx, t=t_ctx,
                                        latent=False, layer=l)
        ks.append(k_l.reshape(n_ctx, t_ctx, N_KV_HEADS, HEAD_DIM))
        vs.append(v_l.reshape(n_ctx, t_ctx, N_KV_HEADS, HEAD_DIM))
        cs_all.append(cs)
        ms_all.append(ms)

        c0t = jnp.swapaxes(state_C[:, l].astype(F32).reshape(n_lat, n_units, M_DK, M_DV), -1, -2)
        n0 = state_n[:, l].astype(F32).reshape(n_lat, n_units, 1, M_DK)
        caug0 = jnp.concatenate(
            [c0t, jnp.broadcast_to(n0, (n_lat, n_units, LANES - M_DV, M_DK))], axis=-2)
        m0 = jnp.broadcast_to(state_m[:, l].astype(F32).reshape(n_lat, n_units, 1),
                              (n_lat, n_units, LANES))
        xs, _ = _layer(xs, mod[l], lat_sample, wts, batch=n_lat, t=t_lat, latent=True, layer=l,
                       ctx=(ck, cv, caug0, m0), rope_tabs=rope_tabs)

    cs_all = jnp.stack(cs_all, axis=1)
    ms_all = jnp.stack(ms_all, axis=1)
    new_c = jnp.swapaxes(cs_all[..., :M_DV, :], -1, -2).reshape(n_ctx, DEPTH, 2, M_HEADS, M_DK, M_DV)
    new_n = cs_all[..., M_DV, :].reshape(n_ctx, DEPTH, 2, M_HEADS, M_DK)
    new_m = ms_all[..., 0].reshape(n_ctx, DEPTH, 2, M_HEADS)
    return (xp.reshape(n_ctx, t_ctx, D_MODEL), xs.reshape(n_lat, t_lat, D_MODEL),
            jnp.stack(ks, axis=1), jnp.stack(vs, axis=1), new_c, new_n, new_m)
```

```python
import functools

import numpy as np
import jax
import jax.numpy as jnp
from jax import lax
from jax.experimental import pallas as pl
from jax.experimental.pallas import tpu as pltpu

D_MODEL = 1024
DEPTH = 2
GRID_W = 64
HEAD_DIM = 64
N_Q_HEADS = 8
N_KV_HEADS = 4
ATTN_WIDTH = N_Q_HEADS * HEAD_DIM
KV_WIDTH = N_KV_HEADS * HEAD_DIM
ROPE_THETA = 10000.0
M_HEADS = 4
M_DK = 64
M_DV = 64
M_WIDTH = M_HEADS * M_DV
F_GROUPS = 4
F_GROUP_CH = 64
F_WIDTH = F_GROUPS * F_GROUP_CH
FF_HIDDEN = -(-8 * D_MODEL // (3 * 256)) * 256
EPS = 1e-6
N_GATES = 4 * M_HEADS
IN_SIZES = (ATTN_WIDTH, KV_WIDTH, KV_WIDTH, M_HEADS * M_DK, M_HEADS * M_DK, M_WIDTH, M_WIDTH,
            N_GATES, F_WIDTH, 3 * D_MODEL)
IN_OFFS = tuple(int(v) for v in np.cumsum((0,) + IN_SIZES))

LANES = 128
GATE_PAD = LANES
N_MOD = 6
MOD_ROWS = 16
M_CHUNK_K = 128
NEG_BIG = -1e30
LOG2_E = 1.4426950408889634
VMEM_LIMIT = 56 * 1024 * 1024

F32 = jnp.float32
BF16 = jnp.bfloat16


def _cparams(n_axes, flags=None):
    return pltpu.CompilerParams(dimension_semantics=("arbitrary",) * n_axes,
                                vmem_limit_bytes=VMEM_LIMIT, flags=flags)


def _layer_spec(shape, layer):
    return pl.BlockSpec((None,) + tuple(shape), lambda *_: (layer,) + (0,) * len(shape),
                        pipeline_mode=pl.Buffered(1))


def _mm(a, b):
    return jnp.dot(a, b, preferred_element_type=F32)


def _mm_nt(a, b):
    return lax.dot_general(a, b, (((1,), (1,)), ((), ())), preferred_element_type=F32)


def _split3(x):
    hi = x.astype(BF16)
    r = x - hi.astype(F32)
    mid = r.astype(BF16)
    lo = (r - mid.astype(F32)).astype(BF16)
    return hi, mid, lo


def _mm_left_f32(x, m_bf16):
    hi, mid, lo = _split3(x)
    return _mm(hi, m_bf16) + _mm(mid, m_bf16) + _mm(lo, m_bf16)


def _mm_right_f32(m_bf16, x):
    hi, mid, lo = _split3(x)
    return _mm(m_bf16, hi) + _mm(m_bf16, mid) + _mm(m_bf16, lo)


def _sigmoid(x):
    return 1.0 / (1.0 + jnp.exp(-x))


def _silu(x):
    return x * _sigmoid(x)


def _log_sigmoid(x):
    return jnp.minimum(x, 0.0) - jnp.log(1.0 + jnp.exp(-jnp.abs(x)))


def _normmod(x, g, scale, shift):
    ms = jnp.mean(x * x, axis=-1, keepdims=True)
    return (x * lax.rsqrt(ms + EPS)) * g * (1.0 + scale) + shift


def _group_inv_rms(x, bd):
    x2 = x * x
    hi = x2.astype(BF16)
    lo = (x2 - hi.astype(F32)).astype(BF16)
    ss = _mm(hi, bd) + _mm(lo, bd)
    return lax.rsqrt(ss * (1.0 / HEAD_DIM) + EPS)


def _rope(x, cos, sin_signed):
    w = x.shape[1]
    lane = lax.broadcasted_iota(jnp.int32, x.shape, 1)
    up = pltpu.roll(x, w - 16, axis=1)
    dn = pltpu.roll(x, 16, axis=1)
    partner = jnp.where((lane & 31) < 16, up, dn)
    return x * cos + partner * sin_signed


def _mod_kernel(c_ref, w_ref, b_ref, o_ref):
    s = _silu(c_ref[...]).astype(BF16)
    o_ref[...] = _mm(s, w_ref[...].astype(BF16)) + b_ref[...]


def _modulation(cvec, w_ada, b_ada):
    tn = 512
    n_out = N_MOD * D_MODEL
    return pl.pallas_call(
        _mod_kernel,
        grid=(DEPTH, n_out // tn),
        in_specs=[
            pl.BlockSpec((MOD_ROWS, D_MODEL), lambda l, j: (0, 0)),
            pl.BlockSpec((None, D_MODEL, tn), lambda l, j: (l, 0, j)),
            pl.BlockSpec((None, 1, tn), lambda l, j: (l, 0, j)),
        ],
        out_specs=pl.BlockSpec((None, MOD_ROWS, tn), lambda l, j: (l, 0, j)),
        out_shape=jax.ShapeDtypeStruct((DEPTH, MOD_ROWS, n_out), F32),
        compiler_params=_cparams(2),
        name="modulation",
    )(cvec, w_ada, b_ada.reshape(DEPTH, 1, n_out))


MIX_COLS = 2 * D_MODEL + F_WIDTH + GATE_PAD


def _in_kernel(x_ref, mod_ref, g_ref, w_ref, za_ref, zm_ref, fu_ref, gt_ref):
    h = _normmod(x_ref[...], g_ref[...], mod_ref[1:2, :], mod_ref[0:1, :])
    z = _mm(h.astype(BF16), w_ref[...])
    za_ref[...] = z[:, :D_MODEL]
    zm_ref[...] = z[:, D_MODEL:2 * D_MODEL]
    fu_ref[...] = z[:, 2 * D_MODEL:2 * D_MODEL + F_WIDTH]
    gt_ref[...] = z[:, 2 * D_MODEL + F_WIDTH:]


def _in_proj(x, mod, g, w_mix, sample_of_tile, tm, layer):
    rows = x.shape[0]
    row_spec = lambda width: pl.BlockSpec((tm, width), lambda i: (i, 0))
    return pl.pallas_call(
        _in_kernel,
        grid=(rows // tm,),
        in_specs=[
            row_spec(D_MODEL),
            pl.BlockSpec((None, None, N_MOD, D_MODEL), lambda i: (layer, sample_of_tile(i), 0, 0)),
            _layer_spec((1, D_MODEL), layer),
            _layer_spec((D_MODEL, MIX_COLS), layer),
        ],
        out_specs=[row_spec(D_MODEL), row_spec(D_MODEL), row_spec(F_WIDTH), row_spec(GATE_PAD)],
        out_shape=[
            jax.ShapeDtypeStruct((rows, D_MODEL), F32),
            jax.ShapeDtypeStruct((rows, D_MODEL), F32),
            jax.ShapeDtypeStruct((rows, F_WIDTH), F32),
            jax.ShapeDtypeStruct((rows, GATE_PAD), F32),
        ],
        compiler_params=_cparams(1),
        name="in_proj",
    )(x, mod, g, w_mix)


def _attn_kernel(*refs, tq, t_new, t_past, latent):
    if latent:
        (q_ref, k_ref, v_ref, ck_ref, cv_ref, cosq_ref, sinq_ref, cosk_ref, sink_ref,
         qg_ref, kg_ref, bdq_ref, bdk_ref, o_ref, ks_ref, vs_ref) = refs
    else:
        (q_ref, k_ref, v_ref, qg_ref, kg_ref, bdq_ref, bdk_ref,
         o_ref, nk_ref, nv_ref, ks_ref, vs_ref) = refs

    @pl.when(pl.program_id(1) == 0)
    def _():
        k = k_ref[...]
        kn = k * _group_inv_rms(k, bdk_ref[...]) * kg_ref[...]
        v = v_ref[...]
        if latent:
            kn = _rope(kn, cosk_ref[...], sink_ref[...])
            ck = ck_ref[...]
            cv = cv_ref[...]
        else:
            nk_ref[...] = kn
            nv_ref[...] = v
        for h in range(N_KV_HEADS):
            sl = slice(HEAD_DIM * h, HEAD_DIM * (h + 1))
            if latent:
                ks_ref[h, :t_past, :] = ck[:, sl].astype(BF16)
                vs_ref[h, :t_past, :HEAD_DIM] = cv[:, sl].astype(BF16)
            ks_ref[h, t_past:, :] = kn[:, sl].astype(BF16)
            vs_ref[h, t_past:, :HEAD_DIM] = v[:, sl].astype(BF16)
            vs_ref[h, :, HEAD_DIM:] = jnp.ones((t_past + t_new, LANES - HEAD_DIM), BF16)

    q = q_ref[...]
    qn = q * _group_inv_rms(q, bdq_ref[...]) * qg_ref[...]
    if latent:
        qn = _rope(qn, cosq_ref[...], sinq_ref[...])
    qn = qn * (HEAD_DIM ** -0.5 * LOG2_E)
    group = N_Q_HEADS // N_KV_HEADS
    outs = []
    for h in range(N_KV_HEADS):
        qh = jnp.concatenate(
            [qn[:, HEAD_DIM * (group * h + g):HEAD_DIM * (group * h + g + 1)] for g in range(group)],
            axis=0).astype(BF16)
        s = _mm_nt(qh, ks_ref[h])
        m = jnp.max(s, axis=-1, keepdims=True)
        p = jnp.exp2(s - m).astype(BF16)
        od = _mm(p, vs_ref[h])
        o = (od / pltpu.roll(od, LANES - HEAD_DIM, axis=1))[:, :HEAD_DIM]
        outs.extend(o[tq * g:tq * (g + 1)] for g in range(group))
    o_ref[...] = jnp.concatenate(outs, axis=1)


def _block_diag_ones(width):
    idx = np.arange(width) // HEAD_DIM
    return jnp.asarray(idx[:, None] == idx[None, :], dtype=BF16)


def _attention(za, qg, kg, *, batch, t_new, latent, cache_k=None, cache_v=None, layer=0, rope_tabs=None):
    tq = 256
    nq = t_new // tq
    t_past = cache_k.shape[2] if latent else 0
    t_keys = t_past + t_new
    kcol = ATTN_WIDTH // KV_WIDTH
    bdq = _block_diag_ones(ATTN_WIDTH)
    bdk = _block_diag_ones(KV_WIDTH)
    const = lambda shape: pl.BlockSpec(shape, lambda b, i: (0,) * len(shape))
    in_specs = [
        pl.BlockSpec((tq, ATTN_WIDTH), lambda b, i: (b * nq + i, 0)),
        pl.BlockSpec((t_new, KV_WIDTH), lambda b, i: (b, kcol)),
        pl.BlockSpec((t_new, KV_WIDTH), lambda b, i: (b, kcol + 1)),
    ]
    args = [za, za, za]
    if latent:
        cos_t, sin_t = rope_tabs
        in_specs += [
            pl.BlockSpec((None, None, t_past, KV_WIDTH), lambda b, i: (b, layer, 0, 0)),
            pl.BlockSpec((None, None, t_past, KV_WIDTH), lambda b, i: (b, layer, 0, 0)),
            pl.BlockSpec((tq, ATTN_WIDTH), lambda b, i: (i, 0)),
            pl.BlockSpec((tq, ATTN_WIDTH), lambda b, i: (i, 0)),
            pl.BlockSpec((t_new, KV_WIDTH), lambda b, i: (0, 0)),
            pl.BlockSpec((t_new, KV_WIDTH), lambda b, i: (0, 0)),
        ]
        args += [cache_k, cache_v, cos_t, sin_t, cos_t, sin_t]
    in_specs += [_layer_spec((1, ATTN_WIDTH), layer), _layer_spec((1, KV_WIDTH), layer),
                 const((ATTN_WIDTH, ATTN_WIDTH)), const((KV_WIDTH, KV_WIDTH))]
    args += [qg, kg, bdq, bdk]
    out_specs = [pl.BlockSpec((tq, ATTN_WIDTH), lambda b, i: (b * nq + i, 0))]
    out_shape = [jax.ShapeDtypeStruct((batch * t_new, ATTN_WIDTH), F32)]
    if not latent:
        out_specs += [pl.BlockSpec((t_new, KV_WIDTH), lambda b, i: (b, 0))] * 2
        out_shape += [jax.ShapeDtypeStruct((batch * t_new, KV_WIDTH), F32)] * 2
    return pl.pallas_call(
        functools.partial(_attn_kernel, tq=tq, t_new=t_new, t_past=t_past, latent=latent),
        grid=(batch, nq),
        in_specs=in_specs,
        out_specs=out_specs,
        out_shape=out_shape,
        scratch_shapes=[pltpu.VMEM((N_KV_HEADS, t_keys, HEAD_DIM), BF16),
                        pltpu.VMEM((N_KV_HEADS, t_keys, LANES), BF16)],
        compiler_params=_cparams(2),
        name="attention_latent" if latent else "attention_context",
    )(*args)


def _running_max_lanes(x, reverse):
    n = x.shape[1]
    lane = lax.broadcasted_iota(jnp.int32, x.shape, 1)
    k = 1
    while k < n:
        if reverse:
            cand = jnp.where(lane < n - k, pltpu.roll(x, n - k, axis=1), NEG_BIG)
        else:
            cand = jnp.where(lane >= k, pltpu.roll(x, k, axis=1), NEG_BIG)
        x = jnp.maximum(x, cand)
        k *= 2
    return x


def _mlstm_kernel(*refs, t, chunk, has_state, emit_state):
    refs = list(refs)
    zm_ref, gt_ref, cw_ref, gb_ref, mg_ref, bd_ref = refs[:6]
    refs = refs[6:]
    if has_state:
        c0_ref, m0_ref = refs[:2]
        refs = refs[2:]
    o_ref = refs[0]
    refs = refs[1:]
    if emit_state:
        cs_ref, ms_ref = refs[:2]
        refs = refs[2:]
    hst_ref = refs[0]

    width = M_HEADS * M_DK
    z = zm_ref[...]
    x = z[:, :2 * width]
    row = lax.broadcasted_iota(jnp.int32, (t, 1), 0)
    x_prev = jnp.where(row == 0, 0.0, pltpu.roll(x, 1, axis=0))
    x_next = jnp.where(row == t - 1, 0.0, pltpu.roll(x, t - 1, axis=0))
    cw = cw_ref[...]
    qk = _silu(x_prev * cw[0:1, :] + x * cw[1:2, :] + x_next * cw[2:3, :])
    q = qk[:, :width]
    k = qk[:, width:] * (M_DK ** -0.5)
    v = z[:, 2 * width:2 * width + M_WIDTH]
    om = z[:, 2 * width + M_WIDTH:]
    heads = [slice(M_DK * h, M_DK * (h + 1)) for h in range(M_HEADS)]
    q_rows = [q[:, hs].astype(BF16) for hs in heads]
    k_rows = [k[:, hs].astype(BF16) for hs in heads]
    qt = q.T.astype(BF16)
    vt = v.T
    ones_r = jnp.ones((LANES - M_DV, t), F32)
    vaug_t = [jnp.concatenate([vt[hs, :], ones_r], axis=0) for hs in heads]
    vaug_tb = [a.astype(BF16) for a in vaug_t]

    gates = gt_ref[...] + gb_ref[...]
    lane = lax.broadcasted_iota(jnp.int32, (1, GATE_PAD), 1)
    is_forget = (lane & M_HEADS) != 0
    gates = jnp.where(is_forget, _log_sigmoid(gates), gates)
    gates_t = gates.T[:N_GATES, :]

    ri = lax.broadcasted_iota(jnp.int32, (chunk, chunk), 0)
    ci = lax.broadcasted_iota(jnp.int32, (chunk, chunk), 1)
    row_le_col = ri <= ci
    row_ge_col = ri >= ci
    le_b = row_le_col.astype(BF16)
    ge_b = row_ge_col.astype(BF16)
    eye_b = (ri == ci).astype(BF16)
    ones_b = jnp.ones((chunk, chunk), BF16)

    if has_state:
        m_state = [m0_ref[M_HEADS * d:M_HEADS * (d + 1), 0:1] for d in range(2)]
        c_state = [[c0_ref[M_HEADS * d + h] for h in range(M_HEADS)] for d in range(2)]
    else:
        m_state = [jnp.zeros((M_HEADS, 1), F32) for _ in range(2)]
        c_state = [[jnp.zeros((LANES, M_DK), F32) for _ in range(M_HEADS)] for _ in range(2)]

    n_chunks = t // chunk
    written = set()
    for j in range(n_chunks):
        for direction in range(2):
            fwd = direction == 0
            c = j if fwd else n_chunks - 1 - j
            rs = slice(chunk * c, chunk * (c + 1))
            last = chunk - 1 if fwd else 0
            valid = row_le_col if fwd else row_ge_col
            r0 = (2 * direction + 1) * M_HEADS
            gtc = gates_t[:, rs]
            i_al = pltpu.roll(gtc, M_HEADS, axis=0)
            cum = _mm_left_f32(gtc, le_b if fwd else ge_b)
            u = i_al - cum
            pm = _running_max_lanes(u, reverse=not fwd)
            b4, u4, pm4 = (a[r0:r0 + M_HEADS, :] for a in (cum, u, pm))
            bl = b4[:, last:last + 1]
            pml = pm4[:, last:last + 1]
            m = m_state[direction]
            mx = jnp.maximum(m, pm4)
            f_loc = jnp.exp(pm4 - mx)
            f_int = jnp.exp(m - mx)
            floor = jnp.exp(-(b4 + mx))
            wk = jnp.exp(u4 - pml)
            m_new = bl + jnp.maximum(m, pml)
            decay = jnp.exp(bl + m - m_new)
            gain = jnp.exp(bl + pml - m_new)
            m_state[direction] = m_new

            ucol, st = [], []
            for h in range(M_HEADS):
                ucol.append(sum(_mm(eye_b * part, ones_b) for part in _split3(u4[h:h + 1, :])))
                st.append(_mm_nt(k_rows[h][rs, :], q_rows[h][rs, :]))
            s_loc, x_loc = [], []
            for h in range(M_HEADS):
                w = jnp.exp(jnp.where(valid, ucol[h] - pm4[h:h + 1, :], NEG_BIG))
                s_loc.append(_mm(vaug_tb[h][:, rs], (st[h] * w).astype(BF16)))
                x_loc.append(_mm((vaug_t[h][:, rs] * wk[h:h + 1, :]).astype(BF16), k_rows[h][rs, :]))
            for h in range(M_HEADS):
                caug = c_state[direction][h]
                inter = _mm(caug.astype(BF16), qt[heads[h], rs])
                nd = f_loc[h:h + 1, :] * s_loc[h] + f_int[h:h + 1, :] * inter
                ht = nd[:M_DV, :] / jnp.maximum(jnp.abs(nd[M_DV:, :]), floor[h:h + 1, :])
                c_state[direction][h] = decay[h:h + 1, :] * caug + gain[h:h + 1, :] * x_loc[h]
                if c in written:
                    hst_ref[heads[h], rs] += ht
                else:
                    hst_ref[heads[h], rs] = ht
            written.add(c)

    if emit_state:
        for d in range(2):
            ms_ref[M_HEADS * d:M_HEADS * (d + 1), :] = jnp.broadcast_to(m_state[d], (M_HEADS, LANES))
            for h in range(M_HEADS):
                cs_ref[M_HEADS * d + h] = c_state[d][h]

    hsum = hst_ref[...].T
    hn = hsum * _group_inv_rms(hsum, bd_ref[...]) * mg_ref[...]
    o_ref[...] = _sigmoid(om) * hn


def _mlstm(zm, gt, conv_w, gate_b, m_norm_g, *, batch, t, layer, caug0=None, m0=None):
    has_state = caug0 is not None
    emit_state = not has_state
    chunk = min(M_CHUNK_K, t)
    n_units = 2 * M_HEADS
    const = lambda shape: pl.BlockSpec(shape, lambda b: (0,) * len(shape))
    in_specs = [
        pl.BlockSpec((t, D_MODEL), lambda b: (b, 0)),
        pl.BlockSpec((t, GATE_PAD), lambda b: (b, 0)),
        _layer_spec((3, 2 * M_HEADS * M_DK), layer),
        _layer_spec((1, GATE_PAD), layer),
        _layer_spec((1, M_WIDTH), layer),
        const((M_WIDTH, M_WIDTH)),
    ]
    args = [zm, gt, conv_w, gate_b, m_norm_g, _block_diag_ones(M_WIDTH)]
    if has_state:
        in_specs += [pl.BlockSpec((None, n_units, LANES, M_DK), lambda b: (b, 0, 0, 0)),
                     pl.BlockSpec((None, n_units, LANES), lambda b: (b, 0, 0))]
        args += [caug0, m0]
    out_specs = [pl.BlockSpec((t, M_WIDTH), lambda b: (b, 0))]
    out_shape = [jax.ShapeDtypeStruct((batch * t, M_WIDTH), F32)]
    if emit_state:
        out_specs += [pl.BlockSpec((None, n_units, LANES, M_DK), lambda b: (b, 0, 0, 0)),
                      pl.BlockSpec((None, n_units, LANES), lambda b: (b, 0, 0))]
        out_shape += [jax.ShapeDtypeStruct((batch, n_units, LANES, M_DK), F32),
                      jax.ShapeDtypeStruct((batch, n_units, LANES), F32)]
    return pl.pallas_call(
        functools.partial(_mlstm_kernel, t=t, chunk=chunk, has_state=has_state, emit_state=emit_state),
        grid=(batch,),
        in_specs=in_specs,
        out_specs=out_specs,
        out_shape=out_shape,
        scratch_shapes=[pltpu.VMEM((M_WIDTH, t), F32)],
        compiler_params=_cparams(1),
        name="mlstm_latent" if has_state else "mlstm_context",
    )(*args)


def _fourier_kernel(u_ref, cs_ref, ct_ref, st_ref, o_ref):
    a = _mm(u_ref[...].astype(BF16), cs_ref[...])
    ac = a[:, :F_WIDTH].astype(BF16)
    asn = a[:, F_WIDTH:].astype(BF16)
    o_ref[...] = _mm(ct_ref[...], ac) - _mm(st_ref[...], asn)


def _dft_tables(t):
    kt = (np.arange(t)[:, None] * np.arange(t)[None, :]) % t
    ang_t = 2.0 * np.pi * kt.astype(np.float64) / t
    ct = np.cos(ang_t) / np.sqrt(t)
    st = np.sin(ang_t) / np.sqrt(t)
    c = F_GROUP_CH
    kc = (np.arange(c)[:, None] * np.arange(c)[None, :]) % c
    ang_c = 2.0 * np.pi * kc.astype(np.float64) / c
    eye = np.eye(F_GROUPS)
    cc = np.kron(eye, np.cos(ang_c) / np.sqrt(c))
    sc = np.kron(eye, np.sin(ang_c) / np.sqrt(c))
    cs = np.concatenate([cc, sc], axis=1)
    to_dev = lambda a: jnp.asarray(a.astype(np.float32)).astype(BF16)
    return to_dev(cs), to_dev(ct), to_dev(st)


def _fourier(fu, *, batch, t):
    cs, ct, st = _dft_tables(t)
    const = lambda shape: pl.BlockSpec(shape, lambda b: (0,) * len(shape))
    return pl.pallas_call(
        _fourier_kernel,
        grid=(batch,),
        in_specs=[pl.BlockSpec((t, F_WIDTH), lambda b: (b, 0)),
                  const((F_WIDTH, 2 * F_WIDTH)), const((t, t)), const((t, t))],
        out_specs=pl.BlockSpec((t, F_WIDTH), lambda b: (b, 0)),
        out_shape=jax.ShapeDtypeStruct((batch * t, F_WIDTH), F32),
        compiler_params=_cparams(1),
        name="fourier",
    )(fu, cs, ct, st)


def _merge_kernel(x_ref, mod_ref, g_ref, a_ref, hm_ref, fo_ref,
                  wbg_ref, wpa_ref, wpm_ref, wpf_ref, wo_ref, o_ref):
    x = x_ref[...]
    h = _normmod(x, g_ref[...], mod_ref[1:2, :], mod_ref[0:1, :]).astype(BF16)

    def branch(j, y_ref, w_ref):
        gate = _sigmoid(_mm(h, wbg_ref[:, D_MODEL * j:D_MODEL * (j + 1)]))
        return gate * _mm(y_ref[...].astype(BF16), w_ref[...])

    merged = branch(0, a_ref, wpa_ref) + branch(1, hm_ref, wpm_ref) + branch(2, fo_ref, wpf_ref)
    o_ref[...] = x + mod_ref[2:3, :] * _mm(merged.astype(BF16), wo_ref[...])


def _merge(x, mod, g, attn, hm, fo, w_bg, w_pa, w_pm, w_pf, w_out, sample_of_tile, tm, layer):
    rows = x.shape[0]
    row_spec = lambda width: pl.BlockSpec((tm, width), lambda i: (i, 0))
    const = lambda shape: _layer_spec(shape, layer)
    return pl.pallas_call(
        _merge_kernel,
        grid=(rows // tm,),
        in_specs=[
            row_spec(D_MODEL),
            pl.BlockSpec((None, None, N_MOD, D_MODEL), lambda i: (layer, sample_of_tile(i), 0, 0)),
            const((1, D_MODEL)),
            row_spec(ATTN_WIDTH), row_spec(M_WIDTH), row_spec(F_WIDTH),
            const((D_MODEL, 3 * D_MODEL)), const((ATTN_WIDTH, D_MODEL)),
            const((M_WIDTH, D_MODEL)), const((F_WIDTH, D_MODEL)), const((D_MODEL, D_MODEL)),
        ],
        out_specs=row_spec(D_MODEL),
        out_shape=jax.ShapeDtypeStruct((rows, D_MODEL), F32),
        compiler_params=_cparams(1),
        name="merge",
    )(x, mod, g, attn, hm, fo, w_bg, w_pa, w_pm, w_pf, w_out)


def _ffn_kernel(x_ref, mod_ref, g_ref, win_ref, wout_ref, o_ref):
    x = x_ref[...]
    h = _normmod(x, g_ref[...], mod_ref[4:5, :], mod_ref[3:4, :]).astype(BF16)
    u = _mm(h, win_ref[...])
    a = (_silu(u[:, :FF_HIDDEN]) * u[:, FF_HIDDEN:]).astype(BF16)
    o_ref[...] = x + mod_ref[5:6, :] * _mm(a, wout_ref[...])


def _ffn(x, mod, g, w_in, w_out, sample_of_tile, tm, layer):
    rows = x.shape[0]
    row_spec = pl.BlockSpec((tm, D_MODEL), lambda i: (i, 0))
    const = lambda shape: _layer_spec(shape, layer)
    return pl.pallas_call(
        _ffn_kernel,
        grid=(rows // tm,),
        in_specs=[
            row_spec,
            pl.BlockSpec((None, None, N_MOD, D_MODEL), lambda i: (layer, sample_of_tile(i), 0, 0)),
            const((1, D_MODEL)),
            const((D_MODEL, 2 * FF_HIDDEN)), const((FF_HIDDEN, D_MODEL)),
        ],
        out_specs=row_spec,
        out_shape=jax.ShapeDtypeStruct((rows, D_MODEL), F32),
        compiler_params=_cparams(1),
        name="ffn",
    )(x, mod, g, w_in, w_out)


def _rope_tables(t):
    n = HEAD_DIM // 4
    inv = 1.0 / (ROPE_THETA ** (np.arange(n, dtype=np.float64) / n))
    pos = np.arange(t)
    ang_r = (pos // GRID_W)[:, None] * inv[None, :]
    ang_c = (pos % GRID_W)[:, None] * inv[None, :]
    cos = np.concatenate([np.cos(ang_r)] * 2 + [np.cos(ang_c)] * 2, axis=1)
    sin = np.concatenate([-np.sin(ang_r), np.sin(ang_r), -np.sin(ang_c), np.sin(ang_c)], axis=1)
    tile = lambda a: jnp.asarray(np.tile(a, (1, N_Q_HEADS)).astype(np.float32))
    return tile(cos), tile(sin)


def _layer(x, mod, sample_of_tile, wts, *, batch, t, latent, layer, ctx=None, rope_tabs=None):
    (norm1_g, w_mix, w_bg, qg, kg, conv_w, gate_b, m_norm_g, w_pa, w_pm, w_pf, w_out,
     norm2_g, w_ffn_in, w_ffn_out) = wts
    tm = 512
    za, zm, fu, gt = _in_proj(x, mod, norm1_g, w_mix, sample_of_tile(tm), tm, layer)
    extra = ()
    if latent:
        cache_k, cache_v, caug0, m0 = ctx
        (attn,) = _attention(za, qg, kg, batch=batch, t_new=t, latent=True, cache_k=cache_k,
                             cache_v=cache_v, layer=layer, rope_tabs=rope_tabs)
        (hm,) = _mlstm(zm, gt, conv_w, gate_b, m_norm_g, batch=batch, t=t, layer=layer,
                       caug0=caug0, m0=m0)
    else:
        attn, new_k, new_v = _attention(za, qg, kg, batch=batch, t_new=t, latent=False, layer=layer)
        hm, cs, ms = _mlstm(zm, gt, conv_w, gate_b, m_norm_g, batch=batch, t=t, layer=layer)
        extra = (new_k, new_v, cs, ms)
    fo = _fourier(fu, batch=batch, t=t)
    x = _merge(x, mod, norm1_g, attn, hm, fo, w_bg, w_pa, w_pm, w_pf, w_out, sample_of_tile(tm), tm,
               layer)
    tm_ffn = 512
    x = _ffn(x, mod, norm2_g, w_ffn_in, w_ffn_out, sample_of_tile(tm_ffn), tm_ffn, layer)
    return x, extra


def kernel(x_prompt, x_sample, cache_k, cache_v, state_C, state_n, state_m, c, c_ctx, w_ada, b_ada,
           norm1_g, w_in, q_norm_g, k_norm_g, m_conv_w, m_gate_b, m_norm_g, w_proj_attn,
           w_proj_mlstm, w_proj_fourier, w_out, norm2_g, w_ffn_in, w_ffn_out):
    n_ctx, t_ctx, _ = x_prompt.shape
    n_lat, t_lat, _ = x_sample.shape
    t_past = cache_k.shape[2]
    n_units = 2 * M_HEADS

    cvec = jnp.concatenate([c_ctx[None, :], c], axis=0)
    cvec = jnp.pad(cvec, ((0, MOD_ROWS - cvec.shape[0]), (0, 0)))
    mod = _modulation(cvec, w_ada, b_ada).reshape(DEPTH, MOD_ROWS, N_MOD, D_MODEL)

    rope_tabs = _rope_tables(t_lat)
    ck = cache_k.reshape(n_lat, DEPTH, t_past, KV_WIDTH)
    cv = cache_v.reshape(n_lat, DEPTH, t_past, KV_WIDTH)

    ctx_sample = lambda tm: (lambda i: 0)
    lat_sample = lambda tm: (lambda i: 1 + i // (t_lat // tm))

    xp = x_prompt.reshape(n_ctx * t_ctx, D_MODEL)
    xs = x_sample.reshape(n_lat * t_lat, D_MODEL)
    ks, vs, cs_all, ms_all = [], [], [], []
    o = IN_OFFS
    w_mix = jnp.concatenate(
        [w_in[:, :, o[0]:o[7]], w_in[:, :, o[8]:o[9]], w_in[:, :, o[7]:o[8]],
         jnp.zeros((DEPTH, D_MODEL, GATE_PAD - N_GATES), w_in.dtype)], axis=2).astype(BF16)
    w_bg = w_in[:, :, o[9]:o[10]].astype(BF16)
    row = lambda a: a[:, None, :]
    wts = (row(norm1_g), w_mix, w_bg,
           row(jnp.tile(q_norm_g, (1, N_Q_HEADS))), row(jnp.tile(k_norm_g, (1, N_KV_HEADS))),
           m_conv_w, row(jnp.pad(m_gate_b, ((0, 0), (0, GATE_PAD - N_GATES)))), row(m_norm_g),
           w_proj_attn.astype(BF16), w_proj_mlstm.astype(BF16), w_proj_fourier.astype(BF16),
           w_out.astype(BF16), row(norm2_g), w_ffn_in.astype(BF16), w_ffn_out.astype(BF16))
    for l in range(DEPTH):
        xp, (k_l, v_l, cs, ms) = _layer(xp, mod, ctx_sample, wts, batch=n_ctx, t=t_ctx,
                                        latent=False, layer=l)
        ks.append(k_l.reshape(n_ctx, t_ctx, N_KV_HEADS, HEAD_DIM))
        vs.append(v_l.reshape(n_ctx, t_ctx, N_KV_HEADS, HEAD_DIM))
        cs_all.append(cs)
        ms_all.append(ms)

        c0t = jnp.swapaxes(state_C[:, l].astype(F32).reshape(n_lat, n_units, M_DK, M_DV), -1, -2)
        n0 = state_n[:, l].astype(F32).reshape(n_lat, n_units, 1, M_DK)
        caug0 = jnp.concatenate(
            [c0t, jnp.broadcast_to(n0, (n_lat, n_units, LANES - M_DV, M_DK))], axis=-2)
        m0 = jnp.broadcast_to(state_m[:, l].astype(F32).reshape(n_lat, n_units, 1),
                              (n_lat, n_units, LANES))
        xs, _ = _layer(xs, mod, lat_sample, wts, batch=n_lat, t=t_lat, latent=True, layer=l,
                       ctx=(ck, cv, caug0, m0), rope_tabs=rope_tabs)

    cs_all = jnp.stack(cs_all, axis=1)
    ms_all = jnp.stack(ms_all, axis=1)
    new_c = jnp.swapaxes(cs_all[..., :M_DV, :], -1, -2).reshape(n_ctx, DEPTH, 2, M_HEADS, M_DK, M_DV)
    new_n = cs_all[..., M_DV, :].reshape(n_ctx, DEPTH, 2, M_HEADS, M_DK)
    new_m = ms_all[..., 0].reshape(n_ctx, DEPTH, 2, M_HEADS)
    return (xp.reshape(n_ctx, t_ctx, D_MODEL), xs.reshape(n_lat, t_lat, D_MODEL),
            jnp.stack(ks, axis=1), jnp.stack(vs, axis=1), new_c, new_n, new_m)
```

```python
import functools

import numpy as np
import jax
import jax.numpy as jnp
from jax import lax
from jax.experimental import pallas as pl
from jax.experimental.pallas import tpu as pltpu

D_MODEL = 1024
DEPTH = 2
GRID_W = 64
HEAD_DIM = 64
N_Q_HEADS = 8
N_KV_HEADS = 4
ATTN_WIDTH = N_Q_HEADS * HEAD_DIM
KV_WIDTH = N_KV_HEADS * HEAD_DIM
ROPE_THETA = 10000.0
M_HEADS = 4
M_DK = 64
M_DV = 64
M_WIDTH = M_HEADS * M_DV
F_GROUPS = 4
F_GROUP_CH = 64
F_WIDTH = F_GROUPS * F_GROUP_CH
FF_HIDDEN = -(-8 * D_MODEL // (3 * 256)) * 256
EPS = 1e-6
N_GATES = 4 * M_HEADS
IN_SIZES = (ATTN_WIDTH, KV_WIDTH, KV_WIDTH, M_HEADS * M_DK, M_HEADS * M_DK, M_WIDTH, M_WIDTH,
            N_GATES, F_WIDTH, 3 * D_MODEL)
IN_OFFS = tuple(int(v) for v in np.cumsum((0,) + IN_SIZES))

LANES = 128
GATE_PAD = LANES
N_MOD = 6
MOD_ROWS = 16
M_CHUNK_K = 128
NEG_BIG = -1e30
LOG2_E = 1.4426950408889634
VMEM_LIMIT = 56 * 1024 * 1024

F32 = jnp.float32
BF16 = jnp.bfloat16


def _cparams(n_axes):
    return pltpu.CompilerParams(dimension_semantics=("arbitrary",) * n_axes,
                                vmem_limit_bytes=VMEM_LIMIT)


def _layer_spec(shape, layer):
    return pl.BlockSpec((None,) + tuple(shape), lambda *_: (layer,) + (0,) * len(shape),
                        pipeline_mode=pl.Buffered(1))


def _mm(a, b):
    return jnp.dot(a, b, preferred_element_type=F32)


def _mm_nt(a, b):
    return lax.dot_general(a, b, (((1,), (1,)), ((), ())), preferred_element_type=F32)


def _split3(x):
    hi = x.astype(BF16)
    r = x - hi.astype(F32)
    mid = r.astype(BF16)
    lo = (r - mid.astype(F32)).astype(BF16)
    return hi, mid, lo


def _mm_left_f32(x, m_bf16):
    hi, mid, lo = _split3(x)
    return _mm(hi, m_bf16) + _mm(mid, m_bf16) + _mm(lo, m_bf16)


def _sigmoid(x):
    return 1.0 / (1.0 + jnp.exp(-x))


def _silu(x):
    return x * _sigmoid(x)


def _log_sigmoid(x):
    return jnp.minimum(x, 0.0) - jnp.log(1.0 + jnp.exp(-jnp.abs(x)))


def _normmod(x, g, scale, shift):
    ms = jnp.mean(x * x, axis=-1, keepdims=True)
    return (x * lax.rsqrt(ms + EPS)) * g * (1.0 + scale) + shift


def _group_inv_rms(x, bd):
    x2 = x * x
    hi = x2.astype(BF16)
    lo = (x2 - hi.astype(F32)).astype(BF16)
    ss = _mm(hi, bd) + _mm(lo, bd)
    return lax.rsqrt(ss * (1.0 / HEAD_DIM) + EPS)


def _rope(x, cos, sin_signed):
    w = x.shape[1]
    lane = lax.broadcasted_iota(jnp.int32, x.shape, 1)
    up = pltpu.roll(x, w - 16, axis=1)
    dn = pltpu.roll(x, 16, axis=1)
    partner = jnp.where((lane & 31) < 16, up, dn)
    return x * cos + partner * sin_signed


def _mod_kernel(c_ref, w_ref, b_ref, o_ref):
    s = _silu(c_ref[...]).astype(BF16)
    o_ref[...] = _mm(s, w_ref[...].astype(BF16)) + b_ref[...]


def _modulation(cvec, w_ada, b_ada):
    tn = 512
    n_out = N_MOD * D_MODEL
    return pl.pallas_call(
        _mod_kernel,
        grid=(DEPTH, n_out // tn),
        in_specs=[
            pl.BlockSpec((MOD_ROWS, D_MODEL), lambda l, j: (0, 0)),
            pl.BlockSpec((None, D_MODEL, tn), lambda l, j: (l, 0, j)),
            pl.BlockSpec((None, 1, tn), lambda l, j: (l, 0, j)),
        ],
        out_specs=pl.BlockSpec((None, MOD_ROWS, tn), lambda l, j: (l, 0, j)),
        out_shape=jax.ShapeDtypeStruct((DEPTH, MOD_ROWS, n_out), F32),
        compiler_params=_cparams(2),
        name="modulation",
    )(cvec, w_ada, b_ada.reshape(DEPTH, 1, n_out))


GATE_OFF, FU_OFF, BG_OFF = IN_OFFS[7], IN_OFFS[8], IN_OFFS[9]


def _in_kernel(x_ref, mod_ref, g_ref, w_ref, za_ref, zm_ref, fu_ref, gt_ref):
    h = _normmod(x_ref[...], g_ref[...], mod_ref[1:2, :], mod_ref[0:1, :])
    z = _mm_nt(h.astype(BF16), w_ref[...])
    za_ref[...] = z[:, :D_MODEL]
    zm_ref[...] = z[:, D_MODEL:GATE_OFF]
    gt_ref[...] = z[:, GATE_OFF:GATE_OFF + GATE_PAD]
    fu_ref[...] = z[:, FU_OFF:FU_OFF + F_WIDTH]


def _in_proj(x, mod, g, w_mix, sample_of_tile, tm, layer):
    rows = x.shape[0]
    row_spec = lambda width: pl.BlockSpec((tm, width), lambda i: (i, 0))
    return pl.pallas_call(
        _in_kernel,
        grid=(rows // tm,),
        in_specs=[
            row_spec(D_MODEL),
            pl.BlockSpec((None, None, N_MOD, D_MODEL), lambda i: (layer, sample_of_tile(i), 0, 0)),
            _layer_spec((1, D_MODEL), layer),
            _layer_spec((BG_OFF, D_MODEL), layer),
        ],
        out_specs=[row_spec(D_MODEL), row_spec(D_MODEL), row_spec(F_WIDTH), row_spec(GATE_PAD)],
        out_shape=[
            jax.ShapeDtypeStruct((rows, D_MODEL), F32),
            jax.ShapeDtypeStruct((rows, D_MODEL), F32),
            jax.ShapeDtypeStruct((rows, F_WIDTH), F32),
            jax.ShapeDtypeStruct((rows, GATE_PAD), F32),
        ],
        compiler_params=_cparams(1),
        name="in_proj",
    )(x, mod, g, w_mix)


def _attn_kernel(*refs, tq, t_new, t_past, latent):
    if latent:
        (q_ref, k_ref, v_ref, ck_ref, cv_ref, cosq_ref, sinq_ref, cosk_ref, sink_ref,
         qg_ref, kg_ref, bdq_ref, bdk_ref, o_ref, ks_ref, vs_ref) = refs
    else:
        (q_ref, k_ref, v_ref, qg_ref, kg_ref, bdq_ref, bdk_ref,
         o_ref, nk_ref, nv_ref, ks_ref, vs_ref) = refs

    @pl.when(pl.program_id(1) == 0)
    def _():
        k = k_ref[...]
        kn = k * _group_inv_rms(k, bdk_ref[...]) * kg_ref[...]
        v = v_ref[...]
        if latent:
            kn = _rope(kn, cosk_ref[...], sink_ref[...])
            ck = ck_ref[...]
            cv = cv_ref[...]
        else:
            nk_ref[...] = kn
            nv_ref[...] = v
        for h in range(N_KV_HEADS):
            sl = slice(HEAD_DIM * h, HEAD_DIM * (h + 1))
            if latent:
                ks_ref[h, :t_past, :] = ck[:, sl].astype(BF16)
                vs_ref[h, :t_past, :HEAD_DIM] = cv[:, sl].astype(BF16)
            ks_ref[h, t_past:, :] = kn[:, sl].astype(BF16)
            vs_ref[h, t_past:, :HEAD_DIM] = v[:, sl].astype(BF16)
            vs_ref[h, :, HEAD_DIM:] = jnp.ones((t_past + t_new, LANES - HEAD_DIM), BF16)

    q = q_ref[...]
    qn = q * _group_inv_rms(q, bdq_ref[...]) * qg_ref[...]
    if latent:
        qn = _rope(qn, cosq_ref[...], sinq_ref[...])
    qn = qn * (HEAD_DIM ** -0.5 * LOG2_E)
    group = N_Q_HEADS // N_KV_HEADS
    outs = []
    for h in range(N_KV_HEADS):
        qh = jnp.concatenate(
            [qn[:, HEAD_DIM * (group * h + g):HEAD_DIM * (group * h + g + 1)] for g in range(group)],
            axis=0).astype(BF16)
        s = _mm_nt(qh, ks_ref[h])
        m = jnp.max(s, axis=-1, keepdims=True)
        p = jnp.exp2(s - m).astype(BF16)
        od = _mm(p, vs_ref[h])
        o = (od / pltpu.roll(od, LANES - HEAD_DIM, axis=1))[:, :HEAD_DIM]
        outs.extend(o[tq * g:tq * (g + 1)] for g in range(group))
    o_ref[...] = jnp.concatenate(outs, axis=1)


def _block_diag_ones(width):
    idx = np.arange(width) // HEAD_DIM
    return jnp.asarray(idx[:, None] == idx[None, :], dtype=BF16)


def _attention(za, qg, kg, *, batch, t_new, latent, cache_k=None, cache_v=None, layer=0, rope_tabs=None):
    tq = 256
    nq = t_new // tq
    t_past = cache_k.shape[2] if latent else 0
    t_keys = t_past + t_new
    kcol = ATTN_WIDTH // KV_WIDTH
    bdq = _block_diag_ones(ATTN_WIDTH)
    bdk = _block_diag_ones(KV_WIDTH)
    const = lambda shape: pl.BlockSpec(shape, lambda b, i: (0,) * len(shape))
    in_specs = [
        pl.BlockSpec((tq, ATTN_WIDTH), lambda b, i: (b * nq + i, 0)),
        pl.BlockSpec((t_new, KV_WIDTH), lambda b, i: (b, kcol)),
        pl.BlockSpec((t_new, KV_WIDTH), lambda b, i: (b, kcol + 1)),
    ]
    args = [za, za, za]
    if latent:
        cos_t, sin_t = rope_tabs
        in_specs += [
            pl.BlockSpec((None, None, t_past, KV_WIDTH), lambda b, i: (b, layer, 0, 0)),
            pl.BlockSpec((None, None, t_past, KV_WIDTH), lambda b, i: (b, layer, 0, 0)),
            pl.BlockSpec((tq, ATTN_WIDTH), lambda b, i: (i, 0)),
            pl.BlockSpec((tq, ATTN_WIDTH), lambda b, i: (i, 0)),
            pl.BlockSpec((t_new, KV_WIDTH), lambda b, i: (0, 0)),
            pl.BlockSpec((t_new, KV_WIDTH), lambda b, i: (0, 0)),
        ]
        args += [cache_k, cache_v, cos_t, sin_t, cos_t, sin_t]
    in_specs += [_layer_spec((1, ATTN_WIDTH), layer), _layer_spec((1, KV_WIDTH), layer),
                 const((ATTN_WIDTH, ATTN_WIDTH)), const((KV_WIDTH, KV_WIDTH))]
    args += [qg, kg, bdq, bdk]
    out_specs = [pl.BlockSpec((tq, ATTN_WIDTH), lambda b, i: (b * nq + i, 0))]
    out_shape = [jax.ShapeDtypeStruct((batch * t_new, ATTN_WIDTH), F32)]
    if not latent:
        out_specs += [pl.BlockSpec((t_new, KV_WIDTH), lambda b, i: (b, 0))] * 2
        out_shape += [jax.ShapeDtypeStruct((batch * t_new, KV_WIDTH), F32)] * 2
    return pl.pallas_call(
        functools.partial(_attn_kernel, tq=tq, t_new=t_new, t_past=t_past, latent=latent),
        grid=(batch, nq),
        in_specs=in_specs,
        out_specs=out_specs,
        out_shape=out_shape,
        scratch_shapes=[pltpu.VMEM((N_KV_HEADS, t_keys, HEAD_DIM), BF16),
                        pltpu.VMEM((N_KV_HEADS, t_keys, LANES), BF16)],
        compiler_params=_cparams(2),
        name="attention_latent" if latent else "attention_context",
    )(*args)


def _running_max_lanes(x, reverse):
    n = x.shape[1]
    lane = lax.broadcasted_iota(jnp.int32, x.shape, 1)
    k = 1
    while k < n:
        if reverse:
            cand = jnp.where(lane < n - k, pltpu.roll(x, n - k, axis=1), NEG_BIG)
        else:
            cand = jnp.where(lane >= k, pltpu.roll(x, k, axis=1), NEG_BIG)
        x = jnp.maximum(x, cand)
        k *= 2
    return x


def _mlstm_kernel(*refs, t, chunk, has_state, emit_state):
    refs = list(refs)
    zm_ref, gt_ref, cw_ref, gb_ref, mg_ref, bd_ref = refs[:6]
    refs = refs[6:]
    if has_state:
        c0_ref, m0_ref = refs[:2]
        refs = refs[2:]
    o_ref = refs[0]
    refs = refs[1:]
    if emit_state:
        cs_ref, ms_ref = refs[:2]
        refs = refs[2:]
    hst_ref = refs[0]

    width = M_HEADS * M_DK
    z = zm_ref[...]
    x = z[:, :2 * width]
    row = lax.broadcasted_iota(jnp.int32, (t, 1), 0)
    x_prev = jnp.where(row == 0, 0.0, pltpu.roll(x, 1, axis=0))
    x_next = jnp.where(row == t - 1, 0.0, pltpu.roll(x, t - 1, axis=0))
    cw = cw_ref[...]
    qk = _silu(x_prev * cw[0:1, :] + x * cw[1:2, :] + x_next * cw[2:3, :])
    q = qk[:, :width]
    k = qk[:, width:] * (M_DK ** -0.5)
    v = z[:, 2 * width:2 * width + M_WIDTH]
    om = z[:, 2 * width + M_WIDTH:]
    heads = [slice(M_DK * h, M_DK * (h + 1)) for h in range(M_HEADS)]
    q_rows = [q[:, hs].astype(BF16) for hs in heads]
    k_rows = [k[:, hs].astype(BF16) for hs in heads]
    qt = q.T.astype(BF16)
    vt = v.T
    ones_r = jnp.ones((LANES - M_DV, t), F32)
    vaug_t = [jnp.concatenate([vt[hs, :], ones_r], axis=0) for hs in heads]
    vaug_tb = [a.astype(BF16) for a in vaug_t]

    gates = gt_ref[...] + gb_ref[...]
    lane = lax.broadcasted_iota(jnp.int32, (1, GATE_PAD), 1)
    is_forget = (lane & M_HEADS) != 0
    gates = jnp.where(is_forget, _log_sigmoid(gates), gates)
    gates_t = gates.T[:N_GATES, :]

    ri = lax.broadcasted_iota(jnp.int32, (chunk, chunk), 0)
    ci = lax.broadcasted_iota(jnp.int32, (chunk, chunk), 1)
    row_le_col = ri <= ci
    row_ge_col = ri >= ci

    n_chunks = t // chunk
    g_all = jnp.concatenate([gates_t[:, chunk * c:chunk * (c + 1)] for c in range(n_chunks)], axis=0)
    i_all = pltpu.roll(g_all, M_HEADS, axis=0)
    rows = lax.broadcasted_iota(jnp.int32, (N_GATES * n_chunks, 1), 0)
    is_fwd_row = (rows & (2 * M_HEADS)) == 0
    cum_p = _mm_left_f32(g_all, row_le_col.astype(BF16))
    cum_s = _mm_left_f32(g_all, row_ge_col.astype(BF16))
    b_all = jnp.where(is_fwd_row, cum_p, cum_s)
    u_all = i_all - b_all
    pm_all = jnp.where(is_fwd_row, _running_max_lanes(u_all, reverse=False),
                       _running_max_lanes(u_all, reverse=True))
    pml_all = jnp.where(is_fwd_row, pm_all[:, chunk - 1:], pm_all[:, :1])
    bl_all = jnp.where(is_fwd_row, b_all[:, chunk - 1:], b_all[:, :1])
    wk_all = jnp.exp(u_all - pml_all)
    u_cols = u_all.T

    if has_state:
        m_state = [m0_ref[M_HEADS * d:M_HEADS * (d + 1), 0:1] for d in range(2)]
        c_state = [[c0_ref[M_HEADS * d + h] for h in range(M_HEADS)] for d in range(2)]
    else:
        m_state = [jnp.zeros((M_HEADS, 1), F32) for _ in range(2)]
        c_state = [[jnp.zeros((LANES, M_DK), F32) for _ in range(M_HEADS)] for _ in range(2)]

    written = set()
    for j in range(n_chunks):
        for direction in range(2):
            fwd = direction == 0
            c = j if fwd else n_chunks - 1 - j
            rs = slice(chunk * c, chunk * (c + 1))
            valid = row_le_col if fwd else row_ge_col
            r0 = N_GATES * c + (2 * direction + 1) * M_HEADS
            b4, pm4, wk4, bl, pml = (a[r0:r0 + M_HEADS, :] for a in (b_all, pm_all, wk_all, bl_all, pml_all))
            m = m_state[direction]
            mx = jnp.maximum(m, pm4)
            f_loc = jnp.exp(pm4 - mx)
            f_int = jnp.exp(m - mx)
            floor = jnp.exp(-(b4 + mx))
            m_new = bl + jnp.maximum(m, pml)
            decay = jnp.exp(bl + m - m_new)
            gain = jnp.exp(bl + pml - m_new)
            m_state[direction] = m_new

            st = [_mm_nt(k_rows[h][rs, :], q_rows[h][rs, :]) for h in range(M_HEADS)]
            s_loc, x_loc = [], []
            for h in range(M_HEADS):
                w = jnp.exp(jnp.where(valid, u_cols[:, r0 + h:r0 + h + 1] - pm4[h:h + 1, :], NEG_BIG))
                s_loc.append(_mm(vaug_tb[h][:, rs], (st[h] * w).astype(BF16)))
                x_loc.append(_mm((vaug_t[h][:, rs] * wk4[h:h + 1, :]).astype(BF16), k_rows[h][rs, :]))
            for h in range(M_HEADS):
                caug = c_state[direction][h]
                inter = _mm(caug.astype(BF16), qt[heads[h], rs])
                nd = f_loc[h:h + 1, :] * s_loc[h] + f_int[h:h + 1, :] * inter
                ht = nd[:M_DV, :] / jnp.maximum(jnp.abs(nd[M_DV:, :]), floor[h:h + 1, :])
                c_state[direction][h] = decay[h:h + 1, :] * caug + gain[h:h + 1, :] * x_loc[h]
                if c in written:
                    hst_ref[heads[h], rs] += ht
                else:
                    hst_ref[heads[h], rs] = ht
            written.add(c)

    if emit_state:
        for d in range(2):
            ms_ref[M_HEADS * d:M_HEADS * (d + 1), :] = jnp.broadcast_to(m_state[d], (M_HEADS, LANES))
            for h in range(M_HEADS):
                cs_ref[M_HEADS * d + h] = c_state[d][h]

    hsum = hst_ref[...].T
    hn = hsum * _group_inv_rms(hsum, bd_ref[...]) * mg_ref[...]
    o_ref[...] = _sigmoid(om) * hn


def _mlstm(zm, gt, conv_w, gate_b, m_norm_g, *, batch, t, layer, caug0=None, m0=None):
    has_state = caug0 is not None
    emit_state = not has_state
    chunk = min(M_CHUNK_K, t)
    n_units = 2 * M_HEADS
    const = lambda shape: pl.BlockSpec(shape, lambda b: (0,) * len(shape))
    in_specs = [
        pl.BlockSpec((t, D_MODEL), lambda b: (b, 0)),
        pl.BlockSpec((t, GATE_PAD), lambda b: (b, 0)),
        _layer_spec((3, 2 * M_HEADS * M_DK), layer),
        _layer_spec((1, GATE_PAD), layer),
        _layer_spec((1, M_WIDTH), layer),
        const((M_WIDTH, M_WIDTH)),
    ]
    args = [zm, gt, conv_w, gate_b, m_norm_g, _block_diag_ones(M_WIDTH)]
    if has_state:
        in_specs += [pl.BlockSpec((None, n_units, LANES, M_DK), lambda b: (b, 0, 0, 0)),
                     pl.BlockSpec((None, n_units, LANES), lambda b: (b, 0, 0))]
        args += [caug0, m0]
    out_specs = [pl.BlockSpec((t, M_WIDTH), lambda b: (b, 0))]
    out_shape = [jax.ShapeDtypeStruct((batch * t, M_WIDTH), F32)]
    if emit_state:
        out_specs += [pl.BlockSpec((None, n_units, LANES, M_DK), lambda b: (b, 0, 0, 0)),
                      pl.BlockSpec((None, n_units, LANES), lambda b: (b, 0, 0))]
        out_shape += [jax.ShapeDtypeStruct((batch, n_units, LANES, M_DK), F32),
                      jax.ShapeDtypeStruct((batch, n_units, LANES), F32)]
    return pl.pallas_call(
        functools.partial(_mlstm_kernel, t=t, chunk=chunk, has_state=has_state, emit_state=emit_state),
        grid=(batch,),
        in_specs=in_specs,
        out_specs=out_specs,
        out_shape=out_shape,
        scratch_shapes=[pltpu.VMEM((M_WIDTH, t), F32)],
        compiler_params=_cparams(1),
        name="mlstm_latent" if has_state else "mlstm_context",
    )(*args)


def _fourier_kernel(u_ref, cs_ref, ct_ref, st_ref, o_ref):
    a = _mm(u_ref[...].astype(BF16), cs_ref[...])
    ac = a[:, :F_WIDTH].astype(BF16)
    asn = a[:, F_WIDTH:].astype(BF16)
    o_ref[...] = _mm(ct_ref[...], ac) - _mm(st_ref[...], asn)


def _dft_tables(t):
    kt = (np.arange(t)[:, None] * np.arange(t)[None, :]) % t
    ang_t = 2.0 * np.pi * kt.astype(np.float64) / t
    ct = np.cos(ang_t) / np.sqrt(t)
    st = np.sin(ang_t) / np.sqrt(t)
    c = F_GROUP_CH
    kc = (np.arange(c)[:, None] * np.arange(c)[None, :]) % c
    ang_c = 2.0 * np.pi * kc.astype(np.float64) / c
    eye = np.eye(F_GROUPS)
    cc = np.kron(eye, np.cos(ang_c) / np.sqrt(c))
    sc = np.kron(eye, np.sin(ang_c) / np.sqrt(c))
    cs = np.concatenate([cc, sc], axis=1)
    to_dev = lambda a: jnp.asarray(a.astype(np.float32)).astype(BF16)
    return to_dev(cs), to_dev(ct), to_dev(st)


def _fourier(fu, *, batch, t):
    cs, ct, st = _dft_tables(t)
    const = lambda shape: pl.BlockSpec(shape, lambda b: (0,) * len(shape))
    return pl.pallas_call(
        _fourier_kernel,
        grid=(batch,),
        in_specs=[pl.BlockSpec((t, F_WIDTH), lambda b: (b, 0)),
                  const((F_WIDTH, 2 * F_WIDTH)), const((t, t)), const((t, t))],
        out_specs=pl.BlockSpec((t, F_WIDTH), lambda b: (b, 0)),
        out_shape=jax.ShapeDtypeStruct((batch * t, F_WIDTH), F32),
        compiler_params=_cparams(1),
        name="fourier",
    )(fu, cs, ct, st)


def _merge_kernel(x_ref, mod_ref, g_ref, a_ref, hm_ref, fo_ref,
                  wbg_ref, wpa_ref, wpm_ref, wpf_ref, wo_ref, o_ref):
    x = x_ref[...]
    h = _normmod(x, g_ref[...], mod_ref[1:2, :], mod_ref[0:1, :]).astype(BF16)

    def branch(j, y_ref, w_ref):
        gate = _sigmoid(_mm_nt(h, wbg_ref[D_MODEL * j:D_MODEL * (j + 1), :]))
        return gate * _mm(y_ref[...].astype(BF16), w_ref[...])

    merged = branch(0, a_ref, wpa_ref) + branch(1, hm_ref, wpm_ref) + branch(2, fo_ref, wpf_ref)
    o_ref[...] = x + mod_ref[2:3, :] * _mm(merged.astype(BF16), wo_ref[...])


def _merge(x, mod, g, attn, hm, fo, w_bg, w_pa, w_pm, w_pf, w_out, sample_of_tile, tm, layer):
    rows = x.shape[0]
    row_spec = lambda width: pl.BlockSpec((tm, width), lambda i: (i, 0))
    const = lambda shape: _layer_spec(shape, layer)
    return pl.pallas_call(
        _merge_kernel,
        grid=(rows // tm,),
        in_specs=[
            row_spec(D_MODEL),
            pl.BlockSpec((None, None, N_MOD, D_MODEL), lambda i: (layer, sample_of_tile(i), 0, 0)),
            const((1, D_MODEL)),
            row_spec(ATTN_WIDTH), row_spec(M_WIDTH), row_spec(F_WIDTH),
            const((3 * D_MODEL, D_MODEL)), const((ATTN_WIDTH, D_MODEL)),
            const((M_WIDTH, D_MODEL)), const((F_WIDTH, D_MODEL)), const((D_MODEL, D_MODEL)),
        ],
        out_specs=row_spec(D_MODEL),
        out_shape=jax.ShapeDtypeStruct((rows, D_MODEL), F32),
        compiler_params=_cparams(1),
        name="merge",
    )(x, mod, g, attn, hm, fo, w_bg, w_pa, w_pm, w_pf, w_out)


def _ffn_kernel(x_ref, mod_ref, g_ref, win_ref, wout_ref, o_ref):
    x = x_ref[...]
    h = _normmod(x, g_ref[...], mod_ref[4:5, :], mod_ref[3:4, :]).astype(BF16)
    u = _mm(h, win_ref[...])
    a = (_silu(u[:, :FF_HIDDEN]) * u[:, FF_HIDDEN:]).astype(BF16)
    o_ref[...] = x + mod_ref[5:6, :] * _mm(a, wout_ref[...])


def _ffn(x, mod, g, w_in, w_out, sample_of_tile, tm, layer):
    rows = x.shape[0]
    row_spec = pl.BlockSpec((tm, D_MODEL), lambda i: (i, 0))
    const = lambda shape: _layer_spec(shape, layer)
    return pl.pallas_call(
        _ffn_kernel,
        grid=(rows // tm,),
        in_specs=[
            row_spec,
            pl.BlockSpec((None, None, N_MOD, D_MODEL), lambda i: (layer, sample_of_tile(i), 0, 0)),
            const((1, D_MODEL)),
            const((D_MODEL, 2 * FF_HIDDEN)), const((FF_HIDDEN, D_MODEL)),
        ],
        out_specs=row_spec,
        out_shape=jax.ShapeDtypeStruct((rows, D_MODEL), F32),
        compiler_params=_cparams(1),
        name="ffn",
    )(x, mod, g, w_in, w_out)


def _rope_tables(t):
    n = HEAD_DIM // 4
    inv = 1.0 / (ROPE_THETA ** (np.arange(n, dtype=np.float64) / n))
    pos = np.arange(t)
    ang_r = (pos // GRID_W)[:, None] * inv[None, :]
    ang_c = (pos % GRID_W)[:, None] * inv[None, :]
    cos = np.concatenate([np.cos(ang_r)] * 2 + [np.cos(ang_c)] * 2, axis=1)
    sin = np.concatenate([-np.sin(ang_r), np.sin(ang_r), -np.sin(ang_c), np.sin(ang_c)], axis=1)
    tile = lambda a: jnp.asarray(np.tile(a, (1, N_Q_HEADS)).astype(np.float32))
    return tile(cos), tile(sin)


def _layer(x, mod, sample_of_tile, wts, *, batch, t, latent, layer, ctx=None, rope_tabs=None):
    (norm1_g, w_mix, w_bg, qg, kg, conv_w, gate_b, m_norm_g, w_pa, w_pm, w_pf, w_out,
     norm2_g, w_ffn_in, w_ffn_out) = wts
    tm = 512
    za, zm, fu, gt = _in_proj(x, mod, norm1_g, w_mix, sample_of_tile(tm), tm, layer)
    extra = ()
    if latent:
        cache_k, cache_v, caug0, m0 = ctx
        (attn,) = _attention(za, qg, kg, batch=batch, t_new=t, latent=True, cache_k=cache_k,
                             cache_v=cache_v, layer=layer, rope_tabs=rope_tabs)
        (hm,) = _mlstm(zm, gt, conv_w, gate_b, m_norm_g, batch=batch, t=t, layer=layer,
                       caug0=caug0, m0=m0)
    else:
        attn, new_k, new_v = _attention(za, qg, kg, batch=batch, t_new=t, latent=False, layer=layer)
        hm, cs, ms = _mlstm(zm, gt, conv_w, gate_b, m_norm_g, batch=batch, t=t, layer=layer)
        extra = (new_k, new_v, cs, ms)
    fo = _fourier(fu, batch=batch, t=t)
    x = _merge(x, mod, norm1_g, attn, hm, fo, w_bg, w_pa, w_pm, w_pf, w_out, sample_of_tile(tm), tm,
               layer)
    tm_ffn = 512
    x = _ffn(x, mod, norm2_g, w_ffn_in, w_ffn_out, sample_of_tile(tm_ffn), tm_ffn, layer)
    return x, extra


def kernel(x_prompt, x_sample, cache_k, cache_v, state_C, state_n, state_m, c, c_ctx, w_ada, b_ada,
           norm1_g, w_in, q_norm_g, k_norm_g, m_conv_w, m_gate_b, m_norm_g, w_proj_attn,
           w_proj_mlstm, w_proj_fourier, w_out, norm2_g, w_ffn_in, w_ffn_out):
    n_ctx, t_ctx, _ = x_prompt.shape
    n_lat, t_lat, _ = x_sample.shape
    t_past = cache_k.shape[2]
    n_units = 2 * M_HEADS

    cvec = jnp.concatenate([c_ctx[None, :], c], axis=0)
    cvec = jnp.pad(cvec, ((0, MOD_ROWS - cvec.shape[0]), (0, 0)))
    mod = _modulation(cvec, w_ada, b_ada).reshape(DEPTH, MOD_ROWS, N_MOD, D_MODEL)

    rope_tabs = _rope_tables(t_lat)
    ck = cache_k.reshape(n_lat, DEPTH, t_past, KV_WIDTH)
    cv = cache_v.reshape(n_lat, DEPTH, t_past, KV_WIDTH)

    ctx_sample = lambda tm: (lambda i: 0)
    lat_sample = lambda tm: (lambda i: 1 + i // (t_lat // tm))

    xp = x_prompt.reshape(n_ctx * t_ctx, D_MODEL)
    xs = x_sample.reshape(n_lat * t_lat, D_MODEL)
    ks, vs, cs_all, ms_all = [], [], [], []
    w_in_t = jnp.swapaxes(w_in, 1, 2)
    w_mix = w_in_t[:, :BG_OFF].astype(BF16)
    w_bg = w_in_t[:, BG_OFF:].astype(BF16)
    row = lambda a: a[:, None, :]
    wts = (row(norm1_g), w_mix, w_bg,
           row(jnp.tile(q_norm_g, (1, N_Q_HEADS))), row(jnp.tile(k_norm_g, (1, N_KV_HEADS))),
           m_conv_w, row(jnp.pad(m_gate_b, ((0, 0), (0, GATE_PAD - N_GATES)))), row(m_norm_g),
           w_proj_attn.astype(BF16), w_proj_mlstm.astype(BF16), w_proj_fourier.astype(BF16),
           w_out.astype(BF16), row(norm2_g), w_ffn_in.astype(BF16), w_ffn_out.astype(BF16))
    for l in range(DEPTH):
        xp, (k_l, v_l, cs, ms) = _layer(xp, mod, ctx_sample, wts, batch=n_ctx, t=t_ctx,
                                        latent=False, layer=l)
        ks.append(k_l.reshape(n_ctx, t_ctx, N_KV_HEADS, HEAD_DIM))
        vs.append(v_l.reshape(n_ctx, t_ctx, N_KV_HEADS, HEAD_DIM))
        cs_all.append(cs)
        ms_all.append(ms)

        c0t = jnp.swapaxes(state_C[:, l].astype(F32).reshape(n_lat, n_units, M_DK, M_DV), -1, -2)
        n0 = state_n[:, l].astype(F32).reshape(n_lat, n_units, 1, M_DK)
        caug0 = jnp.concatenate(
            [c0t, jnp.broadcast_to(n0, (n_lat, n_units, LANES - M_DV, M_DK))], axis=-2)
        m0 = jnp.broadcast_to(state_m[:, l].astype(F32).reshape(n_lat, n_units, 1),
                              (n_lat, n_units, LANES))
        xs, _ = _layer(xs, mod, lat_sample, wts, batch=n_lat, t=t_lat, latent=True, layer=l,
                       ctx=(ck, cv, caug0, m0), rope_tabs=rope_tabs)

    cs_all = jnp.stack(cs_all, axis=1)
    ms_all = jnp.stack(ms_all, axis=1)
    new_c = jnp.swapaxes(cs_all[..., :M_DV, :], -1, -2).reshape(n_ctx, DEPTH, 2, M_HEADS, M_DK, M_DV)
    new_n = cs_all[..., M_DV, :].reshape(n_ctx, DEPTH, 2, M_HEADS, M_DK)
    new_m = ms_all[..., 0].reshape(n_ctx, DEPTH, 2, M_HEADS)
    return (xp.reshape(n_ctx, t_ctx, D_MODEL), xs.reshape(n_lat, t_lat, D_MODEL),
            jnp.stack(ks, axis=1), jnp.stack(vs, axis=1), new_c, new_n, new_m)
```

```python
import functools

import numpy as np
import jax
import jax.numpy as jnp
from jax import lax
from jax.experimental import pallas as pl
from jax.experimental.pallas import tpu as pltpu

D_MODEL = 1024
DEPTH = 2
GRID_W = 64
HEAD_DIM = 64
N_Q_HEADS = 8
N_KV_HEADS = 4
ATTN_WIDTH = N_Q_HEADS * HEAD_DIM
KV_WIDTH = N_KV_HEADS * HEAD_DIM
ROPE_THETA = 10000.0
M_HEADS = 4
M_DK = 64
M_DV = 64
M_WIDTH = M_HEADS * M_DV
F_GROUPS = 4
F_GROUP_CH = 64
F_WIDTH = F_GROUPS * F_GROUP_CH
FF_HIDDEN = -(-8 * D_MODEL // (3 * 256)) * 256
EPS = 1e-6
N_GATES = 4 * M_HEADS
IN_SIZES = (ATTN_WIDTH, KV_WIDTH, KV_WIDTH, M_HEADS * M_DK, M_HEADS * M_DK, M_WIDTH, M_WIDTH,
            N_GATES, F_WIDTH, 3 * D_MODEL)
IN_OFFS = tuple(int(v) for v in np.cumsum((0,) + IN_SIZES))

LANES = 128
GATE_PAD = LANES
N_MOD = 6
MOD_ROWS = 16
M_CHUNK_K = 128
NEG_BIG = -1e30
LOG2_E = 1.4426950408889634
VMEM_LIMIT = 56 * 1024 * 1024

F32 = jnp.float32
BF16 = jnp.bfloat16


def _cparams(n_axes):
    return pltpu.CompilerParams(dimension_semantics=("arbitrary",) * n_axes,
                                vmem_limit_bytes=VMEM_LIMIT)


def _layer_spec(shape, layer):
    return pl.BlockSpec((None,) + tuple(shape), lambda *_: (layer,) + (0,) * len(shape),
                        pipeline_mode=pl.Buffered(1))


def _mm(a, b):
    return jnp.dot(a, b, preferred_element_type=F32)


def _mm_nt(a, b):
    return lax.dot_general(a, b, (((1,), (1,)), ((), ())), preferred_element_type=F32)


def _split3(x):
    hi = x.astype(BF16)
    r = x - hi.astype(F32)
    mid = r.astype(BF16)
    lo = (r - mid.astype(F32)).astype(BF16)
    return hi, mid, lo


def _mm_left_f32(x, m_bf16):
    hi, mid, lo = _split3(x)
    return _mm(hi, m_bf16) + _mm(mid, m_bf16) + _mm(lo, m_bf16)


def _sigmoid(x):
    return 1.0 / (1.0 + jnp.exp(-x))


def _silu(x):
    return x * _sigmoid(x)


def _log_sigmoid(x):
    return jnp.minimum(x, 0.0) - jnp.log(1.0 + jnp.exp(-jnp.abs(x)))


def _normmod(x, g, scale, shift):
    ms = jnp.mean(x * x, axis=-1, keepdims=True)
    return (x * lax.rsqrt(ms + EPS)) * g * (1.0 + scale) + shift


def _group_inv_rms(x, bd):
    x2 = x * x
    hi = x2.astype(BF16)
    lo = (x2 - hi.astype(F32)).astype(BF16)
    ss = _mm(hi, bd) + _mm(lo, bd)
    return lax.rsqrt(ss * (1.0 / HEAD_DIM) + EPS)


def _rope(x, cos, sin_signed):
    w = x.shape[1]
    lane = lax.broadcasted_iota(jnp.int32, x.shape, 1)
    up = pltpu.roll(x, w - 16, axis=1)
    dn = pltpu.roll(x, 16, axis=1)
    partner = jnp.where((lane & 31) < 16, up, dn)
    return x * cos + partner * sin_signed


def _mod_kernel(c_ref, w_ref, b_ref, o_ref):
    s = _silu(c_ref[...]).astype(BF16)
    o_ref[...] = _mm(s, w_ref[...].astype(BF16)) + b_ref[...]


def _modulation(cvec, w_ada, b_ada):
    tn = 512
    n_out = N_MOD * D_MODEL
    return pl.pallas_call(
        _mod_kernel,
        grid=(DEPTH, n_out // tn),
        in_specs=[
            pl.BlockSpec((MOD_ROWS, D_MODEL), lambda l, j: (0, 0)),
            pl.BlockSpec((None, D_MODEL, tn), lambda l, j: (l, 0, j)),
            pl.BlockSpec((None, 1, tn), lambda l, j: (l, 0, j)),
        ],
        out_specs=pl.BlockSpec((None, MOD_ROWS, tn), lambda l, j: (l, 0, j)),
        out_shape=jax.ShapeDtypeStruct((DEPTH, MOD_ROWS, n_out), F32),
        compiler_params=_cparams(2),
        name="modulation",
    )(cvec, w_ada, b_ada.reshape(DEPTH, 1, n_out))


GATE_OFF, FU_OFF, BG_OFF = IN_OFFS[7], IN_OFFS[8], IN_OFFS[9]


def _in_kernel(x_ref, mod_ref, g_ref, w_ref, za_ref, zm_ref, fu_ref, gt_ref):
    h = _normmod(x_ref[...], g_ref[...], mod_ref[1:2, :], mod_ref[0:1, :])
    z = _mm_nt(h.astype(BF16), w_ref[...])
    za_ref[...] = z[:, :D_MODEL]
    zm_ref[...] = z[:, D_MODEL:GATE_OFF]
    gt_ref[...] = z[:, GATE_OFF:GATE_OFF + GATE_PAD]
    fu_ref[...] = z[:, FU_OFF:FU_OFF + F_WIDTH]


def _in_proj(x, mod, g, w_mix, sample_of_tile, tm, layer):
    rows = x.shape[0]
    row_spec = lambda width: pl.BlockSpec((tm, width), lambda i: (i, 0))
    return pl.pallas_call(
        _in_kernel,
        grid=(rows // tm,),
        in_specs=[
            row_spec(D_MODEL),
            pl.BlockSpec((None, None, N_MOD, D_MODEL), lambda i: (layer, sample_of_tile(i), 0, 0)),
            _layer_spec((1, D_MODEL), layer),
            _layer_spec((BG_OFF, D_MODEL), layer),
        ],
        out_specs=[row_spec(D_MODEL), row_spec(D_MODEL), row_spec(F_WIDTH), row_spec(GATE_PAD)],
        out_shape=[
            jax.ShapeDtypeStruct((rows, D_MODEL), F32),
            jax.ShapeDtypeStruct((rows, D_MODEL), F32),
            jax.ShapeDtypeStruct((rows, F_WIDTH), F32),
            jax.ShapeDtypeStruct((rows, GATE_PAD), F32),
        ],
        compiler_params=_cparams(1),
        name="in_proj",
    )(x, mod, g, w_mix)


def _attn_kernel(*refs, tq, t_new, t_past, latent):
    if latent:
        (q_ref, k_ref, v_ref, ck_ref, cv_ref, cosq_ref, sinq_ref, cosk_ref, sink_ref,
         qg_ref, kg_ref, bdq_ref, bdk_ref, o_ref, ks_ref, vs_ref) = refs
    else:
        (q_ref, k_ref, v_ref, qg_ref, kg_ref, bdq_ref, bdk_ref,
         o_ref, nk_ref, nv_ref, ks_ref, vs_ref) = refs

    @pl.when(pl.program_id(1) == 0)
    def _():
        k = k_ref[...]
        kn = k * _group_inv_rms(k, bdk_ref[...]) * kg_ref[...]
        v = v_ref[...]
        if latent:
            kn = _rope(kn, cosk_ref[...], sink_ref[...])
            ck = ck_ref[...]
            cv = cv_ref[...]
        else:
            nk_ref[...] = kn
            nv_ref[...] = v
        for h in range(N_KV_HEADS):
            sl = slice(HEAD_DIM * h, HEAD_DIM * (h + 1))
            if latent:
                ks_ref[h, :t_past, :] = ck[:, sl].astype(BF16)
                vs_ref[h, :t_past, :HEAD_DIM] = cv[:, sl].astype(BF16)
            ks_ref[h, t_past:, :] = kn[:, sl].astype(BF16)
            vs_ref[h, t_past:, :HEAD_DIM] = v[:, sl].astype(BF16)
            vs_ref[h, :, HEAD_DIM:] = jnp.ones((t_past + t_new, LANES - HEAD_DIM), BF16)

    q = q_ref[...]
    qn = q * _group_inv_rms(q, bdq_ref[...]) * qg_ref[...]
    if latent:
        qn = _rope(qn, cosq_ref[...], sinq_ref[...])
    qn = qn * (HEAD_DIM ** -0.5 * LOG2_E)
    group = N_Q_HEADS // N_KV_HEADS
    outs = []

    def scores(h):
        qh = jnp.concatenate(
            [qn[:, HEAD_DIM * (group * h + g):HEAD_DIM * (group * h + g + 1)] for g in range(group)],
            axis=0).astype(BF16)
        return _mm_nt(qh, ks_ref[h])

    s_next = scores(0)
    for h in range(N_KV_HEADS):
        s = s_next
        if h + 1 < N_KV_HEADS:
            s_next = scores(h + 1)
        m = jnp.max(s, axis=-1, keepdims=True)
        p = jnp.exp2(s - m).astype(BF16)
        od = _mm(p, vs_ref[h])
        o = (od / pltpu.roll(od, LANES - HEAD_DIM, axis=1))[:, :HEAD_DIM]
        outs.extend(o[tq * g:tq * (g + 1)] for g in range(group))
    o_ref[...] = jnp.concatenate(outs, axis=1)


def _block_diag_ones(width):
    idx = np.arange(width) // HEAD_DIM
    return jnp.asarray(idx[:, None] == idx[None, :], dtype=BF16)


def _attention(za, qg, kg, *, batch, t_new, latent, cache_k=None, cache_v=None, layer=0, rope_tabs=None):
    tq = min(512, t_new)
    nq = t_new // tq
    t_past = cache_k.shape[2] if latent else 0
    t_keys = t_past + t_new
    kcol = ATTN_WIDTH // KV_WIDTH
    bdq = _block_diag_ones(ATTN_WIDTH)
    bdk = _block_diag_ones(KV_WIDTH)
    const = lambda shape: pl.BlockSpec(shape, lambda b, i: (0,) * len(shape))
    in_specs = [
        pl.BlockSpec((tq, ATTN_WIDTH), lambda b, i: (b * nq + i, 0)),
        pl.BlockSpec((t_new, KV_WIDTH), lambda b, i: (b, kcol)),
        pl.BlockSpec((t_new, KV_WIDTH), lambda b, i: (b, kcol + 1)),
    ]
    args = [za, za, za]
    if latent:
        cos_t, sin_t = rope_tabs
        in_specs += [
            pl.BlockSpec((None, None, t_past, KV_WIDTH), lambda b, i: (b, layer, 0, 0)),
            pl.BlockSpec((None, None, t_past, KV_WIDTH), lambda b, i: (b, layer, 0, 0)),
            pl.BlockSpec((tq, ATTN_WIDTH), lambda b, i: (i, 0)),
            pl.BlockSpec((tq, ATTN_WIDTH), lambda b, i: (i, 0)),
            pl.BlockSpec((t_new, KV_WIDTH), lambda b, i: (0, 0)),
            pl.BlockSpec((t_new, KV_WIDTH), lambda b, i: (0, 0)),
        ]
        args += [cache_k, cache_v, cos_t, sin_t, cos_t, sin_t]
    in_specs += [_layer_spec((1, ATTN_WIDTH), layer), _layer_spec((1, KV_WIDTH), layer),
                 const((ATTN_WIDTH, ATTN_WIDTH)), const((KV_WIDTH, KV_WIDTH))]
    args += [qg, kg, bdq, bdk]
    out_specs = [pl.BlockSpec((tq, ATTN_WIDTH), lambda b, i: (b * nq + i, 0))]
    out_shape = [jax.ShapeDtypeStruct((batch * t_new, ATTN_WIDTH), F32)]
    if not latent:
        out_specs += [pl.BlockSpec((t_new, KV_WIDTH), lambda b, i: (b, 0))] * 2
        out_shape += [jax.ShapeDtypeStruct((batch * t_new, KV_WIDTH), F32)] * 2
    return pl.pallas_call(
        functools.partial(_attn_kernel, tq=tq, t_new=t_new, t_past=t_past, latent=latent),
        grid=(batch, nq),
        in_specs=in_specs,
        out_specs=out_specs,
        out_shape=out_shape,
        scratch_shapes=[pltpu.VMEM((N_KV_HEADS, t_keys, HEAD_DIM), BF16),
                        pltpu.VMEM((N_KV_HEADS, t_keys, LANES), BF16)],
        compiler_params=_cparams(2),
        name="attention_latent" if latent else "attention_context",
    )(*args)


def _running_max_lanes(x, reverse):
    n = x.shape[1]
    lane = lax.broadcasted_iota(jnp.int32, x.shape, 1)
    k = 1
    while k < n:
        if reverse:
            cand = jnp.where(lane < n - k, pltpu.roll(x, n - k, axis=1), NEG_BIG)
        else:
            cand = jnp.where(lane >= k, pltpu.roll(x, k, axis=1), NEG_BIG)
        x = jnp.maximum(x, cand)
        k *= 2
    return x


def _mlstm_kernel(*refs, t, chunk, has_state, emit_state):
    refs = list(refs)
    zm_ref, gt_ref, cw_ref, gb_ref, mg_ref, bd_ref, fu_ref, dft_c_ref, dft_ct_ref, dft_st_ref = refs[:10]
    refs = refs[10:]
    if has_state:
        c0_ref, m0_ref = refs[:2]
        refs = refs[2:]
    o_ref, fo_ref = refs[:2]
    refs = refs[2:]
    if emit_state:
        cs_ref, ms_ref = refs[:2]
        refs = refs[2:]
    hst_ref = refs[0]

    width = M_HEADS * M_DK
    z = zm_ref[...]
    x = z[:, :2 * width]
    row = lax.broadcasted_iota(jnp.int32, (t, 1), 0)
    x_prev = jnp.where(row == 0, 0.0, pltpu.roll(x, 1, axis=0))
    x_next = jnp.where(row == t - 1, 0.0, pltpu.roll(x, t - 1, axis=0))
    cw = cw_ref[...]
    qk = _silu(x_prev * cw[0:1, :] + x * cw[1:2, :] + x_next * cw[2:3, :])
    q = qk[:, :width]
    k = qk[:, width:] * (M_DK ** -0.5)
    v = z[:, 2 * width:2 * width + M_WIDTH]
    om = z[:, 2 * width + M_WIDTH:]
    heads = [slice(M_DK * h, M_DK * (h + 1)) for h in range(M_HEADS)]
    q_rows = [q[:, hs].astype(BF16) for hs in heads]
    k_rows = [k[:, hs].astype(BF16) for hs in heads]
    qt = q.T.astype(BF16)
    vt = v.T
    ones_r = jnp.ones((LANES - M_DV, t), F32)
    vaug_t = [jnp.concatenate([vt[hs, :], ones_r], axis=0) for hs in heads]
    vaug_tb = [a.astype(BF16) for a in vaug_t]

    gates = gt_ref[...] + gb_ref[...]
    lane = lax.broadcasted_iota(jnp.int32, (1, GATE_PAD), 1)
    is_forget = (lane & M_HEADS) != 0
    gates = jnp.where(is_forget, _log_sigmoid(gates), gates)
    gates_t = gates.T[:N_GATES, :]

    ri = lax.broadcasted_iota(jnp.int32, (chunk, chunk), 0)
    ci = lax.broadcasted_iota(jnp.int32, (chunk, chunk), 1)
    row_le_col = ri <= ci
    row_ge_col = ri >= ci

    n_chunks = t // chunk
    g_all = jnp.concatenate([gates_t[:, chunk * c:chunk * (c + 1)] for c in range(n_chunks)], axis=0)
    i_all = pltpu.roll(g_all, M_HEADS, axis=0)
    rows = lax.broadcasted_iota(jnp.int32, (N_GATES * n_chunks, 1), 0)
    is_fwd_row = (rows & (2 * M_HEADS)) == 0
    cum_p = _mm_left_f32(g_all, row_le_col.astype(BF16))
    cum_s = _mm_left_f32(g_all, row_ge_col.astype(BF16))
    b_all = jnp.where(is_fwd_row, cum_p, cum_s)
    u_all = i_all - b_all
    pm_all = jnp.where(is_fwd_row, _running_max_lanes(u_all, reverse=False),
                       _running_max_lanes(u_all, reverse=True))
    pml_all = jnp.where(is_fwd_row, pm_all[:, chunk - 1:], pm_all[:, :1])
    bl_all = jnp.where(is_fwd_row, b_all[:, chunk - 1:], b_all[:, :1])
    wk_all = jnp.exp(u_all - pml_all)
    u_cols = u_all.T

    if has_state:
        m_state = [m0_ref[M_HEADS * d:M_HEADS * (d + 1), 0:1] for d in range(2)]
        c_state = [[c0_ref[M_HEADS * d + h] for h in range(M_HEADS)] for d in range(2)]
    else:
        m_state = [jnp.zeros((M_HEADS, 1), F32) for _ in range(2)]
        c_state = [[jnp.zeros((LANES, M_DK), F32) for _ in range(M_HEADS)] for _ in range(2)]

    steps = [(j if d == 0 else n_chunks - 1 - j, d) for j in range(n_chunks) for d in range(2)]

    def gate_rows(step):
        c, direction = step
        return N_GATES * c + (2 * direction + 1) * M_HEADS

    def key_query(step):
        rs = slice(chunk * step[0], chunk * (step[0] + 1))
        return [_mm_nt(k_rows[h][rs, :], q_rows[h][rs, :]) for h in range(M_HEADS)]

    def local_sums(step, st):
        c, direction = step
        rs = slice(chunk * c, chunk * (c + 1))
        r0 = gate_rows(step)
        valid = row_le_col if direction == 0 else row_ge_col
        s_loc, x_loc = [], []
        for h in range(M_HEADS):
            w = jnp.exp(jnp.where(valid, u_cols[:, r0 + h:r0 + h + 1] - pm_all[r0 + h:r0 + h + 1, :], NEG_BIG))
            s_loc.append(_mm(vaug_tb[h][:, rs], (st[h] * w).astype(BF16)))
            x_loc.append(_mm((vaug_t[h][:, rs] * wk_all[r0 + h:r0 + h + 1, :]).astype(BF16), k_rows[h][rs, :]))
        return s_loc, x_loc

    written = set()
    n_steps = len(steps)
    st_q = [key_query(steps[i]) for i in range(min(2, n_steps))]
    loc_q = [local_sums(steps[0], st_q.pop(0))]
    for i, (c, direction) in enumerate(steps):
        if i + 1 < n_steps:
            loc_q.append(local_sums(steps[i + 1], st_q.pop(0)))
        if i + 2 < n_steps:
            st_q.append(key_query(steps[i + 2]))
        s_loc, x_loc = loc_q.pop(0)
        rs = slice(chunk * c, chunk * (c + 1))
        r0 = gate_rows((c, direction))
        b4, pm4, bl, pml = (a[r0:r0 + M_HEADS, :] for a in (b_all, pm_all, bl_all, pml_all))
        m = m_state[direction]
        mx = jnp.maximum(m, pm4)
        f_loc = jnp.exp(pm4 - mx)
        f_int = jnp.exp(m - mx)
        floor = jnp.exp(-(b4 + mx))
        m_new = bl + jnp.maximum(m, pml)
        decay = jnp.exp(bl + m - m_new)
        gain = jnp.exp(bl + pml - m_new)
        m_state[direction] = m_new
        inter = [_mm(c_state[direction][h].astype(BF16), qt[heads[h], rs]) for h in range(M_HEADS)]
        for h in range(M_HEADS):
            nd = f_loc[h:h + 1, :] * s_loc[h] + f_int[h:h + 1, :] * inter[h]
            ht = nd[:M_DV, :] / jnp.maximum(jnp.abs(nd[M_DV:, :]), floor[h:h + 1, :])
            c_state[direction][h] = decay[h:h + 1, :] * c_state[direction][h] + gain[h:h + 1, :] * x_loc[h]
            if c in written:
                hst_ref[heads[h], rs] += ht
            else:
                hst_ref[heads[h], rs] = ht
        written.add(c)

    if emit_state:
        for d in range(2):
            ms_ref[M_HEADS * d:M_HEADS * (d + 1), :] = jnp.broadcast_to(m_state[d], (M_HEADS, LANES))
            for h in range(M_HEADS):
                cs_ref[M_HEADS * d + h] = c_state[d][h]

    hsum = hst_ref[...].T
    hn = hsum * _group_inv_rms(hsum, bd_ref[...]) * mg_ref[...]
    o_ref[...] = _sigmoid(om) * hn

    fa = _mm(fu_ref[...].astype(BF16), dft_c_ref[...])
    fo_ref[...] = (_mm(dft_ct_ref[...], fa[:, :F_WIDTH].astype(BF16))
                   - _mm(dft_st_ref[...], fa[:, F_WIDTH:].astype(BF16)))


def _mlstm_fourier(zm, gt, fu, conv_w, gate_b, m_norm_g, *, batch, t, layer, caug0=None, m0=None):
    has_state = caug0 is not None
    emit_state = not has_state
    chunk = min(M_CHUNK_K, t)
    n_units = 2 * M_HEADS
    const = lambda shape: pl.BlockSpec(shape, lambda b: (0,) * len(shape), pipeline_mode=pl.Buffered(1))
    in_specs = [
        pl.BlockSpec((t, D_MODEL), lambda b: (b, 0)),
        pl.BlockSpec((t, GATE_PAD), lambda b: (b, 0)),
        _layer_spec((3, 2 * M_HEADS * M_DK), layer),
        _layer_spec((1, GATE_PAD), layer),
        _layer_spec((1, M_WIDTH), layer),
        const((M_WIDTH, M_WIDTH)),
        pl.BlockSpec((t, F_WIDTH), lambda b: (b, 0)),
        const((F_WIDTH, 2 * F_WIDTH)), const((t, t)), const((t, t)),
    ]
    args = [zm, gt, conv_w, gate_b, m_norm_g, _block_diag_ones(M_WIDTH), fu, *_dft_tables(t)]
    if has_state:
        in_specs += [pl.BlockSpec((None, n_units, LANES, M_DK), lambda b: (b, 0, 0, 0)),
                     pl.BlockSpec((None, n_units, LANES), lambda b: (b, 0, 0))]
        args += [caug0, m0]
    out_specs = [pl.BlockSpec((t, M_WIDTH), lambda b: (b, 0)), pl.BlockSpec((t, F_WIDTH), lambda b: (b, 0))]
    out_shape = [jax.ShapeDtypeStruct((batch * t, M_WIDTH), F32), jax.ShapeDtypeStruct((batch * t, F_WIDTH), F32)]
    if emit_state:
        out_specs += [pl.BlockSpec((None, n_units, LANES, M_DK), lambda b: (b, 0, 0, 0)),
                      pl.BlockSpec((None, n_units, LANES), lambda b: (b, 0, 0))]
        out_shape += [jax.ShapeDtypeStruct((batch, n_units, LANES, M_DK), F32),
                      jax.ShapeDtypeStruct((batch, n_units, LANES), F32)]
    return pl.pallas_call(
        functools.partial(_mlstm_kernel, t=t, chunk=chunk, has_state=has_state, emit_state=emit_state),
        grid=(batch,),
        in_specs=in_specs,
        out_specs=out_specs,
        out_shape=out_shape,
        scratch_shapes=[pltpu.VMEM((M_WIDTH, t), F32)],
        compiler_params=_cparams(1),
        name="mlstm_fourier_latent" if has_state else "mlstm_fourier_context",
    )(*args)


def _dft_tables(t):
    kt = (np.arange(t)[:, None] * np.arange(t)[None, :]) % t
    ang_t = 2.0 * np.pi * kt.astype(np.float64) / t
    ct = np.cos(ang_t) / np.sqrt(t)
    st = np.sin(ang_t) / np.sqrt(t)
    c = F_GROUP_CH
    kc = (np.arange(c)[:, None] * np.arange(c)[None, :]) % c
    ang_c = 2.0 * np.pi * kc.astype(np.float64) / c
    eye = np.eye(F_GROUPS)
    cc = np.kron(eye, np.cos(ang_c) / np.sqrt(c))
    sc = np.kron(eye, np.sin(ang_c) / np.sqrt(c))
    cs = np.concatenate([cc, sc], axis=1)
    to_dev = lambda a: jnp.asarray(a.astype(np.float32)).astype(BF16)
    return to_dev(cs), to_dev(ct), to_dev(st)


def _merge_kernel(x_ref, mod_ref, g_ref, a_ref, hm_ref, fo_ref,
                  wbg_ref, wpa_ref, wpm_ref, wpf_ref, wo_ref, o_ref):
    x = x_ref[...]
    h = _normmod(x, g_ref[...], mod_ref[1:2, :], mod_ref[0:1, :]).astype(BF16)

    def branch(j, y_ref, w_ref):
        gate = _sigmoid(_mm_nt(h, wbg_ref[D_MODEL * j:D_MODEL * (j + 1), :]))
        return gate * _mm(y_ref[...].astype(BF16), w_ref[...])

    merged = branch(0, a_ref, wpa_ref) + branch(1, hm_ref, wpm_ref) + branch(2, fo_ref, wpf_ref)
    o_ref[...] = x + mod_ref[2:3, :] * _mm(merged.astype(BF16), wo_ref[...])


def _merge(x, mod, g, attn, hm, fo, w_bg, w_pa, w_pm, w_pf, w_out, sample_of_tile, tm, layer):
    rows = x.shape[0]
    row_spec = lambda width: pl.BlockSpec((tm, width), lambda i: (i, 0))
    const = lambda shape: _layer_spec(shape, layer)
    return pl.pallas_call(
        _merge_kernel,
        grid=(rows // tm,),
        in_specs=[
            row_spec(D_MODEL),
            pl.BlockSpec((None, None, N_MOD, D_MODEL), lambda i: (layer, sample_of_tile(i), 0, 0)),
            const((1, D_MODEL)),
            row_spec(ATTN_WIDTH), row_spec(M_WIDTH), row_spec(F_WIDTH),
            const((3 * D_MODEL, D_MODEL)), const((ATTN_WIDTH, D_MODEL)),
            const((M_WIDTH, D_MODEL)), const((F_WIDTH, D_MODEL)), const((D_MODEL, D_MODEL)),
        ],
        out_specs=row_spec(D_MODEL),
        out_shape=jax.ShapeDtypeStruct((rows, D_MODEL), F32),
        compiler_params=_cparams(1),
        name="merge",
    )(x, mod, g, attn, hm, fo, w_bg, w_pa, w_pm, w_pf, w_out)


def _ffn_kernel(x_ref, mod_ref, g_ref, win_ref, wout_ref, o_ref):
    x = x_ref[...]
    h = _normmod(x, g_ref[...], mod_ref[4:5, :], mod_ref[3:4, :]).astype(BF16)
    u = _mm(h, win_ref[...])
    a = (_silu(u[:, :FF_HIDDEN]) * u[:, FF_HIDDEN:]).astype(BF16)
    o_ref[...] = x + mod_ref[5:6, :] * _mm(a, wout_ref[...])


def _ffn(x, mod, g, w_in, w_out, sample_of_tile, tm, layer):
    rows = x.shape[0]
    row_spec = pl.BlockSpec((tm, D_MODEL), lambda i: (i, 0))
    const = lambda shape: _layer_spec(shape, layer)
    return pl.pallas_call(
        _ffn_kernel,
        grid=(rows // tm,),
        in_specs=[
            row_spec,
            pl.BlockSpec((None, None, N_MOD, D_MODEL), lambda i: (layer, sample_of_tile(i), 0, 0)),
            const((1, D_MODEL)),
            const((D_MODEL, 2 * FF_HIDDEN)), const((FF_HIDDEN, D_MODEL)),
        ],
        out_specs=row_spec,
        out_shape=jax.ShapeDtypeStruct((rows, D_MODEL), F32),
        compiler_params=_cparams(1),
        name="ffn",
    )(x, mod, g, w_in, w_out)


def _rope_tables(t):
    n = HEAD_DIM // 4
    inv = 1.0 / (ROPE_THETA ** (np.arange(n, dtype=np.float64) / n))
    pos = np.arange(t)
    ang_r = (pos // GRID_W)[:, None] * inv[None, :]
    ang_c = (pos % GRID_W)[:, None] * inv[None, :]
    cos = np.concatenate([np.cos(ang_r)] * 2 + [np.cos(ang_c)] * 2, axis=1)
    sin = np.concatenate([-np.sin(ang_r), np.sin(ang_r), -np.sin(ang_c), np.sin(ang_c)], axis=1)
    tile = lambda a: jnp.asarray(np.tile(a, (1, N_Q_HEADS)).astype(np.float32))
    return tile(cos), tile(sin)


def _layer(x, mod, sample_of_tile, wts, *, batch, t, latent, layer, ctx=None, rope_tabs=None):
    (norm1_g, w_mix, w_bg, qg, kg, conv_w, gate_b, m_norm_g, w_pa, w_pm, w_pf, w_out,
     norm2_g, w_ffn_in, w_ffn_out) = wts
    tm = 512
    za, zm, fu, gt = _in_proj(x, mod, norm1_g, w_mix, sample_of_tile(tm), tm, layer)
    extra = ()
    if latent:
        cache_k, cache_v, caug0, m0 = ctx
        (attn,) = _attention(za, qg, kg, batch=batch, t_new=t, latent=True, cache_k=cache_k,
                             cache_v=cache_v, layer=layer, rope_tabs=rope_tabs)
        hm, fo = _mlstm_fourier(zm, gt, fu, conv_w, gate_b, m_norm_g, batch=batch, t=t, layer=layer,
                                caug0=caug0, m0=m0)
    else:
        attn, new_k, new_v = _attention(za, qg, kg, batch=batch, t_new=t, latent=False, layer=layer)
        hm, fo, cs, ms = _mlstm_fourier(zm, gt, fu, conv_w, gate_b, m_norm_g, batch=batch, t=t, layer=layer)
        extra = (new_k, new_v, cs, ms)
    x = _merge(x, mod, norm1_g, attn, hm, fo, w_bg, w_pa, w_pm, w_pf, w_out, sample_of_tile(tm), tm,
               layer)
    tm_ffn = 512
    x = _ffn(x, mod, norm2_g, w_ffn_in, w_ffn_out, sample_of_tile(tm_ffn), tm_ffn, layer)
    return x, extra


def kernel(x_prompt, x_sample, cache_k, cache_v, state_C, state_n, state_m, c, c_ctx, w_ada, b_ada,
           norm1_g, w_in, q_norm_g, k_norm_g, m_conv_w, m_gate_b, m_norm_g, w_proj_attn,
           w_proj_mlstm, w_proj_fourier, w_out, norm2_g, w_ffn_in, w_ffn_out):
    n_ctx, t_ctx, _ = x_prompt.shape
    n_lat, t_lat, _ = x_sample.shape
    t_past = cache_k.shape[2]
    n_units = 2 * M_HEADS

    cvec = jnp.concatenate([c_ctx[None, :], c], axis=0)
    cvec = jnp.pad(cvec, ((0, MOD_ROWS - cvec.shape[0]), (0, 0)))
    mod = _modulation(cvec, w_ada, b_ada).reshape(DEPTH, MOD_ROWS, N_MOD, D_MODEL)

    rope_tabs = _rope_tables(t_lat)
    ck = cache_k.reshape(n_lat, DEPTH, t_past, KV_WIDTH)
    cv = cache_v.reshape(n_lat, DEPTH, t_past, KV_WIDTH)

    ctx_sample = lambda tm: (lambda i: 0)
    lat_sample = lambda tm: (lambda i: 1 + i // (t_lat // tm))

    xp = x_prompt.reshape(n_ctx * t_ctx, D_MODEL)
    xs = x_sample.reshape(n_lat * t_lat, D_MODEL)
    ks, vs, cs_all, ms_all = [], [], [], []
    w_in_t = jnp.swapaxes(w_in, 1, 2)
    w_mix = w_in_t[:, :BG_OFF].astype(BF16)
    w_bg = w_in_t[:, BG_OFF:].astype(BF16)
    row = lambda a: a[:, None, :]
    wts = (row(norm1_g), w_mix, w_bg,
           row(jnp.tile(q_norm_g, (1, N_Q_HEADS))), row(jnp.tile(k_norm_g, (1, N_KV_HEADS))),
           m_conv_w, row(jnp.pad(m_gate_b, ((0, 0), (0, GATE_PAD - N_GATES)))), row(m_norm_g),
           w_proj_attn.astype(BF16), w_proj_mlstm.astype(BF16), w_proj_fourier.astype(BF16),
           w_out.astype(BF16), row(norm2_g), w_ffn_in.astype(BF16), w_ffn_out.astype(BF16))
    for l in range(DEPTH):
        xp, (k_l, v_l, cs, ms) = _layer(xp, mod, ctx_sample, wts, batch=n_ctx, t=t_ctx,
                                        latent=False, layer=l)
        ks.append(k_l.reshape(n_ctx, t_ctx, N_KV_HEADS, HEAD_DIM))
        vs.append(v_l.reshape(n_ctx, t_ctx, N_KV_HEADS, HEAD_DIM))
        cs_all.append(cs)
        ms_all.append(ms)

        c0t = jnp.swapaxes(state_C[:, l].astype(F32).reshape(n_lat, n_units, M_DK, M_DV), -1, -2)
        n0 = state_n[:, l].astype(F32).reshape(n_lat, n_units, 1, M_DK)
        caug0 = jnp.concatenate(
            [c0t, jnp.broadcast_to(n0, (n_lat, n_units, LANES - M_DV, M_DK))], axis=-2)
        m0 = jnp.broadcast_to(state_m[:, l].astype(F32).reshape(n_lat, n_units, 1),
                              (n_lat, n_units, LANES))
        xs, _ = _layer(xs, mod, lat_sample, wts, batch=n_lat, t=t_lat, latent=True, layer=l,
                       ctx=(ck, cv, caug0, m0), rope_tabs=rope_tabs)

    cs_all = jnp.stack(cs_all, axis=1)
    ms_all = jnp.stack(ms_all, axis=1)
    new_c = jnp.swapaxes(cs_all[..., :M_DV, :], -1, -2).reshape(n_ctx, DEPTH, 2, M_HEADS, M_DK, M_DV)
    new_n = cs_all[..., M_DV, :].reshape(n_ctx, DEPTH, 2, M_HEADS, M_DK)
    new_m = ms_all[..., 0].reshape(n_ctx, DEPTH, 2, M_HEADS)
    return (xp.reshape(n_ctx, t_ctx, D_MODEL), xs.reshape(n_lat, t_lat, D_MODEL),
            jnp.stack(ks, axis=1), jnp.stack(vs, axis=1), new_c, new_n, new_m)
```

```python
import functools

import numpy as np
import jax
import jax.numpy as jnp
from jax import lax
from jax.experimental import pallas as pl
from jax.experimental.pallas import tpu as pltpu

D_MODEL = 1024
DEPTH = 2
GRID_W = 64
HEAD_DIM = 64
N_Q_HEADS = 8
N_KV_HEADS = 4
ATTN_WIDTH = N_Q_HEADS * HEAD_DIM
KV_WIDTH = N_KV_HEADS * HEAD_DIM
ROPE_THETA = 10000.0
M_HEADS = 4
M_DK = 64
M_DV = 64
M_WIDTH = M_HEADS * M_DV
F_GROUPS = 4
F_GROUP_CH = 64
F_WIDTH = F_GROUPS * F_GROUP_CH
FF_HIDDEN = -(-8 * D_MODEL // (3 * 256)) * 256
EPS = 1e-6
N_GATES = 4 * M_HEADS
IN_SIZES = (ATTN_WIDTH, KV_WIDTH, KV_WIDTH, M_HEADS * M_DK, M_HEADS * M_DK, M_WIDTH, M_WIDTH,
            N_GATES, F_WIDTH, 3 * D_MODEL)
IN_OFFS = tuple(int(v) for v in np.cumsum((0,) + IN_SIZES))

LANES = 128
GATE_PAD = LANES
N_MOD = 6
MOD_ROWS = 16
M_CHUNK_K = 128
NEG_BIG = -1e30
LOG2_E = 1.4426950408889634
VMEM_LIMIT = 56 * 1024 * 1024

F32 = jnp.float32
BF16 = jnp.bfloat16


def _cparams(n_axes):
    return pltpu.CompilerParams(dimension_semantics=("arbitrary",) * n_axes,
                                vmem_limit_bytes=VMEM_LIMIT)


def _layer_spec(shape, layer):
    return pl.BlockSpec((None,) + tuple(shape), lambda *_: (layer,) + (0,) * len(shape),
                        pipeline_mode=pl.Buffered(1))


def _mm(a, b):
    return jnp.dot(a, b, preferred_element_type=F32)


def _mm_nt(a, b):
    return lax.dot_general(a, b, (((1,), (1,)), ((), ())), preferred_element_type=F32)


def _split3(x):
    hi = x.astype(BF16)
    r = x - hi.astype(F32)
    mid = r.astype(BF16)
    lo = (r - mid.astype(F32)).astype(BF16)
    return hi, mid, lo


def _mm_left_f32(x, m_bf16):
    hi, mid, lo = _split3(x)
    return _mm(hi, m_bf16) + _mm(mid, m_bf16) + _mm(lo, m_bf16)


def _sigmoid(x):
    return 1.0 / (1.0 + jnp.exp(-x))


def _silu(x):
    return x * _sigmoid(x)


def _log_sigmoid(x):
    return jnp.minimum(x, 0.0) - jnp.log(1.0 + jnp.exp(-jnp.abs(x)))


def _normmod(x, g, scale, shift):
    ms = jnp.mean(x * x, axis=-1, keepdims=True)
    return (x * lax.rsqrt(ms + EPS)) * g * (1.0 + scale) + shift


def _group_inv_rms(x, bd):
    x2 = x * x
    hi = x2.astype(BF16)
    lo = (x2 - hi.astype(F32)).astype(BF16)
    ss = _mm(hi, bd) + _mm(lo, bd)
    return lax.rsqrt(ss * (1.0 / HEAD_DIM) + EPS)


def _rope(x, cos, sin_signed):
    w = x.shape[1]
    lane = lax.broadcasted_iota(jnp.int32, x.shape, 1)
    up = pltpu.roll(x, w - 16, axis=1)
    dn = pltpu.roll(x, 16, axis=1)
    partner = jnp.where((lane & 31) < 16, up, dn)
    return x * cos + partner * sin_signed


def _mod_kernel(c_ref, w_ref, b_ref, o_ref):
    s = _silu(c_ref[...]).astype(BF16)
    o_ref[...] = _mm(s, w_ref[...].astype(BF16)) + b_ref[...]


def _modulation(cvec, w_ada, b_ada):
    tn = 512
    n_out = N_MOD * D_MODEL
    return pl.pallas_call(
        _mod_kernel,
        grid=(DEPTH, n_out // tn),
        in_specs=[
            pl.BlockSpec((MOD_ROWS, D_MODEL), lambda l, j: (0, 0)),
            pl.BlockSpec((None, D_MODEL, tn), lambda l, j: (l, 0, j)),
            pl.BlockSpec((None, 1, tn), lambda l, j: (l, 0, j)),
        ],
        out_specs=pl.BlockSpec((None, MOD_ROWS, tn), lambda l, j: (l, 0, j)),
        out_shape=jax.ShapeDtypeStruct((DEPTH, MOD_ROWS, n_out), F32),
        compiler_params=_cparams(2),
        name="modulation",
    )(cvec, w_ada, b_ada.reshape(DEPTH, 1, n_out))


GATE_OFF, FU_OFF, BG_OFF = IN_OFFS[7], IN_OFFS[8], IN_OFFS[9]


def _in_kernel(x_ref, mod_ref, g_ref, w_ref, za_ref, zm_ref, fu_ref, gt_ref):
    h = _normmod(x_ref[...], g_ref[...], mod_ref[1:2, :], mod_ref[0:1, :])
    z = _mm_nt(h.astype(BF16), w_ref[...])
    za_ref[...] = z[:, :D_MODEL]
    zm_ref[...] = z[:, D_MODEL:GATE_OFF]
    gt_ref[...] = z[:, GATE_OFF:GATE_OFF + GATE_PAD]
    fu_ref[...] = z[:, FU_OFF:FU_OFF + F_WIDTH].astype(BF16)


def _in_proj(x, mod, g, w_mix, sample_of_tile, tm, layer):
    rows = x.shape[0]
    row_spec = lambda width: pl.BlockSpec((tm, width), lambda i: (i, 0))
    return pl.pallas_call(
        _in_kernel,
        grid=(rows // tm,),
        in_specs=[
            row_spec(D_MODEL),
            pl.BlockSpec((None, None, N_MOD, D_MODEL), lambda i: (layer, sample_of_tile(i), 0, 0)),
            _layer_spec((1, D_MODEL), layer),
            _layer_spec((BG_OFF, D_MODEL), layer),
        ],
        out_specs=[row_spec(D_MODEL), row_spec(D_MODEL), row_spec(F_WIDTH), row_spec(GATE_PAD)],
        out_shape=[
            jax.ShapeDtypeStruct((rows, D_MODEL), F32),
            jax.ShapeDtypeStruct((rows, D_MODEL), F32),
            jax.ShapeDtypeStruct((rows, F_WIDTH), BF16),
            jax.ShapeDtypeStruct((rows, GATE_PAD), F32),
        ],
        compiler_params=_cparams(1),
        name="in_proj",
    )(x, mod, g, w_mix)


def _attn_kernel(*refs, tq, t_new, t_past, latent):
    if latent:
        (q_ref, k_ref, v_ref, ck_ref, cv_ref, cosq_ref, sinq_ref, cosk_ref, sink_ref,
         qg_ref, kg_ref, bdq_ref, bdk_ref, o_ref, ks_ref, vs_ref) = refs
    else:
        (q_ref, k_ref, v_ref, qg_ref, kg_ref, bdq_ref, bdk_ref,
         o_ref, nk_ref, nv_ref, ks_ref, vs_ref) = refs

    @pl.when(pl.program_id(1) == 0)
    def _():
        k = k_ref[...]
        kn = k * _group_inv_rms(k, bdk_ref[...]) * kg_ref[...]
        v = v_ref[...]
        if latent:
            kn = _rope(kn, cosk_ref[...], sink_ref[...])
            ck = ck_ref[...]
            cv = cv_ref[...]
        else:
            nk_ref[...] = kn
            nv_ref[...] = v
        for h in range(N_KV_HEADS):
            sl = slice(HEAD_DIM * h, HEAD_DIM * (h + 1))
            if latent:
                ks_ref[h, :t_past, :] = ck[:, sl].astype(BF16)
                vs_ref[h, :t_past, :HEAD_DIM] = cv[:, sl].astype(BF16)
            ks_ref[h, t_past:, :] = kn[:, sl].astype(BF16)
            vs_ref[h, t_past:, :HEAD_DIM] = v[:, sl].astype(BF16)
            vs_ref[h, :, HEAD_DIM:] = jnp.ones((t_past + t_new, LANES - HEAD_DIM), BF16)

    q = q_ref[...]
    qn = q * _group_inv_rms(q, bdq_ref[...]) * qg_ref[...]
    if latent:
        qn = _rope(qn, cosq_ref[...], sinq_ref[...])
    qn = qn * (HEAD_DIM ** -0.5 * LOG2_E)
    group = N_Q_HEADS // N_KV_HEADS
    outs = []

    def scores(h):
        qh = jnp.concatenate(
            [qn[:, HEAD_DIM * (group * h + g):HEAD_DIM * (group * h + g + 1)] for g in range(group)],
            axis=0).astype(BF16)
        return _mm_nt(qh, ks_ref[h])

    s_next = scores(0)
    for h in range(N_KV_HEADS):
        s = s_next
        if h + 1 < N_KV_HEADS:
            s_next = scores(h + 1)
        m = jnp.max(s, axis=-1, keepdims=True)
        p = jnp.exp2(s - m).astype(BF16)
        od = _mm(p, vs_ref[h])
        o = (od / pltpu.roll(od, LANES - HEAD_DIM, axis=1))[:, :HEAD_DIM]
        outs.extend(o[tq * g:tq * (g + 1)] for g in range(group))
    o_ref[...] = jnp.concatenate(outs, axis=1).astype(BF16)


def _block_diag_ones(width):
    idx = np.arange(width) // HEAD_DIM
    return jnp.asarray(idx[:, None] == idx[None, :], dtype=BF16)


def _attention(za, qg, kg, *, batch, t_new, latent, cache_k=None, cache_v=None, layer=0, rope_tabs=None):
    tq = min(512, t_new)
    nq = t_new // tq
    t_past = cache_k.shape[2] if latent else 0
    t_keys = t_past + t_new
    kcol = ATTN_WIDTH // KV_WIDTH
    bdq = _block_diag_ones(ATTN_WIDTH)
    bdk = _block_diag_ones(KV_WIDTH)
    const = lambda shape: pl.BlockSpec(shape, lambda b, i: (0,) * len(shape))
    in_specs = [
        pl.BlockSpec((tq, ATTN_WIDTH), lambda b, i: (b * nq + i, 0)),
        pl.BlockSpec((t_new, KV_WIDTH), lambda b, i: (b, kcol)),
        pl.BlockSpec((t_new, KV_WIDTH), lambda b, i: (b, kcol + 1)),
    ]
    args = [za, za, za]
    if latent:
        cos_t, sin_t = rope_tabs
        in_specs += [
            pl.BlockSpec((None, None, t_past, KV_WIDTH), lambda b, i: (b, layer, 0, 0)),
            pl.BlockSpec((None, None, t_past, KV_WIDTH), lambda b, i: (b, layer, 0, 0)),
            pl.BlockSpec((tq, ATTN_WIDTH), lambda b, i: (i, 0)),
            pl.BlockSpec((tq, ATTN_WIDTH), lambda b, i: (i, 0)),
            pl.BlockSpec((t_new, KV_WIDTH), lambda b, i: (0, 0)),
            pl.BlockSpec((t_new, KV_WIDTH), lambda b, i: (0, 0)),
        ]
        args += [cache_k, cache_v, cos_t, sin_t, cos_t, sin_t]
    in_specs += [_layer_spec((1, ATTN_WIDTH), layer), _layer_spec((1, KV_WIDTH), layer),
                 const((ATTN_WIDTH, ATTN_WIDTH)), const((KV_WIDTH, KV_WIDTH))]
    args += [qg, kg, bdq, bdk]
    out_specs = [pl.BlockSpec((tq, ATTN_WIDTH), lambda b, i: (b * nq + i, 0))]
    out_shape = [jax.ShapeDtypeStruct((batch * t_new, ATTN_WIDTH), BF16)]
    if not latent:
        out_specs += [pl.BlockSpec((t_new, KV_WIDTH), lambda b, i: (b, 0))] * 2
        out_shape += [jax.ShapeDtypeStruct((batch * t_new, KV_WIDTH), F32)] * 2
    return pl.pallas_call(
        functools.partial(_attn_kernel, tq=tq, t_new=t_new, t_past=t_past, latent=latent),
        grid=(batch, nq),
        in_specs=in_specs,
        out_specs=out_specs,
        out_shape=out_shape,
        scratch_shapes=[pltpu.VMEM((N_KV_HEADS, t_keys, HEAD_DIM), BF16),
                        pltpu.VMEM((N_KV_HEADS, t_keys, LANES), BF16)],
        compiler_params=_cparams(2),
        name="attention_latent" if latent else "attention_context",
    )(*args)


def _running_max_sublanes(x, reverse):
    n = x.shape[0]
    row = lax.broadcasted_iota(jnp.int32, x.shape, 0)
    k = 1
    while k < n:
        if reverse:
            cand = jnp.where(row < n - k, pltpu.roll(x, n - k, axis=0), NEG_BIG)
        else:
            cand = jnp.where(row >= k, pltpu.roll(x, k, axis=0), NEG_BIG)
        x = jnp.maximum(x, cand)
        k *= 2
    return x


def _mlstm_kernel(*refs, t, chunk, has_state, emit_state):
    refs = list(refs)
    zm_ref, gt_ref, cw_ref, gb_ref, mg_ref, bd_ref, fu_ref, dft_c_ref, dft_ct_ref, dft_st_ref = refs[:10]
    refs = refs[10:]
    if has_state:
        c0_ref, m0_ref = refs[:2]
        refs = refs[2:]
    o_ref, fo_ref = refs[:2]
    refs = refs[2:]
    if emit_state:
        cs_ref, ms_ref = refs[:2]
        refs = refs[2:]
    hst_ref = refs[0]

    width = M_HEADS * M_DK
    z = zm_ref[...]
    x = z[:, :2 * width]
    row = lax.broadcasted_iota(jnp.int32, (t, 1), 0)
    x_prev = jnp.where(row == 0, 0.0, pltpu.roll(x, 1, axis=0))
    x_next = jnp.where(row == t - 1, 0.0, pltpu.roll(x, t - 1, axis=0))
    cw = cw_ref[...]
    qk = _silu(x_prev * cw[0:1, :] + x * cw[1:2, :] + x_next * cw[2:3, :])
    q = qk[:, :width]
    k = qk[:, width:] * (M_DK ** -0.5)
    v = z[:, 2 * width:2 * width + M_WIDTH]
    om = z[:, 2 * width + M_WIDTH:]
    heads = [slice(M_DK * h, M_DK * (h + 1)) for h in range(M_HEADS)]
    q_rows = [q[:, hs].astype(BF16) for hs in heads]
    k_rows = [k[:, hs].astype(BF16) for hs in heads]
    qt = q.T.astype(BF16)
    vt = v.T
    ones_r = jnp.ones((LANES - M_DV, t), F32)
    vaug_t = [jnp.concatenate([vt[hs, :], ones_r], axis=0) for hs in heads]
    vaug_tb = [a.astype(BF16) for a in vaug_t]

    gates = gt_ref[...] + gb_ref[...]
    lane = lax.broadcasted_iota(jnp.int32, (1, GATE_PAD), 1)
    is_forget = (lane & M_HEADS) != 0
    gates = jnp.where(is_forget, _log_sigmoid(gates), gates)
    gates_t = gates.T[:N_GATES, :]

    ri = lax.broadcasted_iota(jnp.int32, (chunk, chunk), 0)
    ci = lax.broadcasted_iota(jnp.int32, (chunk, chunk), 1)
    row_le_col = ri <= ci
    row_ge_col = ri >= ci

    n_chunks = t // chunk
    g_all = jnp.concatenate([gates_t[:, chunk * c:chunk * (c + 1)] for c in range(n_chunks)], axis=0)
    i_all = pltpu.roll(g_all, M_HEADS, axis=0)
    rows = lax.broadcasted_iota(jnp.int32, (N_GATES * n_chunks, 1), 0)
    is_fwd_row = (rows & (2 * M_HEADS)) == 0
    cum_p = _mm_left_f32(g_all, row_le_col.astype(BF16))
    cum_s = _mm_left_f32(g_all, row_ge_col.astype(BF16))
    b_all = jnp.where(is_fwd_row, cum_p, cum_s)
    u_all = i_all - b_all
    u_cols = u_all.T
    cols = lax.broadcasted_iota(jnp.int32, (1, N_GATES * n_chunks), 1)
    pm_cols = jnp.where((cols & (2 * M_HEADS)) == 0, _running_max_sublanes(u_cols, reverse=False),
                        _running_max_sublanes(u_cols, reverse=True))
    pm_all = pm_cols.T
    u_cols2 = u_cols * LOG2_E
    pm_all2 = pm_all * LOG2_E
    pml_all = jnp.where(is_fwd_row, pm_all[:, chunk - 1:], pm_all[:, :1])
    bl_all = jnp.where(is_fwd_row, b_all[:, chunk - 1:], b_all[:, :1])
    wk_all = jnp.exp(u_all - pml_all)

    if has_state:
        m_state = [m0_ref[M_HEADS * d:M_HEADS * (d + 1), 0:1] for d in range(2)]
        c_state = [[c0_ref[M_HEADS * d + h] for h in range(M_HEADS)] for d in range(2)]
    else:
        m_state = [jnp.zeros((M_HEADS, 1), F32) for _ in range(2)]
        c_state = [[jnp.zeros((LANES, M_DK), F32) for _ in range(M_HEADS)] for _ in range(2)]

    steps = [(j if d == 0 else n_chunks - 1 - j, d) for j in range(n_chunks) for d in range(2)]

    def gate_rows(step):
        c, direction = step
        return N_GATES * c + (2 * direction + 1) * M_HEADS

    def key_query(step):
        rs = slice(chunk * step[0], chunk * (step[0] + 1))
        return [_mm_nt(k_rows[h][rs, :], q_rows[h][rs, :]) for h in range(M_HEADS)]

    def local_sums(step, st):
        c, direction = step
        rs = slice(chunk * c, chunk * (c + 1))
        r0 = gate_rows(step)
        valid = row_le_col if direction == 0 else row_ge_col
        s_loc, x_loc = [], []
        for h in range(M_HEADS):
            w = jnp.exp2(jnp.where(valid, u_cols2[:, r0 + h:r0 + h + 1] - pm_all2[r0 + h:r0 + h + 1, :], NEG_BIG))
            s_loc.append(_mm(vaug_tb[h][:, rs], (st[h] * w).astype(BF16)))
            x_loc.append(_mm((vaug_t[h][:, rs] * wk_all[r0 + h:r0 + h + 1, :]).astype(BF16), k_rows[h][rs, :]))
        return s_loc, x_loc

    written = set()
    n_steps = len(steps)
    st_q = [key_query(steps[i]) for i in range(min(2, n_steps))]
    loc_q = [local_sums(steps[0], st_q.pop(0))]
    for i, (c, direction) in enumerate(steps):
        if i + 1 < n_steps:
            loc_q.append(local_sums(steps[i + 1], st_q.pop(0)))
        if i + 2 < n_steps:
            st_q.append(key_query(steps[i + 2]))
        s_loc, x_loc = loc_q.pop(0)
        rs = slice(chunk * c, chunk * (c + 1))
        r0 = gate_rows((c, direction))
        b4, pm4, bl, pml = (a[r0:r0 + M_HEADS, :] for a in (b_all, pm_all, bl_all, pml_all))
        m = m_state[direction]
        mx = jnp.maximum(m, pm4)
        f_loc = jnp.exp(pm4 - mx)
        f_int = jnp.exp(m - mx)
        floor = jnp.exp(-(b4 + mx))
        m_new = bl + jnp.maximum(m, pml)
        decay = jnp.exp(bl + m - m_new)
        gain = jnp.exp(bl + pml - m_new)
        m_state[direction] = m_new
        inter = [_mm(c_state[direction][h].astype(BF16), qt[heads[h], rs]) for h in range(M_HEADS)]
        for h in range(M_HEADS):
            nd = f_loc[h:h + 1, :] * s_loc[h] + f_int[h:h + 1, :] * inter[h]
            ht = nd[:M_DV, :] / jnp.maximum(jnp.abs(nd[M_DV:, :]), floor[h:h + 1, :])
            c_state[direction][h] = decay[h:h + 1, :] * c_state[direction][h] + gain[h:h + 1, :] * x_loc[h]
            if c in written:
                hst_ref[heads[h], rs] += ht
            else:
                hst_ref[heads[h], rs] = ht
        written.add(c)

    if emit_state:
        for d in range(2):
            ms_ref[M_HEADS * d:M_HEADS * (d + 1), :] = jnp.broadcast_to(m_state[d], (M_HEADS, LANES))
            for h in range(M_HEADS):
                cs_ref[M_HEADS * d + h] = c_state[d][h]

    hsum = hst_ref[...].T
    hn = hsum * _group_inv_rms(hsum, bd_ref[...]) * mg_ref[...]
    o_ref[...] = (_sigmoid(om) * hn).astype(BF16)

    fa = _mm(fu_ref[...].astype(BF16), dft_c_ref[...])
    fo_ref[...] = (_mm(dft_ct_ref[...], fa[:, :F_WIDTH].astype(BF16))
                   - _mm(dft_st_ref[...], fa[:, F_WIDTH:].astype(BF16))).astype(BF16)


def _mlstm_fourier(zm, gt, fu, conv_w, gate_b, m_norm_g, *, batch, t, layer, caug0=None, m0=None):
    has_state = caug0 is not None
    emit_state = not has_state
    chunk = min(M_CHUNK_K, t)
    n_units = 2 * M_HEADS
    const = lambda shape: pl.BlockSpec(shape, lambda b: (0,) * len(shape), pipeline_mode=pl.Buffered(1))
    in_specs = [
        pl.BlockSpec((t, D_MODEL), lambda b: (b, 0)),
        pl.BlockSpec((t, GATE_PAD), lambda b: (b, 0)),
        _layer_spec((3, 2 * M_HEADS * M_DK), layer),
        _layer_spec((1, GATE_PAD), layer),
        _layer_spec((1, M_WIDTH), layer),
        const((M_WIDTH, M_WIDTH)),
        pl.BlockSpec((t, F_WIDTH), lambda b: (b, 0)),
        const((F_WIDTH, 2 * F_WIDTH)), const((t, t)), const((t, t)),
    ]
    args = [zm, gt, conv_w, gate_b, m_norm_g, _block_diag_ones(M_WIDTH), fu, *_dft_tables(t)]
    if has_state:
        in_specs += [pl.BlockSpec((None, n_units, LANES, M_DK), lambda b: (b, 0, 0, 0)),
                     pl.BlockSpec((None, n_units, LANES), lambda b: (b, 0, 0))]
        args += [caug0, m0]
    out_specs = [pl.BlockSpec((t, M_WIDTH), lambda b: (b, 0)), pl.BlockSpec((t, F_WIDTH), lambda b: (b, 0))]
    out_shape = [jax.ShapeDtypeStruct((batch * t, M_WIDTH), BF16), jax.ShapeDtypeStruct((batch * t, F_WIDTH), BF16)]
    if emit_state:
        out_specs += [pl.BlockSpec((None, n_units, LANES, M_DK), lambda b: (b, 0, 0, 0)),
                      pl.BlockSpec((None, n_units, LANES), lambda b: (b, 0, 0))]
        out_shape += [jax.ShapeDtypeStruct((batch, n_units, LANES, M_DK), F32),
                      jax.ShapeDtypeStruct((batch, n_units, LANES), F32)]
    return pl.pallas_call(
        functools.partial(_mlstm_kernel, t=t, chunk=chunk, has_state=has_state, emit_state=emit_state),
        grid=(batch,),
        in_specs=in_specs,
        out_specs=out_specs,
        out_shape=out_shape,
        scratch_shapes=[pltpu.VMEM((M_WIDTH, t), F32)],
        compiler_params=_cparams(1),
        name="mlstm_fourier_latent" if has_state else "mlstm_fourier_context",
    )(*args)


def _dft_tables(t):
    kt = (np.arange(t)[:, None] * np.arange(t)[None, :]) % t
    ang_t = 2.0 * np.pi * kt.astype(np.float64) / t
    ct = np.cos(ang_t) / np.sqrt(t)
    st = np.sin(ang_t) / np.sqrt(t)
    c = F_GROUP_CH
    kc = (np.arange(c)[:, None] * np.arange(c)[None, :]) % c
    ang_c = 2.0 * np.pi * kc.astype(np.float64) / c
    eye = np.eye(F_GROUPS)
    cc = np.kron(eye, np.cos(ang_c) / np.sqrt(c))
    sc = np.kron(eye, np.sin(ang_c) / np.sqrt(c))
    cs = np.concatenate([cc, sc], axis=1)
    to_dev = lambda a: jnp.asarray(a.astype(np.float32)).astype(BF16)
    return to_dev(cs), to_dev(ct), to_dev(st)


def _merge_kernel(x_ref, mod_ref, g_ref, a_ref, hm_ref, fo_ref,
                  wbg_ref, wpa_ref, wpm_ref, wpf_ref, wo_ref, o_ref):
    x = x_ref[...]
    h = _normmod(x, g_ref[...], mod_ref[1:2, :], mod_ref[0:1, :]).astype(BF16)

    def branch(j, y_ref, w_ref):
        gate = _sigmoid(_mm_nt(h, wbg_ref[D_MODEL * j:D_MODEL * (j + 1), :]))
        return gate * _mm(y_ref[...].astype(BF16), w_ref[...])

    merged = branch(0, a_ref, wpa_ref) + branch(1, hm_ref, wpm_ref) + branch(2, fo_ref, wpf_ref)
    o_ref[...] = x + mod_ref[2:3, :] * _mm(merged.astype(BF16), wo_ref[...])


def _merge(x, mod, g, attn, hm, fo, w_bg, w_pa, w_pm, w_pf, w_out, sample_of_tile, tm, layer):
    rows = x.shape[0]
    row_spec = lambda width: pl.BlockSpec((tm, width), lambda i: (i, 0))
    const = lambda shape: _layer_spec(shape, layer)
    return pl.pallas_call(
        _merge_kernel,
        grid=(rows // tm,),
        in_specs=[
            row_spec(D_MODEL),
            pl.BlockSpec((None, None, N_MOD, D_MODEL), lambda i: (layer, sample_of_tile(i), 0, 0)),
            const((1, D_MODEL)),
            row_spec(ATTN_WIDTH), row_spec(M_WIDTH), row_spec(F_WIDTH),
            const((3 * D_MODEL, D_MODEL)), const((ATTN_WIDTH, D_MODEL)),
            const((M_WIDTH, D_MODEL)), const((F_WIDTH, D_MODEL)), const((D_MODEL, D_MODEL)),
        ],
        out_specs=row_spec(D_MODEL),
        out_shape=jax.ShapeDtypeStruct((rows, D_MODEL), F32),
        compiler_params=_cparams(1),
        name="merge",
    )(x, mod, g, attn, hm, fo, w_bg, w_pa, w_pm, w_pf, w_out)


def _ffn_kernel(x_ref, mod_ref, g_ref, win_ref, wout_ref, o_ref):
    x = x_ref[...]
    h = _normmod(x, g_ref[...], mod_ref[4:5, :], mod_ref[3:4, :]).astype(BF16)
    u = _mm(h, win_ref[...])
    a = (_silu(u[:, :FF_HIDDEN]) * u[:, FF_HIDDEN:]).astype(BF16)
    o_ref[...] = x + mod_ref[5:6, :] * _mm(a, wout_ref[...])


def _ffn(x, mod, g, w_in, w_out, sample_of_tile, tm, layer):
    rows = x.shape[0]
    row_spec = pl.BlockSpec((tm, D_MODEL), lambda i: (i, 0))
    const = lambda shape: _layer_spec(shape, layer)
    return pl.pallas_call(
        _ffn_kernel,
        grid=(rows // tm,),
        in_specs=[
            row_spec,
            pl.BlockSpec((None, None, N_MOD, D_MODEL), lambda i: (layer, sample_of_tile(i), 0, 0)),
            const((1, D_MODEL)),
            const((D_MODEL, 2 * FF_HIDDEN)), const((FF_HIDDEN, D_MODEL)),
        ],
        out_specs=row_spec,
        out_shape=jax.ShapeDtypeStruct((rows, D_MODEL), F32),
        compiler_params=_cparams(1),
        name="ffn",
    )(x, mod, g, w_in, w_out)


def _rope_tables(t):
    n = HEAD_DIM // 4
    inv = 1.0 / (ROPE_THETA ** (np.arange(n, dtype=np.float64) / n))
    pos = np.arange(t)
    ang_r = (pos // GRID_W)[:, None] * inv[None, :]
    ang_c = (pos % GRID_W)[:, None] * inv[None, :]
    cos = np.concatenate([np.cos(ang_r)] * 2 + [np.cos(ang_c)] * 2, axis=1)
    sin = np.concatenate([-np.sin(ang_r), np.sin(ang_r), -np.sin(ang_c), np.sin(ang_c)], axis=1)
    tile = lambda a: jnp.asarray(np.tile(a, (1, N_Q_HEADS)).astype(np.float32))
    return tile(cos), tile(sin)


def _layer(x, mod, sample_of_tile, wts, *, batch, t, latent, layer, ctx=None, rope_tabs=None):
    (norm1_g, w_mix, w_bg, qg, kg, conv_w, gate_b, m_norm_g, w_pa, w_pm, w_pf, w_out,
     norm2_g, w_ffn_in, w_ffn_out) = wts
    tm = 512
    za, zm, fu, gt = _in_proj(x, mod, norm1_g, w_mix, sample_of_tile(tm), tm, layer)
    extra = ()
    if latent:
        cache_k, cache_v, caug0, m0 = ctx
        (attn,) = _attention(za, qg, kg, batch=batch, t_new=t, latent=True, cache_k=cache_k,
                             cache_v=cache_v, layer=layer, rope_tabs=rope_tabs)
        hm, fo = _mlstm_fourier(zm, gt, fu, conv_w, gate_b, m_norm_g, batch=batch, t=t, layer=layer,
                                caug0=caug0, m0=m0)
    else:
        attn, new_k, new_v = _attention(za, qg, kg, batch=batch, t_new=t, latent=False, layer=layer)
        hm, fo, cs, ms = _mlstm_fourier(zm, gt, fu, conv_w, gate_b, m_norm_g, batch=batch, t=t, layer=layer)
        extra = (new_k, new_v, cs, ms)
    x = _merge(x, mod, norm1_g, attn, hm, fo, w_bg, w_pa, w_pm, w_pf, w_out, sample_of_tile(tm), tm,
               layer)
    tm_ffn = 512
    x = _ffn(x, mod, norm2_g, w_ffn_in, w_ffn_out, sample_of_tile(tm_ffn), tm_ffn, layer)
    return x, extra


def kernel(x_prompt, x_sample, cache_k, cache_v, state_C, state_n, state_m, c, c_ctx, w_ada, b_ada,
           norm1_g, w_in, q_norm_g, k_norm_g, m_conv_w, m_gate_b, m_norm_g, w_proj_attn,
           w_proj_mlstm, w_proj_fourier, w_out, norm2_g, w_ffn_in, w_ffn_out):
    n_ctx, t_ctx, _ = x_prompt.shape
    n_lat, t_lat, _ = x_sample.shape
    t_past = cache_k.shape[2]
    n_units = 2 * M_HEADS

    cvec = jnp.concatenate([c_ctx[None, :], c], axis=0)
    cvec = jnp.pad(cvec, ((0, MOD_ROWS - cvec.shape[0]), (0, 0)))
    mod = _modulation(cvec, w_ada, b_ada).reshape(DEPTH, MOD_ROWS, N_MOD, D_MODEL)

    rope_tabs = _rope_tables(t_lat)
    ck = cache_k.reshape(n_lat, DEPTH, t_past, KV_WIDTH)
    cv = cache_v.reshape(n_lat, DEPTH, t_past, KV_WIDTH)

    ctx_sample = lambda tm: (lambda i: 0)
    lat_sample = lambda tm: (lambda i: 1 + i // (t_lat // tm))

    xp = x_prompt.reshape(n_ctx * t_ctx, D_MODEL)
    xs = x_sample.reshape(n_lat * t_lat, D_MODEL)
    ks, vs, cs_all, ms_all = [], [], [], []
    w_in_t = jnp.swapaxes(w_in, 1, 2)
    w_mix = w_in_t[:, :BG_OFF].astype(BF16)
    w_bg = w_in_t[:, BG_OFF:].astype(BF16)
    row = lambda a: a[:, None, :]
    wts = (row(norm1_g), w_mix, w_bg,
           row(jnp.tile(q_norm_g, (1, N_Q_HEADS))), row(jnp.tile(k_norm_g, (1, N_KV_HEADS))),
           m_conv_w, row(jnp.pad(m_gate_b, ((0, 0), (0, GATE_PAD - N_GATES)))), row(m_norm_g),
           w_proj_attn.astype(BF16), w_proj_mlstm.astype(BF16), w_proj_fourier.astype(BF16),
           w_out.astype(BF16), row(norm2_g), w_ffn_in.astype(BF16), w_ffn_out.astype(BF16))
    for l in range(DEPTH):
        xp, (k_l, v_l, cs, ms) = _layer(xp, mod, ctx_sample, wts, batch=n_ctx, t=t_ctx,
                                        latent=False, layer=l)
        ks.append(k_l.reshape(n_ctx, t_ctx, N_KV_HEADS, HEAD_DIM))
        vs.append(v_l.reshape(n_ctx, t_ctx, N_KV_HEADS, HEAD_DIM))
        cs_all.append(cs)
        ms_all.append(ms)

        c0t = jnp.swapaxes(state_C[:, l].astype(F32).reshape(n_lat, n_units, M_DK, M_DV), -1, -2)
        n0 = state_n[:, l].astype(F32).reshape(n_lat, n_units, 1, M_DK)
        caug0 = jnp.concatenate(
            [c0t, jnp.broadcast_to(n0, (n_lat, n_units, LANES - M_DV, M_DK))], axis=-2)
        m0 = jnp.broadcast_to(state_m[:, l].astype(F32).reshape(n_lat, n_units, 1),
                              (n_lat, n_units, LANES))
        xs, _ = _layer(xs, mod, lat_sample, wts, batch=n_lat, t=t_lat, latent=True, layer=l,
                       ctx=(ck, cv, caug0, m0), rope_tabs=rope_tabs)

    cs_all = jnp.stack(cs_all, axis=1)
    ms_all = jnp.stack(ms_all, axis=1)
    new_c = jnp.swapaxes(cs_all[..., :M_DV, :], -1, -2).reshape(n_ctx, DEPTH, 2, M_HEADS, M_DK, M_DV)
    new_n = cs_all[..., M_DV, :].reshape(n_ctx, DEPTH, 2, M_HEADS, M_DK)
    new_m = ms_all[..., 0].reshape(n_ctx, DEPTH, 2, M_HEADS)
    return (xp.reshape(n_ctx, t_ctx, D_MODEL), xs.reshape(n_lat, t_lat, D_MODEL),
            jnp.stack(ks, axis=1), jnp.stack(vs, axis=1), new_c, new_n, new_m)
```

```python
import functools

import numpy as np
import jax
import jax.numpy as jnp
from jax import lax
from jax.experimental import pallas as pl
from jax.experimental.pallas import tpu as pltpu

D_MODEL = 1024
DEPTH = 2
GRID_W = 64
HEAD_DIM = 64
N_Q_HEADS = 8
N_KV_HEADS = 4
ATTN_WIDTH = N_Q_HEADS * HEAD_DIM
KV_WIDTH = N_KV_HEADS * HEAD_DIM
ROPE_THETA = 10000.0
M_HEADS = 4
M_DK = 64
M_DV = 64
M_WIDTH = M_HEADS * M_DV
F_GROUPS = 4
F_GROUP_CH = 64
F_WIDTH = F_GROUPS * F_GROUP_CH
FF_HIDDEN = -(-8 * D_MODEL // (3 * 256)) * 256
EPS = 1e-6
N_GATES = 4 * M_HEADS
IN_SIZES = (ATTN_WIDTH, KV_WIDTH, KV_WIDTH, M_HEADS * M_DK, M_HEADS * M_DK, M_WIDTH, M_WIDTH,
            N_GATES, F_WIDTH, 3 * D_MODEL)
IN_OFFS = tuple(int(v) for v in np.cumsum((0,) + IN_SIZES))

LANES = 128
GATE_PAD = LANES
N_MOD = 6
MOD_ROWS = 16
M_CHUNK_K = 128
NEG_BIG = -1e30
LOG2_E = 1.4426950408889634
VMEM_LIMIT = 56 * 1024 * 1024
TM = 512
W_STAGE = 8

F32 = jnp.float32
BF16 = jnp.bfloat16


def _cparams(n_axes):
    return pltpu.CompilerParams(dimension_semantics=("arbitrary",) * n_axes,
                                vmem_limit_bytes=VMEM_LIMIT)


def _layer_spec(shape, layer):
    return pl.BlockSpec((None,) + tuple(shape), lambda *_: (layer,) + (0,) * len(shape),
                        pipeline_mode=pl.Buffered(1))


class _Rows:
    def __init__(self, rows_ctx, t_lat, n_stage=0):
        self.n_ctx = rows_ctx // TM
        self.per_seq = t_lat // TM
        self.n_stage = n_stage

    def tile(self, i):
        return jnp.maximum(i - self.n_stage, 0)

    def both(self, i):
        return (self.tile(i), 0)

    def ctx(self, i):
        return (jnp.minimum(self.tile(i), self.n_ctx - 1), 0)

    def lat(self, i):
        return (jnp.maximum(self.tile(i) - self.n_ctx, 0), 0)

    def mod_spec(self, layer):
        def index(i):
            t = self.tile(i)
            return (layer, jnp.where(t < self.n_ctx, 0, 1 + (t - self.n_ctx) // self.per_seq), 0, 0)
        return pl.BlockSpec((None, None, N_MOD, D_MODEL), index)

    def spec(self, width, which):
        return pl.BlockSpec((TM, width), getattr(self, which))

    def is_ctx(self):
        return pl.program_id(0) - self.n_stage < self.n_ctx


def _pick(rows, ctx_ref, lat_ref):
    return jnp.where(rows.is_ctx(), ctx_ref[...], lat_ref[...])


def _staged_spec(shape, layer):
    rows, cols = shape
    return pl.BlockSpec((None, rows // W_STAGE, cols), lambda i: (layer, jnp.minimum(i, W_STAGE - 1), 0))


def _stage(w_ref, scratch_ref):
    i = pl.program_id(0)
    rc = w_ref.shape[0]

    @pl.when(i < W_STAGE)
    def _():
        scratch_ref[pl.ds(pl.multiple_of(i * rc, rc), rc), :] = w_ref[...].astype(BF16)


def _mm(a, b):
    return jnp.dot(a, b, preferred_element_type=F32)


def _mm_nt(a, b):
    return lax.dot_general(a, b, (((1,), (1,)), ((), ())), preferred_element_type=F32)


def _split3(x):
    hi = x.astype(BF16)
    r = x - hi.astype(F32)
    mid = r.astype(BF16)
    lo = (r - mid.astype(F32)).astype(BF16)
    return hi, mid, lo


def _mm_left_f32(x, m_bf16):
    hi, mid, lo = _split3(x)
    return _mm(hi, m_bf16) + _mm(mid, m_bf16) + _mm(lo, m_bf16)


def _sigmoid(x):
    return 1.0 / (1.0 + jnp.exp(-x))


def _silu(x):
    return x * _sigmoid(x)


def _log_sigmoid(x):
    return jnp.minimum(x, 0.0) - jnp.log(1.0 + jnp.exp(-jnp.abs(x)))


def _normmod(x, g, scale, shift):
    ms = jnp.mean(x * x, axis=-1, keepdims=True)
    return (x * lax.rsqrt(ms + EPS)) * g * (1.0 + scale) + shift


def _group_inv_rms(x, bd):
    x2 = x * x
    hi = x2.astype(BF16)
    lo = (x2 - hi.astype(F32)).astype(BF16)
    ss = _mm(hi, bd) + _mm(lo, bd)
    return lax.rsqrt(ss * (1.0 / HEAD_DIM) + EPS)


def _rope(x, cos, sin_signed):
    w = x.shape[1]
    lane = lax.broadcasted_iota(jnp.int32, x.shape, 1)
    up = pltpu.roll(x, w - 16, axis=1)
    dn = pltpu.roll(x, 16, axis=1)
    partner = jnp.where((lane & 31) < 16, up, dn)
    return x * cos + partner * sin_signed


def _mod_kernel(c_ref, w_ref, b_ref, o_ref):
    s = _silu(c_ref[...]).astype(BF16)
    o_ref[...] = _mm(s, w_ref[...].astype(BF16)) + b_ref[...]


def _modulation(cvec, w_ada, b_ada):
    tn = 512
    n_out = N_MOD * D_MODEL
    return pl.pallas_call(
        _mod_kernel,
        grid=(DEPTH, n_out // tn),
        in_specs=[
            pl.BlockSpec((MOD_ROWS, D_MODEL), lambda l, j: (0, 0)),
            pl.BlockSpec((None, D_MODEL, tn), lambda l, j: (l, 0, j)),
            pl.BlockSpec((None, 1, tn), lambda l, j: (l, 0, j)),
        ],
        out_specs=pl.BlockSpec((None, MOD_ROWS, tn), lambda l, j: (l, 0, j)),
        out_shape=jax.ShapeDtypeStruct((DEPTH, MOD_ROWS, n_out), F32),
        compiler_params=_cparams(2),
        name="modulation",
    )(cvec, w_ada, b_ada.reshape(DEPTH, 1, n_out))


GATE_OFF, FU_OFF, BG_OFF = IN_OFFS[7], IN_OFFS[8], IN_OFFS[9]


def _in_kernel(*refs, rows, split_x):
    if split_x:
        xc_ref, xl_ref, mod_ref, g_ref, w_ref, za_ref, zm_ref, fu_ref, gt_ref = refs
        x = _pick(rows, xc_ref, xl_ref)
    else:
        x_ref, mod_ref, g_ref, w_ref, za_ref, zm_ref, fu_ref, gt_ref = refs
        x = x_ref[...]
    h = _normmod(x, g_ref[...], mod_ref[1:2, :], mod_ref[0:1, :])
    z = _mm_nt(h.astype(BF16), w_ref[...])
    za_ref[...] = z[:, :D_MODEL]
    zm_ref[...] = z[:, D_MODEL:GATE_OFF]
    gt_ref[...] = z[:, GATE_OFF:GATE_OFF + GATE_PAD]
    fu_ref[...] = z[:, FU_OFF:FU_OFF + F_WIDTH].astype(BF16)


def _in_proj(xs, mod, g, w_mix, rows, layer):
    split_x = len(xs) == 2
    n_rows = sum(x.shape[0] for x in xs)
    x_specs = [rows.spec(D_MODEL, "ctx"), rows.spec(D_MODEL, "lat")] if split_x else [rows.spec(D_MODEL, "both")]
    widths = (D_MODEL, D_MODEL, F_WIDTH, GATE_PAD)
    dtypes = (F32, F32, BF16, F32)
    return pl.pallas_call(
        functools.partial(_in_kernel, rows=rows, split_x=split_x),
        grid=(n_rows // TM,),
        in_specs=x_specs + [
            rows.mod_spec(layer),
            _layer_spec((1, D_MODEL), layer),
            _layer_spec((BG_OFF, D_MODEL), layer),
        ],
        out_specs=[rows.spec(w, "both") for w in widths],
        out_shape=[jax.ShapeDtypeStruct((n_rows, w), d) for w, d in zip(widths, dtypes)],
        compiler_params=_cparams(1),
        name="in_proj",
    )(*xs, mod, g, w_mix)


def _attn_kernel(*refs, tq, t_new, t_past, latent):
    if latent:
        (q_ref, k_ref, v_ref, ck_ref, cv_ref, cosq_ref, sinq_ref, cosk_ref, sink_ref,
         qg_ref, kg_ref, bdq_ref, bdk_ref, o_ref, ks_ref, vs_ref) = refs
    else:
        (q_ref, k_ref, v_ref, qg_ref, kg_ref, bdq_ref, bdk_ref,
         o_ref, nk_ref, nv_ref, ks_ref, vs_ref) = refs

    @pl.when(pl.program_id(1) == 0)
    def _():
        k = k_ref[...]
        kn = k * _group_inv_rms(k, bdk_ref[...]) * kg_ref[...]
        v = v_ref[...]
        if latent:
            kn = _rope(kn, cosk_ref[...], sink_ref[...])
            ck = ck_ref[...]
            cv = cv_ref[...]
        else:
            nk_ref[...] = kn
            nv_ref[...] = v
        for h in range(N_KV_HEADS):
            sl = slice(HEAD_DIM * h, HEAD_DIM * (h + 1))
            if latent:
                ks_ref[h, :t_past, :] = ck[:, sl].astype(BF16)
                vs_ref[h, :t_past, :HEAD_DIM] = cv[:, sl].astype(BF16)
            ks_ref[h, t_past:, :] = kn[:, sl].astype(BF16)
            vs_ref[h, t_past:, :HEAD_DIM] = v[:, sl].astype(BF16)
            vs_ref[h, :, HEAD_DIM:] = jnp.ones((t_past + t_new, LANES - HEAD_DIM), BF16)

    q = q_ref[...]
    qn = q * _group_inv_rms(q, bdq_ref[...]) * qg_ref[...]
    if latent:
        qn = _rope(qn, cosq_ref[...], sinq_ref[...])
    qn = qn * (HEAD_DIM ** -0.5 * LOG2_E)
    group = N_Q_HEADS // N_KV_HEADS
    outs = []

    def scores(h):
        qh = jnp.concatenate(
            [qn[:, HEAD_DIM * (group * h + g):HEAD_DIM * (group * h + g + 1)] for g in range(group)],
            axis=0).astype(BF16)
        return _mm_nt(qh, ks_ref[h])

    s_next = scores(0)
    for h in range(N_KV_HEADS):
        s = s_next
        if h + 1 < N_KV_HEADS:
            s_next = scores(h + 1)
        m = jnp.max(s, axis=-1, keepdims=True)
        p = jnp.exp2(s - m).astype(BF16)
        od = _mm(p, vs_ref[h])
        o = (od / pltpu.roll(od, LANES - HEAD_DIM, axis=1))[:, :HEAD_DIM]
        outs.extend(o[tq * g:tq * (g + 1)] for g in range(group))
    o_ref[...] = jnp.concatenate(outs, axis=1).astype(BF16)


def _block_diag_ones(width):
    idx = np.arange(width) // HEAD_DIM
    return jnp.asarray(idx[:, None] == idx[None, :], dtype=BF16)


def _attention(za, qg, kg, *, batch, t_new, latent, row0, cache_k=None, cache_v=None, layer=0, rope_tabs=None):
    tq = min(512, t_new)
    nq = t_new // tq
    t_past = cache_k.shape[2] if latent else 0
    t_keys = t_past + t_new
    kcol = ATTN_WIDTH // KV_WIDTH
    bdq = _block_diag_ones(ATTN_WIDTH)
    bdk = _block_diag_ones(KV_WIDTH)
    const = lambda shape: pl.BlockSpec(shape, lambda b, i: (0,) * len(shape))
    q0, s0 = row0 // tq, row0 // t_new
    in_specs = [
        pl.BlockSpec((tq, ATTN_WIDTH), lambda b, i: (q0 + b * nq + i, 0)),
        pl.BlockSpec((t_new, KV_WIDTH), lambda b, i: (s0 + b, kcol)),
        pl.BlockSpec((t_new, KV_WIDTH), lambda b, i: (s0 + b, kcol + 1)),
    ]
    args = [za, za, za]
    if latent:
        cos_t, sin_t = rope_tabs
        in_specs += [
            pl.BlockSpec((None, None, t_past, KV_WIDTH), lambda b, i: (b, layer, 0, 0)),
            pl.BlockSpec((None, None, t_past, KV_WIDTH), lambda b, i: (b, layer, 0, 0)),
            pl.BlockSpec((tq, ATTN_WIDTH), lambda b, i: (i, 0)),
            pl.BlockSpec((tq, ATTN_WIDTH), lambda b, i: (i, 0)),
            pl.BlockSpec((t_new, KV_WIDTH), lambda b, i: (0, 0)),
            pl.BlockSpec((t_new, KV_WIDTH), lambda b, i: (0, 0)),
        ]
        args += [cache_k, cache_v, cos_t, sin_t, cos_t, sin_t]
    in_specs += [_layer_spec((1, ATTN_WIDTH), layer), _layer_spec((1, KV_WIDTH), layer),
                 const((ATTN_WIDTH, ATTN_WIDTH)), const((KV_WIDTH, KV_WIDTH))]
    args += [qg, kg, bdq, bdk]
    out_specs = [pl.BlockSpec((tq, ATTN_WIDTH), lambda b, i: (b * nq + i, 0))]
    out_shape = [jax.ShapeDtypeStruct((batch * t_new, ATTN_WIDTH), BF16)]
    if not latent:
        out_specs += [pl.BlockSpec((t_new, KV_WIDTH), lambda b, i: (b, 0))] * 2
        out_shape += [jax.ShapeDtypeStruct((batch * t_new, KV_WIDTH), F32)] * 2
    return pl.pallas_call(
        functools.partial(_attn_kernel, tq=tq, t_new=t_new, t_past=t_past, latent=latent),
        grid=(batch, nq),
        in_specs=in_specs,
        out_specs=out_specs,
        out_shape=out_shape,
        scratch_shapes=[pltpu.VMEM((N_KV_HEADS, t_keys, HEAD_DIM), BF16),
                        pltpu.VMEM((N_KV_HEADS, t_keys, LANES), BF16)],
        compiler_params=_cparams(2),
        name="attention_latent" if latent else "attention_context",
    )(*args)


def _running_max_sublanes(x, reverse):
    n = x.shape[0]
    row = lax.broadcasted_iota(jnp.int32, x.shape, 0)
    k = 1
    while k < n:
        if reverse:
            cand = jnp.where(row < n - k, pltpu.roll(x, n - k, axis=0), NEG_BIG)
        else:
            cand = jnp.where(row >= k, pltpu.roll(x, k, axis=0), NEG_BIG)
        x = jnp.maximum(x, cand)
        k *= 2
    return x


def _mlstm_kernel(*refs, t, chunk, has_state, emit_state):
    refs = list(refs)
    zm_ref, gt_ref, cw_ref, gb_ref, mg_ref, bd_ref, fu_ref, dft_c_ref, dft_ct_ref, dft_st_ref = refs[:10]
    refs = refs[10:]
    if has_state:
        c0_ref, m0_ref = refs[:2]
        refs = refs[2:]
    o_ref, fo_ref = refs[:2]
    refs = refs[2:]
    if emit_state:
        cs_ref, ms_ref = refs[:2]
        refs = refs[2:]
    hst_ref = refs[0]

    width = M_HEADS * M_DK
    z = zm_ref[...]
    x = z[:, :2 * width]
    row = lax.broadcasted_iota(jnp.int32, (t, 1), 0)
    x_prev = jnp.where(row == 0, 0.0, pltpu.roll(x, 1, axis=0))
    x_next = jnp.where(row == t - 1, 0.0, pltpu.roll(x, t - 1, axis=0))
    cw = cw_ref[...]
    qk = _silu(x_prev * cw[0:1, :] + x * cw[1:2, :] + x_next * cw[2:3, :])
    q = qk[:, :width]
    k = qk[:, width:] * (M_DK ** -0.5)
    v = z[:, 2 * width:2 * width + M_WIDTH]
    om = z[:, 2 * width + M_WIDTH:]
    heads = [slice(M_DK * h, M_DK * (h + 1)) for h in range(M_HEADS)]
    q_rows = [q[:, hs].astype(BF16) for hs in heads]
    k_rows = [k[:, hs].astype(BF16) for hs in heads]
    qt = q.T.astype(BF16)
    vt = v.T
    ones_r = jnp.ones((LANES - M_DV, t), F32)
    vaug_t = [jnp.concatenate([vt[hs, :], ones_r], axis=0) for hs in heads]
    vaug_tb = [a.astype(BF16) for a in vaug_t]

    gates = gt_ref[...] + gb_ref[...]
    lane = lax.broadcasted_iota(jnp.int32, (1, GATE_PAD), 1)
    is_forget = (lane & M_HEADS) != 0
    gates = jnp.where(is_forget, _log_sigmoid(gates), gates)
    gates_t = gates.T[:N_GATES, :]

    ri = lax.broadcasted_iota(jnp.int32, (chunk, chunk), 0)
    ci = lax.broadcasted_iota(jnp.int32, (chunk, chunk), 1)
    row_le_col = ri <= ci
    row_ge_col = ri >= ci

    n_chunks = t // chunk
    g_all = jnp.concatenate([gates_t[:, chunk * c:chunk * (c + 1)] for c in range(n_chunks)], axis=0)
    i_all = pltpu.roll(g_all, M_HEADS, axis=0)
    rows = lax.broadcasted_iota(jnp.int32, (N_GATES * n_chunks, 1), 0)
    is_fwd_row = (rows & (2 * M_HEADS)) == 0
    cum_p = _mm_left_f32(g_all, row_le_col.astype(BF16))
    cum_s = _mm_left_f32(g_all, row_ge_col.astype(BF16))
    b_all = jnp.where(is_fwd_row, cum_p, cum_s)
    u_all = i_all - b_all
    u_cols = u_all.T
    cols = lax.broadcasted_iota(jnp.int32, (1, N_GATES * n_chunks), 1)
    pm_cols = jnp.where((cols & (2 * M_HEADS)) == 0, _running_max_sublanes(u_cols, reverse=False),
                        _running_max_sublanes(u_cols, reverse=True))
    pm_all = pm_cols.T
    u_cols2 = u_cols * LOG2_E
    pm_all2 = pm_all * LOG2_E
    pml_all = jnp.where(is_fwd_row, pm_all[:, chunk - 1:], pm_all[:, :1])
    bl_all = jnp.where(is_fwd_row, b_all[:, chunk - 1:], b_all[:, :1])
    wk_all = jnp.exp(u_all - pml_all)

    if has_state:
        m_state = [m0_ref[M_HEADS * d:M_HEADS * (d + 1), 0:1] for d in range(2)]
        c_state = [[c0_ref[M_HEADS * d + h] for h in range(M_HEADS)] for d in range(2)]
    else:
        m_state = [jnp.zeros((M_HEADS, 1), F32) for _ in range(2)]
        c_state = [[jnp.zeros((LANES, M_DK), F32) for _ in range(M_HEADS)] for _ in range(2)]

    steps = [(j if d == 0 else n_chunks - 1 - j, d) for j in range(n_chunks) for d in range(2)]

    def gate_rows(step):
        c, direction = step
        return N_GATES * c + (2 * direction + 1) * M_HEADS

    def key_query(step):
        rs = slice(chunk * step[0], chunk * (step[0] + 1))
        return [_mm_nt(k_rows[h][rs, :], q_rows[h][rs, :]) for h in range(M_HEADS)]

    def local_sums(step, st):
        c, direction = step
        rs = slice(chunk * c, chunk * (c + 1))
        r0 = gate_rows(step)
        valid = row_le_col if direction == 0 else row_ge_col
        s_loc, x_loc = [], []
        for h in range(M_HEADS):
            w = jnp.exp2(jnp.where(valid, u_cols2[:, r0 + h:r0 + h + 1] - pm_all2[r0 + h:r0 + h + 1, :], NEG_BIG))
            s_loc.append(_mm(vaug_tb[h][:, rs], (st[h] * w).astype(BF16)))
            x_loc.append(_mm((vaug_t[h][:, rs] * wk_all[r0 + h:r0 + h + 1, :]).astype(BF16), k_rows[h][rs, :]))
        return s_loc, x_loc

    written = set()
    n_steps = len(steps)
    st_q = [key_query(steps[i]) for i in range(min(2, n_steps))]
    loc_q = [local_sums(steps[0], st_q.pop(0))]
    for i, (c, direction) in enumerate(steps):
        if i + 1 < n_steps:
            loc_q.append(local_sums(steps[i + 1], st_q.pop(0)))
        if i + 2 < n_steps:
            st_q.append(key_query(steps[i + 2]))
        s_loc, x_loc = loc_q.pop(0)
        rs = slice(chunk * c, chunk * (c + 1))
        r0 = gate_rows((c, direction))
        b4, pm4, bl, pml = (a[r0:r0 + M_HEADS, :] for a in (b_all, pm_all, bl_all, pml_all))
        m = m_state[direction]
        mx = jnp.maximum(m, pm4)
        f_loc = jnp.exp(pm4 - mx)
        f_int = jnp.exp(m - mx)
        floor = jnp.exp(-(b4 + mx))
        m_new = bl + jnp.maximum(m, pml)
        decay = jnp.exp(bl + m - m_new)
        gain = jnp.exp(bl + pml - m_new)
        m_state[direction] = m_new
        inter = [_mm(c_state[direction][h].astype(BF16), qt[heads[h], rs]) for h in range(M_HEADS)]
        for h in range(M_HEADS):
            nd = f_loc[h:h + 1, :] * s_loc[h] + f_int[h:h + 1, :] * inter[h]
            ht = nd[:M_DV, :] / jnp.maximum(jnp.abs(nd[M_DV:, :]), floor[h:h + 1, :])
            c_state[direction][h] = decay[h:h + 1, :] * c_state[direction][h] + gain[h:h + 1, :] * x_loc[h]
            if c in written:
                hst_ref[heads[h], rs] += ht
            else:
                hst_ref[heads[h], rs] = ht
        written.add(c)

    if emit_state:
        for d in range(2):
            ms_ref[M_HEADS * d:M_HEADS * (d + 1), :] = jnp.broadcast_to(m_state[d], (M_HEADS, LANES))
            for h in range(M_HEADS):
                cs_ref[M_HEADS * d + h] = c_state[d][h]

    hsum = hst_ref[...].T
    hn = hsum * _group_inv_rms(hsum, bd_ref[...]) * mg_ref[...]
    o_ref[...] = (_sigmoid(om) * hn).astype(BF16)

    fa = _mm(fu_ref[...].astype(BF16), dft_c_ref[...])
    fo_ref[...] = (_mm(dft_ct_ref[...], fa[:, :F_WIDTH].astype(BF16))
                   - _mm(dft_st_ref[...], fa[:, F_WIDTH:].astype(BF16))).astype(BF16)


def _mlstm_fourier(zm, gt, fu, conv_w, gate_b, m_norm_g, *, batch, t, layer, row0, caug0=None, m0=None):
    has_state = caug0 is not None
    emit_state = not has_state
    chunk = min(M_CHUNK_K, t)
    n_units = 2 * M_HEADS
    const = lambda shape: pl.BlockSpec(shape, lambda b: (0,) * len(shape), pipeline_mode=pl.Buffered(1))
    s0 = row0 // t
    in_specs = [
        pl.BlockSpec((t, D_MODEL), lambda b: (s0 + b, 0)),
        pl.BlockSpec((t, GATE_PAD), lambda b: (s0 + b, 0)),
        _layer_spec((3, 2 * M_HEADS * M_DK), layer),
        _layer_spec((1, GATE_PAD), layer),
        _layer_spec((1, M_WIDTH), layer),
        const((M_WIDTH, M_WIDTH)),
        pl.BlockSpec((t, F_WIDTH), lambda b: (s0 + b, 0)),
        const((F_WIDTH, 2 * F_WIDTH)), const((t, t)), const((t, t)),
    ]
    args = [zm, gt, conv_w, gate_b, m_norm_g, _block_diag_ones(M_WIDTH), fu, *_dft_tables(t)]
    if has_state:
        in_specs += [pl.BlockSpec((None, n_units, LANES, M_DK), lambda b: (b, 0, 0, 0)),
                     pl.BlockSpec((None, n_units, LANES), lambda b: (b, 0, 0))]
        args += [caug0, m0]
    out_specs = [pl.BlockSpec((t, M_WIDTH), lambda b: (b, 0)), pl.BlockSpec((t, F_WIDTH), lambda b: (b, 0))]
    out_shape = [jax.ShapeDtypeStruct((batch * t, M_WIDTH), BF16), jax.ShapeDtypeStruct((batch * t, F_WIDTH), BF16)]
    if emit_state:
        out_specs += [pl.BlockSpec((None, n_units, LANES, M_DK), lambda b: (b, 0, 0, 0)),
                      pl.BlockSpec((None, n_units, LANES), lambda b: (b, 0, 0))]
        out_shape += [jax.ShapeDtypeStruct((batch, n_units, LANES, M_DK), F32),
                      jax.ShapeDtypeStruct((batch, n_units, LANES), F32)]
    return pl.pallas_call(
        functools.partial(_mlstm_kernel, t=t, chunk=chunk, has_state=has_state, emit_state=emit_state),
        grid=(batch,),
        in_specs=in_specs,
        out_specs=out_specs,
        out_shape=out_shape,
        scratch_shapes=[pltpu.VMEM((M_WIDTH, t), F32)],
        compiler_params=_cparams(1),
        name="mlstm_fourier_latent" if has_state else "mlstm_fourier_context",
    )(*args)


def _dft_tables(t):
    kt = (np.arange(t)[:, None] * np.arange(t)[None, :]) % t
    ang_t = 2.0 * np.pi * kt.astype(np.float64) / t
    ct = np.cos(ang_t) / np.sqrt(t)
    st = np.sin(ang_t) / np.sqrt(t)
    c = F_GROUP_CH
    kc = (np.arange(c)[:, None] * np.arange(c)[None, :]) % c
    ang_c = 2.0 * np.pi * kc.astype(np.float64) / c
    eye = np.eye(F_GROUPS)
    cc = np.kron(eye, np.cos(ang_c) / np.sqrt(c))
    sc = np.kron(eye, np.sin(ang_c) / np.sqrt(c))
    cs = np.concatenate([cc, sc], axis=1)
    to_dev = lambda a: jnp.asarray(a.astype(np.float32)).astype(BF16)
    return to_dev(cs), to_dev(ct), to_dev(st)


def _merge_kernel(*refs, rows, split_x):
    refs = list(refs)
    x_refs = [refs.pop(0) for _ in range(2 if split_x else 1)]
    (mod_ref, g_ref, ac_ref, al_ref, hc_ref, hl_ref, fc_ref, fl_ref, wbg_ref,
     wpa_ref, wpm_ref, wpf_ref, wo_ref, o_ref, wpa_s, wpm_s, wpf_s, wo_s) = refs
    for w_ref, w_s in ((wpa_ref, wpa_s), (wpm_ref, wpm_s), (wpf_ref, wpf_s), (wo_ref, wo_s)):
        _stage(w_ref, w_s)

    @pl.when(pl.program_id(0) >= W_STAGE)
    def _():
        x = _pick(rows, *x_refs) if split_x else x_refs[0][...]
        h = _normmod(x, g_ref[...], mod_ref[1:2, :], mod_ref[0:1, :]).astype(BF16)

        def branch(j, yc_ref, yl_ref, w_s):
            gate = _sigmoid(_mm_nt(h, wbg_ref[D_MODEL * j:D_MODEL * (j + 1), :]))
            return gate * _mm(_pick(rows, yc_ref, yl_ref), w_s[...])

        merged = (branch(0, ac_ref, al_ref, wpa_s) + branch(1, hc_ref, hl_ref, wpm_s)
                  + branch(2, fc_ref, fl_ref, wpf_s))
        o_ref[...] = x + mod_ref[2:3, :] * _mm(merged.astype(BF16), wo_s[...])


def _merge(xs, mod, g, attn, hm, fo, w_bg, w_pa, w_pm, w_pf, w_out, rows, layer):
    split_x = len(xs) == 2
    n_rows = sum(x.shape[0] for x in xs)
    x_specs = [rows.spec(D_MODEL, "ctx"), rows.spec(D_MODEL, "lat")] if split_x else [rows.spec(D_MODEL, "both")]
    pair = lambda width: [rows.spec(width, "ctx"), rows.spec(width, "lat")]
    staged = ((ATTN_WIDTH, D_MODEL), (M_WIDTH, D_MODEL), (F_WIDTH, D_MODEL), (D_MODEL, D_MODEL))
    return pl.pallas_call(
        functools.partial(_merge_kernel, rows=rows, split_x=split_x),
        grid=(W_STAGE + n_rows // TM,),
        in_specs=x_specs + [rows.mod_spec(layer), _layer_spec((1, D_MODEL), layer)]
        + pair(ATTN_WIDTH) + pair(M_WIDTH) + pair(F_WIDTH)
        + [_layer_spec((3 * D_MODEL, D_MODEL), layer)] + [_staged_spec(shape, layer) for shape in staged],
        out_specs=rows.spec(D_MODEL, "both"),
        out_shape=jax.ShapeDtypeStruct((n_rows, D_MODEL), F32),
        scratch_shapes=[pltpu.VMEM(shape, BF16) for shape in staged],
        compiler_params=_cparams(1),
        name="merge",
    )(*xs, mod, g, *attn, *hm, *fo, w_bg, w_pa, w_pm, w_pf, w_out)


def _ffn_kernel(*refs, rows, split_out):
    x_ref, mod_ref, g_ref, win_ref, wout_ref = refs[:5]
    out_refs = refs[5:7] if split_out else refs[5:6]
    win_s, wout_s = refs[-2:]
    _stage(win_ref, win_s)
    _stage(wout_ref, wout_s)

    @pl.when(pl.program_id(0) >= W_STAGE)
    def _():
        x = x_ref[...]
        h = _normmod(x, g_ref[...], mod_ref[4:5, :], mod_ref[3:4, :]).astype(BF16)
        u = _mm(h, win_s[...])
        a = (_silu(u[:, :FF_HIDDEN]) * u[:, FF_HIDDEN:]).astype(BF16)
        y = x + mod_ref[5:6, :] * _mm(a, wout_s[...])
        if split_out:
            @pl.when(rows.is_ctx())
            def _():
                out_refs[0][...] = y

            @pl.when(jnp.logical_not(rows.is_ctx()))
            def _():
                out_refs[1][...] = y
        else:
            out_refs[0][...] = y


def _ffn(x, mod, g, w_in, w_out, rows, layer, split_out):
    n_rows = x.shape[0]
    n_ctx_rows = rows.n_ctx * TM
    staged = ((D_MODEL, 2 * FF_HIDDEN), (FF_HIDDEN, D_MODEL))
    if split_out:
        out_specs = [rows.spec(D_MODEL, "ctx"), rows.spec(D_MODEL, "lat")]
        out_shape = [jax.ShapeDtypeStruct((n_ctx_rows, D_MODEL), F32),
                     jax.ShapeDtypeStruct((n_rows - n_ctx_rows, D_MODEL), F32)]
    else:
        out_specs = [rows.spec(D_MODEL, "both")]
        out_shape = [jax.ShapeDtypeStruct((n_rows, D_MODEL), F32)]
    return pl.pallas_call(
        functools.partial(_ffn_kernel, rows=rows, split_out=split_out),
        grid=(W_STAGE + n_rows // TM,),
        in_specs=[rows.spec(D_MODEL, "both"), rows.mod_spec(layer), _layer_spec((1, D_MODEL), layer)]
        + [_staged_spec(shape, layer) for shape in staged],
        out_specs=out_specs,
        out_shape=out_shape,
        scratch_shapes=[pltpu.VMEM(shape, BF16) for shape in staged],
        compiler_params=_cparams(1),
        name="ffn",
    )(x, mod, g, w_in, w_out)


def _rope_tables(t):
    n = HEAD_DIM // 4
    inv = 1.0 / (ROPE_THETA ** (np.arange(n, dtype=np.float64) / n))
    pos = np.arange(t)
    ang_r = (pos // GRID_W)[:, None] * inv[None, :]
    ang_c = (pos % GRID_W)[:, None] * inv[None, :]
    cos = np.concatenate([np.cos(ang_r)] * 2 + [np.cos(ang_c)] * 2, axis=1)
    sin = np.concatenate([-np.sin(ang_r), np.sin(ang_r), -np.sin(ang_c), np.sin(ang_c)], axis=1)
    tile = lambda a: jnp.asarray(np.tile(a, (1, N_Q_HEADS)).astype(np.float32))
    return tile(cos), tile(sin)


def kernel(x_prompt, x_sample, cache_k, cache_v, state_C, state_n, state_m, c, c_ctx, w_ada, b_ada,
           norm1_g, w_in, q_norm_g, k_norm_g, m_conv_w, m_gate_b, m_norm_g, w_proj_attn,
           w_proj_mlstm, w_proj_fourier, w_out, norm2_g, w_ffn_in, w_ffn_out):
    n_ctx, t_ctx, _ = x_prompt.shape
    n_lat, t_lat, _ = x_sample.shape
    t_past = cache_k.shape[2]
    n_units = 2 * M_HEADS

    cvec = jnp.concatenate([c_ctx[None, :], c], axis=0)
    cvec = jnp.pad(cvec, ((0, MOD_ROWS - cvec.shape[0]), (0, 0)))
    mod = _modulation(cvec, w_ada, b_ada).reshape(DEPTH, MOD_ROWS, N_MOD, D_MODEL)

    rope_tabs = _rope_tables(t_lat)
    ck = cache_k.reshape(n_lat, DEPTH, t_past, KV_WIDTH)
    cv = cache_v.reshape(n_lat, DEPTH, t_past, KV_WIDTH)
    rows_ctx = n_ctx * t_ctx
    rows = _Rows(rows_ctx, t_lat)
    rows_staged = _Rows(rows_ctx, t_lat, n_stage=W_STAGE)

    w_in_t = jnp.swapaxes(w_in, 1, 2)
    w_mix = w_in_t[:, :BG_OFF].astype(BF16)
    w_bg = w_in_t[:, BG_OFF:].astype(BF16)
    row = lambda a: a[:, None, :]
    norm1, norm2 = row(norm1_g), row(norm2_g)
    qg, kg = row(jnp.tile(q_norm_g, (1, N_Q_HEADS))), row(jnp.tile(k_norm_g, (1, N_KV_HEADS)))
    gate_b = row(jnp.pad(m_gate_b, ((0, 0), (0, GATE_PAD - N_GATES))))
    m_norm = row(m_norm_g)

    xs = (x_prompt.reshape(rows_ctx, D_MODEL), x_sample.reshape(n_lat * t_lat, D_MODEL))
    ks, vs, cs_all, ms_all = [], [], [], []
    for l in range(DEPTH):
        za, zm, fu, gt = _in_proj(xs, mod, norm1, w_mix, rows, l)
        attn_c, k_l, v_l = _attention(za, qg, kg, batch=n_ctx, t_new=t_ctx, latent=False, row0=0, layer=l)
        hm_c, fo_c, cs, ms = _mlstm_fourier(zm, gt, fu, m_conv_w, gate_b, m_norm, batch=n_ctx, t=t_ctx,
                                            layer=l, row0=0)
        ks.append(k_l.reshape(n_ctx, t_ctx, N_KV_HEADS, HEAD_DIM))
        vs.append(v_l.reshape(n_ctx, t_ctx, N_KV_HEADS, HEAD_DIM))
        cs_all.append(cs)
        ms_all.append(ms)

        c0t = jnp.swapaxes(state_C[:, l].astype(F32).reshape(n_lat, n_units, M_DK, M_DV), -1, -2)
        n0 = state_n[:, l].astype(F32).reshape(n_lat, n_units, 1, M_DK)
        caug0 = jnp.concatenate(
            [c0t, jnp.broadcast_to(n0, (n_lat, n_units, LANES - M_DV, M_DK))], axis=-2)
        m0 = jnp.broadcast_to(state_m[:, l].astype(F32).reshape(n_lat, n_units, 1),
                              (n_lat, n_units, LANES))
        (attn_l,) = _attention(za, qg, kg, batch=n_lat, t_new=t_lat, latent=True, row0=rows_ctx,
                               cache_k=ck, cache_v=cv, layer=l, rope_tabs=rope_tabs)
        hm_l, fo_l = _mlstm_fourier(zm, gt, fu, m_conv_w, gate_b, m_norm, batch=n_lat, t=t_lat, layer=l,
                                    row0=rows_ctx, caug0=caug0, m0=m0)

        x1 = _merge(xs, mod, norm1, (attn_c, attn_l), (hm_c, hm_l), (fo_c, fo_l), w_bg,
                    w_proj_attn, w_proj_mlstm, w_proj_fourier, w_out, rows_staged, l)
        xs = tuple(_ffn(x1, mod, norm2, w_ffn_in, w_ffn_out, rows_staged, l, split_out=l == DEPTH - 1))
    xp, xs = xs

    cs_all = jnp.stack(cs_all, axis=1)
    ms_all = jnp.stack(ms_all, axis=1)
    new_c = jnp.swapaxes(cs_all[..., :M_DV, :], -1, -2).reshape(n_ctx, DEPTH, 2, M_HEADS, M_DK, M_DV)
    new_n = cs_all[..., M_DV, :].reshape(n_ctx, DEPTH, 2, M_HEADS, M_DK)
    new_m = ms_all[..., 0].reshape(n_ctx, DEPTH, 2, M_HEADS)
    return (xp.reshape(n_ctx, t_ctx, D_MODEL), xs.reshape(n_lat, t_lat, D_MODEL),
            jnp.stack(ks, axis=1), jnp.stack(vs, axis=1), new_c, new_n, new_m)
```

```python
import functools

import numpy as np
import jax
import jax.numpy as jnp
from jax import lax
from jax.experimental import pallas as pl
from jax.experimental.pallas import tpu as pltpu

D_MODEL = 1024
DEPTH = 2
GRID_W = 64
HEAD_DIM = 64
N_Q_HEADS = 8
N_KV_HEADS = 4
ATTN_WIDTH = N_Q_HEADS * HEAD_DIM
KV_WIDTH = N_KV_HEADS * HEAD_DIM
ROPE_THETA = 10000.0
M_HEADS = 4
M_DK = 64
M_DV = 64
M_WIDTH = M_HEADS * M_DV
F_GROUPS = 4
F_GROUP_CH = 64
F_WIDTH = F_GROUPS * F_GROUP_CH
FF_HIDDEN = -(-8 * D_MODEL // (3 * 256)) * 256
EPS = 1e-6
N_GATES = 4 * M_HEADS
IN_SIZES = (ATTN_WIDTH, KV_WIDTH, KV_WIDTH, M_HEADS * M_DK, M_HEADS * M_DK, M_WIDTH, M_WIDTH,
            N_GATES, F_WIDTH, 3 * D_MODEL)
IN_OFFS = tuple(int(v) for v in np.cumsum((0,) + IN_SIZES))

LANES = 128
GATE_PAD = LANES
N_MOD = 6
MOD_ROWS = 16
M_CHUNK_K = 128
NEG_BIG = -1e30
LOG2_E = 1.4426950408889634
VMEM_LIMIT = 56 * 1024 * 1024
TM = 512
W_STAGE = 8
W_IN_ROWS = 256

F32 = jnp.float32
BF16 = jnp.bfloat16


def _cparams(n_axes):
    return pltpu.CompilerParams(dimension_semantics=("arbitrary",) * n_axes,
                                vmem_limit_bytes=VMEM_LIMIT)


def _layer_spec(shape, layer):
    return pl.BlockSpec((None,) + tuple(shape), lambda *_: (layer,) + (0,) * len(shape),
                        pipeline_mode=pl.Buffered(1))


class _Rows:
    def __init__(self, rows_ctx, t_lat, n_stage=0):
        self.n_ctx = rows_ctx // TM
        self.per_seq = t_lat // TM
        self.n_stage = n_stage

    def tile(self, i):
        return jnp.maximum(i - self.n_stage, 0)

    def both(self, i):
        return (self.tile(i), 0)

    def ctx(self, i):
        return (jnp.minimum(self.tile(i), self.n_ctx - 1), 0)

    def lat(self, i):
        return (jnp.maximum(self.tile(i) - self.n_ctx, 0), 0)

    def mod_spec(self, layer):
        def index(i):
            t = self.tile(i)
            return (layer, jnp.where(t < self.n_ctx, 0, 1 + (t - self.n_ctx) // self.per_seq), 0, 0)
        return pl.BlockSpec((None, None, N_MOD, D_MODEL), index)

    def spec(self, width, which):
        return pl.BlockSpec((TM, width), getattr(self, which))

    def is_ctx(self):
        return pl.program_id(0) - self.n_stage < self.n_ctx


def _pick(rows, ctx_ref, lat_ref):
    return jnp.where(rows.is_ctx(), ctx_ref[...], lat_ref[...])


def _staged_spec(shape, layer, n_chunks=W_STAGE, first=0):
    rc, cols = shape[0] // n_chunks, shape[1]
    return pl.BlockSpec((None, rc, cols), lambda i: (layer, first + jnp.minimum(i, n_chunks - 1), 0))


def _stage(w_ref, scratch_ref, n_chunks=W_STAGE, last_rows=None):
    i = pl.program_id(0)
    rc = w_ref.shape[0]
    n_full = n_chunks if last_rows is None else n_chunks - 1

    @pl.when(i < n_full)
    def _():
        scratch_ref[pl.ds(pl.multiple_of(i * rc, rc), rc), :] = w_ref[...].astype(BF16)

    if last_rows is not None:
        @pl.when(i == n_full)
        def _():
            scratch_ref[n_full * rc:n_full * rc + last_rows, :] = w_ref[:last_rows, :].astype(BF16)


def _mm(a, b):
    return jnp.dot(a, b, preferred_element_type=F32)


def _mm_nt(a, b):
    return lax.dot_general(a, b, (((1,), (1,)), ((), ())), preferred_element_type=F32)


def _split3(x):
    hi = x.astype(BF16)
    r = x - hi.astype(F32)
    mid = r.astype(BF16)
    lo = (r - mid.astype(F32)).astype(BF16)
    return hi, mid, lo


def _mm_left_f32(x, m_bf16):
    hi, mid, lo = _split3(x)
    return _mm(hi, m_bf16) + _mm(mid, m_bf16) + _mm(lo, m_bf16)


def _sigmoid(x):
    return 1.0 / (1.0 + jnp.exp(-x))


def _silu(x):
    return x * _sigmoid(x)


def _log_sigmoid(x):
    return jnp.minimum(x, 0.0) - jnp.log(1.0 + jnp.exp(-jnp.abs(x)))


def _normmod(x, g, scale, shift):
    ms = jnp.mean(x * x, axis=-1, keepdims=True)
    return (x * lax.rsqrt(ms + EPS)) * g * (1.0 + scale) + shift


def _group_inv_rms(x, bd):
    x2 = x * x
    hi = x2.astype(BF16)
    lo = (x2 - hi.astype(F32)).astype(BF16)
    ss = _mm(hi, bd) + _mm(lo, bd)
    return lax.rsqrt(ss * (1.0 / HEAD_DIM) + EPS)


def _rope(x, cos, sin_signed):
    w = x.shape[1]
    lane = lax.broadcasted_iota(jnp.int32, x.shape, 1)
    up = pltpu.roll(x, w - 16, axis=1)
    dn = pltpu.roll(x, 16, axis=1)
    partner = jnp.where((lane & 31) < 16, up, dn)
    return x * cos + partner * sin_signed


def _mod_kernel(c_ref, w_ref, b_ref, o_ref):
    s = _silu(c_ref[...]).astype(BF16)
    o_ref[...] = _mm(s, w_ref[...].astype(BF16)) + b_ref[...]


def _modulation(cvec, w_ada, b_ada):
    tn = 1024
    n_out = N_MOD * D_MODEL
    return pl.pallas_call(
        _mod_kernel,
        grid=(DEPTH, n_out // tn),
        in_specs=[
            pl.BlockSpec((MOD_ROWS, D_MODEL), lambda l, j: (0, 0)),
            pl.BlockSpec((None, D_MODEL, tn), lambda l, j: (l, 0, j)),
            pl.BlockSpec((None, 1, tn), lambda l, j: (l, 0, j)),
        ],
        out_specs=pl.BlockSpec((None, MOD_ROWS, tn), lambda l, j: (l, 0, j)),
        out_shape=jax.ShapeDtypeStruct((DEPTH, MOD_ROWS, n_out), F32),
        compiler_params=_cparams(2),
        name="modulation",
    )(cvec, w_ada, b_ada.reshape(DEPTH, 1, n_out))


GATE_OFF, FU_OFF, BG_OFF = IN_OFFS[7], IN_OFFS[8], IN_OFFS[9]
IN_COLS = IN_OFFS[-1]
MIX_CHUNKS = -(-BG_OFF // W_IN_ROWS)
BG_FIRST = BG_OFF // W_IN_ROWS
BG_CHUNKS = -(-IN_COLS // W_IN_ROWS) - BG_FIRST
BG_SKIP = BG_OFF - BG_FIRST * W_IN_ROWS
BG_LAST = IN_COLS - (BG_FIRST + BG_CHUNKS - 1) * W_IN_ROWS


def _in_kernel(*refs, rows, split_x):
    refs = list(refs)
    x_refs = [refs.pop(0) for _ in range(2 if split_x else 1)]
    mod_ref, g_ref, w_ref, za_ref, zm_ref, fu_ref, gt_ref, w_s = refs
    _stage(w_ref, w_s, MIX_CHUNKS)

    @pl.when(pl.program_id(0) >= MIX_CHUNKS)
    def _():
        x = _pick(rows, *x_refs) if split_x else x_refs[0][...]
        h = _normmod(x, g_ref[...], mod_ref[1:2, :], mod_ref[0:1, :])
        z = _mm_nt(h.astype(BF16), w_s[:BG_OFF, :])
        za_ref[...] = z[:, :D_MODEL]
        zm_ref[...] = z[:, D_MODEL:GATE_OFF]
        gt_ref[...] = z[:, GATE_OFF:GATE_OFF + GATE_PAD]
        fu_ref[...] = z[:, FU_OFF:FU_OFF + F_WIDTH].astype(BF16)


def _in_proj(xs, mod, g, w_in_t, rows, layer):
    split_x = len(xs) == 2
    n_rows = sum(x.shape[0] for x in xs)
    x_specs = [rows.spec(D_MODEL, "ctx"), rows.spec(D_MODEL, "lat")] if split_x else [rows.spec(D_MODEL, "both")]
    widths = (D_MODEL, D_MODEL, F_WIDTH, GATE_PAD)
    dtypes = (F32, F32, BF16, F32)
    return pl.pallas_call(
        functools.partial(_in_kernel, rows=rows, split_x=split_x),
        grid=(MIX_CHUNKS + n_rows // TM,),
        in_specs=x_specs + [
            rows.mod_spec(layer),
            _layer_spec((1, D_MODEL), layer),
            _staged_spec((MIX_CHUNKS * W_IN_ROWS, D_MODEL), layer, MIX_CHUNKS),
        ],
        out_specs=[rows.spec(w, "both") for w in widths],
        out_shape=[jax.ShapeDtypeStruct((n_rows, w), d) for w, d in zip(widths, dtypes)],
        scratch_shapes=[pltpu.VMEM((MIX_CHUNKS * W_IN_ROWS, D_MODEL), BF16)],
        compiler_params=_cparams(1),
        name="in_proj",
    )(*xs, mod, g, w_in_t)


def _attn_kernel(*refs, tq, t_new, t_past, latent):
    if latent:
        (q_ref, k_ref, v_ref, ck_ref, cv_ref, cosq_ref, sinq_ref, cosk_ref, sink_ref,
         qg_ref, kg_ref, bdq_ref, bdk_ref, o_ref, ks_ref, vs_ref) = refs
    else:
        (q_ref, k_ref, v_ref, qg_ref, kg_ref, bdq_ref, bdk_ref,
         o_ref, nk_ref, nv_ref, ks_ref, vs_ref) = refs

    @pl.when(pl.program_id(1) == 0)
    def _():
        k = k_ref[...]
        kn = k * _group_inv_rms(k, bdk_ref[...]) * kg_ref[...]
        v = v_ref[...]
        if latent:
            kn = _rope(kn, cosk_ref[...], sink_ref[...])
            ck = ck_ref[...]
            cv = cv_ref[...]
        else:
            nk_ref[...] = kn
            nv_ref[...] = v
        for h in range(N_KV_HEADS):
            sl = slice(HEAD_DIM * h, HEAD_DIM * (h + 1))
            if latent:
                ks_ref[h, :t_past, :] = ck[:, sl].astype(BF16)
                vs_ref[h, :t_past, :HEAD_DIM] = cv[:, sl].astype(BF16)
            ks_ref[h, t_past:, :] = kn[:, sl].astype(BF16)
            vs_ref[h, t_past:, :HEAD_DIM] = v[:, sl].astype(BF16)
            vs_ref[h, :, HEAD_DIM:] = jnp.ones((t_past + t_new, LANES - HEAD_DIM), BF16)

    q = q_ref[...]
    qn = q * _group_inv_rms(q, bdq_ref[...]) * qg_ref[...]
    if latent:
        qn = _rope(qn, cosq_ref[...], sinq_ref[...])
    qn = qn * (HEAD_DIM ** -0.5 * LOG2_E)
    group = N_Q_HEADS // N_KV_HEADS
    outs = []

    def scores(h):
        qh = jnp.concatenate(
            [qn[:, HEAD_DIM * (group * h + g):HEAD_DIM * (group * h + g + 1)] for g in range(group)],
            axis=0).astype(BF16)
        return _mm_nt(qh, ks_ref[h])

    s_next = scores(0)
    for h in range(N_KV_HEADS):
        s = s_next
        if h + 1 < N_KV_HEADS:
            s_next = scores(h + 1)
        m = jnp.max(s, axis=-1, keepdims=True)
        p = jnp.exp2(s - m).astype(BF16)
        od = _mm(p, vs_ref[h])
        o = (od / pltpu.roll(od, LANES - HEAD_DIM, axis=1))[:, :HEAD_DIM]
        outs.extend(o[tq * g:tq * (g + 1)] for g in range(group))
    o_ref[...] = jnp.concatenate(outs, axis=1).astype(BF16)


def _block_diag_ones(width):
    idx = np.arange(width) // HEAD_DIM
    return jnp.asarray(idx[:, None] == idx[None, :], dtype=BF16)


def _attention(za, qg, kg, *, batch, t_new, latent, row0, cache_k=None, cache_v=None, layer=0, rope_tabs=None):
    tq = min(512, t_new)
    nq = t_new // tq
    t_past = cache_k.shape[2] if latent else 0
    t_keys = t_past + t_new
    kcol = ATTN_WIDTH // KV_WIDTH
    bdq = _block_diag_ones(ATTN_WIDTH)
    bdk = _block_diag_ones(KV_WIDTH)
    const = lambda shape: pl.BlockSpec(shape, lambda b, i: (0,) * len(shape))
    q0, s0 = row0 // tq, row0 // t_new
    in_specs = [
        pl.BlockSpec((tq, ATTN_WIDTH), lambda b, i: (q0 + b * nq + i, 0)),
        pl.BlockSpec((t_new, KV_WIDTH), lambda b, i: (s0 + b, kcol)),
        pl.BlockSpec((t_new, KV_WIDTH), lambda b, i: (s0 + b, kcol + 1)),
    ]
    args = [za, za, za]
    if latent:
        cos_t, sin_t = rope_tabs
        in_specs += [
            pl.BlockSpec((None, None, t_past, KV_WIDTH), lambda b, i: (b, layer, 0, 0)),
            pl.BlockSpec((None, None, t_past, KV_WIDTH), lambda b, i: (b, layer, 0, 0)),
            pl.BlockSpec((tq, ATTN_WIDTH), lambda b, i: (i, 0)),
            pl.BlockSpec((tq, ATTN_WIDTH), lambda b, i: (i, 0)),
            pl.BlockSpec((t_new, KV_WIDTH), lambda b, i: (0, 0)),
            pl.BlockSpec((t_new, KV_WIDTH), lambda b, i: (0, 0)),
        ]
        args += [cache_k, cache_v, cos_t, sin_t, cos_t, sin_t]
    in_specs += [_layer_spec((1, ATTN_WIDTH), layer), _layer_spec((1, KV_WIDTH), layer),
                 const((ATTN_WIDTH, ATTN_WIDTH)), const((KV_WIDTH, KV_WIDTH))]
    args += [qg, kg, bdq, bdk]
    out_specs = [pl.BlockSpec((tq, ATTN_WIDTH), lambda b, i: (b * nq + i, 0))]
    out_shape = [jax.ShapeDtypeStruct((batch * t_new, ATTN_WIDTH), BF16)]
    if not latent:
        out_specs += [pl.BlockSpec((t_new, KV_WIDTH), lambda b, i: (b, 0))] * 2
        out_shape += [jax.ShapeDtypeStruct((batch * t_new, KV_WIDTH), F32)] * 2
    return pl.pallas_call(
        functools.partial(_attn_kernel, tq=tq, t_new=t_new, t_past=t_past, latent=latent),
        grid=(batch, nq),
        in_specs=in_specs,
        out_specs=out_specs,
        out_shape=out_shape,
        scratch_shapes=[pltpu.VMEM((N_KV_HEADS, t_keys, HEAD_DIM), BF16),
                        pltpu.VMEM((N_KV_HEADS, t_keys, LANES), BF16)],
        compiler_params=_cparams(2),
        name="attention_latent" if latent else "attention_context",
    )(*args)


def _running_max_sublanes(x, reverse):
    n = x.shape[0]
    row = lax.broadcasted_iota(jnp.int32, x.shape, 0)
    k = 1
    while k < n:
        if reverse:
            cand = jnp.where(row < n - k, pltpu.roll(x, n - k, axis=0), NEG_BIG)
        else:
            cand = jnp.where(row >= k, pltpu.roll(x, k, axis=0), NEG_BIG)
        x = jnp.maximum(x, cand)
        k *= 2
    return x


def _mlstm_kernel(*refs, t, chunk, has_state, emit_state):
    refs = list(refs)
    zm_ref, gt_ref, cw_ref, gb_ref, mg_ref, bd_ref, fu_ref, dft_c_ref, dft_ct_ref, dft_st_ref = refs[:10]
    refs = refs[10:]
    if has_state:
        c0_ref, m0_ref = refs[:2]
        refs = refs[2:]
    o_ref, fo_ref = refs[:2]
    refs = refs[2:]
    if emit_state:
        cs_ref, ms_ref = refs[:2]
        refs = refs[2:]
    hst_ref = refs[0]

    width = M_HEADS * M_DK
    z = zm_ref[...]
    x = z[:, :2 * width]
    row = lax.broadcasted_iota(jnp.int32, (t, 1), 0)
    x_prev = jnp.where(row == 0, 0.0, pltpu.roll(x, 1, axis=0))
    x_next = jnp.where(row == t - 1, 0.0, pltpu.roll(x, t - 1, axis=0))
    cw = cw_ref[...]
    qk = _silu(x_prev * cw[0:1, :] + x * cw[1:2, :] + x_next * cw[2:3, :])
    q = qk[:, :width]
    k = qk[:, width:] * (M_DK ** -0.5)
    v = z[:, 2 * width:2 * width + M_WIDTH]
    om = z[:, 2 * width + M_WIDTH:]
    heads = [slice(M_DK * h, M_DK * (h + 1)) for h in range(M_HEADS)]
    q_rows = [q[:, hs].astype(BF16) for hs in heads]
    k_rows = [k[:, hs].astype(BF16) for hs in heads]
    qt = q.T.astype(BF16)
    vt = v.T
    ones_r = jnp.ones((LANES - M_DV, t), F32)
    vaug_t = [jnp.concatenate([vt[hs, :], ones_r], axis=0) for hs in heads]
    vaug_tb = [a.astype(BF16) for a in vaug_t]

    gates = gt_ref[...] + gb_ref[...]
    lane = lax.broadcasted_iota(jnp.int32, (1, GATE_PAD), 1)
    is_forget = (lane & M_HEADS) != 0
    gates = jnp.where(is_forget, _log_sigmoid(gates), gates)
    gates_t = gates.T[:N_GATES, :]

    ri = lax.broadcasted_iota(jnp.int32, (chunk, chunk), 0)
    ci = lax.broadcasted_iota(jnp.int32, (chunk, chunk), 1)
    row_le_col = ri <= ci
    row_ge_col = ri >= ci

    n_chunks = t // chunk
    g_all = jnp.concatenate([gates_t[:, chunk * c:chunk * (c + 1)] for c in range(n_chunks)], axis=0)
    i_all = pltpu.roll(g_all, M_HEADS, axis=0)
    rows = lax.broadcasted_iota(jnp.int32, (N_GATES * n_chunks, 1), 0)
    is_fwd_row = (rows & (2 * M_HEADS)) == 0
    cum_p = _mm_left_f32(g_all, row_le_col.astype(BF16))
    cum_s = _mm_left_f32(g_all, row_ge_col.astype(BF16))
    b_all = jnp.where(is_fwd_row, cum_p, cum_s)
    u_all = i_all - b_all
    u_cols = u_all.T
    cols = lax.broadcasted_iota(jnp.int32, (1, N_GATES * n_chunks), 1)
    pm_cols = jnp.where((cols & (2 * M_HEADS)) == 0, _running_max_sublanes(u_cols, reverse=False),
                        _running_max_sublanes(u_cols, reverse=True))
    pm_all = pm_cols.T
    u_cols2 = u_cols * LOG2_E
    pm_all2 = pm_all * LOG2_E
    pml_all = jnp.where(is_fwd_row, pm_all[:, chunk - 1:], pm_all[:, :1])
    bl_all = jnp.where(is_fwd_row, b_all[:, chunk - 1:], b_all[:, :1])
    wk_all = jnp.exp(u_all - pml_all)

    if has_state:
        m_state = [m0_ref[M_HEADS * d:M_HEADS * (d + 1), 0:1] for d in range(2)]
        c_state = [[c0_ref[M_HEADS * d + h] for h in range(M_HEADS)] for d in range(2)]
    else:
        m_state = [jnp.zeros((M_HEADS, 1), F32) for _ in range(2)]
        c_state = [[jnp.zeros((LANES, M_DK), F32) for _ in range(M_HEADS)] for _ in range(2)]

    steps = [(j if d == 0 else n_chunks - 1 - j, d) for j in range(n_chunks) for d in range(2)]

    def gate_rows(step):
        c, direction = step
        return N_GATES * c + (2 * direction + 1) * M_HEADS

    def key_query(step):
        rs = slice(chunk * step[0], chunk * (step[0] + 1))
        return [_mm_nt(k_rows[h][rs, :], q_rows[h][rs, :]) for h in range(M_HEADS)]

    def local_sums(step, st):
        c, direction = step
        rs = slice(chunk * c, chunk * (c + 1))
        r0 = gate_rows(step)
        valid = row_le_col if direction == 0 else row_ge_col
        s_loc, x_loc = [], []
        for h in range(M_HEADS):
            w = jnp.exp2(jnp.where(valid, u_cols2[:, r0 + h:r0 + h + 1] - pm_all2[r0 + h:r0 + h + 1, :], NEG_BIG))
            s_loc.append(_mm(vaug_tb[h][:, rs], (st[h] * w).astype(BF16)))
            x_loc.append(_mm((vaug_t[h][:, rs] * wk_all[r0 + h:r0 + h + 1, :]).astype(BF16), k_rows[h][rs, :]))
        return s_loc, x_loc

    written = set()
    n_steps = len(steps)
    st_q = [key_query(steps[i]) for i in range(min(2, n_steps))]
    loc_q = [local_sums(steps[0], st_q.pop(0))]
    for i, (c, direction) in enumerate(steps):
        if i + 1 < n_steps:
            loc_q.append(local_sums(steps[i + 1], st_q.pop(0)))
        if i + 2 < n_steps:
            st_q.append(key_query(steps[i + 2]))
        s_loc, x_loc = loc_q.pop(0)
        rs = slice(chunk * c, chunk * (c + 1))
        r0 = gate_rows((c, direction))
        b4, pm4, bl, pml = (a[r0:r0 + M_HEADS, :] for a in (b_all, pm_all, bl_all, pml_all))
        m = m_state[direction]
        mx = jnp.maximum(m, pm4)
        f_loc = jnp.exp(pm4 - mx)
        f_int = jnp.exp(m - mx)
        floor = jnp.exp(-(b4 + mx))
        m_new = bl + jnp.maximum(m, pml)
        decay = jnp.exp(bl + m - m_new)
        gain = jnp.exp(bl + pml - m_new)
        m_state[direction] = m_new
        inter = [_mm(c_state[direction][h].astype(BF16), qt[heads[h], rs]) for h in range(M_HEADS)]
        for h in range(M_HEADS):
            nd = f_loc[h:h + 1, :] * s_loc[h] + f_int[h:h + 1, :] * inter[h]
            ht = nd[:M_DV, :] / jnp.maximum(jnp.abs(nd[M_DV:, :]), floor[h:h + 1, :])
            c_state[direction][h] = decay[h:h + 1, :] * c_state[direction][h] + gain[h:h + 1, :] * x_loc[h]
            if c in written:
                hst_ref[heads[h], rs] += ht
            else:
                hst_ref[heads[h], rs] = ht
        written.add(c)

    if emit_state:
        for d in range(2):
            ms_ref[M_HEADS * d:M_HEADS * (d + 1), :] = jnp.broadcast_to(m_state[d], (M_HEADS, LANES))
            for h in range(M_HEADS):
                cs_ref[M_HEADS * d + h] = c_state[d][h]

    hsum = hst_ref[...].T
    hn = hsum * _group_inv_rms(hsum, bd_ref[...]) * mg_ref[...]
    o_ref[...] = (_sigmoid(om) * hn).astype(BF16)

    fa = _mm(fu_ref[...].astype(BF16), dft_c_ref[...])
    fo_ref[...] = (_mm(dft_ct_ref[...], fa[:, :F_WIDTH].astype(BF16))
                   - _mm(dft_st_ref[...], fa[:, F_WIDTH:].astype(BF16))).astype(BF16)


def _mlstm_fourier(zm, gt, fu, conv_w, gate_b, m_norm_g, *, batch, t, layer, row0, caug0=None, m0=None):
    has_state = caug0 is not None
    emit_state = not has_state
    chunk = min(M_CHUNK_K, t)
    n_units = 2 * M_HEADS
    const = lambda shape: pl.BlockSpec(shape, lambda b: (0,) * len(shape), pipeline_mode=pl.Buffered(1))
    s0 = row0 // t
    in_specs = [
        pl.BlockSpec((t, D_MODEL), lambda b: (s0 + b, 0)),
        pl.BlockSpec((t, GATE_PAD), lambda b: (s0 + b, 0)),
        _layer_spec((3, 2 * M_HEADS * M_DK), layer),
        _layer_spec((1, GATE_PAD), layer),
        _layer_spec((1, M_WIDTH), layer),
        const((M_WIDTH, M_WIDTH)),
        pl.BlockSpec((t, F_WIDTH), lambda b: (s0 + b, 0)),
        const((F_WIDTH, 2 * F_WIDTH)), const((t, t)), const((t, t)),
    ]
    args = [zm, gt, conv_w, gate_b, m_norm_g, _block_diag_ones(M_WIDTH), fu, *_dft_tables(t)]
    if has_state:
        in_specs += [pl.BlockSpec((None, n_units, LANES, M_DK), lambda b: (b, 0, 0, 0)),
                     pl.BlockSpec((None, n_units, LANES), lambda b: (b, 0, 0))]
        args += [caug0, m0]
    out_specs = [pl.BlockSpec((t, M_WIDTH), lambda b: (b, 0)), pl.BlockSpec((t, F_WIDTH), lambda b: (b, 0))]
    out_shape = [jax.ShapeDtypeStruct((batch * t, M_WIDTH), BF16), jax.ShapeDtypeStruct((batch * t, F_WIDTH), BF16)]
    if emit_state:
        out_specs += [pl.BlockSpec((None, n_units, LANES, M_DK), lambda b: (b, 0, 0, 0)),
                      pl.BlockSpec((None, n_units, LANES), lambda b: (b, 0, 0))]
        out_shape += [jax.ShapeDtypeStruct((batch, n_units, LANES, M_DK), F32),
                      jax.ShapeDtypeStruct((batch, n_units, LANES), F32)]
    return pl.pallas_call(
        functools.partial(_mlstm_kernel, t=t, chunk=chunk, has_state=has_state, emit_state=emit_state),
        grid=(batch,),
        in_specs=in_specs,
        out_specs=out_specs,
        out_shape=out_shape,
        scratch_shapes=[pltpu.VMEM((M_WIDTH, t), F32)],
        compiler_params=_cparams(1),
        name="mlstm_fourier_latent" if has_state else "mlstm_fourier_context",
    )(*args)


def _dft_tables(t):
    kt = (np.arange(t)[:, None] * np.arange(t)[None, :]) % t
    ang_t = 2.0 * np.pi * kt.astype(np.float64) / t
    ct = np.cos(ang_t) / np.sqrt(t)
    st = np.sin(ang_t) / np.sqrt(t)
    c = F_GROUP_CH
    kc = (np.arange(c)[:, None] * np.arange(c)[None, :]) % c
    ang_c = 2.0 * np.pi * kc.astype(np.float64) / c
    eye = np.eye(F_GROUPS)
    cc = np.kron(eye, np.cos(ang_c) / np.sqrt(c))
    sc = np.kron(eye, np.sin(ang_c) / np.sqrt(c))
    cs = np.concatenate([cc, sc], axis=1)
    to_dev = lambda a: jnp.asarray(a.astype(np.float32)).astype(BF16)
    return to_dev(cs), to_dev(ct), to_dev(st)


def _merge_kernel(*refs, rows, split_x):
    refs = list(refs)
    x_refs = [refs.pop(0) for _ in range(2 if split_x else 1)]
    (mod_ref, g_ref, ac_ref, al_ref, hc_ref, hl_ref, fc_ref, fl_ref, wbg_ref,
     wpa_ref, wpm_ref, wpf_ref, wo_ref, o_ref, wbg_s, wpa_s, wpm_s, wpf_s, wo_s) = refs
    _stage(wbg_ref, wbg_s, BG_CHUNKS, last_rows=BG_LAST)
    for w_ref, w_s in ((wpa_ref, wpa_s), (wpm_ref, wpm_s), (wpf_ref, wpf_s), (wo_ref, wo_s)):
        _stage(w_ref, w_s)

    @pl.when(pl.program_id(0) >= BG_CHUNKS)
    def _():
        x = _pick(rows, *x_refs) if split_x else x_refs[0][...]
        h = _normmod(x, g_ref[...], mod_ref[1:2, :], mod_ref[0:1, :]).astype(BF16)

        def branch(j, yc_ref, yl_ref, w_s):
            gate = _sigmoid(_mm_nt(h, wbg_s[BG_SKIP + D_MODEL * j:BG_SKIP + D_MODEL * (j + 1), :]))
            return gate * _mm(_pick(rows, yc_ref, yl_ref), w_s[...])

        merged = (branch(0, ac_ref, al_ref, wpa_s) + branch(1, hc_ref, hl_ref, wpm_s)
                  + branch(2, fc_ref, fl_ref, wpf_s))
        o_ref[...] = x + mod_ref[2:3, :] * _mm(merged.astype(BF16), wo_s[...])


def _merge(xs, mod, g, attn, hm, fo, w_in_t, w_pa, w_pm, w_pf, w_out, rows, layer):
    split_x = len(xs) == 2
    n_rows = sum(x.shape[0] for x in xs)
    x_specs = [rows.spec(D_MODEL, "ctx"), rows.spec(D_MODEL, "lat")] if split_x else [rows.spec(D_MODEL, "both")]
    pair = lambda width: [rows.spec(width, "ctx"), rows.spec(width, "lat")]
    staged = ((ATTN_WIDTH, D_MODEL), (M_WIDTH, D_MODEL), (F_WIDTH, D_MODEL), (D_MODEL, D_MODEL))
    return pl.pallas_call(
        functools.partial(_merge_kernel, rows=rows, split_x=split_x),
        grid=(BG_CHUNKS + n_rows // TM,),
        in_specs=x_specs + [rows.mod_spec(layer), _layer_spec((1, D_MODEL), layer)]
        + pair(ATTN_WIDTH) + pair(M_WIDTH) + pair(F_WIDTH)
        + [_staged_spec((BG_CHUNKS * W_IN_ROWS, D_MODEL), layer, BG_CHUNKS, first=BG_FIRST)]
        + [_staged_spec(shape, layer) for shape in staged],
        out_specs=rows.spec(D_MODEL, "both"),
        out_shape=jax.ShapeDtypeStruct((n_rows, D_MODEL), F32),
        scratch_shapes=[pltpu.VMEM((BG_CHUNKS * W_IN_ROWS, D_MODEL), BF16)]
        + [pltpu.VMEM(shape, BF16) for shape in staged],
        compiler_params=_cparams(1),
        name="merge",
    )(*xs, mod, g, *attn, *hm, *fo, w_in_t, w_pa, w_pm, w_pf, w_out)


def _ffn_kernel(*refs, rows, split_out):
    x_ref, mod_ref, g_ref, win_ref, wout_ref = refs[:5]
    out_refs = refs[5:7] if split_out else refs[5:6]
    win_s, wout_s = refs[-2:]
    _stage(win_ref, win_s)
    _stage(wout_ref, wout_s)

    @pl.when(pl.program_id(0) >= W_STAGE)
    def _():
        x = x_ref[...]
        h = _normmod(x, g_ref[...], mod_ref[4:5, :], mod_ref[3:4, :]).astype(BF16)
        u = _mm(h, win_s[...])
        a = (_silu(u[:, :FF_HIDDEN]) * u[:, FF_HIDDEN:]).astype(BF16)
        y = x + mod_ref[5:6, :] * _mm(a, wout_s[...])
        if split_out:
            @pl.when(rows.is_ctx())
            def _():
                out_refs[0][...] = y

            @pl.when(jnp.logical_not(rows.is_ctx()))
            def _():
                out_refs[1][...] = y
        else:
            out_refs[0][...] = y


def _ffn(x, mod, g, w_in, w_out, rows, layer, split_out):
    n_rows = x.shape[0]
    n_ctx_rows = rows.n_ctx * TM
    staged = ((D_MODEL, 2 * FF_HIDDEN), (FF_HIDDEN, D_MODEL))
    if split_out:
        out_specs = [rows.spec(D_MODEL, "ctx"), rows.spec(D_MODEL, "lat")]
        out_shape = [jax.ShapeDtypeStruct((n_ctx_rows, D_MODEL), F32),
                     jax.ShapeDtypeStruct((n_rows - n_ctx_rows, D_MODEL), F32)]
    else:
        out_specs = [rows.spec(D_MODEL, "both")]
        out_shape = [jax.ShapeDtypeStruct((n_rows, D_MODEL), F32)]
    return pl.pallas_call(
        functools.partial(_ffn_kernel, rows=rows, split_out=split_out),
        grid=(W_STAGE + n_rows // TM,),
        in_specs=[rows.spec(D_MODEL, "both"), rows.mod_spec(layer), _layer_spec((1, D_MODEL), layer)]
        + [_staged_spec(shape, layer) for shape in staged],
        out_specs=out_specs,
        out_shape=out_shape,
        scratch_shapes=[pltpu.VMEM(shape, BF16) for shape in staged],
        compiler_params=_cparams(1),
        name="ffn",
    )(x, mod, g, w_in, w_out)


def _rope_tables(t):
    n = HEAD_DIM // 4
    inv = 1.0 / (ROPE_THETA ** (np.arange(n, dtype=np.float64) / n))
    pos = np.arange(t)
    ang_r = (pos // GRID_W)[:, None] * inv[None, :]
    ang_c = (pos % GRID_W)[:, None] * inv[None, :]
    cos = np.concatenate([np.cos(ang_r)] * 2 + [np.cos(ang_c)] * 2, axis=1)
    sin = np.concatenate([-np.sin(ang_r), np.sin(ang_r), -np.sin(ang_c), np.sin(ang_c)], axis=1)
    tile = lambda a: jnp.asarray(np.tile(a, (1, N_Q_HEADS)).astype(np.float32))
    return tile(cos), tile(sin)


def kernel(x_prompt, x_sample, cache_k, cache_v, state_C, state_n, state_m, c, c_ctx, w_ada, b_ada,
           norm1_g, w_in, q_norm_g, k_norm_g, m_conv_w, m_gate_b, m_norm_g, w_proj_attn,
           w_proj_mlstm, w_proj_fourier, w_out, norm2_g, w_ffn_in, w_ffn_out):
    n_ctx, t_ctx, _ = x_prompt.shape
    n_lat, t_lat, _ = x_sample.shape
    t_past = cache_k.shape[2]
    n_units = 2 * M_HEADS

    cvec = jnp.concatenate([c_ctx[None, :], c], axis=0)
    cvec = jnp.pad(cvec, ((0, MOD_ROWS - cvec.shape[0]), (0, 0)))
    mod = _modulation(cvec, w_ada, b_ada).reshape(DEPTH, MOD_ROWS, N_MOD, D_MODEL)

    rope_tabs = _rope_tables(t_lat)
    ck = cache_k.reshape(n_lat, DEPTH, t_past, KV_WIDTH)
    cv = cache_v.reshape(n_lat, DEPTH, t_past, KV_WIDTH)
    rows_ctx = n_ctx * t_ctx
    rows_in = _Rows(rows_ctx, t_lat, n_stage=MIX_CHUNKS)
    rows_merge = _Rows(rows_ctx, t_lat, n_stage=BG_CHUNKS)
    rows_ffn = _Rows(rows_ctx, t_lat, n_stage=W_STAGE)

    w_in_t = jnp.swapaxes(w_in, 1, 2)
    row = lambda a: a[:, None, :]
    norm1, norm2 = row(norm1_g), row(norm2_g)
    qg, kg = row(jnp.tile(q_norm_g, (1, N_Q_HEADS))), row(jnp.tile(k_norm_g, (1, N_KV_HEADS)))
    gate_b = row(jnp.pad(m_gate_b, ((0, 0), (0, GATE_PAD - N_GATES))))
    m_norm = row(m_norm_g)

    xs = (x_prompt.reshape(rows_ctx, D_MODEL), x_sample.reshape(n_lat * t_lat, D_MODEL))
    ks, vs, cs_all, ms_all = [], [], [], []
    for l in range(DEPTH):
        za, zm, fu, gt = _in_proj(xs, mod, norm1, w_in_t, rows_in, l)
        attn_c, k_l, v_l = _attention(za, qg, kg, batch=n_ctx, t_new=t_ctx, latent=False, row0=0, layer=l)
        hm_c, fo_c, cs, ms = _mlstm_fourier(zm, gt, fu, m_conv_w, gate_b, m_norm, batch=n_ctx, t=t_ctx,
                                            layer=l, row0=0)
        ks.append(k_l.reshape(n_ctx, t_ctx, N_KV_HEADS, HEAD_DIM))
        vs.append(v_l.reshape(n_ctx, t_ctx, N_KV_HEADS, HEAD_DIM))
        cs_all.append(cs)
        ms_all.append(ms)

        c0t = jnp.swapaxes(state_C[:, l].astype(F32).reshape(n_lat, n_units, M_DK, M_DV), -1, -2)
        n0 = state_n[:, l].astype(F32).reshape(n_lat, n_units, 1, M_DK)
        caug0 = jnp.concatenate(
            [c0t, jnp.broadcast_to(n0, (n_lat, n_units, LANES - M_DV, M_DK))], axis=-2)
        m0 = jnp.broadcast_to(state_m[:, l].astype(F32).reshape(n_lat, n_units, 1),
                              (n_lat, n_units, LANES))
        (attn_l,) = _attention(za, qg, kg, batch=n_lat, t_new=t_lat, latent=True, row0=rows_ctx,
                               cache_k=ck, cache_v=cv, layer=l, rope_tabs=rope_tabs)
        hm_l, fo_l = _mlstm_fourier(zm, gt, fu, m_conv_w, gate_b, m_norm, batch=n_lat, t=t_lat, layer=l,
                                    row0=rows_ctx, caug0=caug0, m0=m0)

        x1 = _merge(xs, mod, norm1, (attn_c, attn_l), (hm_c, hm_l), (fo_c, fo_l), w_in_t,
                    w_proj_attn, w_proj_mlstm, w_proj_fourier, w_out, rows_merge, l)
        xs = tuple(_ffn(x1, mod, norm2, w_ffn_in, w_ffn_out, rows_ffn, l, split_out=l == DEPTH - 1))
    xp, xs = xs

    cs_all = jnp.stack(cs_all, axis=1)
    ms_all = jnp.stack(ms_all, axis=1)
    new_c = jnp.swapaxes(cs_all[..., :M_DV, :], -1, -2).reshape(n_ctx, DEPTH, 2, M_HEADS, M_DK, M_DV)
    new_n = cs_all[..., M_DV, :].reshape(n_ctx, DEPTH, 2, M_HEADS, M_DK)
    new_m = ms_all[..., 0].reshape(n_ctx, DEPTH, 2, M_HEADS)
    return (xp.reshape(n_ctx, t_ctx, D_MODEL), xs.reshape(n_lat, t_lat, D_MODEL),
            jnp.stack(ks, axis=1), jnp.stack(vs, axis=1), new_c, new_n, new_m)
```

```python
import functools

import numpy as np
import jax
import jax.numpy as jnp
from jax import lax
from jax.experimental import pallas as pl
from jax.experimental.pallas import tpu as pltpu

D_MODEL = 1024
DEPTH = 2
GRID_W = 64
HEAD_DIM = 64
N_Q_HEADS = 8
N_KV_HEADS = 4
ATTN_WIDTH = N_Q_HEADS * HEAD_DIM
KV_WIDTH = N_KV_HEADS * HEAD_DIM
ROPE_THETA = 10000.0
M_HEADS = 4
M_DK = 64
M_DV = 64
M_WIDTH = M_HEADS * M_DV
F_GROUPS = 4
F_GROUP_CH = 64
F_WIDTH = F_GROUPS * F_GROUP_CH
FF_HIDDEN = -(-8 * D_MODEL // (3 * 256)) * 256
EPS = 1e-6
N_GATES = 4 * M_HEADS
IN_SIZES = (ATTN_WIDTH, KV_WIDTH, KV_WIDTH, M_HEADS * M_DK, M_HEADS * M_DK, M_WIDTH, M_WIDTH,
            N_GATES, F_WIDTH, 3 * D_MODEL)
IN_OFFS = tuple(int(v) for v in np.cumsum((0,) + IN_SIZES))

LANES = 128
GATE_PAD = LANES
N_MOD = 6
MOD_ROWS = 16
M_CHUNK_K = 128
NEG_BIG = -1e30
LOG2_E = 1.4426950408889634
VMEM_LIMIT = 56 * 1024 * 1024
TM = 512
W_STAGE = 8
W_IN_ROWS = 256

F32 = jnp.float32
BF16 = jnp.bfloat16


def _cparams(n_axes):
    return pltpu.CompilerParams(dimension_semantics=("arbitrary",) * n_axes,
                                vmem_limit_bytes=VMEM_LIMIT)


def _layer_spec(shape, layer):
    return pl.BlockSpec((None,) + tuple(shape), lambda *_: (layer,) + (0,) * len(shape),
                        pipeline_mode=pl.Buffered(1))


class _Rows:
    def __init__(self, rows_ctx, t_lat, n_stage=0):
        self.n_ctx = rows_ctx // TM
        self.per_seq = t_lat // TM
        self.n_stage = n_stage

    def tile(self, i):
        return jnp.maximum(i - self.n_stage, 0)

    def both(self, i):
        return (self.tile(i), 0)

    def ctx(self, i):
        return (jnp.minimum(self.tile(i), self.n_ctx - 1), 0)

    def lat(self, i):
        return (jnp.maximum(self.tile(i) - self.n_ctx, 0), 0)

    def mod_spec(self, layer):
        def index(i):
            t = self.tile(i)
            return (layer, jnp.where(t < self.n_ctx, 0, 1 + (t - self.n_ctx) // self.per_seq), 0, 0)
        return pl.BlockSpec((None, None, N_MOD, D_MODEL), index)

    def spec(self, width, which):
        return pl.BlockSpec((TM, width), getattr(self, which))

    def is_ctx(self):
        return pl.program_id(0) - self.n_stage < self.n_ctx


def _pick(rows, ctx_ref, lat_ref):
    return jnp.where(rows.is_ctx(), ctx_ref[...], lat_ref[...])


def _staged_spec(shape, layer, n_chunks=W_STAGE, first=0):
    rc, cols = shape[0] // n_chunks, shape[1]
    return pl.BlockSpec((None, rc, cols), lambda i: (layer, first + jnp.minimum(i, n_chunks - 1), 0))


def _stage(w_ref, scratch_ref, n_chunks=W_STAGE, last_rows=None):
    i = pl.program_id(0)
    rc = w_ref.shape[0]
    n_full = n_chunks if last_rows is None else n_chunks - 1

    @pl.when(i < n_full)
    def _():
        scratch_ref[pl.ds(pl.multiple_of(i * rc, rc), rc), :] = w_ref[...].astype(BF16)

    if last_rows is not None:
        @pl.when(i == n_full)
        def _():
            scratch_ref[n_full * rc:n_full * rc + last_rows, :] = w_ref[:last_rows, :].astype(BF16)


def _mm(a, b):
    return jnp.dot(a, b, preferred_element_type=F32)


def _mm_nt(a, b):
    return lax.dot_general(a, b, (((1,), (1,)), ((), ())), preferred_element_type=F32)


def _split3(x):
    hi = x.astype(BF16)
    r = x - hi.astype(F32)
    mid = r.astype(BF16)
    lo = (r - mid.astype(F32)).astype(BF16)
    return hi, mid, lo


def _mm_left_f32(x, m_bf16):
    hi, mid, lo = _split3(x)
    return _mm(hi, m_bf16) + _mm(mid, m_bf16) + _mm(lo, m_bf16)


def _sigmoid(x):
    return 1.0 / (1.0 + jnp.exp(-x))


def _silu(x):
    return x * _sigmoid(x)


def _log_sigmoid(x):
    return jnp.minimum(x, 0.0) - jnp.log(1.0 + jnp.exp(-jnp.abs(x)))


def _normmod(x, g, scale, shift):
    ms = jnp.mean(x * x, axis=-1, keepdims=True)
    return (x * lax.rsqrt(ms + EPS)) * g * (1.0 + scale) + shift


def _group_inv_rms(x, bd):
    x2 = x * x
    hi = x2.astype(BF16)
    lo = (x2 - hi.astype(F32)).astype(BF16)
    ss = _mm(hi, bd) + _mm(lo, bd)
    return lax.rsqrt(ss * (1.0 / HEAD_DIM) + EPS)


def _rope(x, cos, sin_signed):
    w = x.shape[1]
    lane = lax.broadcasted_iota(jnp.int32, x.shape, 1)
    up = pltpu.roll(x, w - 16, axis=1)
    dn = pltpu.roll(x, 16, axis=1)
    partner = jnp.where((lane & 31) < 16, up, dn)
    return x * cos + partner * sin_signed


def _mod_kernel(c_ref, w_ref, b_ref, o_ref):
    s = _silu(c_ref[...]).astype(BF16)
    o_ref[...] = _mm(s, w_ref[...].astype(BF16)) + b_ref[...]


def _modulation(cvec, w_ada, b_ada):
    tn = 1024
    n_out = N_MOD * D_MODEL
    return pl.pallas_call(
        _mod_kernel,
        grid=(DEPTH, n_out // tn),
        in_specs=[
            pl.BlockSpec((MOD_ROWS, D_MODEL), lambda l, j: (0, 0)),
            pl.BlockSpec((None, D_MODEL, tn), lambda l, j: (l, 0, j)),
            pl.BlockSpec((None, 1, tn), lambda l, j: (l, 0, j)),
        ],
        out_specs=pl.BlockSpec((None, MOD_ROWS, tn), lambda l, j: (l, 0, j)),
        out_shape=jax.ShapeDtypeStruct((DEPTH, MOD_ROWS, n_out), F32),
        compiler_params=_cparams(2),
        name="modulation",
    )(cvec, w_ada, b_ada.reshape(DEPTH, 1, n_out))


GATE_OFF, FU_OFF, BG_OFF = IN_OFFS[7], IN_OFFS[8], IN_OFFS[9]
IN_COLS = IN_OFFS[-1]
MIX_CHUNKS = -(-BG_OFF // W_IN_ROWS)
BG_FIRST = BG_OFF // W_IN_ROWS
BG_CHUNKS = -(-IN_COLS // W_IN_ROWS) - BG_FIRST
BG_SKIP = BG_OFF - BG_FIRST * W_IN_ROWS
BG_LAST = IN_COLS - (BG_FIRST + BG_CHUNKS - 1) * W_IN_ROWS


def _in_kernel(*refs, rows, split_x):
    refs = list(refs)
    x_refs = [refs.pop(0) for _ in range(2 if split_x else 1)]
    mod_ref, g_ref, w_ref, za_ref, zm_ref, fu_ref, gt_ref, w_s = refs
    _stage(w_ref, w_s, MIX_CHUNKS)

    @pl.when(pl.program_id(0) >= MIX_CHUNKS)
    def _():
        x = _pick(rows, *x_refs) if split_x else x_refs[0][...]
        h = _normmod(x, g_ref[...], mod_ref[1:2, :], mod_ref[0:1, :])
        z = _mm_nt(h.astype(BF16), w_s[:BG_OFF, :])
        za_ref[...] = z[:, :D_MODEL]
        zm_ref[...] = z[:, D_MODEL:GATE_OFF]
        gt_ref[...] = z[:, GATE_OFF:GATE_OFF + GATE_PAD]
        fu_ref[...] = z[:, FU_OFF:FU_OFF + F_WIDTH].astype(BF16)


def _in_proj(xs, mod, g, w_in_t, rows, layer):
    split_x = len(xs) == 2
    n_rows = sum(x.shape[0] for x in xs)
    x_specs = [rows.spec(D_MODEL, "ctx"), rows.spec(D_MODEL, "lat")] if split_x else [rows.spec(D_MODEL, "both")]
    widths = (D_MODEL, D_MODEL, F_WIDTH, GATE_PAD)
    dtypes = (F32, F32, BF16, F32)
    return pl.pallas_call(
        functools.partial(_in_kernel, rows=rows, split_x=split_x),
        grid=(MIX_CHUNKS + n_rows // TM,),
        in_specs=x_specs + [
            rows.mod_spec(layer),
            _layer_spec((1, D_MODEL), layer),
            _staged_spec((MIX_CHUNKS * W_IN_ROWS, D_MODEL), layer, MIX_CHUNKS),
        ],
        out_specs=[rows.spec(w, "both") for w in widths],
        out_shape=[jax.ShapeDtypeStruct((n_rows, w), d) for w, d in zip(widths, dtypes)],
        scratch_shapes=[pltpu.VMEM((MIX_CHUNKS * W_IN_ROWS, D_MODEL), BF16)],
        compiler_params=_cparams(1),
        name="in_proj",
    )(*xs, mod, g, w_in_t)


def _attn_kernel(*refs, tq, t_new, t_past, latent):
    if latent:
        (q_ref, k_ref, v_ref, ck_ref, cv_ref, cosq_ref, sinq_ref, cosk_ref, sink_ref,
         qg_ref, kg_ref, bdq_ref, bdk_ref, o_ref, ks_ref, vs_ref) = refs
    else:
        (q_ref, k_ref, v_ref, qg_ref, kg_ref, bdq_ref, bdk_ref,
         o_ref, nk_ref, nv_ref, ks_ref, vs_ref) = refs

    @pl.when(pl.program_id(1) == 0)
    def _():
        k = k_ref[...]
        kn = k * _group_inv_rms(k, bdk_ref[...]) * kg_ref[...]
        v = v_ref[...]
        if latent:
            kn = _rope(kn, cosk_ref[...], sink_ref[...])
            ck = ck_ref[...]
            cv = cv_ref[...]
        else:
            nk_ref[...] = kn
            nv_ref[...] = v
        for h in range(N_KV_HEADS):
            sl = slice(HEAD_DIM * h, HEAD_DIM * (h + 1))
            if latent:
                ks_ref[h, :t_past, :] = ck[:, sl].astype(BF16)
                vs_ref[h, :t_past, :HEAD_DIM] = cv[:, sl].astype(BF16)
            ks_ref[h, t_past:, :] = kn[:, sl].astype(BF16)
            vs_ref[h, t_past:, :HEAD_DIM] = v[:, sl].astype(BF16)
            vs_ref[h, :, HEAD_DIM:] = jnp.ones((t_past + t_new, LANES - HEAD_DIM), BF16)

    q = q_ref[...]
    qn = q * _group_inv_rms(q, bdq_ref[...]) * qg_ref[...]
    if latent:
        qn = _rope(qn, cosq_ref[...], sinq_ref[...])
    qn = qn * (HEAD_DIM ** -0.5 * LOG2_E)
    group = N_Q_HEADS // N_KV_HEADS
    outs = []

    def scores(h):
        qh = jnp.concatenate(
            [qn[:, HEAD_DIM * (group * h + g):HEAD_DIM * (group * h + g + 1)] for g in range(group)],
            axis=0).astype(BF16)
        return _mm_nt(qh, ks_ref[h])

    ahead = 1 if latent else N_KV_HEADS - 1
    s_q = [scores(h) for h in range(ahead)]
    for h in range(N_KV_HEADS):
        s = s_q.pop(0)
        if h + ahead < N_KV_HEADS:
            s_q.append(scores(h + ahead))
        m = jnp.max(s, axis=-1, keepdims=True)
        p = jnp.exp2(s - m).astype(BF16)
        od = _mm(p, vs_ref[h])
        o = (od / pltpu.roll(od, LANES - HEAD_DIM, axis=1))[:, :HEAD_DIM]
        outs.extend(o[tq * g:tq * (g + 1)] for g in range(group))
    o_ref[...] = jnp.concatenate(outs, axis=1).astype(BF16)


def _block_diag_ones(width):
    idx = np.arange(width) // HEAD_DIM
    return jnp.asarray(idx[:, None] == idx[None, :], dtype=BF16)


def _attention(za, qg, kg, *, batch, t_new, latent, row0, cache_k=None, cache_v=None, layer=0, rope_tabs=None):
    tq = min(1024, t_new)
    nq = t_new // tq
    t_past = cache_k.shape[2] if latent else 0
    t_keys = t_past + t_new
    kcol = ATTN_WIDTH // KV_WIDTH
    bdq = _block_diag_ones(ATTN_WIDTH)
    bdk = _block_diag_ones(KV_WIDTH)
    const = lambda shape: pl.BlockSpec(shape, lambda b, i: (0,) * len(shape))
    q0, s0 = row0 // tq, row0 // t_new
    in_specs = [
        pl.BlockSpec((tq, ATTN_WIDTH), lambda b, i: (q0 + b * nq + i, 0)),
        pl.BlockSpec((t_new, KV_WIDTH), lambda b, i: (s0 + b, kcol)),
        pl.BlockSpec((t_new, KV_WIDTH), lambda b, i: (s0 + b, kcol + 1)),
    ]
    args = [za, za, za]
    if latent:
        cos_t, sin_t = rope_tabs
        in_specs += [
            pl.BlockSpec((None, None, t_past, KV_WIDTH), lambda b, i: (b, layer, 0, 0)),
            pl.BlockSpec((None, None, t_past, KV_WIDTH), lambda b, i: (b, layer, 0, 0)),
            pl.BlockSpec((tq, ATTN_WIDTH), lambda b, i: (i, 0)),
            pl.BlockSpec((tq, ATTN_WIDTH), lambda b, i: (i, 0)),
            pl.BlockSpec((t_new, KV_WIDTH), lambda b, i: (0, 0)),
            pl.BlockSpec((t_new, KV_WIDTH), lambda b, i: (0, 0)),
        ]
        args += [cache_k, cache_v, cos_t, sin_t, cos_t, sin_t]
    in_specs += [_layer_spec((1, ATTN_WIDTH), layer), _layer_spec((1, KV_WIDTH), layer),
                 const((ATTN_WIDTH, ATTN_WIDTH)), const((KV_WIDTH, KV_WIDTH))]
    args += [qg, kg, bdq, bdk]
    out_specs = [pl.BlockSpec((tq, ATTN_WIDTH), lambda b, i: (b * nq + i, 0))]
    out_shape = [jax.ShapeDtypeStruct((batch * t_new, ATTN_WIDTH), BF16)]
    if not latent:
        out_specs += [pl.BlockSpec((t_new, KV_WIDTH), lambda b, i: (b, 0))] * 2
        out_shape += [jax.ShapeDtypeStruct((batch * t_new, KV_WIDTH), F32)] * 2
    return pl.pallas_call(
        functools.partial(_attn_kernel, tq=tq, t_new=t_new, t_past=t_past, latent=latent),
        grid=(batch, nq),
        in_specs=in_specs,
        out_specs=out_specs,
        out_shape=out_shape,
        scratch_shapes=[pltpu.VMEM((N_KV_HEADS, t_keys, HEAD_DIM), BF16),
                        pltpu.VMEM((N_KV_HEADS, t_keys, LANES), BF16)],
        compiler_params=_cparams(2),
        name="attention_latent" if latent else "attention_context",
    )(*args)


def _running_max_sublanes(x, reverse):
    n = x.shape[0]
    row = lax.broadcasted_iota(jnp.int32, x.shape, 0)
    k = 1
    while k < n:
        if reverse:
            cand = jnp.where(row < n - k, pltpu.roll(x, n - k, axis=0), NEG_BIG)
        else:
            cand = jnp.where(row >= k, pltpu.roll(x, k, axis=0), NEG_BIG)
        x = jnp.maximum(x, cand)
        k *= 2
    return x


def _mlstm_kernel(*refs, t, chunk, has_state, emit_state):
    refs = list(refs)
    zm_ref, gt_ref, cw_ref, gb_ref, mg_ref, bd_ref, fu_ref, dft_c_ref, dft_ct_ref, dft_st_ref = refs[:10]
    refs = refs[10:]
    if has_state:
        c0_ref, m0_ref = refs[:2]
        refs = refs[2:]
    o_ref, fo_ref = refs[:2]
    refs = refs[2:]
    if emit_state:
        cs_ref, ms_ref = refs[:2]
        refs = refs[2:]
    hst_ref = refs[0]

    width = M_HEADS * M_DK
    z = zm_ref[...]
    x = z[:, :2 * width]
    row = lax.broadcasted_iota(jnp.int32, (t, 1), 0)
    x_prev = jnp.where(row == 0, 0.0, pltpu.roll(x, 1, axis=0))
    x_next = jnp.where(row == t - 1, 0.0, pltpu.roll(x, t - 1, axis=0))
    cw = cw_ref[...]
    qk = _silu(x_prev * cw[0:1, :] + x * cw[1:2, :] + x_next * cw[2:3, :])
    q = qk[:, :width]
    k = qk[:, width:] * (M_DK ** -0.5)
    v = z[:, 2 * width:2 * width + M_WIDTH]
    om = z[:, 2 * width + M_WIDTH:]
    heads = [slice(M_DK * h, M_DK * (h + 1)) for h in range(M_HEADS)]
    q_rows = [q[:, hs].astype(BF16) for hs in heads]
    k_rows = [k[:, hs].astype(BF16) for hs in heads]
    qt = q.T.astype(BF16)
    vt = v.T
    ones_r = jnp.ones((LANES - M_DV, t), F32)
    vaug_t = [jnp.concatenate([vt[hs, :], ones_r], axis=0) for hs in heads]
    vaug_tb = [a.astype(BF16) for a in vaug_t]

    gates = gt_ref[...] + gb_ref[...]
    lane = lax.broadcasted_iota(jnp.int32, (1, GATE_PAD), 1)
    is_forget = (lane & M_HEADS) != 0
    gates = jnp.where(is_forget, _log_sigmoid(gates), gates)
    gates_t = gates.T[:N_GATES, :]

    ri = lax.broadcasted_iota(jnp.int32, (chunk, chunk), 0)
    ci = lax.broadcasted_iota(jnp.int32, (chunk, chunk), 1)
    row_le_col = ri <= ci
    row_ge_col = ri >= ci

    n_chunks = t // chunk
    g_all = jnp.concatenate([gates_t[:, chunk * c:chunk * (c + 1)] for c in range(n_chunks)], axis=0)
    i_all = pltpu.roll(g_all, M_HEADS, axis=0)
    rows = lax.broadcasted_iota(jnp.int32, (N_GATES * n_chunks, 1), 0)
    is_fwd_row = (rows & (2 * M_HEADS)) == 0
    cum_p = _mm_left_f32(g_all, row_le_col.astype(BF16))
    cum_s = _mm_left_f32(g_all, row_ge_col.astype(BF16))
    b_all = jnp.where(is_fwd_row, cum_p, cum_s)
    u_all = i_all - b_all
    u_cols = u_all.T
    cols = lax.broadcasted_iota(jnp.int32, (1, N_GATES * n_chunks), 1)
    pm_cols = jnp.where((cols & (2 * M_HEADS)) == 0, _running_max_sublanes(u_cols, reverse=False),
                        _running_max_sublanes(u_cols, reverse=True))
    pm_all = pm_cols.T
    u_cols2 = u_cols * LOG2_E
    pm_all2 = pm_all * LOG2_E
    pml_all = jnp.where(is_fwd_row, pm_all[:, chunk - 1:], pm_all[:, :1])
    bl_all = jnp.where(is_fwd_row, b_all[:, chunk - 1:], b_all[:, :1])
    wk_all = jnp.exp(u_all - pml_all)

    if has_state:
        m_state = [m0_ref[M_HEADS * d:M_HEADS * (d + 1), 0:1] for d in range(2)]
        c_state = [[c0_ref[M_HEADS * d + h] for h in range(M_HEADS)] for d in range(2)]
    else:
        m_state = [jnp.zeros((M_HEADS, 1), F32) for _ in range(2)]
        c_state = [[jnp.zeros((LANES, M_DK), F32) for _ in range(M_HEADS)] for _ in range(2)]

    steps = [(j if d == 0 else n_chunks - 1 - j, d) for j in range(n_chunks) for d in range(2)]

    def gate_rows(step):
        c, direction = step
        return N_GATES * c + (2 * direction + 1) * M_HEADS

    def key_query(step):
        rs = slice(chunk * step[0], chunk * (step[0] + 1))
        return [_mm_nt(k_rows[h][rs, :], q_rows[h][rs, :]) for h in range(M_HEADS)]

    def local_sums(step, st):
        c, direction = step
        rs = slice(chunk * c, chunk * (c + 1))
        r0 = gate_rows(step)
        valid = row_le_col if direction == 0 else row_ge_col
        s_loc, x_loc = [], []
        for h in range(M_HEADS):
            w = jnp.exp2(jnp.where(valid, u_cols2[:, r0 + h:r0 + h + 1] - pm_all2[r0 + h:r0 + h + 1, :], NEG_BIG))
            s_loc.append(_mm(vaug_tb[h][:, rs], (st[h] * w).astype(BF16)))
            x_loc.append(_mm((vaug_t[h][:, rs] * wk_all[r0 + h:r0 + h + 1, :]).astype(BF16), k_rows[h][rs, :]))
        return s_loc, x_loc

    written = set()
    n_steps = len(steps)
    st_q = [key_query(steps[i]) for i in range(min(2, n_steps))]
    loc_q = [local_sums(steps[0], st_q.pop(0))]
    for i, (c, direction) in enumerate(steps):
        if i + 1 < n_steps:
            loc_q.append(local_sums(steps[i + 1], st_q.pop(0)))
        if i + 2 < n_steps:
            st_q.append(key_query(steps[i + 2]))
        s_loc, x_loc = loc_q.pop(0)
        rs = slice(chunk * c, chunk * (c + 1))
        r0 = gate_rows((c, direction))
        b4, pm4, bl, pml = (a[r0:r0 + M_HEADS, :] for a in (b_all, pm_all, bl_all, pml_all))
        m = m_state[direction]
        mx = jnp.maximum(m, pm4)
        f_loc = jnp.exp(pm4 - mx)
        f_int = jnp.exp(m - mx)
        floor = jnp.exp(-(b4 + mx))
        m_new = bl + jnp.maximum(m, pml)
        decay = jnp.exp(bl + m - m_new)
        gain = jnp.exp(bl + pml - m_new)
        m_state[direction] = m_new
        inter = [_mm(c_state[direction][h].astype(BF16), qt[heads[h], rs]) for h in range(M_HEADS)]
        for h in range(M_HEADS):
            nd = f_loc[h:h + 1, :] * s_loc[h] + f_int[h:h + 1, :] * inter[h]
            ht = nd[:M_DV, :] / jnp.maximum(jnp.abs(nd[M_DV:, :]), floor[h:h + 1, :])
            c_state[direction][h] = decay[h:h + 1, :] * c_state[direction][h] + gain[h:h + 1, :] * x_loc[h]
            if c in written:
                hst_ref[heads[h], rs] += ht
            else:
                hst_ref[heads[h], rs] = ht
        written.add(c)

    if emit_state:
        for d in range(2):
            ms_ref[M_HEADS * d:M_HEADS * (d + 1), :] = jnp.broadcast_to(m_state[d], (M_HEADS, LANES))
            for h in range(M_HEADS):
                cs_ref[M_HEADS * d + h] = c_state[d][h]

    hsum = hst_ref[...].T
    hn = hsum * _group_inv_rms(hsum, bd_ref[...]) * mg_ref[...]
    o_ref[...] = (_sigmoid(om) * hn).astype(BF16)

    fa = _mm(fu_ref[...].astype(BF16), dft_c_ref[...])
    fo_ref[...] = (_mm(dft_ct_ref[...], fa[:, :F_WIDTH].astype(BF16))
                   - _mm(dft_st_ref[...], fa[:, F_WIDTH:].astype(BF16))).astype(BF16)


def _mlstm_fourier(zm, gt, fu, conv_w, gate_b, m_norm_g, *, batch, t, layer, row0, caug0=None, m0=None):
    has_state = caug0 is not None
    emit_state = not has_state
    chunk = min(M_CHUNK_K, t)
    n_units = 2 * M_HEADS
    const = lambda shape: pl.BlockSpec(shape, lambda b: (0,) * len(shape), pipeline_mode=pl.Buffered(1))
    s0 = row0 // t
    in_specs = [
        pl.BlockSpec((t, D_MODEL), lambda b: (s0 + b, 0)),
        pl.BlockSpec((t, GATE_PAD), lambda b: (s0 + b, 0)),
        _layer_spec((3, 2 * M_HEADS * M_DK), layer),
        _layer_spec((1, GATE_PAD), layer),
        _layer_spec((1, M_WIDTH), layer),
        const((M_WIDTH, M_WIDTH)),
        pl.BlockSpec((t, F_WIDTH), lambda b: (s0 + b, 0)),
        const((F_WIDTH, 2 * F_WIDTH)), const((t, t)), const((t, t)),
    ]
    args = [zm, gt, conv_w, gate_b, m_norm_g, _block_diag_ones(M_WIDTH), fu, *_dft_tables(t)]
    if has_state:
        in_specs += [pl.BlockSpec((None, n_units, LANES, M_DK), lambda b: (b, 0, 0, 0)),
                     pl.BlockSpec((None, n_units, LANES), lambda b: (b, 0, 0))]
        args += [caug0, m0]
    out_specs = [pl.BlockSpec((t, M_WIDTH), lambda b: (b, 0)), pl.BlockSpec((t, F_WIDTH), lambda b: (b, 0))]
    out_shape = [jax.ShapeDtypeStruct((batch * t, M_WIDTH), BF16), jax.ShapeDtypeStruct((batch * t, F_WIDTH), BF16)]
    if emit_state:
        out_specs += [pl.BlockSpec((None, n_units, LANES, M_DK), lambda b: (b, 0, 0, 0)),
                      pl.BlockSpec((None, n_units, LANES), lambda b: (b, 0, 0))]
        out_shape += [jax.ShapeDtypeStruct((batch, n_units, LANES, M_DK), F32),
                      jax.ShapeDtypeStruct((batch, n_units, LANES), F32)]
    return pl.pallas_call(
        functools.partial(_mlstm_kernel, t=t, chunk=chunk, has_state=has_state, emit_state=emit_state),
        grid=(batch,),
        in_specs=in_specs,
        out_specs=out_specs,
        out_shape=out_shape,
        scratch_shapes=[pltpu.VMEM((M_WIDTH, t), F32)],
        compiler_params=_cparams(1),
        name="mlstm_fourier_latent" if has_state else "mlstm_fourier_context",
    )(*args)


def _dft_tables(t):
    kt = (np.arange(t)[:, None] * np.arange(t)[None, :]) % t
    ang_t = 2.0 * np.pi * kt.astype(np.float64) / t
    ct = np.cos(ang_t) / np.sqrt(t)
    st = np.sin(ang_t) / np.sqrt(t)
    c = F_GROUP_CH
    kc = (np.arange(c)[:, None] * np.arange(c)[None, :]) % c
    ang_c = 2.0 * np.pi * kc.astype(np.float64) / c
    eye = np.eye(F_GROUPS)
    cc = np.kron(eye, np.cos(ang_c) / np.sqrt(c))
    sc = np.kron(eye, np.sin(ang_c) / np.sqrt(c))
    cs = np.concatenate([cc, sc], axis=1)
    to_dev = lambda a: jnp.asarray(a.astype(np.float32)).astype(BF16)
    return to_dev(cs), to_dev(ct), to_dev(st)


def _merge_kernel(*refs, rows, split_x):
    refs = list(refs)
    x_refs = [refs.pop(0) for _ in range(2 if split_x else 1)]
    (mod_ref, g_ref, ac_ref, al_ref, hc_ref, hl_ref, fc_ref, fl_ref, wbg_ref,
     wpa_ref, wpm_ref, wpf_ref, wo_ref, o_ref, wbg_s, wpa_s, wpm_s, wpf_s, wo_s) = refs
    _stage(wbg_ref, wbg_s, BG_CHUNKS, last_rows=BG_LAST)
    for w_ref, w_s in ((wpa_ref, wpa_s), (wpm_ref, wpm_s), (wpf_ref, wpf_s), (wo_ref, wo_s)):
        _stage(w_ref, w_s)

    @pl.when(pl.program_id(0) >= BG_CHUNKS)
    def _():
        x = _pick(rows, *x_refs) if split_x else x_refs[0][...]
        h = _normmod(x, g_ref[...], mod_ref[1:2, :], mod_ref[0:1, :]).astype(BF16)

        def branch(j, yc_ref, yl_ref, w_s):
            gate = _sigmoid(_mm_nt(h, wbg_s[BG_SKIP + D_MODEL * j:BG_SKIP + D_MODEL * (j + 1), :]))
            return gate * _mm(_pick(rows, yc_ref, yl_ref), w_s[...])

        merged = (branch(0, ac_ref, al_ref, wpa_s) + branch(1, hc_ref, hl_ref, wpm_s)
                  + branch(2, fc_ref, fl_ref, wpf_s))
        o_ref[...] = x + mod_ref[2:3, :] * _mm(merged.astype(BF16), wo_s[...])


def _merge(xs, mod, g, attn, hm, fo, w_in_t, w_pa, w_pm, w_pf, w_out, rows, layer):
    split_x = len(xs) == 2
    n_rows = sum(x.shape[0] for x in xs)
    x_specs = [rows.spec(D_MODEL, "ctx"), rows.spec(D_MODEL, "lat")] if split_x else [rows.spec(D_MODEL, "both")]
    pair = lambda width: [rows.spec(width, "ctx"), rows.spec(width, "lat")]
    staged = ((ATTN_WIDTH, D_MODEL), (M_WIDTH, D_MODEL), (F_WIDTH, D_MODEL), (D_MODEL, D_MODEL))
    return pl.pallas_call(
        functools.partial(_merge_kernel, rows=rows, split_x=split_x),
        grid=(BG_CHUNKS + n_rows // TM,),
        in_specs=x_specs + [rows.mod_spec(layer), _layer_spec((1, D_MODEL), layer)]
        + pair(ATTN_WIDTH) + pair(M_WIDTH) + pair(F_WIDTH)
        + [_staged_spec((BG_CHUNKS * W_IN_ROWS, D_MODEL), layer, BG_CHUNKS, first=BG_FIRST)]
        + [_staged_spec(shape, layer) for shape in staged],
        out_specs=rows.spec(D_MODEL, "both"),
        out_shape=jax.ShapeDtypeStruct((n_rows, D_MODEL), F32),
        scratch_shapes=[pltpu.VMEM((BG_CHUNKS * W_IN_ROWS, D_MODEL), BF16)]
        + [pltpu.VMEM(shape, BF16) for shape in staged],
        compiler_params=_cparams(1),
        name="merge",
    )(*xs, mod, g, *attn, *hm, *fo, w_in_t, w_pa, w_pm, w_pf, w_out)


def _ffn_kernel(*refs, rows, split_out):
    x_ref, mod_ref, g_ref, win_ref, wout_ref = refs[:5]
    out_refs = refs[5:7] if split_out else refs[5:6]
    win_s, wout_s = refs[-2:]
    _stage(win_ref, win_s)
    _stage(wout_ref, wout_s)

    @pl.when(pl.program_id(0) >= W_STAGE)
    def _():
        x = x_ref[...]
        h = _normmod(x, g_ref[...], mod_ref[4:5, :], mod_ref[3:4, :]).astype(BF16)
        u = _mm(h, win_s[...])
        a = (_silu(u[:, :FF_HIDDEN]) * u[:, FF_HIDDEN:]).astype(BF16)
        y = x + mod_ref[5:6, :] * _mm(a, wout_s[...])
        if split_out:
            @pl.when(rows.is_ctx())
            def _():
                out_refs[0][...] = y

            @pl.when(jnp.logical_not(rows.is_ctx()))
            def _():
                out_refs[1][...] = y
        else:
            out_refs[0][...] = y


def _ffn(x, mod, g, w_in, w_out, rows, layer, split_out):
    n_rows = x.shape[0]
    n_ctx_rows = rows.n_ctx * TM
    staged = ((D_MODEL, 2 * FF_HIDDEN), (FF_HIDDEN, D_MODEL))
    if split_out:
        out_specs = [rows.spec(D_MODEL, "ctx"), rows.spec(D_MODEL, "lat")]
        out_shape = [jax.ShapeDtypeStruct((n_ctx_rows, D_MODEL), F32),
                     jax.ShapeDtypeStruct((n_rows - n_ctx_rows, D_MODEL), F32)]
    else:
        out_specs = [rows.spec(D_MODEL, "both")]
        out_shape = [jax.ShapeDtypeStruct((n_rows, D_MODEL), F32)]
    return pl.pallas_call(
        functools.partial(_ffn_kernel, rows=rows, split_out=split_out),
        grid=(W_STAGE + n_rows // TM,),
        in_specs=[rows.spec(D_MODEL, "both"), rows.mod_spec(layer), _layer_spec((1, D_MODEL), layer)]
        + [_staged_spec(shape, layer) for shape in staged],
        out_specs=out_specs,
        out_shape=out_shape,
        scratch_shapes=[pltpu.VMEM(shape, BF16) for shape in staged],
        compiler_params=_cparams(1),
        name="ffn",
    )(x, mod, g, w_in, w_out)


def _rope_tables(t):
    n = HEAD_DIM // 4
    inv = 1.0 / (ROPE_THETA ** (np.arange(n, dtype=np.float64) / n))
    pos = np.arange(t)
    ang_r = (pos // GRID_W)[:, None] * inv[None, :]
    ang_c = (pos % GRID_W)[:, None] * inv[None, :]
    cos = np.concatenate([np.cos(ang_r)] * 2 + [np.cos(ang_c)] * 2, axis=1)
    sin = np.concatenate([-np.sin(ang_r), np.sin(ang_r), -np.sin(ang_c), np.sin(ang_c)], axis=1)
    tile = lambda a: jnp.asarray(np.tile(a, (1, N_Q_HEADS)).astype(np.float32))
    return tile(cos), tile(sin)


def kernel(x_prompt, x_sample, cache_k, cache_v, state_C, state_n, state_m, c, c_ctx, w_ada, b_ada,
           norm1_g, w_in, q_norm_g, k_norm_g, m_conv_w, m_gate_b, m_norm_g, w_proj_attn,
           w_proj_mlstm, w_proj_fourier, w_out, norm2_g, w_ffn_in, w_ffn_out):
    n_ctx, t_ctx, _ = x_prompt.shape
    n_lat, t_lat, _ = x_sample.shape
    t_past = cache_k.shape[2]
    n_units = 2 * M_HEADS

    cvec = jnp.concatenate([c_ctx[None, :], c], axis=0)
    cvec = jnp.pad(cvec, ((0, MOD_ROWS - cvec.shape[0]), (0, 0)))
    mod = _modulation(cvec, w_ada, b_ada).reshape(DEPTH, MOD_ROWS, N_MOD, D_MODEL)

    rope_tabs = _rope_tables(t_lat)
    ck = cache_k.reshape(n_lat, DEPTH, t_past, KV_WIDTH)
    cv = cache_v.reshape(n_lat, DEPTH, t_past, KV_WIDTH)
    rows_ctx = n_ctx * t_ctx
    rows_in = _Rows(rows_ctx, t_lat, n_stage=MIX_CHUNKS)
    rows_merge = _Rows(rows_ctx, t_lat, n_stage=BG_CHUNKS)
    rows_ffn = _Rows(rows_ctx, t_lat, n_stage=W_STAGE)

    w_in_t = jnp.swapaxes(w_in, 1, 2)
    row = lambda a: a[:, None, :]
    norm1, norm2 = row(norm1_g), row(norm2_g)
    qg, kg = row(jnp.tile(q_norm_g, (1, N_Q_HEADS))), row(jnp.tile(k_norm_g, (1, N_KV_HEADS)))
    gate_b = row(jnp.pad(m_gate_b, ((0, 0), (0, GATE_PAD - N_GATES))))
    m_norm = row(m_norm_g)

    xs = (x_prompt.reshape(rows_ctx, D_MODEL), x_sample.reshape(n_lat * t_lat, D_MODEL))
    ks, vs, cs_all, ms_all = [], [], [], []
    for l in range(DEPTH):
        za, zm, fu, gt = _in_proj(xs, mod, norm1, w_in_t, rows_in, l)
        attn_c, k_l, v_l = _attention(za, qg, kg, batch=n_ctx, t_new=t_ctx, latent=False, row0=0, layer=l)
        hm_c, fo_c, cs, ms = _mlstm_fourier(zm, gt, fu, m_conv_w, gate_b, m_norm, batch=n_ctx, t=t_ctx,
                                            layer=l, row0=0)
        ks.append(k_l.reshape(n_ctx, t_ctx, N_KV_HEADS, HEAD_DIM))
        vs.append(v_l.reshape(n_ctx, t_ctx, N_KV_HEADS, HEAD_DIM))
        cs_all.append(cs)
        ms_all.append(ms)

        c0t = jnp.swapaxes(state_C[:, l].astype(F32).reshape(n_lat, n_units, M_DK, M_DV), -1, -2)
        n0 = state_n[:, l].astype(F32).reshape(n_lat, n_units, 1, M_DK)
        caug0 = jnp.concatenate(
            [c0t, jnp.broadcast_to(n0, (n_lat, n_units, LANES - M_DV, M_DK))], axis=-2)
        m0 = jnp.broadcast_to(state_m[:, l].astype(F32).reshape(n_lat, n_units, 1),
                              (n_lat, n_units, LANES))
        (attn_l,) = _attention(za, qg, kg, batch=n_lat, t_new=t_lat, latent=True, row0=rows_ctx,
                               cache_k=ck, cache_v=cv, layer=l, rope_tabs=rope_tabs)
        hm_l, fo_l = _mlstm_fourier(zm, gt, fu, m_conv_w, gate_b, m_norm, batch=n_lat, t=t_lat, layer=l,
                                    row0=rows_ctx, caug0=caug0, m0=m0)

        x1 = _merge(xs, mod, norm1, (attn_c, attn_l), (hm_c, hm_l), (fo_c, fo_l), w_in_t,
                    w_proj_attn, w_proj_mlstm, w_proj_fourier, w_out, rows_merge, l)
        xs = tuple(_ffn(x1, mod, norm2, w_ffn_in, w_ffn_out, rows_ffn, l, split_out=l == DEPTH - 1))
    xp, xs = xs

    cs_all = jnp.stack(cs_all, axis=1)
    ms_all = jnp.stack(ms_all, axis=1)
    new_c = jnp.swapaxes(cs_all[..., :M_DV, :], -1, -2).reshape(n_ctx, DEPTH, 2, M_HEADS, M_DK, M_DV)
    new_n = cs_all[..., M_DV, :].reshape(n_ctx, DEPTH, 2, M_HEADS, M_DK)
    new_m = ms_all[..., 0].reshape(n_ctx, DEPTH, 2, M_HEADS)
    return (xp.reshape(n_ctx, t_ctx, D_MODEL), xs.reshape(n_lat, t_lat, D_MODEL),
            jnp.stack(ks, axis=1), jnp.stack(vs, axis=1), new_c, new_n, new_m)
```

```python
import functools

import numpy as np
import jax
import jax.numpy as jnp
from jax import lax
from jax.experimental import pallas as pl
from jax.experimental.pallas import tpu as pltpu

D_MODEL = 1024
DEPTH = 2
GRID_W = 64
HEAD_DIM = 64
N_Q_HEADS = 8
N_KV_HEADS = 4
ATTN_WIDTH = N_Q_HEADS * HEAD_DIM
KV_WIDTH = N_KV_HEADS * HEAD_DIM
ROPE_THETA = 10000.0
M_HEADS = 4
M_DK = 64
M_DV = 64
M_WIDTH = M_HEADS * M_DV
F_GROUPS = 4
F_GROUP_CH = 64
F_WIDTH = F_GROUPS * F_GROUP_CH
FF_HIDDEN = -(-8 * D_MODEL // (3 * 256)) * 256
EPS = 1e-6
N_GATES = 4 * M_HEADS
IN_SIZES = (ATTN_WIDTH, KV_WIDTH, KV_WIDTH, M_HEADS * M_DK, M_HEADS * M_DK, M_WIDTH, M_WIDTH,
            N_GATES, F_WIDTH, 3 * D_MODEL)
IN_OFFS = tuple(int(v) for v in np.cumsum((0,) + IN_SIZES))

LANES = 128
GATE_PAD = LANES
N_MOD = 6
MOD_ROWS = 16
M_CHUNK_K = 128
NEG_BIG = -1e30
LOG2_E = 1.4426950408889634
VMEM_LIMIT = 56 * 1024 * 1024
TM = 512
TM_IN = 1024
W_IN_ROWS = 256

F32 = jnp.float32
BF16 = jnp.bfloat16


def _cparams(n_axes):
    return pltpu.CompilerParams(dimension_semantics=("arbitrary",) * n_axes,
                                vmem_limit_bytes=VMEM_LIMIT)


def _layer_spec(shape, layer):
    return pl.BlockSpec((None,) + tuple(shape), lambda *_: (layer,) + (0,) * len(shape),
                        pipeline_mode=pl.Buffered(1))


class _Rows:
    def __init__(self, rows_ctx, t_lat, n_stage=0, tm=TM):
        self.tm = tm
        self.n_ctx = rows_ctx // tm
        self.per_seq = t_lat // tm
        self.n_stage = n_stage

    def tile(self, i):
        return jnp.maximum(i - self.n_stage, 0)

    def both(self, i):
        return (self.tile(i), 0)

    def ctx(self, i):
        return (jnp.minimum(self.tile(i), self.n_ctx - 1), 0)

    def lat(self, i):
        return (jnp.maximum(self.tile(i) - self.n_ctx, 0), 0)

    def mod_spec(self, layer):
        def index(i):
            t = self.tile(i)
            return (layer, jnp.where(t < self.n_ctx, 0, 1 + (t - self.n_ctx) // self.per_seq), 0, 0)
        return pl.BlockSpec((None, None, N_MOD, D_MODEL), index)

    def spec(self, width, which):
        return pl.BlockSpec((self.tm, width), getattr(self, which))

    def is_ctx(self):
        return pl.program_id(0) - self.n_stage < self.n_ctx


def _pick(rows, ctx_ref, lat_ref):
    return jnp.where(rows.is_ctx(), ctx_ref[...], lat_ref[...])


def _staged_spec(shape, layer, n_chunks):
    rc, cols = shape[0] // n_chunks, shape[1]
    return pl.BlockSpec((None, rc, cols), lambda i: (layer, jnp.minimum(i, n_chunks - 1), 0))


def _stage(w_ref, scratch_ref, n_chunks):
    i = pl.program_id(0)
    rc = w_ref.shape[0]

    @pl.when(i < n_chunks)
    def _():
        scratch_ref[pl.ds(pl.multiple_of(i * rc, rc), rc), :] = w_ref[...].astype(BF16)


def _mm(a, b):
    return jnp.dot(a, b, preferred_element_type=F32)


def _mm_nt(a, b):
    return lax.dot_general(a, b, (((1,), (1,)), ((), ())), preferred_element_type=F32)


def _split3(x):
    hi = x.astype(BF16)
    r = x - hi.astype(F32)
    mid = r.astype(BF16)
    lo = (r - mid.astype(F32)).astype(BF16)
    return hi, mid, lo


def _mm_left_f32(x, m_bf16):
    hi, mid, lo = _split3(x)
    return _mm(hi, m_bf16) + _mm(mid, m_bf16) + _mm(lo, m_bf16)


def _sigmoid(x):
    return 1.0 / (1.0 + jnp.exp(-x))


def _silu(x):
    return x * _sigmoid(x)


def _log_sigmoid(x):
    return jnp.minimum(x, 0.0) - jnp.log(1.0 + jnp.exp(-jnp.abs(x)))


def _normmod(x, g, scale, shift):
    ms = jnp.mean(x * x, axis=-1, keepdims=True)
    return (x * lax.rsqrt(ms + EPS)) * g * (1.0 + scale) + shift


def _group_inv_rms(x, bd):
    x2 = x * x
    hi = x2.astype(BF16)
    lo = (x2 - hi.astype(F32)).astype(BF16)
    ss = _mm(hi, bd) + _mm(lo, bd)
    return lax.rsqrt(ss * (1.0 / HEAD_DIM) + EPS)


def _rope(x, cos, sin_signed):
    w = x.shape[1]
    lane = lax.broadcasted_iota(jnp.int32, x.shape, 1)
    up = pltpu.roll(x, w - 16, axis=1)
    dn = pltpu.roll(x, 16, axis=1)
    partner = jnp.where((lane & 31) < 16, up, dn)
    return x * cos + partner * sin_signed


def _mod_kernel(c_ref, w_ref, b_ref, o_ref):
    s = _silu(c_ref[...]).astype(BF16)
    o_ref[...] = _mm(s, w_ref[...].astype(BF16)) + b_ref[...]


def _modulation(cvec, w_ada, b_ada):
    tn = 1024
    n_out = N_MOD * D_MODEL
    return pl.pallas_call(
        _mod_kernel,
        grid=(DEPTH, n_out // tn),
        in_specs=[
            pl.BlockSpec((MOD_ROWS, D_MODEL), lambda l, j: (0, 0)),
            pl.BlockSpec((None, D_MODEL, tn), lambda l, j: (l, 0, j)),
            pl.BlockSpec((None, 1, tn), lambda l, j: (l, 0, j)),
        ],
        out_specs=pl.BlockSpec((None, MOD_ROWS, tn), lambda l, j: (l, 0, j)),
        out_shape=jax.ShapeDtypeStruct((DEPTH, MOD_ROWS, n_out), F32),
        compiler_params=_cparams(2),
        name="modulation",
    )(cvec, w_ada, b_ada.reshape(DEPTH, 1, n_out))


GATE_OFF, FU_OFF, BG_OFF = IN_OFFS[7], IN_OFFS[8], IN_OFFS[9]
IN_COLS = IN_OFFS[-1]
MIX_CHUNKS = -(-BG_OFF // W_IN_ROWS)
BG_FIRST = BG_OFF // W_IN_ROWS
BG_CHUNKS = -(-IN_COLS // W_IN_ROWS) - BG_FIRST
BG_SKIP = BG_OFF - BG_FIRST * W_IN_ROWS
BG_LAST = IN_COLS - (BG_FIRST + BG_CHUNKS - 1) * W_IN_ROWS


def _in_kernel(*refs, rows, split_x):
    refs = list(refs)
    x_refs = [refs.pop(0) for _ in range(2 if split_x else 1)]
    mod_ref, g_ref, w_ref, za_ref, zm_ref, fu_ref, gt_ref, w_s = refs
    _stage(w_ref, w_s, MIX_CHUNKS)

    @pl.when(pl.program_id(0) >= MIX_CHUNKS)
    def _():
        x = _pick(rows, *x_refs) if split_x else x_refs[0][...]
        h = _normmod(x, g_ref[...], mod_ref[1:2, :], mod_ref[0:1, :])
        z = _mm_nt(h.astype(BF16), w_s[:BG_OFF, :])
        za_ref[...] = z[:, :D_MODEL]
        zm_ref[...] = z[:, D_MODEL:GATE_OFF]
        gt_ref[...] = z[:, GATE_OFF:GATE_OFF + GATE_PAD]
        fu_ref[...] = z[:, FU_OFF:FU_OFF + F_WIDTH].astype(BF16)


def _in_proj(xs, mod, g, w_in_t, rows, layer):
    split_x = len(xs) == 2
    n_rows = sum(x.shape[0] for x in xs)
    x_specs = [rows.spec(D_MODEL, "ctx"), rows.spec(D_MODEL, "lat")] if split_x else [rows.spec(D_MODEL, "both")]
    widths = (D_MODEL, D_MODEL, F_WIDTH, GATE_PAD)
    dtypes = (F32, F32, BF16, F32)
    return pl.pallas_call(
        functools.partial(_in_kernel, rows=rows, split_x=split_x),
        grid=(MIX_CHUNKS + n_rows // rows.tm,),
        in_specs=x_specs + [
            rows.mod_spec(layer),
            _layer_spec((1, D_MODEL), layer),
            _staged_spec((MIX_CHUNKS * W_IN_ROWS, D_MODEL), layer, MIX_CHUNKS),
        ],
        out_specs=[rows.spec(w, "both") for w in widths],
        out_shape=[jax.ShapeDtypeStruct((n_rows, w), d) for w, d in zip(widths, dtypes)],
        scratch_shapes=[pltpu.VMEM((MIX_CHUNKS * W_IN_ROWS, D_MODEL), BF16)],
        compiler_params=_cparams(1),
        name="in_proj",
    )(*xs, mod, g, w_in_t)


def _attn_kernel(*refs, tq, t_new, t_past, latent):
    if latent:
        (q_ref, k_ref, v_ref, ck_ref, cv_ref, cosq_ref, sinq_ref, cosk_ref, sink_ref,
         qg_ref, kg_ref, bdq_ref, bdk_ref, o_ref, ks_ref, vs_ref) = refs
    else:
        (q_ref, k_ref, v_ref, qg_ref, kg_ref, bdq_ref, bdk_ref,
         o_ref, nk_ref, nv_ref, ks_ref, vs_ref) = refs

    @pl.when(pl.program_id(1) == 0)
    def _():
        k = k_ref[...]
        kn = k * _group_inv_rms(k, bdk_ref[...]) * kg_ref[...]
        v = v_ref[...]
        if latent:
            kn = _rope(kn, cosk_ref[...], sink_ref[...])
            ck = ck_ref[...]
            cv = cv_ref[...]
        else:
            nk_ref[...] = kn
            nv_ref[...] = v
        for h in range(N_KV_HEADS):
            sl = slice(HEAD_DIM * h, HEAD_DIM * (h + 1))
            if latent:
                ks_ref[h, :t_past, :] = ck[:, sl].astype(BF16)
                vs_ref[h, :t_past, :HEAD_DIM] = cv[:, sl].astype(BF16)
            ks_ref[h, t_past:, :] = kn[:, sl].astype(BF16)
            vs_ref[h, t_past:, :HEAD_DIM] = v[:, sl].astype(BF16)
            vs_ref[h, :, HEAD_DIM:] = jnp.ones((t_past + t_new, LANES - HEAD_DIM), BF16)

    q = q_ref[...]
    qn = q * _group_inv_rms(q, bdq_ref[...]) * qg_ref[...]
    if latent:
        qn = _rope(qn, cosq_ref[...], sinq_ref[...])
    qn = qn * (HEAD_DIM ** -0.5 * LOG2_E)
    group = N_Q_HEADS // N_KV_HEADS
    outs = []

    def scores(h):
        qh = jnp.concatenate(
            [qn[:, HEAD_DIM * (group * h + g):HEAD_DIM * (group * h + g + 1)] for g in range(group)],
            axis=0).astype(BF16)
        return _mm_nt(qh, ks_ref[h])

    ahead = 1 if latent else N_KV_HEADS - 1
    s_q = [scores(h) for h in range(ahead)]
    for h in range(N_KV_HEADS):
        s = s_q.pop(0)
        if h + ahead < N_KV_HEADS:
            s_q.append(scores(h + ahead))
        m = jnp.max(s, axis=-1, keepdims=True)
        p = jnp.exp2(s - m).astype(BF16)
        od = _mm(p, vs_ref[h])
        o = (od / pltpu.roll(od, LANES - HEAD_DIM, axis=1))[:, :HEAD_DIM]
        outs.extend(o[tq * g:tq * (g + 1)] for g in range(group))
    o_ref[...] = jnp.concatenate(outs, axis=1).astype(BF16)


def _block_diag_ones(width):
    idx = np.arange(width) // HEAD_DIM
    return jnp.asarray(idx[:, None] == idx[None, :], dtype=BF16)


def _attention(za, qg, kg, *, batch, t_new, latent, row0, cache_k=None, cache_v=None, layer=0, rope_tabs=None):
    tq = min(1024, t_new)
    nq = t_new // tq
    t_past = cache_k.shape[2] if latent else 0
    t_keys = t_past + t_new
    kcol = ATTN_WIDTH // KV_WIDTH
    bdq = _block_diag_ones(ATTN_WIDTH)
    bdk = _block_diag_ones(KV_WIDTH)
    const = lambda shape: pl.BlockSpec(shape, lambda b, i: (0,) * len(shape))
    q0, s0 = row0 // tq, row0 // t_new
    in_specs = [
        pl.BlockSpec((tq, ATTN_WIDTH), lambda b, i: (q0 + b * nq + i, 0)),
        pl.BlockSpec((t_new, KV_WIDTH), lambda b, i: (s0 + b, kcol)),
        pl.BlockSpec((t_new, KV_WIDTH), lambda b, i: (s0 + b, kcol + 1)),
    ]
    args = [za, za, za]
    if latent:
        cos_t, sin_t = rope_tabs
        in_specs += [
            pl.BlockSpec((None, None, t_past, KV_WIDTH), lambda b, i: (b, layer, 0, 0)),
            pl.BlockSpec((None, None, t_past, KV_WIDTH), lambda b, i: (b, layer, 0, 0)),
            pl.BlockSpec((tq, ATTN_WIDTH), lambda b, i: (i, 0)),
            pl.BlockSpec((tq, ATTN_WIDTH), lambda b, i: (i, 0)),
            pl.BlockSpec((t_new, KV_WIDTH), lambda b, i: (0, 0)),
            pl.BlockSpec((t_new, KV_WIDTH), lambda b, i: (0, 0)),
        ]
        args += [cache_k, cache_v, cos_t, sin_t, cos_t, sin_t]
    in_specs += [_layer_spec((1, ATTN_WIDTH), layer), _layer_spec((1, KV_WIDTH), layer),
                 const((ATTN_WIDTH, ATTN_WIDTH)), const((KV_WIDTH, KV_WIDTH))]
    args += [qg, kg, bdq, bdk]
    out_specs = [pl.BlockSpec((tq, ATTN_WIDTH), lambda b, i: (b * nq + i, 0))]
    out_shape = [jax.ShapeDtypeStruct((batch * t_new, ATTN_WIDTH), BF16)]
    if not latent:
        out_specs += [pl.BlockSpec((t_new, KV_WIDTH), lambda b, i: (b, 0))] * 2
        out_shape += [jax.ShapeDtypeStruct((batch * t_new, KV_WIDTH), F32)] * 2
    return pl.pallas_call(
        functools.partial(_attn_kernel, tq=tq, t_new=t_new, t_past=t_past, latent=latent),
        grid=(batch, nq),
        in_specs=in_specs,
        out_specs=out_specs,
        out_shape=out_shape,
        scratch_shapes=[pltpu.VMEM((N_KV_HEADS, t_keys, HEAD_DIM), BF16),
                        pltpu.VMEM((N_KV_HEADS, t_keys, LANES), BF16)],
        compiler_params=_cparams(2),
        name="attention_latent" if latent else "attention_context",
    )(*args)


def _running_max_sublanes(x, reverse):
    n = x.shape[0]
    row = lax.broadcasted_iota(jnp.int32, x.shape, 0)
    k = 1
    while k < n:
        if reverse:
            cand = jnp.where(row < n - k, pltpu.roll(x, n - k, axis=0), NEG_BIG)
        else:
            cand = jnp.where(row >= k, pltpu.roll(x, k, axis=0), NEG_BIG)
        x = jnp.maximum(x, cand)
        k *= 2
    return x


def _mlstm_kernel(*refs, t, chunk, has_state, emit_state, convert):
    refs = list(refs)
    zm_ref, gt_ref, cw_ref, gb_ref, mg_ref, bd_ref, fu_ref, dft_c_ref, dft_ct_ref, dft_st_ref = refs[:10]
    refs = refs[10:]
    if has_state:
        c0_ref, m0_ref = refs[:2]
        refs = refs[2:]
    convert_in, refs = refs[:len(convert)], refs[len(convert):]
    o_ref, fo_ref = refs[:2]
    refs = refs[2:]
    if emit_state:
        cs_ref, ms_ref = refs[:2]
        refs = refs[2:]
    convert_out, refs = refs[:len(convert)], refs[len(convert):]
    hst_ref = refs[0]

    for (n_chunks, last_valid), src_ref, dst_ref in zip(convert, convert_in, convert_out):
        w = src_ref[...]
        if last_valid is not None:
            rc = w.shape[0]
            limit = jnp.where(pl.program_id(0) >= n_chunks - 1, last_valid, rc)
            w = jnp.where(lax.broadcasted_iota(jnp.int32, (rc, 1), 0) < limit, w, 0.0)
        dst_ref[...] = w.astype(BF16)

    width = M_HEADS * M_DK
    z = zm_ref[...]
    x = z[:, :2 * width]
    row = lax.broadcasted_iota(jnp.int32, (t, 1), 0)
    x_prev = jnp.where(row == 0, 0.0, pltpu.roll(x, 1, axis=0))
    x_next = jnp.where(row == t - 1, 0.0, pltpu.roll(x, t - 1, axis=0))
    cw = cw_ref[...]
    qk = _silu(x_prev * cw[0:1, :] + x * cw[1:2, :] + x_next * cw[2:3, :])
    q = qk[:, :width]
    k = qk[:, width:] * (M_DK ** -0.5)
    v = z[:, 2 * width:2 * width + M_WIDTH]
    om = z[:, 2 * width + M_WIDTH:]
    heads = [slice(M_DK * h, M_DK * (h + 1)) for h in range(M_HEADS)]
    q_rows = [q[:, hs].astype(BF16) for hs in heads]
    k_rows = [k[:, hs].astype(BF16) for hs in heads]
    qt = q.T.astype(BF16)
    vt = v.T
    ones_r = jnp.ones((LANES - M_DV, t), F32)
    vaug_t = [jnp.concatenate([vt[hs, :], ones_r], axis=0) for hs in heads]
    vaug_tb = [a.astype(BF16) for a in vaug_t]

    gates = gt_ref[...] + gb_ref[...]
    lane = lax.broadcasted_iota(jnp.int32, (1, GATE_PAD), 1)
    is_forget = (lane & M_HEADS) != 0
    gates = jnp.where(is_forget, _log_sigmoid(gates), gates)
    gates_t = gates.T[:N_GATES, :]

    ri = lax.broadcasted_iota(jnp.int32, (chunk, chunk), 0)
    ci = lax.broadcasted_iota(jnp.int32, (chunk, chunk), 1)
    row_le_col = ri <= ci
    row_ge_col = ri >= ci

    n_chunks = t // chunk
    g_all = jnp.concatenate([gates_t[:, chunk * c:chunk * (c + 1)] for c in range(n_chunks)], axis=0)
    i_all = pltpu.roll(g_all, M_HEADS, axis=0)
    rows = lax.broadcasted_iota(jnp.int32, (N_GATES * n_chunks, 1), 0)
    is_fwd_row = (rows & (2 * M_HEADS)) == 0
    cum_p = _mm_left_f32(g_all, row_le_col.astype(BF16))
    cum_s = _mm_left_f32(g_all, row_ge_col.astype(BF16))
    b_all = jnp.where(is_fwd_row, cum_p, cum_s)
    u_all = i_all - b_all
    u_cols = u_all.T
    cols = lax.broadcasted_iota(jnp.int32, (1, N_GATES * n_chunks), 1)
    pm_cols = jnp.where((cols & (2 * M_HEADS)) == 0, _running_max_sublanes(u_cols, reverse=False),
                        _running_max_sublanes(u_cols, reverse=True))
    pm_all = pm_cols.T
    u_cols2 = u_cols * LOG2_E
    pm_all2 = pm_all * LOG2_E
    pml_all = jnp.where(is_fwd_row, pm_all[:, chunk - 1:], pm_all[:, :1])
    bl_all = jnp.where(is_fwd_row, b_all[:, chunk - 1:], b_all[:, :1])
    wk_all = jnp.exp(u_all - pml_all)

    if has_state:
        m_state = [m0_ref[M_HEADS * d:M_HEADS * (d + 1), 0:1] for d in range(2)]
        c_state = [[c0_ref[M_HEADS * d + h] for h in range(M_HEADS)] for d in range(2)]
    else:
        m_state = [jnp.zeros((M_HEADS, 1), F32) for _ in range(2)]
        c_state = [[jnp.zeros((LANES, M_DK), F32) for _ in range(M_HEADS)] for _ in range(2)]

    steps = [(j if d == 0 else n_chunks - 1 - j, d) for j in range(n_chunks) for d in range(2)]

    def gate_rows(step):
        c, direction = step
        return N_GATES * c + (2 * direction + 1) * M_HEADS

    def key_query(step):
        rs = slice(chunk * step[0], chunk * (step[0] + 1))
        return [_mm_nt(k_rows[h][rs, :], q_rows[h][rs, :]) for h in range(M_HEADS)]

    def local_sums(step, st):
        c, direction = step
        rs = slice(chunk * c, chunk * (c + 1))
        r0 = gate_rows(step)
        valid = row_le_col if direction == 0 else row_ge_col
        s_loc, x_loc = [], []
        for h in range(M_HEADS):
            w = jnp.exp2(jnp.where(valid, u_cols2[:, r0 + h:r0 + h + 1] - pm_all2[r0 + h:r0 + h + 1, :], NEG_BIG))
            s_loc.append(_mm(vaug_tb[h][:, rs], (st[h] * w).astype(BF16)))
            x_loc.append(_mm((vaug_t[h][:, rs] * wk_all[r0 + h:r0 + h + 1, :]).astype(BF16), k_rows[h][rs, :]))
        return s_loc, x_loc

    written = set()
    n_steps = len(steps)
    st_q = [key_query(steps[i]) for i in range(min(2, n_steps))]
    loc_q = [local_sums(steps[0], st_q.pop(0))]
    for i, (c, direction) in enumerate(steps):
        if i + 1 < n_steps:
            loc_q.append(local_sums(steps[i + 1], st_q.pop(0)))
        if i + 2 < n_steps:
            st_q.append(key_query(steps[i + 2]))
        s_loc, x_loc = loc_q.pop(0)
        rs = slice(chunk * c, chunk * (c + 1))
        r0 = gate_rows((c, direction))
        b4, pm4, bl, pml = (a[r0:r0 + M_HEADS, :] for a in (b_all, pm_all, bl_all, pml_all))
        m = m_state[direction]
        mx = jnp.maximum(m, pm4)
        f_loc = jnp.exp(pm4 - mx)
        f_int = jnp.exp(m - mx)
        floor = jnp.exp(-(b4 + mx))
        m_new = bl + jnp.maximum(m, pml)
        decay = jnp.exp(bl + m - m_new)
        gain = jnp.exp(bl + pml - m_new)
        m_state[direction] = m_new
        inter = [_mm(c_state[direction][h].astype(BF16), qt[heads[h], rs]) for h in range(M_HEADS)]
        for h in range(M_HEADS):
            nd = f_loc[h:h + 1, :] * s_loc[h] + f_int[h:h + 1, :] * inter[h]
            ht = nd[:M_DV, :] / jnp.maximum(jnp.abs(nd[M_DV:, :]), floor[h:h + 1, :])
            c_state[direction][h] = decay[h:h + 1, :] * c_state[direction][h] + gain[h:h + 1, :] * x_loc[h]
            if c in written:
                hst_ref[heads[h], rs] += ht
            else:
                hst_ref[heads[h], rs] = ht
        written.add(c)

    if emit_state:
        for d in range(2):
            ms_ref[M_HEADS * d:M_HEADS * (d + 1), :] = jnp.broadcast_to(m_state[d], (M_HEADS, LANES))
            for h in range(M_HEADS):
                cs_ref[M_HEADS * d + h] = c_state[d][h]

    hsum = hst_ref[...].T
    hn = hsum * _group_inv_rms(hsum, bd_ref[...]) * mg_ref[...]
    o_ref[...] = (_sigmoid(om) * hn).astype(BF16)

    fa = _mm(fu_ref[...].astype(BF16), dft_c_ref[...])
    fo_ref[...] = (_mm(dft_ct_ref[...], fa[:, :F_WIDTH].astype(BF16))
                   - _mm(dft_st_ref[...], fa[:, F_WIDTH:].astype(BF16))).astype(BF16)


def _mlstm_fourier(zm, gt, fu, conv_w, gate_b, m_norm_g, *, batch, t, layer, row0, caug0=None, m0=None,
                   convert=()):
    has_state = caug0 is not None
    emit_state = not has_state
    chunk = min(M_CHUNK_K, t)
    n_units = 2 * M_HEADS
    const = lambda shape: pl.BlockSpec(shape, lambda b: (0,) * len(shape), pipeline_mode=pl.Buffered(1))
    s0 = row0 // t
    in_specs = [
        pl.BlockSpec((t, D_MODEL), lambda b: (s0 + b, 0)),
        pl.BlockSpec((t, GATE_PAD), lambda b: (s0 + b, 0)),
        _layer_spec((3, 2 * M_HEADS * M_DK), layer),
        _layer_spec((1, GATE_PAD), layer),
        _layer_spec((1, M_WIDTH), layer),
        const((M_WIDTH, M_WIDTH)),
        pl.BlockSpec((t, F_WIDTH), lambda b: (s0 + b, 0)),
        const((F_WIDTH, 2 * F_WIDTH)), const((t, t)), const((t, t)),
    ]
    args = [zm, gt, conv_w, gate_b, m_norm_g, _block_diag_ones(M_WIDTH), fu, *_dft_tables(t)]
    if has_state:
        in_specs += [pl.BlockSpec((None, n_units, LANES, M_DK), lambda b: (b, 0, 0, 0)),
                     pl.BlockSpec((None, n_units, LANES), lambda b: (b, 0, 0))]
        args += [caug0, m0]
    out_specs = [pl.BlockSpec((t, M_WIDTH), lambda b: (b, 0)), pl.BlockSpec((t, F_WIDTH), lambda b: (b, 0))]
    out_shape = [jax.ShapeDtypeStruct((batch * t, M_WIDTH), BF16), jax.ShapeDtypeStruct((batch * t, F_WIDTH), BF16)]
    if emit_state:
        out_specs += [pl.BlockSpec((None, n_units, LANES, M_DK), lambda b: (b, 0, 0, 0)),
                      pl.BlockSpec((None, n_units, LANES), lambda b: (b, 0, 0))]
        out_shape += [jax.ShapeDtypeStruct((batch, n_units, LANES, M_DK), F32),
                      jax.ShapeDtypeStruct((batch, n_units, LANES), F32)]
    for w, rc, first, n_chunks, _ in convert:
        assert n_chunks <= batch
        w_cols = w.shape[2]
        in_specs.append(pl.BlockSpec(
            (None, rc, w_cols), lambda b, first=first, n=n_chunks: (layer, first + jnp.minimum(b, n - 1), 0)))
        args.append(w)
        out_specs.append(pl.BlockSpec((rc, w_cols), lambda b, n=n_chunks: (jnp.minimum(b, n - 1), 0)))
        out_shape.append(jax.ShapeDtypeStruct((rc * n_chunks, w_cols), BF16))
    return pl.pallas_call(
        functools.partial(_mlstm_kernel, t=t, chunk=chunk, has_state=has_state, emit_state=emit_state,
                          convert=tuple((n, last) for _, _, _, n, last in convert)),
        grid=(batch,),
        in_specs=in_specs,
        out_specs=out_specs,
        out_shape=out_shape,
        scratch_shapes=[pltpu.VMEM((M_WIDTH, t), F32)],
        compiler_params=_cparams(1),
        name="mlstm_fourier_latent" if has_state else "mlstm_fourier_context",
    )(*args)


def _dft_tables(t):
    kt = (np.arange(t)[:, None] * np.arange(t)[None, :]) % t
    ang_t = 2.0 * np.pi * kt.astype(np.float64) / t
    ct = np.cos(ang_t) / np.sqrt(t)
    st = np.sin(ang_t) / np.sqrt(t)
    c = F_GROUP_CH
    kc = (np.arange(c)[:, None] * np.arange(c)[None, :]) % c
    ang_c = 2.0 * np.pi * kc.astype(np.float64) / c
    eye = np.eye(F_GROUPS)
    cc = np.kron(eye, np.cos(ang_c) / np.sqrt(c))
    sc = np.kron(eye, np.sin(ang_c) / np.sqrt(c))
    cs = np.concatenate([cc, sc], axis=1)
    to_dev = lambda a: jnp.asarray(a.astype(np.float32)).astype(BF16)
    return to_dev(cs), to_dev(ct), to_dev(st)


def _merge_kernel(*refs, rows, split_x):
    refs = list(refs)
    x_refs = [refs.pop(0) for _ in range(2 if split_x else 1)]
    (mod_ref, g_ref, ac_ref, al_ref, hc_ref, hl_ref, fc_ref, fl_ref, wbg_ref,
     wpa_ref, wpm_ref, wpf_ref, wo_ref, o_ref) = refs
    x = _pick(rows, *x_refs) if split_x else x_refs[0][...]
    h = _normmod(x, g_ref[...], mod_ref[1:2, :], mod_ref[0:1, :]).astype(BF16)

    def branch(j, yc_ref, yl_ref, w_ref):
        gate = _sigmoid(_mm_nt(h, wbg_ref[BG_SKIP + D_MODEL * j:BG_SKIP + D_MODEL * (j + 1), :]))
        return gate * _mm(_pick(rows, yc_ref, yl_ref), w_ref[...])

    merged = (branch(0, ac_ref, al_ref, wpa_ref) + branch(1, hc_ref, hl_ref, wpm_ref)
              + branch(2, fc_ref, fl_ref, wpf_ref))
    o_ref[...] = x + mod_ref[2:3, :] * _mm(merged.astype(BF16), wo_ref[...])


def _merge(xs, mod, g, attn, hm, fo, weights, rows, layer):
    split_x = len(xs) == 2
    n_rows = sum(x.shape[0] for x in xs)
    x_specs = [rows.spec(D_MODEL, "ctx"), rows.spec(D_MODEL, "lat")] if split_x else [rows.spec(D_MODEL, "both")]
    pair = lambda width: [rows.spec(width, "ctx"), rows.spec(width, "lat")]
    resident = lambda w: pl.BlockSpec(w.shape, lambda i: (0, 0), pipeline_mode=pl.Buffered(1))
    return pl.pallas_call(
        functools.partial(_merge_kernel, rows=rows, split_x=split_x),
        grid=(n_rows // rows.tm,),
        in_specs=x_specs + [rows.mod_spec(layer), _layer_spec((1, D_MODEL), layer)]
        + pair(ATTN_WIDTH) + pair(M_WIDTH) + pair(F_WIDTH) + [resident(w) for w in weights],
        out_specs=rows.spec(D_MODEL, "both"),
        out_shape=jax.ShapeDtypeStruct((n_rows, D_MODEL), F32),
        compiler_params=_cparams(1),
        name="merge",
    )(*xs, mod, g, *attn, *hm, *fo, *weights)


def _ffn_kernel(*refs, rows, split_out):
    x_ref, mod_ref, g_ref, win_ref, wout_ref = refs[:5]
    out_refs = refs[5:]
    x = x_ref[...]
    h = _normmod(x, g_ref[...], mod_ref[4:5, :], mod_ref[3:4, :]).astype(BF16)
    u = _mm(h, win_ref[...])
    a = (_silu(u[:, :FF_HIDDEN]) * u[:, FF_HIDDEN:]).astype(BF16)
    y = x + mod_ref[5:6, :] * _mm(a, wout_ref[...])
    if split_out:
        @pl.when(rows.is_ctx())
        def _():
            out_refs[0][...] = y

        @pl.when(jnp.logical_not(rows.is_ctx()))
        def _():
            out_refs[1][...] = y
    else:
        out_refs[0][...] = y


def _ffn(x, mod, g, w_in, w_out, rows, layer, split_out):
    n_rows = x.shape[0]
    n_ctx_rows = rows.n_ctx * rows.tm
    resident = lambda w: pl.BlockSpec(w.shape, lambda i: (0, 0), pipeline_mode=pl.Buffered(1))
    if split_out:
        out_specs = [rows.spec(D_MODEL, "ctx"), rows.spec(D_MODEL, "lat")]
        out_shape = [jax.ShapeDtypeStruct((n_ctx_rows, D_MODEL), F32),
                     jax.ShapeDtypeStruct((n_rows - n_ctx_rows, D_MODEL), F32)]
    else:
        out_specs = [rows.spec(D_MODEL, "both")]
        out_shape = [jax.ShapeDtypeStruct((n_rows, D_MODEL), F32)]
    return pl.pallas_call(
        functools.partial(_ffn_kernel, rows=rows, split_out=split_out),
        grid=(n_rows // rows.tm,),
        in_specs=[rows.spec(D_MODEL, "both"), rows.mod_spec(layer), _layer_spec((1, D_MODEL), layer),
                  resident(w_in), resident(w_out)],
        out_specs=out_specs,
        out_shape=out_shape,
        compiler_params=_cparams(1),
        name="ffn",
    )(x, mod, g, w_in, w_out)


def _rope_tables(t):
    n = HEAD_DIM // 4
    inv = 1.0 / (ROPE_THETA ** (np.arange(n, dtype=np.float64) / n))
    pos = np.arange(t)
    ang_r = (pos // GRID_W)[:, None] * inv[None, :]
    ang_c = (pos % GRID_W)[:, None] * inv[None, :]
    cos = np.concatenate([np.cos(ang_r)] * 2 + [np.cos(ang_c)] * 2, axis=1)
    sin = np.concatenate([-np.sin(ang_r), np.sin(ang_r), -np.sin(ang_c), np.sin(ang_c)], axis=1)
    tile = lambda a: jnp.asarray(np.tile(a, (1, N_Q_HEADS)).astype(np.float32))
    return tile(cos), tile(sin)


def kernel(x_prompt, x_sample, cache_k, cache_v, state_C, state_n, state_m, c, c_ctx, w_ada, b_ada,
           norm1_g, w_in, q_norm_g, k_norm_g, m_conv_w, m_gate_b, m_norm_g, w_proj_attn,
           w_proj_mlstm, w_proj_fourier, w_out, norm2_g, w_ffn_in, w_ffn_out):
    n_ctx, t_ctx, _ = x_prompt.shape
    n_lat, t_lat, _ = x_sample.shape
    t_past = cache_k.shape[2]
    n_units = 2 * M_HEADS

    cvec = jnp.concatenate([c_ctx[None, :], c], axis=0)
    cvec = jnp.pad(cvec, ((0, MOD_ROWS - cvec.shape[0]), (0, 0)))
    mod = _modulation(cvec, w_ada, b_ada).reshape(DEPTH, MOD_ROWS, N_MOD, D_MODEL)

    rope_tabs = _rope_tables(t_lat)
    ck = cache_k.reshape(n_lat, DEPTH, t_past, KV_WIDTH)
    cv = cache_v.reshape(n_lat, DEPTH, t_past, KV_WIDTH)
    rows_ctx = n_ctx * t_ctx
    rows_in = _Rows(rows_ctx, t_lat, n_stage=MIX_CHUNKS, tm=TM_IN)
    rows_merge = _Rows(rows_ctx, t_lat)
    rows_ffn = _Rows(rows_ctx, t_lat)

    w_in_t = jnp.swapaxes(w_in, 1, 2)
    row = lambda a: a[:, None, :]
    norm1, norm2 = row(norm1_g), row(norm2_g)
    qg, kg = row(jnp.tile(q_norm_g, (1, N_Q_HEADS))), row(jnp.tile(k_norm_g, (1, N_KV_HEADS)))
    gate_b = row(jnp.pad(m_gate_b, ((0, 0), (0, GATE_PAD - N_GATES))))
    m_norm = row(m_norm_g)

    per_step = lambda w, n: (w, w.shape[1] // n, 0, n, None)
    xs = (x_prompt.reshape(rows_ctx, D_MODEL), x_sample.reshape(n_lat * t_lat, D_MODEL))
    ks, vs, cs_all, ms_all = [], [], [], []
    for l in range(DEPTH):
        za, zm, fu, gt = _in_proj(xs, mod, norm1, w_in_t, rows_in, l)
        attn_c, k_l, v_l = _attention(za, qg, kg, batch=n_ctx, t_new=t_ctx, latent=False, row0=0, layer=l)
        hm_c, fo_c, cs, ms, *merge_w = _mlstm_fourier(
            zm, gt, fu, m_conv_w, gate_b, m_norm, batch=n_ctx, t=t_ctx, layer=l, row0=0,
            convert=((w_in_t, W_IN_ROWS, BG_FIRST, BG_CHUNKS, BG_LAST),)
            + tuple(per_step(w, n_ctx) for w in (w_proj_attn, w_proj_mlstm, w_proj_fourier, w_out)))
        ks.append(k_l.reshape(n_ctx, t_ctx, N_KV_HEADS, HEAD_DIM))
        vs.append(v_l.reshape(n_ctx, t_ctx, N_KV_HEADS, HEAD_DIM))
        cs_all.append(cs)
        ms_all.append(ms)

        c0t = jnp.swapaxes(state_C[:, l].astype(F32).reshape(n_lat, n_units, M_DK, M_DV), -1, -2)
        n0 = state_n[:, l].astype(F32).reshape(n_lat, n_units, 1, M_DK)
        caug0 = jnp.concatenate(
            [c0t, jnp.broadcast_to(n0, (n_lat, n_units, LANES - M_DV, M_DK))], axis=-2)
        m0 = jnp.broadcast_to(state_m[:, l].astype(F32).reshape(n_lat, n_units, 1),
                              (n_lat, n_units, LANES))
        (attn_l,) = _attention(za, qg, kg, batch=n_lat, t_new=t_lat, latent=True, row0=rows_ctx,
                               cache_k=ck, cache_v=cv, layer=l, rope_tabs=rope_tabs)
        hm_l, fo_l, w_ffn_in_b, w_ffn_out_b = _mlstm_fourier(
            zm, gt, fu, m_conv_w, gate_b, m_norm, batch=n_lat, t=t_lat, layer=l, row0=rows_ctx,
            caug0=caug0, m0=m0, convert=(per_step(w_ffn_in, n_lat), per_step(w_ffn_out, n_lat)))

        x1 = _merge(xs, mod, norm1, (attn_c, attn_l), (hm_c, hm_l), (fo_c, fo_l), merge_w, rows_merge, l)
        xs = tuple(_ffn(x1, mod, norm2, w_ffn_in_b, w_ffn_out_b, rows_ffn, l, split_out=l == DEPTH - 1))
    xp, xs = xs

    cs_all = jnp.stack(cs_all, axis=1)
    ms_all = jnp.stack(ms_all, axis=1)
    new_c = jnp.swapaxes(cs_all[..., :M_DV, :], -1, -2).reshape(n_ctx, DEPTH, 2, M_HEADS, M_DK, M_DV)
    new_n = cs_all[..., M_DV, :].reshape(n_ctx, DEPTH, 2, M_HEADS, M_DK)
    new_m = ms_all[..., 0].reshape(n_ctx, DEPTH, 2, M_HEADS)
    return (xp.reshape(n_ctx, t_ctx, D_MODEL), xs.reshape(n_lat, t_lat, D_MODEL),
            jnp.stack(ks, axis=1), jnp.stack(vs, axis=1), new_c, new_n, new_m)
```

```python
import functools

import numpy as np
import jax
import jax.numpy as jnp
from jax import lax
from jax.experimental import pallas as pl
from jax.experimental.pallas import tpu as pltpu

D_MODEL = 1024
DEPTH = 2
GRID_W = 64
HEAD_DIM = 64
N_Q_HEADS = 8
N_KV_HEADS = 4
ATTN_WIDTH = N_Q_HEADS * HEAD_DIM
KV_WIDTH = N_KV_HEADS * HEAD_DIM
ROPE_THETA = 10000.0
M_HEADS = 4
M_DK = 64
M_DV = 64
M_WIDTH = M_HEADS * M_DV
F_GROUPS = 4
F_GROUP_CH = 64
F_WIDTH = F_GROUPS * F_GROUP_CH
FF_HIDDEN = -(-8 * D_MODEL // (3 * 256)) * 256
EPS = 1e-6
N_GATES = 4 * M_HEADS
IN_SIZES = (ATTN_WIDTH, KV_WIDTH, KV_WIDTH, M_HEADS * M_DK, M_HEADS * M_DK, M_WIDTH, M_WIDTH,
            N_GATES, F_WIDTH, 3 * D_MODEL)
IN_OFFS = tuple(int(v) for v in np.cumsum((0,) + IN_SIZES))

LANES = 128
GATE_PAD = LANES
N_MOD = 6
MOD_ROWS = 16
M_CHUNK_K = 128
NEG_BIG = -1e30
LOG2_E = 1.4426950408889634
VMEM_LIMIT = 56 * 1024 * 1024
TM = 512
TM_IN = 1024
W_IN_ROWS = 256

F32 = jnp.float32
BF16 = jnp.bfloat16


def _cparams(n_axes):
    return pltpu.CompilerParams(dimension_semantics=("arbitrary",) * n_axes,
                                vmem_limit_bytes=VMEM_LIMIT)


def _layer_spec(shape, layer):
    return pl.BlockSpec((None,) + tuple(shape), lambda *_: (layer,) + (0,) * len(shape),
                        pipeline_mode=pl.Buffered(1))


class _Rows:
    def __init__(self, rows_ctx, t_lat, n_stage=0, tm=TM):
        self.tm = tm
        self.n_ctx = rows_ctx // tm
        self.per_seq = t_lat // tm
        self.n_stage = n_stage

    def tile(self, i):
        return jnp.maximum(i - self.n_stage, 0)

    def both(self, i):
        return (self.tile(i), 0)

    def ctx(self, i):
        return (jnp.minimum(self.tile(i), self.n_ctx - 1), 0)

    def lat(self, i):
        return (jnp.maximum(self.tile(i) - self.n_ctx, 0), 0)

    def mod_spec(self, layer):
        def index(i):
            t = self.tile(i)
            return (layer, jnp.where(t < self.n_ctx, 0, 1 + (t - self.n_ctx) // self.per_seq), 0, 0)
        return pl.BlockSpec((None, None, N_MOD, D_MODEL), index)

    def spec(self, width, which):
        return pl.BlockSpec((self.tm, width), getattr(self, which))

    def is_ctx(self):
        return pl.program_id(0) - self.n_stage < self.n_ctx


def _pick(rows, ctx_ref, lat_ref):
    return jnp.where(rows.is_ctx(), ctx_ref[...], lat_ref[...])


def _staged_spec(shape, layer, n_chunks):
    rc, cols = shape[0] // n_chunks, shape[1]
    return pl.BlockSpec((None, rc, cols), lambda i: (layer, jnp.minimum(i, n_chunks - 1), 0))


def _stage(w_ref, scratch_ref, n_chunks):
    i = pl.program_id(0)
    rc = w_ref.shape[0]

    @pl.when(i < n_chunks)
    def _():
        scratch_ref[pl.ds(pl.multiple_of(i * rc, rc), rc), :] = w_ref[...].astype(BF16)


def _mm(a, b):
    return jnp.dot(a, b, preferred_element_type=F32)


def _mm_nt(a, b):
    return lax.dot_general(a, b, (((1,), (1,)), ((), ())), preferred_element_type=F32)


def _split3(x):
    hi = x.astype(BF16)
    r = x - hi.astype(F32)
    mid = r.astype(BF16)
    lo = (r - mid.astype(F32)).astype(BF16)
    return hi, mid, lo


def _mm_left_f32(x, m_bf16):
    hi, mid, lo = _split3(x)
    return _mm(hi, m_bf16) + _mm(mid, m_bf16) + _mm(lo, m_bf16)


def _sigmoid(x):
    return 1.0 / (1.0 + jnp.exp(-x))


def _silu(x):
    return x * _sigmoid(x)


def _log_sigmoid(x):
    return jnp.minimum(x, 0.0) - jnp.log(1.0 + jnp.exp(-jnp.abs(x)))


def _normmod(x, g, scale, shift):
    ms = jnp.mean(x * x, axis=-1, keepdims=True)
    return (x * lax.rsqrt(ms + EPS)) * g * (1.0 + scale) + shift


def _group_inv_rms(x, bd):
    x2 = x * x
    hi = x2.astype(BF16)
    lo = (x2 - hi.astype(F32)).astype(BF16)
    ss = _mm(hi, bd) + _mm(lo, bd)
    return lax.rsqrt(ss * (1.0 / HEAD_DIM) + EPS)


def _rope(x, cos, sin_signed):
    w = x.shape[1]
    lane = lax.broadcasted_iota(jnp.int32, x.shape, 1)
    up = pltpu.roll(x, w - 16, axis=1)
    dn = pltpu.roll(x, 16, axis=1)
    partner = jnp.where((lane & 31) < 16, up, dn)
    return x * cos + partner * sin_signed


def _mod_kernel(c_ref, w_ref, b_ref, o_ref):
    s = _silu(c_ref[...]).astype(BF16)
    o_ref[...] = _mm(s, w_ref[...].astype(BF16)) + b_ref[...]


def _modulation(cvec, w_ada, b_ada):
    tn = 1024
    n_out = N_MOD * D_MODEL
    return pl.pallas_call(
        _mod_kernel,
        grid=(DEPTH, n_out // tn),
        in_specs=[
            pl.BlockSpec((MOD_ROWS, D_MODEL), lambda l, j: (0, 0)),
            pl.BlockSpec((None, D_MODEL, tn), lambda l, j: (l, 0, j)),
            pl.BlockSpec((None, 1, tn), lambda l, j: (l, 0, j)),
        ],
        out_specs=pl.BlockSpec((None, MOD_ROWS, tn), lambda l, j: (l, 0, j)),
        out_shape=jax.ShapeDtypeStruct((DEPTH, MOD_ROWS, n_out), F32),
        compiler_params=_cparams(2),
        name="modulation",
    )(cvec, w_ada, b_ada.reshape(DEPTH, 1, n_out))


GATE_OFF, FU_OFF, BG_OFF = IN_OFFS[7], IN_OFFS[8], IN_OFFS[9]
IN_COLS = IN_OFFS[-1]
MIX_CHUNKS = -(-BG_OFF // W_IN_ROWS)
BG_FIRST = BG_OFF // W_IN_ROWS
BG_CHUNKS = -(-IN_COLS // W_IN_ROWS) - BG_FIRST
BG_SKIP = BG_OFF - BG_FIRST * W_IN_ROWS
BG_LAST = IN_COLS - (BG_FIRST + BG_CHUNKS - 1) * W_IN_ROWS


def _in_kernel(*refs, rows, split_x):
    refs = list(refs)
    x_refs = [refs.pop(0) for _ in range(2 if split_x else 1)]
    mod_ref, g_ref, w_ref, za_ref, zm_ref, fu_ref, gt_ref, w_s = refs
    _stage(w_ref, w_s, MIX_CHUNKS)

    @pl.when(pl.program_id(0) >= MIX_CHUNKS)
    def _():
        x = _pick(rows, *x_refs) if split_x else x_refs[0][...]
        h = _normmod(x, g_ref[...], mod_ref[1:2, :], mod_ref[0:1, :])
        z = _mm_nt(h.astype(BF16), w_s[:BG_OFF, :])
        za_ref[...] = z[:, :D_MODEL]
        zm_ref[...] = z[:, D_MODEL:GATE_OFF]
        gt_ref[...] = z[:, GATE_OFF:GATE_OFF + GATE_PAD]
        fu_ref[...] = z[:, FU_OFF:FU_OFF + F_WIDTH].astype(BF16)


def _in_proj(xs, mod, g, w_in_t, rows, layer):
    split_x = len(xs) == 2
    n_rows = sum(x.shape[0] for x in xs)
    x_specs = [rows.spec(D_MODEL, "ctx"), rows.spec(D_MODEL, "lat")] if split_x else [rows.spec(D_MODEL, "both")]
    widths = (D_MODEL, D_MODEL, F_WIDTH, GATE_PAD)
    dtypes = (F32, F32, BF16, F32)
    return pl.pallas_call(
        functools.partial(_in_kernel, rows=rows, split_x=split_x),
        grid=(MIX_CHUNKS + n_rows // rows.tm,),
        in_specs=x_specs + [
            rows.mod_spec(layer),
            _layer_spec((1, D_MODEL), layer),
            _staged_spec((MIX_CHUNKS * W_IN_ROWS, D_MODEL), layer, MIX_CHUNKS),
        ],
        out_specs=[rows.spec(w, "both") for w in widths],
        out_shape=[jax.ShapeDtypeStruct((n_rows, w), d) for w, d in zip(widths, dtypes)],
        scratch_shapes=[pltpu.VMEM((MIX_CHUNKS * W_IN_ROWS, D_MODEL), BF16)],
        compiler_params=_cparams(1),
        name="in_proj",
    )(*xs, mod, g, w_in_t)


def _attn_kernel(*refs, tq, t_new, t_past, latent):
    if latent:
        (q_ref, k_ref, v_ref, ck_ref, cv_ref, cosq_ref, sinq_ref, cosk_ref, sink_ref,
         qg_ref, kg_ref, bdq_ref, bdk_ref, o_ref, ks_ref, vs_ref) = refs
    else:
        (q_ref, k_ref, v_ref, qg_ref, kg_ref, bdq_ref, bdk_ref,
         o_ref, nk_ref, nv_ref, ks_ref, vs_ref) = refs

    @pl.when(pl.program_id(1) == 0)
    def _():
        k = k_ref[...]
        kn = k * _group_inv_rms(k, bdk_ref[...]) * kg_ref[...]
        v = v_ref[...]
        if latent:
            kn = _rope(kn, cosk_ref[...], sink_ref[...])
            ck = ck_ref[...].T
            cv = cv_ref[...].T
        else:
            nk_ref[...] = kn.T
            nv_ref[...] = v.T
        for h in range(N_KV_HEADS):
            sl = slice(HEAD_DIM * h, HEAD_DIM * (h + 1))
            if latent:
                ks_ref[h, :t_past, :] = ck[:, sl].astype(BF16)
                vs_ref[h, :t_past, :HEAD_DIM] = cv[:, sl].astype(BF16)
            ks_ref[h, t_past:, :] = kn[:, sl].astype(BF16)
            vs_ref[h, t_past:, :HEAD_DIM] = v[:, sl].astype(BF16)
            vs_ref[h, :, HEAD_DIM:] = jnp.ones((t_past + t_new, LANES - HEAD_DIM), BF16)

    q = q_ref[...]
    qn = q * _group_inv_rms(q, bdq_ref[...]) * qg_ref[...]
    if latent:
        qn = _rope(qn, cosq_ref[...], sinq_ref[...])
    qn = qn * (HEAD_DIM ** -0.5 * LOG2_E)
    group = N_Q_HEADS // N_KV_HEADS
    outs = []

    def scores(h):
        qh = jnp.concatenate(
            [qn[:, HEAD_DIM * (group * h + g):HEAD_DIM * (group * h + g + 1)] for g in range(group)],
            axis=0).astype(BF16)
        return _mm_nt(qh, ks_ref[h])

    ahead = 1 if latent else N_KV_HEADS - 1
    s_q = [scores(h) for h in range(ahead)]
    for h in range(N_KV_HEADS):
        s = s_q.pop(0)
        if h + ahead < N_KV_HEADS:
            s_q.append(scores(h + ahead))
        m = jnp.max(s, axis=-1, keepdims=True)
        p = jnp.exp2(s - m).astype(BF16)
        od = _mm(p, vs_ref[h])
        o = (od / pltpu.roll(od, LANES - HEAD_DIM, axis=1))[:, :HEAD_DIM]
        outs.extend(o[tq * g:tq * (g + 1)] for g in range(group))
    o_ref[...] = jnp.concatenate(outs, axis=1).astype(BF16)


def _block_diag_ones(width):
    idx = np.arange(width) // HEAD_DIM
    return jnp.asarray(idx[:, None] == idx[None, :], dtype=BF16)


def _attention(za, qg, kg, *, batch, t_new, latent, row0, cache_k=None, cache_v=None, layer=0, rope_tabs=None):
    tq = min(1024, t_new)
    nq = t_new // tq
    t_past = cache_k.shape[3] if latent else 0
    t_keys = t_past + t_new
    kcol = ATTN_WIDTH // KV_WIDTH
    bdq = _block_diag_ones(ATTN_WIDTH)
    bdk = _block_diag_ones(KV_WIDTH)
    const = lambda shape: pl.BlockSpec(shape, lambda b, i: (0,) * len(shape))
    q0, s0 = row0 // tq, row0 // t_new
    in_specs = [
        pl.BlockSpec((tq, ATTN_WIDTH), lambda b, i: (q0 + b * nq + i, 0)),
        pl.BlockSpec((t_new, KV_WIDTH), lambda b, i: (s0 + b, kcol)),
        pl.BlockSpec((t_new, KV_WIDTH), lambda b, i: (s0 + b, kcol + 1)),
    ]
    args = [za, za, za]
    if latent:
        cos_t, sin_t = rope_tabs
        in_specs += [
            pl.BlockSpec((None, None, KV_WIDTH, t_past), lambda b, i: (b, layer, 0, 0)),
            pl.BlockSpec((None, None, KV_WIDTH, t_past), lambda b, i: (b, layer, 0, 0)),
            pl.BlockSpec((tq, ATTN_WIDTH), lambda b, i: (i, 0)),
            pl.BlockSpec((tq, ATTN_WIDTH), lambda b, i: (i, 0)),
            pl.BlockSpec((t_new, KV_WIDTH), lambda b, i: (0, 0)),
            pl.BlockSpec((t_new, KV_WIDTH), lambda b, i: (0, 0)),
        ]
        args += [cache_k, cache_v, cos_t, sin_t, cos_t, sin_t]
    in_specs += [_layer_spec((1, ATTN_WIDTH), layer), _layer_spec((1, KV_WIDTH), layer),
                 const((ATTN_WIDTH, ATTN_WIDTH)), const((KV_WIDTH, KV_WIDTH))]
    args += [qg, kg, bdq, bdk]
    out_specs = [pl.BlockSpec((tq, ATTN_WIDTH), lambda b, i: (b * nq + i, 0))]
    out_shape = [jax.ShapeDtypeStruct((batch * t_new, ATTN_WIDTH), BF16)]
    if not latent:
        out_specs += [pl.BlockSpec((None, KV_WIDTH, t_new), lambda b, i: (b, 0, 0))] * 2
        out_shape += [jax.ShapeDtypeStruct((batch, KV_WIDTH, t_new), F32)] * 2
    return pl.pallas_call(
        functools.partial(_attn_kernel, tq=tq, t_new=t_new, t_past=t_past, latent=latent),
        grid=(batch, nq),
        in_specs=in_specs,
        out_specs=out_specs,
        out_shape=out_shape,
        scratch_shapes=[pltpu.VMEM((N_KV_HEADS, t_keys, HEAD_DIM), BF16),
                        pltpu.VMEM((N_KV_HEADS, t_keys, LANES), BF16)],
        compiler_params=_cparams(2),
        name="attention_latent" if latent else "attention_context",
    )(*args)


def _running_max_sublanes(x, reverse):
    n = x.shape[0]
    row = lax.broadcasted_iota(jnp.int32, x.shape, 0)
    k = 1
    while k < n:
        if reverse:
            cand = jnp.where(row < n - k, pltpu.roll(x, n - k, axis=0), NEG_BIG)
        else:
            cand = jnp.where(row >= k, pltpu.roll(x, k, axis=0), NEG_BIG)
        x = jnp.maximum(x, cand)
        k *= 2
    return x


def _mlstm_kernel(*refs, t, chunk, has_state, emit_state, convert):
    refs = list(refs)
    zm_ref, gt_ref, cw_ref, gb_ref, mg_ref, bd_ref, fu_ref, dft_c_ref, dft_ct_ref, dft_st_ref = refs[:10]
    refs = refs[10:]
    if has_state:
        c0_ref, m0_ref = refs[:2]
        refs = refs[2:]
    convert_in, refs = refs[:len(convert)], refs[len(convert):]
    o_ref, fo_ref = refs[:2]
    refs = refs[2:]
    if emit_state:
        cs_ref, ns_ref, ms_ref = refs[:3]
        refs = refs[3:]
    convert_out, refs = refs[:len(convert)], refs[len(convert):]
    hst_ref = refs[0]

    for (n_chunks, last_valid), src_ref, dst_ref in zip(convert, convert_in, convert_out):
        w = src_ref[...]
        if last_valid is not None:
            rc = w.shape[0]
            limit = jnp.where(pl.program_id(0) >= n_chunks - 1, last_valid, rc)
            w = jnp.where(lax.broadcasted_iota(jnp.int32, (rc, 1), 0) < limit, w, 0.0)
        dst_ref[...] = w.astype(BF16)

    width = M_HEADS * M_DK
    z = zm_ref[...]
    x = z[:, :2 * width]
    row = lax.broadcasted_iota(jnp.int32, (t, 1), 0)
    x_prev = jnp.where(row == 0, 0.0, pltpu.roll(x, 1, axis=0))
    x_next = jnp.where(row == t - 1, 0.0, pltpu.roll(x, t - 1, axis=0))
    cw = cw_ref[...]
    qk = _silu(x_prev * cw[0:1, :] + x * cw[1:2, :] + x_next * cw[2:3, :])
    q = qk[:, :width]
    k = qk[:, width:] * (M_DK ** -0.5)
    v = z[:, 2 * width:2 * width + M_WIDTH]
    om = z[:, 2 * width + M_WIDTH:]
    heads = [slice(M_DK * h, M_DK * (h + 1)) for h in range(M_HEADS)]
    q_rows = [q[:, hs].astype(BF16) for hs in heads]
    k_rows = [k[:, hs].astype(BF16) for hs in heads]
    qt = q.T.astype(BF16)
    vt = v.T
    ones_r = jnp.ones((LANES - M_DV, t), F32)
    vaug_t = [jnp.concatenate([vt[hs, :], ones_r], axis=0) for hs in heads]
    vaug_tb = [a.astype(BF16) for a in vaug_t]

    gates = gt_ref[...] + gb_ref[...]
    lane = lax.broadcasted_iota(jnp.int32, (1, GATE_PAD), 1)
    is_forget = (lane & M_HEADS) != 0
    gates = jnp.where(is_forget, _log_sigmoid(gates), gates)
    gates_t = gates.T[:N_GATES, :]

    ri = lax.broadcasted_iota(jnp.int32, (chunk, chunk), 0)
    ci = lax.broadcasted_iota(jnp.int32, (chunk, chunk), 1)
    row_le_col = ri <= ci
    row_ge_col = ri >= ci

    n_chunks = t // chunk
    g_all = jnp.concatenate([gates_t[:, chunk * c:chunk * (c + 1)] for c in range(n_chunks)], axis=0)
    i_all = pltpu.roll(g_all, M_HEADS, axis=0)
    rows = lax.broadcasted_iota(jnp.int32, (N_GATES * n_chunks, 1), 0)
    is_fwd_row = (rows & (2 * M_HEADS)) == 0
    cum_p = _mm_left_f32(g_all, row_le_col.astype(BF16))
    cum_s = _mm_left_f32(g_all, row_ge_col.astype(BF16))
    b_all = jnp.where(is_fwd_row, cum_p, cum_s)
    u_all = i_all - b_all
    u_cols = u_all.T
    cols = lax.broadcasted_iota(jnp.int32, (1, N_GATES * n_chunks), 1)
    pm_cols = jnp.where((cols & (2 * M_HEADS)) == 0, _running_max_sublanes(u_cols, reverse=False),
                        _running_max_sublanes(u_cols, reverse=True))
    pm_all = pm_cols.T
    u_cols2 = u_cols * LOG2_E
    pm_all2 = pm_all * LOG2_E
    pml_all = jnp.where(is_fwd_row, pm_all[:, chunk - 1:], pm_all[:, :1])
    bl_all = jnp.where(is_fwd_row, b_all[:, chunk - 1:], b_all[:, :1])
    wk_all = jnp.exp(u_all - pml_all)

    if has_state:
        m_state = [m0_ref[M_HEADS * d:M_HEADS * (d + 1), 0:1] for d in range(2)]
        c_state = [[c0_ref[M_HEADS * d + h] for h in range(M_HEADS)] for d in range(2)]
    else:
        m_state = [jnp.zeros((M_HEADS, 1), F32) for _ in range(2)]
        c_state = [[jnp.zeros((LANES, M_DK), F32) for _ in range(M_HEADS)] for _ in range(2)]

    steps = [(j if d == 0 else n_chunks - 1 - j, d) for j in range(n_chunks) for d in range(2)]

    def gate_rows(step):
        c, direction = step
        return N_GATES * c + (2 * direction + 1) * M_HEADS

    def key_query(step):
        rs = slice(chunk * step[0], chunk * (step[0] + 1))
        return [_mm_nt(k_rows[h][rs, :], q_rows[h][rs, :]) for h in range(M_HEADS)]

    def local_sums(step, st):
        c, direction = step
        rs = slice(chunk * c, chunk * (c + 1))
        r0 = gate_rows(step)
        valid = row_le_col if direction == 0 else row_ge_col
        s_loc, x_loc = [], []
        for h in range(M_HEADS):
            w = jnp.exp2(jnp.where(valid, u_cols2[:, r0 + h:r0 + h + 1] - pm_all2[r0 + h:r0 + h + 1, :], NEG_BIG))
            s_loc.append(_mm(vaug_tb[h][:, rs], (st[h] * w).astype(BF16)))
            x_loc.append(_mm((vaug_t[h][:, rs] * wk_all[r0 + h:r0 + h + 1, :]).astype(BF16), k_rows[h][rs, :]))
        return s_loc, x_loc

    written = set()
    n_steps = len(steps)
    st_q = [key_query(steps[i]) for i in range(min(2, n_steps))]
    loc_q = [local_sums(steps[0], st_q.pop(0))]
    for i, (c, direction) in enumerate(steps):
        if i + 1 < n_steps:
            loc_q.append(local_sums(steps[i + 1], st_q.pop(0)))
        if i + 2 < n_steps:
            st_q.append(key_query(steps[i + 2]))
        s_loc, x_loc = loc_q.pop(0)
        rs = slice(chunk * c, chunk * (c + 1))
        r0 = gate_rows((c, direction))
        b4, pm4, bl, pml = (a[r0:r0 + M_HEADS, :] for a in (b_all, pm_all, bl_all, pml_all))
        m = m_state[direction]
        mx = jnp.maximum(m, pm4)
        f_loc = jnp.exp(pm4 - mx)
        f_int = jnp.exp(m - mx)
        floor = jnp.exp(-(b4 + mx))
        m_new = bl + jnp.maximum(m, pml)
        decay = jnp.exp(bl + m - m_new)
        gain = jnp.exp(bl + pml - m_new)
        m_state[direction] = m_new
        inter = [_mm(c_state[direction][h].astype(BF16), qt[heads[h], rs]) for h in range(M_HEADS)]
        for h in range(M_HEADS):
            nd = f_loc[h:h + 1, :] * s_loc[h] + f_int[h:h + 1, :] * inter[h]
            ht = nd[:M_DV, :] / jnp.maximum(jnp.abs(nd[M_DV:, :]), floor[h:h + 1, :])
            c_state[direction][h] = decay[h:h + 1, :] * c_state[direction][h] + gain[h:h + 1, :] * x_loc[h]
            if c in written:
                hst_ref[heads[h], rs] += ht
            else:
                hst_ref[heads[h], rs] = ht
        written.add(c)

    if emit_state:
        for d in range(2):
            ms_ref[M_HEADS * d:M_HEADS * (d + 1), :] = jnp.broadcast_to(m_state[d], (M_HEADS, LANES))
            for h in range(M_HEADS):
                u = M_HEADS * d + h
                cs_ref[u] = c_state[d][h][:M_DV, :].T
                ns_ref[u:u + 1, :] = c_state[d][h][M_DV:M_DV + 1, :]

    hsum = hst_ref[...].T
    hn = hsum * _group_inv_rms(hsum, bd_ref[...]) * mg_ref[...]
    o_ref[...] = (_sigmoid(om) * hn).astype(BF16)

    fa = _mm(fu_ref[...].astype(BF16), dft_c_ref[...])
    fo_ref[...] = (_mm(dft_ct_ref[...], fa[:, :F_WIDTH].astype(BF16))
                   - _mm(dft_st_ref[...], fa[:, F_WIDTH:].astype(BF16))).astype(BF16)


def _mlstm_fourier(zm, gt, fu, conv_w, gate_b, m_norm_g, *, batch, t, layer, row0, caug0=None, m0=None,
                   convert=()):
    has_state = caug0 is not None
    emit_state = not has_state
    chunk = min(M_CHUNK_K, t)
    n_units = 2 * M_HEADS
    const = lambda shape: pl.BlockSpec(shape, lambda b: (0,) * len(shape), pipeline_mode=pl.Buffered(1))
    s0 = row0 // t
    in_specs = [
        pl.BlockSpec((t, D_MODEL), lambda b: (s0 + b, 0)),
        pl.BlockSpec((t, GATE_PAD), lambda b: (s0 + b, 0)),
        _layer_spec((3, 2 * M_HEADS * M_DK), layer),
        _layer_spec((1, GATE_PAD), layer),
        _layer_spec((1, M_WIDTH), layer),
        const((M_WIDTH, M_WIDTH)),
        pl.BlockSpec((t, F_WIDTH), lambda b: (s0 + b, 0)),
        const((F_WIDTH, 2 * F_WIDTH)), const((t, t)), const((t, t)),
    ]
    args = [zm, gt, conv_w, gate_b, m_norm_g, _block_diag_ones(M_WIDTH), fu, *_dft_tables(t)]
    if has_state:
        in_specs += [pl.BlockSpec((None, n_units, LANES, M_DK), lambda b: (b, 0, 0, 0)),
                     pl.BlockSpec((None, n_units, LANES), lambda b: (b, 0, 0))]
        args += [caug0, m0]
    out_specs = [pl.BlockSpec((t, M_WIDTH), lambda b: (b, 0)), pl.BlockSpec((t, F_WIDTH), lambda b: (b, 0))]
    out_shape = [jax.ShapeDtypeStruct((batch * t, M_WIDTH), BF16), jax.ShapeDtypeStruct((batch * t, F_WIDTH), BF16)]
    if emit_state:
        out_specs += [pl.BlockSpec((None, n_units, M_DK, M_DV), lambda b: (b, 0, 0, 0)),
                      pl.BlockSpec((None, n_units, M_DK), lambda b: (b, 0, 0)),
                      pl.BlockSpec((None, n_units, LANES), lambda b: (b, 0, 0))]
        out_shape += [jax.ShapeDtypeStruct((batch, n_units, M_DK, M_DV), F32),
                      jax.ShapeDtypeStruct((batch, n_units, M_DK), F32),
                      jax.ShapeDtypeStruct((batch, n_units, LANES), F32)]
    for w, rc, first, n_chunks, _ in convert:
        assert n_chunks <= batch
        w_cols = w.shape[2]
        in_specs.append(pl.BlockSpec(
            (None, rc, w_cols), lambda b, first=first, n=n_chunks: (layer, first + jnp.minimum(b, n - 1), 0)))
        args.append(w)
        out_specs.append(pl.BlockSpec((rc, w_cols), lambda b, n=n_chunks: (jnp.minimum(b, n - 1), 0)))
        out_shape.append(jax.ShapeDtypeStruct((rc * n_chunks, w_cols), BF16))
    return pl.pallas_call(
        functools.partial(_mlstm_kernel, t=t, chunk=chunk, has_state=has_state, emit_state=emit_state,
                          convert=tuple((n, last) for _, _, _, n, last in convert)),
        grid=(batch,),
        in_specs=in_specs,
        out_specs=out_specs,
        out_shape=out_shape,
        scratch_shapes=[pltpu.VMEM((M_WIDTH, t), F32)],
        compiler_params=_cparams(1),
        name="mlstm_fourier_latent" if has_state else "mlstm_fourier_context",
    )(*args)


def _dft_tables(t):
    kt = (np.arange(t)[:, None] * np.arange(t)[None, :]) % t
    ang_t = 2.0 * np.pi * kt.astype(np.float64) / t
    ct = np.cos(ang_t) / np.sqrt(t)
    st = np.sin(ang_t) / np.sqrt(t)
    c = F_GROUP_CH
    kc = (np.arange(c)[:, None] * np.arange(c)[None, :]) % c
    ang_c = 2.0 * np.pi * kc.astype(np.float64) / c
    eye = np.eye(F_GROUPS)
    cc = np.kron(eye, np.cos(ang_c) / np.sqrt(c))
    sc = np.kron(eye, np.sin(ang_c) / np.sqrt(c))
    cs = np.concatenate([cc, sc], axis=1)
    to_dev = lambda a: jnp.asarray(a.astype(np.float32)).astype(BF16)
    return to_dev(cs), to_dev(ct), to_dev(st)


def _merge_kernel(*refs, rows, split_x):
    refs = list(refs)
    x_refs = [refs.pop(0) for _ in range(2 if split_x else 1)]
    (mod_ref, g_ref, ac_ref, al_ref, hc_ref, hl_ref, fc_ref, fl_ref, wbg_ref,
     wpa_ref, wpm_ref, wpf_ref, wo_ref, o_ref) = refs
    x = _pick(rows, *x_refs) if split_x else x_refs[0][...]
    h = _normmod(x, g_ref[...], mod_ref[1:2, :], mod_ref[0:1, :]).astype(BF16)

    def branch(j, yc_ref, yl_ref, w_ref):
        gate = _sigmoid(_mm_nt(h, wbg_ref[BG_SKIP + D_MODEL * j:BG_SKIP + D_MODEL * (j + 1), :]))
        return gate * _mm(_pick(rows, yc_ref, yl_ref), w_ref[...])

    merged = (branch(0, ac_ref, al_ref, wpa_ref) + branch(1, hc_ref, hl_ref, wpm_ref)
              + branch(2, fc_ref, fl_ref, wpf_ref))
    o_ref[...] = x + mod_ref[2:3, :] * _mm(merged.astype(BF16), wo_ref[...])


def _merge(xs, mod, g, attn, hm, fo, weights, rows, layer):
    split_x = len(xs) == 2
    n_rows = sum(x.shape[0] for x in xs)
    x_specs = [rows.spec(D_MODEL, "ctx"), rows.spec(D_MODEL, "lat")] if split_x else [rows.spec(D_MODEL, "both")]
    pair = lambda width: [rows.spec(width, "ctx"), rows.spec(width, "lat")]
    resident = lambda w: pl.BlockSpec(w.shape, lambda i: (0, 0), pipeline_mode=pl.Buffered(1))
    return pl.pallas_call(
        functools.partial(_merge_kernel, rows=rows, split_x=split_x),
        grid=(n_rows // rows.tm,),
        in_specs=x_specs + [rows.mod_spec(layer), _layer_spec((1, D_MODEL), layer)]
        + pair(ATTN_WIDTH) + pair(M_WIDTH) + pair(F_WIDTH) + [resident(w) for w in weights],
        out_specs=rows.spec(D_MODEL, "both"),
        out_shape=jax.ShapeDtypeStruct((n_rows, D_MODEL), F32),
        compiler_params=_cparams(1),
        name="merge",
    )(*xs, mod, g, *attn, *hm, *fo, *weights)


def _ffn_kernel(*refs, rows, split_out):
    x_ref, mod_ref, g_ref, win_ref, wout_ref = refs[:5]
    out_refs = refs[5:]
    x = x_ref[...]
    h = _normmod(x, g_ref[...], mod_ref[4:5, :], mod_ref[3:4, :]).astype(BF16)
    u = _mm(h, win_ref[...])
    a = (_silu(u[:, :FF_HIDDEN]) * u[:, FF_HIDDEN:]).astype(BF16)
    y = x + mod_ref[5:6, :] * _mm(a, wout_ref[...])
    if split_out:
        @pl.when(rows.is_ctx())
        def _():
            out_refs[0][...] = y

        @pl.when(jnp.logical_not(rows.is_ctx()))
        def _():
            out_refs[1][...] = y
    else:
        out_refs[0][...] = y


def _ffn(x, mod, g, w_in, w_out, rows, layer, split_out):
    n_rows = x.shape[0]
    n_ctx_rows = rows.n_ctx * rows.tm
    resident = lambda w: pl.BlockSpec(w.shape, lambda i: (0, 0), pipeline_mode=pl.Buffered(1))
    if split_out:
        out_specs = [rows.spec(D_MODEL, "ctx"), rows.spec(D_MODEL, "lat")]
        out_shape = [jax.ShapeDtypeStruct((n_ctx_rows, D_MODEL), F32),
                     jax.ShapeDtypeStruct((n_rows - n_ctx_rows, D_MODEL), F32)]
    else:
        out_specs = [rows.spec(D_MODEL, "both")]
        out_shape = [jax.ShapeDtypeStruct((n_rows, D_MODEL), F32)]
    return pl.pallas_call(
        functools.partial(_ffn_kernel, rows=rows, split_out=split_out),
        grid=(n_rows // rows.tm,),
        in_specs=[rows.spec(D_MODEL, "both"), rows.mod_spec(layer), _layer_spec((1, D_MODEL), layer),
                  resident(w_in), resident(w_out)],
        out_specs=out_specs,
        out_shape=out_shape,
        compiler_params=_cparams(1),
        name="ffn",
    )(x, mod, g, w_in, w_out)


def _rope_tables(t):
    n = HEAD_DIM // 4
    inv = 1.0 / (ROPE_THETA ** (np.arange(n, dtype=np.float64) / n))
    pos = np.arange(t)
    ang_r = (pos // GRID_W)[:, None] * inv[None, :]
    ang_c = (pos % GRID_W)[:, None] * inv[None, :]
    cos = np.concatenate([np.cos(ang_r)] * 2 + [np.cos(ang_c)] * 2, axis=1)
    sin = np.concatenate([-np.sin(ang_r), np.sin(ang_r), -np.sin(ang_c), np.sin(ang_c)], axis=1)
    tile = lambda a: jnp.asarray(np.tile(a, (1, N_Q_HEADS)).astype(np.float32))
    return tile(cos), tile(sin)


def kernel(x_prompt, x_sample, cache_k, cache_v, state_C, state_n, state_m, c, c_ctx, w_ada, b_ada,
           norm1_g, w_in, q_norm_g, k_norm_g, m_conv_w, m_gate_b, m_norm_g, w_proj_attn,
           w_proj_mlstm, w_proj_fourier, w_out, norm2_g, w_ffn_in, w_ffn_out):
    n_ctx, t_ctx, _ = x_prompt.shape
    n_lat, t_lat, _ = x_sample.shape
    t_past = cache_k.shape[2]
    n_units = 2 * M_HEADS

    cvec = jnp.concatenate([c_ctx[None, :], c], axis=0)
    cvec = jnp.pad(cvec, ((0, MOD_ROWS - cvec.shape[0]), (0, 0)))
    mod = _modulation(cvec, w_ada, b_ada).reshape(DEPTH, MOD_ROWS, N_MOD, D_MODEL)

    rope_tabs = _rope_tables(t_lat)
    to_rows = lambda a: jnp.transpose(a, (0, 1, 3, 4, 2)).reshape(n_lat, DEPTH, KV_WIDTH, t_past)
    ck, cv = to_rows(cache_k), to_rows(cache_v)
    rows_ctx = n_ctx * t_ctx
    rows_in = _Rows(rows_ctx, t_lat, n_stage=MIX_CHUNKS, tm=TM_IN)
    rows_merge = _Rows(rows_ctx, t_lat)
    rows_ffn = _Rows(rows_ctx, t_lat)

    w_in_t = jnp.swapaxes(w_in, 1, 2)
    row = lambda a: a[:, None, :]
    norm1, norm2 = row(norm1_g), row(norm2_g)
    qg, kg = row(jnp.tile(q_norm_g, (1, N_Q_HEADS))), row(jnp.tile(k_norm_g, (1, N_KV_HEADS)))
    gate_b = row(jnp.pad(m_gate_b, ((0, 0), (0, GATE_PAD - N_GATES))))
    m_norm = row(m_norm_g)

    per_step = lambda w, n: (w, w.shape[1] // n, 0, n, None)
    xs = (x_prompt.reshape(rows_ctx, D_MODEL), x_sample.reshape(n_lat * t_lat, D_MODEL))
    ks, vs, cs_all, ns_all, ms_all = [], [], [], [], []
    for l in range(DEPTH):
        za, zm, fu, gt = _in_proj(xs, mod, norm1, w_in_t, rows_in, l)
        attn_c, k_l, v_l = _attention(za, qg, kg, batch=n_ctx, t_new=t_ctx, latent=False, row0=0, layer=l)
        hm_c, fo_c, cs, ns, ms, *merge_w = _mlstm_fourier(
            zm, gt, fu, m_conv_w, gate_b, m_norm, batch=n_ctx, t=t_ctx, layer=l, row0=0,
            convert=((w_in_t, W_IN_ROWS, BG_FIRST, BG_CHUNKS, BG_LAST),)
            + tuple(per_step(w, n_ctx) for w in (w_proj_attn, w_proj_mlstm, w_proj_fourier, w_out)))
        ks.append(k_l)
        vs.append(v_l)
        cs_all.append(cs)
        ns_all.append(ns)
        ms_all.append(ms)

        c0t = jnp.swapaxes(state_C[:, l].astype(F32).reshape(n_lat, n_units, M_DK, M_DV), -1, -2)
        n0 = state_n[:, l].astype(F32).reshape(n_lat, n_units, 1, M_DK)
        caug0 = jnp.concatenate(
            [c0t, jnp.broadcast_to(n0, (n_lat, n_units, LANES - M_DV, M_DK))], axis=-2)
        m0 = jnp.broadcast_to(state_m[:, l].astype(F32).reshape(n_lat, n_units, 1),
                              (n_lat, n_units, LANES))
        (attn_l,) = _attention(za, qg, kg, batch=n_lat, t_new=t_lat, latent=True, row0=rows_ctx,
                               cache_k=ck, cache_v=cv, layer=l, rope_tabs=rope_tabs)
        hm_l, fo_l, w_ffn_in_b, w_ffn_out_b = _mlstm_fourier(
            zm, gt, fu, m_conv_w, gate_b, m_norm, batch=n_lat, t=t_lat, layer=l, row0=rows_ctx,
            caug0=caug0, m0=m0, convert=(per_step(w_ffn_in, n_lat), per_step(w_ffn_out, n_lat)))

        x1 = _merge(xs, mod, norm1, (attn_c, attn_l), (hm_c, hm_l), (fo_c, fo_l), merge_w, rows_merge, l)
        xs = tuple(_ffn(x1, mod, norm2, w_ffn_in_b, w_ffn_out_b, rows_ffn, l, split_out=l == DEPTH - 1))
    xp, xs = xs

    kv_out = lambda per_layer: jnp.transpose(
        jnp.stack(per_layer, axis=1).reshape(n_ctx, DEPTH, N_KV_HEADS, HEAD_DIM, t_ctx), (0, 1, 4, 2, 3))
    new_c = jnp.stack(cs_all, axis=1).reshape(n_ctx, DEPTH, 2, M_HEADS, M_DK, M_DV)
    new_n = jnp.stack(ns_all, axis=1).reshape(n_ctx, DEPTH, 2, M_HEADS, M_DK)
    new_m = jnp.stack(ms_all, axis=1)[..., 0].reshape(n_ctx, DEPTH, 2, M_HEADS)
    return (xp.reshape(n_ctx, t_ctx, D_MODEL), xs.reshape(n_lat, t_lat, D_MODEL),
            kv_out(ks), kv_out(vs), new_c, new_n, new_m)
```

```python
import functools

import numpy as np
import jax
import jax.numpy as jnp
from jax import lax
from jax.experimental import pallas as pl
from jax.experimental.pallas import tpu as pltpu

D_MODEL = 1024
DEPTH = 2
GRID_W = 64
HEAD_DIM = 64
N_Q_HEADS = 8
N_KV_HEADS = 4
ATTN_WIDTH = N_Q_HEADS * HEAD_DIM
KV_WIDTH = N_KV_HEADS * HEAD_DIM
ROPE_THETA = 10000.0
M_HEADS = 4
M_DK = 64
M_DV = 64
M_WIDTH = M_HEADS * M_DV
F_GROUPS = 4
F_GROUP_CH = 64
F_WIDTH = F_GROUPS * F_GROUP_CH
FF_HIDDEN = -(-8 * D_MODEL // (3 * 256)) * 256
EPS = 1e-6
N_GATES = 4 * M_HEADS
IN_SIZES = (ATTN_WIDTH, KV_WIDTH, KV_WIDTH, M_HEADS * M_DK, M_HEADS * M_DK, M_WIDTH, M_WIDTH,
            N_GATES, F_WIDTH, 3 * D_MODEL)
IN_OFFS = tuple(int(v) for v in np.cumsum((0,) + IN_SIZES))

LANES = 128
GATE_PAD = LANES
N_MOD = 6
MOD_ROWS = 16
M_CHUNK_K = 128
NEG_BIG = -1e30
LOG2_E = 1.4426950408889634
VMEM_LIMIT = 56 * 1024 * 1024
TM = 512
TM_IN = 1024
W_IN_ROWS = 256

F32 = jnp.float32
BF16 = jnp.bfloat16


def _cparams(n_axes):
    return pltpu.CompilerParams(dimension_semantics=("arbitrary",) * n_axes,
                                vmem_limit_bytes=VMEM_LIMIT)


def _layer_spec(shape, layer):
    return pl.BlockSpec((None,) + tuple(shape), lambda *_: (layer,) + (0,) * len(shape),
                        pipeline_mode=pl.Buffered(1))


class _Rows:
    def __init__(self, rows_ctx, t_lat, n_stage=0, tm=TM):
        self.tm = tm
        self.n_ctx = rows_ctx // tm
        self.per_seq = t_lat // tm
        self.n_stage = n_stage

    def tile(self, i):
        return jnp.maximum(i - self.n_stage, 0)

    def both(self, i):
        return (self.tile(i), 0)

    def ctx(self, i):
        return (jnp.minimum(self.tile(i), self.n_ctx - 1), 0)

    def lat(self, i):
        return (jnp.maximum(self.tile(i) - self.n_ctx, 0), 0)

    def mod_spec(self, layer):
        def index(i):
            t = self.tile(i)
            return (layer, jnp.where(t < self.n_ctx, 0, 1 + (t - self.n_ctx) // self.per_seq), 0, 0)
        return pl.BlockSpec((None, None, N_MOD, D_MODEL), index)

    def spec(self, width, which):
        return pl.BlockSpec((self.tm, width), getattr(self, which))

    def is_ctx(self):
        return pl.program_id(0) - self.n_stage < self.n_ctx


def _pick(rows, ctx_ref, lat_ref):
    return jnp.where(rows.is_ctx(), ctx_ref[...], lat_ref[...])


def _staged_spec(shape, layer, n_chunks):
    rc, cols = shape[0] // n_chunks, shape[1]
    return pl.BlockSpec((None, rc, cols), lambda i: (layer, jnp.minimum(i, n_chunks - 1), 0))


def _stage(w_ref, scratch_ref, n_chunks):
    i = pl.program_id(0)
    rc = w_ref.shape[0]

    @pl.when(i < n_chunks)
    def _():
        scratch_ref[pl.ds(pl.multiple_of(i * rc, rc), rc), :] = w_ref[...].astype(BF16)


def _mm(a, b):
    return jnp.dot(a, b, preferred_element_type=F32)


def _mm_nt(a, b):
    return lax.dot_general(a, b, (((1,), (1,)), ((), ())), preferred_element_type=F32)


def _split3(x):
    hi = x.astype(BF16)
    r = x - hi.astype(F32)
    mid = r.astype(BF16)
    lo = (r - mid.astype(F32)).astype(BF16)
    return hi, mid, lo


def _mm_left_f32(x, m_bf16):
    hi, mid, lo = _split3(x)
    return _mm(hi, m_bf16) + _mm(mid, m_bf16) + _mm(lo, m_bf16)


def _sigmoid(x):
    return 1.0 / (1.0 + jnp.exp(-x))


def _silu(x):
    return x * _sigmoid(x)


def _log_sigmoid(x):
    return jnp.minimum(x, 0.0) - jnp.log(1.0 + jnp.exp(-jnp.abs(x)))


def _normmod(x, g, scale, shift):
    ms = jnp.mean(x * x, axis=-1, keepdims=True)
    return (x * lax.rsqrt(ms + EPS)) * g * (1.0 + scale) + shift


def _group_inv_rms(x, bd):
    x2 = x * x
    hi = x2.astype(BF16)
    lo = (x2 - hi.astype(F32)).astype(BF16)
    ss = _mm(hi, bd) + _mm(lo, bd)
    return lax.rsqrt(ss * (1.0 / HEAD_DIM) + EPS)


def _rope(x, cos, sin_signed):
    w = x.shape[1]
    lane = lax.broadcasted_iota(jnp.int32, x.shape, 1)
    up = pltpu.roll(x, w - 16, axis=1)
    dn = pltpu.roll(x, 16, axis=1)
    partner = jnp.where((lane & 31) < 16, up, dn)
    return x * cos + partner * sin_signed


def _mod_kernel(c_ref, w_ref, b_ref, o_ref):
    s = _silu(c_ref[...]).astype(BF16)
    o_ref[...] = _mm(s, w_ref[...].astype(BF16)) + b_ref[...]


def _modulation(cvec, w_ada, b_ada):
    tn = 1024
    n_out = N_MOD * D_MODEL
    return pl.pallas_call(
        _mod_kernel,
        grid=(DEPTH, n_out // tn),
        in_specs=[
            pl.BlockSpec((MOD_ROWS, D_MODEL), lambda l, j: (0, 0)),
            pl.BlockSpec((None, D_MODEL, tn), lambda l, j: (l, 0, j)),
            pl.BlockSpec((None, 1, tn), lambda l, j: (l, 0, j)),
        ],
        out_specs=pl.BlockSpec((None, MOD_ROWS, tn), lambda l, j: (l, 0, j)),
        out_shape=jax.ShapeDtypeStruct((DEPTH, MOD_ROWS, n_out), F32),
        compiler_params=_cparams(2),
        name="modulation",
    )(cvec, w_ada, b_ada.reshape(DEPTH, 1, n_out))


GATE_OFF, FU_OFF, BG_OFF = IN_OFFS[7], IN_OFFS[8], IN_OFFS[9]
IN_COLS = IN_OFFS[-1]
MIX_CHUNKS = -(-BG_OFF // W_IN_ROWS)
BG_FIRST = BG_OFF // W_IN_ROWS
BG_CHUNKS = -(-IN_COLS // W_IN_ROWS) - BG_FIRST
BG_SKIP = BG_OFF - BG_FIRST * W_IN_ROWS
BG_LAST = IN_COLS - (BG_FIRST + BG_CHUNKS - 1) * W_IN_ROWS


def _in_kernel(*refs, rows, split_x):
    refs = list(refs)
    x_refs = [refs.pop(0) for _ in range(2 if split_x else 1)]
    mod_ref, g_ref, w_ref, za_ref, zm_ref, fu_ref, gt_ref, w_s = refs
    _stage(w_ref, w_s, MIX_CHUNKS)

    @pl.when(pl.program_id(0) >= MIX_CHUNKS)
    def _():
        x = _pick(rows, *x_refs) if split_x else x_refs[0][...]
        h = _normmod(x, g_ref[...], mod_ref[1:2, :], mod_ref[0:1, :])
        z = _mm_nt(h.astype(BF16), w_s[:BG_OFF, :])
        za_ref[...] = z[:, :D_MODEL]
        zm_ref[...] = z[:, D_MODEL:GATE_OFF]
        gt_ref[...] = z[:, GATE_OFF:GATE_OFF + GATE_PAD]
        fu_ref[...] = z[:, FU_OFF:FU_OFF + F_WIDTH].astype(BF16)


def _in_proj(xs, mod, g, w_in_t, rows, layer):
    split_x = len(xs) == 2
    n_rows = sum(x.shape[0] for x in xs)
    x_specs = [rows.spec(D_MODEL, "ctx"), rows.spec(D_MODEL, "lat")] if split_x else [rows.spec(D_MODEL, "both")]
    widths = (D_MODEL, D_MODEL, F_WIDTH, GATE_PAD)
    dtypes = (F32, F32, BF16, F32)
    return pl.pallas_call(
        functools.partial(_in_kernel, rows=rows, split_x=split_x),
        grid=(MIX_CHUNKS + n_rows // rows.tm,),
        in_specs=x_specs + [
            rows.mod_spec(layer),
            _layer_spec((1, D_MODEL), layer),
            _staged_spec((MIX_CHUNKS * W_IN_ROWS, D_MODEL), layer, MIX_CHUNKS),
        ],
        out_specs=[rows.spec(w, "both") for w in widths],
        out_shape=[jax.ShapeDtypeStruct((n_rows, w), d) for w, d in zip(widths, dtypes)],
        scratch_shapes=[pltpu.VMEM((MIX_CHUNKS * W_IN_ROWS, D_MODEL), BF16)],
        compiler_params=_cparams(1),
        name="in_proj",
    )(*xs, mod, g, w_in_t)


def _attn_kernel(*refs, tq, t_new, t_past, latent, fill_layer):
    if latent:
        (q_ref, k_ref, v_ref, ck_ref, cv_ref, cosq_ref, sinq_ref, cosk_ref, sink_ref,
         qg_ref, kg_ref, bdq_ref, bdk_ref, o_ref, ks_ref, vs_ref) = refs
    else:
        q_ref, k_ref, v_ref, qg_ref, kg_ref, bdq_ref, bdk_ref = refs[:7]
        o_ref, nk_ref, nv_ref, ks_ref, vs_ref = refs[-5:]

    @pl.when(pl.program_id(1) == 0)
    def _():
        k = k_ref[...]
        kn = k * _group_inv_rms(k, bdk_ref[...]) * kg_ref[...]
        v = v_ref[...]
        if latent:
            kn = _rope(kn, cosk_ref[...], sink_ref[...])
            ck = ck_ref[...].T
            cv = cv_ref[...].T
        else:
            if fill_layer is None:
                nk_ref[...] = kn.T
                nv_ref[...] = v.T
            else:
                for l2 in range(DEPTH):
                    nk_ref[l2] = kn.T if l2 == fill_layer else jnp.zeros((KV_WIDTH, t_new), F32)
                    nv_ref[l2] = v.T if l2 == fill_layer else jnp.zeros((KV_WIDTH, t_new), F32)
        for h in range(N_KV_HEADS):
            sl = slice(HEAD_DIM * h, HEAD_DIM * (h + 1))
            if latent:
                ks_ref[h, :t_past, :] = ck[:, sl].astype(BF16)
                vs_ref[h, :t_past, :HEAD_DIM] = cv[:, sl].astype(BF16)
            ks_ref[h, t_past:, :] = kn[:, sl].astype(BF16)
            vs_ref[h, t_past:, :HEAD_DIM] = v[:, sl].astype(BF16)
            vs_ref[h, :, HEAD_DIM:] = jnp.ones((t_past + t_new, LANES - HEAD_DIM), BF16)

    q = q_ref[...]
    qn = q * _group_inv_rms(q, bdq_ref[...]) * qg_ref[...]
    if latent:
        qn = _rope(qn, cosq_ref[...], sinq_ref[...])
    qn = qn * (HEAD_DIM ** -0.5 * LOG2_E)
    group = N_Q_HEADS // N_KV_HEADS
    outs = []

    def scores(h):
        qh = jnp.concatenate(
            [qn[:, HEAD_DIM * (group * h + g):HEAD_DIM * (group * h + g + 1)] for g in range(group)],
            axis=0).astype(BF16)
        return _mm_nt(qh, ks_ref[h])

    ahead = 1 if latent else N_KV_HEADS - 1
    s_q = [scores(h) for h in range(ahead)]
    for h in range(N_KV_HEADS):
        s = s_q.pop(0)
        if h + ahead < N_KV_HEADS:
            s_q.append(scores(h + ahead))
        m = jnp.max(s, axis=-1, keepdims=True)
        p = jnp.exp2(s - m).astype(BF16)
        od = _mm(p, vs_ref[h])
        o = (od / pltpu.roll(od, LANES - HEAD_DIM, axis=1))[:, :HEAD_DIM]
        outs.extend(o[tq * g:tq * (g + 1)] for g in range(group))
    o_ref[...] = jnp.concatenate(outs, axis=1).astype(BF16)


def _block_diag_ones(width):
    idx = np.arange(width) // HEAD_DIM
    return jnp.asarray(idx[:, None] == idx[None, :], dtype=BF16)


def _attention(za, qg, kg, *, batch, t_new, latent, row0, cache_k=None, cache_v=None, layer=0, rope_tabs=None,
               carried=None):
    tq = min(1024, t_new)
    nq = t_new // tq
    t_past = cache_k.shape[3] if latent else 0
    t_keys = t_past + t_new
    kcol = ATTN_WIDTH // KV_WIDTH
    bdq = _block_diag_ones(ATTN_WIDTH)
    bdk = _block_diag_ones(KV_WIDTH)
    const = lambda shape: pl.BlockSpec(shape, lambda b, i: (0,) * len(shape))
    q0, s0 = row0 // tq, row0 // t_new
    in_specs = [
        pl.BlockSpec((tq, ATTN_WIDTH), lambda b, i: (q0 + b * nq + i, 0)),
        pl.BlockSpec((t_new, KV_WIDTH), lambda b, i: (s0 + b, kcol)),
        pl.BlockSpec((t_new, KV_WIDTH), lambda b, i: (s0 + b, kcol + 1)),
    ]
    args = [za, za, za]
    if latent:
        cos_t, sin_t = rope_tabs
        in_specs += [
            pl.BlockSpec((None, None, KV_WIDTH, t_past), lambda b, i: (b, layer, 0, 0)),
            pl.BlockSpec((None, None, KV_WIDTH, t_past), lambda b, i: (b, layer, 0, 0)),
            pl.BlockSpec((tq, ATTN_WIDTH), lambda b, i: (i, 0)),
            pl.BlockSpec((tq, ATTN_WIDTH), lambda b, i: (i, 0)),
            pl.BlockSpec((t_new, KV_WIDTH), lambda b, i: (0, 0)),
            pl.BlockSpec((t_new, KV_WIDTH), lambda b, i: (0, 0)),
        ]
        args += [cache_k, cache_v, cos_t, sin_t, cos_t, sin_t]
    in_specs += [_layer_spec((1, ATTN_WIDTH), layer), _layer_spec((1, KV_WIDTH), layer),
                 const((ATTN_WIDTH, ATTN_WIDTH)), const((KV_WIDTH, KV_WIDTH))]
    args += [qg, kg, bdq, bdk]
    out_specs = [pl.BlockSpec((tq, ATTN_WIDTH), lambda b, i: (b * nq + i, 0))]
    out_shape = [jax.ShapeDtypeStruct((batch * t_new, ATTN_WIDTH), BF16)]
    aliases = {}
    if not latent:
        out_shape += [jax.ShapeDtypeStruct((batch, DEPTH, KV_WIDTH, t_new), F32)] * 2
        if carried is None:
            out_specs += [pl.BlockSpec((None, DEPTH, KV_WIDTH, t_new), lambda b, i: (b, 0, 0, 0))] * 2
        else:
            out_specs += [pl.BlockSpec((None, None, KV_WIDTH, t_new), lambda b, i: (b, layer, 0, 0))] * 2
            aliases = {len(args) + j: 1 + j for j in range(2)}
            in_specs += [pl.BlockSpec(memory_space=pl.ANY)] * 2
            args += list(carried)
    return pl.pallas_call(
        functools.partial(_attn_kernel, tq=tq, t_new=t_new, t_past=t_past, latent=latent,
                          fill_layer=layer if not latent and carried is None else None),
        grid=(batch, nq),
        in_specs=in_specs,
        out_specs=out_specs,
        out_shape=out_shape,
        scratch_shapes=[pltpu.VMEM((N_KV_HEADS, t_keys, HEAD_DIM), BF16),
                        pltpu.VMEM((N_KV_HEADS, t_keys, LANES), BF16)],
        input_output_aliases=aliases,
        compiler_params=_cparams(2),
        name="attention_latent" if latent else "attention_context",
    )(*args)


def _running_max_sublanes(x, reverse):
    n = x.shape[0]
    row = lax.broadcasted_iota(jnp.int32, x.shape, 0)
    k = 1
    while k < n:
        if reverse:
            cand = jnp.where(row < n - k, pltpu.roll(x, n - k, axis=0), NEG_BIG)
        else:
            cand = jnp.where(row >= k, pltpu.roll(x, k, axis=0), NEG_BIG)
        x = jnp.maximum(x, cand)
        k *= 2
    return x


def _mlstm_kernel(*refs, t, chunk, has_state, emit_state, convert, n_carried, fill_layer):
    refs = list(refs)
    zm_ref, gt_ref, cw_ref, gb_ref, mg_ref, bd_ref, fu_ref, dft_c_ref, dft_ct_ref, dft_st_ref = refs[:10]
    refs = refs[10:]
    if has_state:
        c0_ref, m0_ref = refs[:2]
        refs = refs[2:]
    convert_in, refs = refs[:len(convert)], refs[len(convert):]
    refs = refs[n_carried:]
    o_ref, fo_ref = refs[:2]
    refs = refs[2:]
    if emit_state:
        cs_ref, ns_ref, ms_ref = refs[:3]
        refs = refs[3:]
    convert_out, refs = refs[:len(convert)], refs[len(convert):]
    hst_ref = refs[0]

    for (n_chunks, last_valid), src_ref, dst_ref in zip(convert, convert_in, convert_out):
        w = src_ref[...]
        if last_valid is not None:
            rc = w.shape[0]
            limit = jnp.where(pl.program_id(0) >= n_chunks - 1, last_valid, rc)
            w = jnp.where(lax.broadcasted_iota(jnp.int32, (rc, 1), 0) < limit, w, 0.0)
        dst_ref[...] = w.astype(BF16)

    width = M_HEADS * M_DK
    z = zm_ref[...]
    x = z[:, :2 * width]
    row = lax.broadcasted_iota(jnp.int32, (t, 1), 0)
    x_prev = jnp.where(row == 0, 0.0, pltpu.roll(x, 1, axis=0))
    x_next = jnp.where(row == t - 1, 0.0, pltpu.roll(x, t - 1, axis=0))
    cw = cw_ref[...]
    qk = _silu(x_prev * cw[0:1, :] + x * cw[1:2, :] + x_next * cw[2:3, :])
    q = qk[:, :width]
    k = qk[:, width:] * (M_DK ** -0.5)
    v = z[:, 2 * width:2 * width + M_WIDTH]
    om = z[:, 2 * width + M_WIDTH:]
    heads = [slice(M_DK * h, M_DK * (h + 1)) for h in range(M_HEADS)]
    q_rows = [q[:, hs].astype(BF16) for hs in heads]
    k_rows = [k[:, hs].astype(BF16) for hs in heads]
    qt = q.T.astype(BF16)
    vt = v.T
    ones_r = jnp.ones((LANES - M_DV, t), F32)
    vaug_t = [jnp.concatenate([vt[hs, :], ones_r], axis=0) for hs in heads]
    vaug_tb = [a.astype(BF16) for a in vaug_t]

    gates = gt_ref[...] + gb_ref[...]
    lane = lax.broadcasted_iota(jnp.int32, (1, GATE_PAD), 1)
    is_forget = (lane & M_HEADS) != 0
    gates = jnp.where(is_forget, _log_sigmoid(gates), gates)
    gates_t = gates.T[:N_GATES, :]

    ri = lax.broadcasted_iota(jnp.int32, (chunk, chunk), 0)
    ci = lax.broadcasted_iota(jnp.int32, (chunk, chunk), 1)
    row_le_col = ri <= ci
    row_ge_col = ri >= ci

    n_chunks = t // chunk
    g_all = jnp.concatenate([gates_t[:, chunk * c:chunk * (c + 1)] for c in range(n_chunks)], axis=0)
    i_all = pltpu.roll(g_all, M_HEADS, axis=0)
    rows = lax.broadcasted_iota(jnp.int32, (N_GATES * n_chunks, 1), 0)
    is_fwd_row = (rows & (2 * M_HEADS)) == 0
    cum_p = _mm_left_f32(g_all, row_le_col.astype(BF16))
    cum_s = _mm_left_f32(g_all, row_ge_col.astype(BF16))
    b_all = jnp.where(is_fwd_row, cum_p, cum_s)
    u_all = i_all - b_all
    u_cols = u_all.T
    cols = lax.broadcasted_iota(jnp.int32, (1, N_GATES * n_chunks), 1)
    pm_cols = jnp.where((cols & (2 * M_HEADS)) == 0, _running_max_sublanes(u_cols, reverse=False),
                        _running_max_sublanes(u_cols, reverse=True))
    pm_all = pm_cols.T
    u_cols2 = u_cols * LOG2_E
    pm_all2 = pm_all * LOG2_E
    pml_all = jnp.where(is_fwd_row, pm_all[:, chunk - 1:], pm_all[:, :1])
    bl_all = jnp.where(is_fwd_row, b_all[:, chunk - 1:], b_all[:, :1])
    wk_all = jnp.exp(u_all - pml_all)

    if has_state:
        m_state = [m0_ref[M_HEADS * d:M_HEADS * (d + 1), 0:1] for d in range(2)]
        c_state = [[c0_ref[M_HEADS * d + h] for h in range(M_HEADS)] for d in range(2)]
    else:
        m_state = [jnp.zeros((M_HEADS, 1), F32) for _ in range(2)]
        c_state = [[jnp.zeros((LANES, M_DK), F32) for _ in range(M_HEADS)] for _ in range(2)]

    steps = [(j if d == 0 else n_chunks - 1 - j, d) for j in range(n_chunks) for d in range(2)]

    def gate_rows(step):
        c, direction = step
        return N_GATES * c + (2 * direction + 1) * M_HEADS

    def key_query(step):
        rs = slice(chunk * step[0], chunk * (step[0] + 1))
        return [_mm_nt(k_rows[h][rs, :], q_rows[h][rs, :]) for h in range(M_HEADS)]

    def local_sums(step, st):
        c, direction = step
        rs = slice(chunk * c, chunk * (c + 1))
        r0 = gate_rows(step)
        valid = row_le_col if direction == 0 else row_ge_col
        s_loc, x_loc = [], []
        for h in range(M_HEADS):
            w = jnp.exp2(jnp.where(valid, u_cols2[:, r0 + h:r0 + h + 1] - pm_all2[r0 + h:r0 + h + 1, :], NEG_BIG))
            s_loc.append(_mm(vaug_tb[h][:, rs], (st[h] * w).astype(BF16)))
            x_loc.append(_mm((vaug_t[h][:, rs] * wk_all[r0 + h:r0 + h + 1, :]).astype(BF16), k_rows[h][rs, :]))
        return s_loc, x_loc

    written = set()
    n_steps = len(steps)
    st_q = [key_query(steps[i]) for i in range(min(2, n_steps))]
    loc_q = [local_sums(steps[0], st_q.pop(0))]
    for i, (c, direction) in enumerate(steps):
        if i + 1 < n_steps:
            loc_q.append(local_sums(steps[i + 1], st_q.pop(0)))
        if i + 2 < n_steps:
            st_q.append(key_query(steps[i + 2]))
        s_loc, x_loc = loc_q.pop(0)
        rs = slice(chunk * c, chunk * (c + 1))
        r0 = gate_rows((c, direction))
        b4, pm4, bl, pml = (a[r0:r0 + M_HEADS, :] for a in (b_all, pm_all, bl_all, pml_all))
        m = m_state[direction]
        mx = jnp.maximum(m, pm4)
        f_loc = jnp.exp(pm4 - mx)
        f_int = jnp.exp(m - mx)
        floor = jnp.exp(-(b4 + mx))
        m_new = bl + jnp.maximum(m, pml)
        decay = jnp.exp(bl + m - m_new)
        gain = jnp.exp(bl + pml - m_new)
        m_state[direction] = m_new
        inter = [_mm(c_state[direction][h].astype(BF16), qt[heads[h], rs]) for h in range(M_HEADS)]
        for h in range(M_HEADS):
            nd = f_loc[h:h + 1, :] * s_loc[h] + f_int[h:h + 1, :] * inter[h]
            ht = nd[:M_DV, :] / jnp.maximum(jnp.abs(nd[M_DV:, :]), floor[h:h + 1, :])
            c_state[direction][h] = decay[h:h + 1, :] * c_state[direction][h] + gain[h:h + 1, :] * x_loc[h]
            if c in written:
                hst_ref[heads[h], rs] += ht
            else:
                hst_ref[heads[h], rs] = ht
        written.add(c)

    if emit_state:
        if fill_layer is None:
            cs_out, ns_out, ms_out = cs_ref, ns_ref, ms_ref
        else:
            for l2 in range(DEPTH):
                if l2 != fill_layer:
                    cs_ref[l2] = jnp.zeros(cs_ref.shape[1:], F32)
                    ns_ref[l2] = jnp.zeros(ns_ref.shape[1:], F32)
                    ms_ref[l2] = jnp.zeros(ms_ref.shape[1:], F32)
            cs_out, ns_out, ms_out = cs_ref.at[fill_layer], ns_ref.at[fill_layer], ms_ref.at[fill_layer]
        for d in range(2):
            ms_out[M_HEADS * d:M_HEADS * (d + 1), :] = jnp.broadcast_to(m_state[d], (M_HEADS, LANES))
            for h in range(M_HEADS):
                u = M_HEADS * d + h
                cs_out[u] = c_state[d][h][:M_DV, :].T
                ns_out[u:u + 1, :] = c_state[d][h][M_DV:M_DV + 1, :]

    hsum = hst_ref[...].T
    hn = hsum * _group_inv_rms(hsum, bd_ref[...]) * mg_ref[...]
    o_ref[...] = (_sigmoid(om) * hn).astype(BF16)

    fa = _mm(fu_ref[...].astype(BF16), dft_c_ref[...])
    fo_ref[...] = (_mm(dft_ct_ref[...], fa[:, :F_WIDTH].astype(BF16))
                   - _mm(dft_st_ref[...], fa[:, F_WIDTH:].astype(BF16))).astype(BF16)


def _mlstm_fourier(zm, gt, fu, conv_w, gate_b, m_norm_g, *, batch, t, layer, row0, caug0=None, m0=None,
                   convert=(), carried=None):
    has_state = caug0 is not None
    emit_state = not has_state
    chunk = min(M_CHUNK_K, t)
    n_units = 2 * M_HEADS
    const = lambda shape: pl.BlockSpec(shape, lambda b: (0,) * len(shape), pipeline_mode=pl.Buffered(1))
    s0 = row0 // t
    in_specs = [
        pl.BlockSpec((t, D_MODEL), lambda b: (s0 + b, 0)),
        pl.BlockSpec((t, GATE_PAD), lambda b: (s0 + b, 0)),
        _layer_spec((3, 2 * M_HEADS * M_DK), layer),
        _layer_spec((1, GATE_PAD), layer),
        _layer_spec((1, M_WIDTH), layer),
        const((M_WIDTH, M_WIDTH)),
        pl.BlockSpec((t, F_WIDTH), lambda b: (s0 + b, 0)),
        const((F_WIDTH, 2 * F_WIDTH)), const((t, t)), const((t, t)),
    ]
    args = [zm, gt, conv_w, gate_b, m_norm_g, _block_diag_ones(M_WIDTH), fu, *_dft_tables(t)]
    if has_state:
        in_specs += [pl.BlockSpec((None, n_units, LANES, M_DK), lambda b: (b, 0, 0, 0)),
                     pl.BlockSpec((None, n_units, LANES), lambda b: (b, 0, 0))]
        args += [caug0, m0]
    out_specs = [pl.BlockSpec((t, M_WIDTH), lambda b: (b, 0)), pl.BlockSpec((t, F_WIDTH), lambda b: (b, 0))]
    out_shape = [jax.ShapeDtypeStruct((batch * t, M_WIDTH), BF16), jax.ShapeDtypeStruct((batch * t, F_WIDTH), BF16)]
    if emit_state:
        if carried is None:
            out_specs += [pl.BlockSpec((None, DEPTH, n_units, M_DK, M_DV), lambda b: (b, 0, 0, 0, 0)),
                          pl.BlockSpec((None, DEPTH, n_units, M_DK), lambda b: (b, 0, 0, 0)),
                          pl.BlockSpec((None, DEPTH, n_units, LANES), lambda b: (b, 0, 0, 0))]
        else:
            out_specs += [pl.BlockSpec((None, None, n_units, M_DK, M_DV), lambda b: (b, layer, 0, 0, 0)),
                          pl.BlockSpec((None, None, n_units, M_DK), lambda b: (b, layer, 0, 0)),
                          pl.BlockSpec((None, None, n_units, LANES), lambda b: (b, layer, 0, 0))]
        out_shape += [jax.ShapeDtypeStruct((batch, DEPTH, n_units, M_DK, M_DV), F32),
                      jax.ShapeDtypeStruct((batch, DEPTH, n_units, M_DK), F32),
                      jax.ShapeDtypeStruct((batch, DEPTH, n_units, LANES), F32)]
    aliases = {}
    for w, rc, first, n_chunks, _ in convert:
        assert n_chunks <= batch
        w_cols = w.shape[2]
        in_specs.append(pl.BlockSpec(
            (None, rc, w_cols), lambda b, first=first, n=n_chunks: (layer, first + jnp.minimum(b, n - 1), 0)))
        args.append(w)
        out_specs.append(pl.BlockSpec((rc, w_cols), lambda b, n=n_chunks: (jnp.minimum(b, n - 1), 0)))
        out_shape.append(jax.ShapeDtypeStruct((rc * n_chunks, w_cols), BF16))
    if carried is not None:
        aliases = {len(args) + j: 2 + j for j in range(3)}
        in_specs += [pl.BlockSpec(memory_space=pl.ANY)] * 3
        args += list(carried)
    return pl.pallas_call(
        functools.partial(_mlstm_kernel, t=t, chunk=chunk, has_state=has_state, emit_state=emit_state,
                          convert=tuple((n, last) for _, _, _, n, last in convert),
                          n_carried=0 if carried is None else 3,
                          fill_layer=layer if emit_state and carried is None else None),
        grid=(batch,),
        in_specs=in_specs,
        out_specs=out_specs,
        out_shape=out_shape,
        scratch_shapes=[pltpu.VMEM((M_WIDTH, t), F32)],
        input_output_aliases=aliases,
        compiler_params=_cparams(1),
        name="mlstm_fourier_latent" if has_state else "mlstm_fourier_context",
    )(*args)


def _dft_tables(t):
    kt = (np.arange(t)[:, None] * np.arange(t)[None, :]) % t
    ang_t = 2.0 * np.pi * kt.astype(np.float64) / t
    ct = np.cos(ang_t) / np.sqrt(t)
    st = np.sin(ang_t) / np.sqrt(t)
    c = F_GROUP_CH
    kc = (np.arange(c)[:, None] * np.arange(c)[None, :]) % c
    ang_c = 2.0 * np.pi * kc.astype(np.float64) / c
    eye = np.eye(F_GROUPS)
    cc = np.kron(eye, np.cos(ang_c) / np.sqrt(c))
    sc = np.kron(eye, np.sin(ang_c) / np.sqrt(c))
    cs = np.concatenate([cc, sc], axis=1)
    to_dev = lambda a: jnp.asarray(a.astype(np.float32)).astype(BF16)
    return to_dev(cs), to_dev(ct), to_dev(st)


def _merge_kernel(*refs, rows, split_x):
    refs = list(refs)
    x_refs = [refs.pop(0) for _ in range(2 if split_x else 1)]
    (mod_ref, g_ref, ac_ref, al_ref, hc_ref, hl_ref, fc_ref, fl_ref, wbg_ref,
     wpa_ref, wpm_ref, wpf_ref, wo_ref, o_ref) = refs
    x = _pick(rows, *x_refs) if split_x else x_refs[0][...]
    h = _normmod(x, g_ref[...], mod_ref[1:2, :], mod_ref[0:1, :]).astype(BF16)

    def branch(j, yc_ref, yl_ref, w_ref):
        gate = _sigmoid(_mm_nt(h, wbg_ref[BG_SKIP + D_MODEL * j:BG_SKIP + D_MODEL * (j + 1), :]))
        return gate * _mm(_pick(rows, yc_ref, yl_ref), w_ref[...])

    merged = (branch(0, ac_ref, al_ref, wpa_ref) + branch(1, hc_ref, hl_ref, wpm_ref)
              + branch(2, fc_ref, fl_ref, wpf_ref))
    o_ref[...] = x + mod_ref[2:3, :] * _mm(merged.astype(BF16), wo_ref[...])


def _merge(xs, mod, g, attn, hm, fo, weights, rows, layer):
    split_x = len(xs) == 2
    n_rows = sum(x.shape[0] for x in xs)
    x_specs = [rows.spec(D_MODEL, "ctx"), rows.spec(D_MODEL, "lat")] if split_x else [rows.spec(D_MODEL, "both")]
    pair = lambda width: [rows.spec(width, "ctx"), rows.spec(width, "lat")]
    resident = lambda w: pl.BlockSpec(w.shape, lambda i: (0, 0), pipeline_mode=pl.Buffered(1))
    return pl.pallas_call(
        functools.partial(_merge_kernel, rows=rows, split_x=split_x),
        grid=(n_rows // rows.tm,),
        in_specs=x_specs + [rows.mod_spec(layer), _layer_spec((1, D_MODEL), layer)]
        + pair(ATTN_WIDTH) + pair(M_WIDTH) + pair(F_WIDTH) + [resident(w) for w in weights],
        out_specs=rows.spec(D_MODEL, "both"),
        out_shape=jax.ShapeDtypeStruct((n_rows, D_MODEL), F32),
        compiler_params=_cparams(1),
        name="merge",
    )(*xs, mod, g, *attn, *hm, *fo, *weights)


def _ffn_kernel(*refs, rows, split_out):
    x_ref, mod_ref, g_ref, win_ref, wout_ref = refs[:5]
    out_refs = refs[5:]
    x = x_ref[...]
    h = _normmod(x, g_ref[...], mod_ref[4:5, :], mod_ref[3:4, :]).astype(BF16)
    u = _mm(h, win_ref[...])
    a = (_silu(u[:, :FF_HIDDEN]) * u[:, FF_HIDDEN:]).astype(BF16)
    y = x + mod_ref[5:6, :] * _mm(a, wout_ref[...])
    if split_out:
        @pl.when(rows.is_ctx())
        def _():
            out_refs[0][...] = y

        @pl.when(jnp.logical_not(rows.is_ctx()))
        def _():
            out_refs[1][...] = y
    else:
        out_refs[0][...] = y


def _ffn(x, mod, g, w_in, w_out, rows, layer, split_out):
    n_rows = x.shape[0]
    n_ctx_rows = rows.n_ctx * rows.tm
    resident = lambda w: pl.BlockSpec(w.shape, lambda i: (0, 0), pipeline_mode=pl.Buffered(1))
    if split_out:
        out_specs = [rows.spec(D_MODEL, "ctx"), rows.spec(D_MODEL, "lat")]
        out_shape = [jax.ShapeDtypeStruct((n_ctx_rows, D_MODEL), F32),
                     jax.ShapeDtypeStruct((n_rows - n_ctx_rows, D_MODEL), F32)]
    else:
        out_specs = [rows.spec(D_MODEL, "both")]
        out_shape = [jax.ShapeDtypeStruct((n_rows, D_MODEL), F32)]
    return pl.pallas_call(
        functools.partial(_ffn_kernel, rows=rows, split_out=split_out),
        grid=(n_rows // rows.tm,),
        in_specs=[rows.spec(D_MODEL, "both"), rows.mod_spec(layer), _layer_spec((1, D_MODEL), layer),
                  resident(w_in), resident(w_out)],
        out_specs=out_specs,
        out_shape=out_shape,
        compiler_params=_cparams(1),
        name="ffn",
    )(x, mod, g, w_in, w_out)


def _rope_tables(t):
    n = HEAD_DIM // 4
    inv = 1.0 / (ROPE_THETA ** (np.arange(n, dtype=np.float64) / n))
    pos = np.arange(t)
    ang_r = (pos // GRID_W)[:, None] * inv[None, :]
    ang_c = (pos % GRID_W)[:, None] * inv[None, :]
    cos = np.concatenate([np.cos(ang_r)] * 2 + [np.cos(ang_c)] * 2, axis=1)
    sin = np.concatenate([-np.sin(ang_r), np.sin(ang_r), -np.sin(ang_c), np.sin(ang_c)], axis=1)
    tile = lambda a: jnp.asarray(np.tile(a, (1, N_Q_HEADS)).astype(np.float32))
    return tile(cos), tile(sin)


def kernel(x_prompt, x_sample, cache_k, cache_v, state_C, state_n, state_m, c, c_ctx, w_ada, b_ada,
           norm1_g, w_in, q_norm_g, k_norm_g, m_conv_w, m_gate_b, m_norm_g, w_proj_attn,
           w_proj_mlstm, w_proj_fourier, w_out, norm2_g, w_ffn_in, w_ffn_out):
    n_ctx, t_ctx, _ = x_prompt.shape
    n_lat, t_lat, _ = x_sample.shape
    t_past = cache_k.shape[2]
    n_units = 2 * M_HEADS

    cvec = jnp.concatenate([c_ctx[None, :], c], axis=0)
    cvec = jnp.pad(cvec, ((0, MOD_ROWS - cvec.shape[0]), (0, 0)))
    mod = _modulation(cvec, w_ada, b_ada).reshape(DEPTH, MOD_ROWS, N_MOD, D_MODEL)

    rope_tabs = _rope_tables(t_lat)
    to_rows = lambda a: jnp.transpose(a, (0, 1, 3, 4, 2)).reshape(n_lat, DEPTH, KV_WIDTH, t_past)
    ck, cv = to_rows(cache_k), to_rows(cache_v)
    rows_ctx = n_ctx * t_ctx
    rows_in = _Rows(rows_ctx, t_lat, n_stage=MIX_CHUNKS, tm=TM_IN)
    rows_merge = _Rows(rows_ctx, t_lat)
    rows_ffn = _Rows(rows_ctx, t_lat)

    w_in_t = jnp.swapaxes(w_in, 1, 2)
    row = lambda a: a[:, None, :]
    norm1, norm2 = row(norm1_g), row(norm2_g)
    qg, kg = row(jnp.tile(q_norm_g, (1, N_Q_HEADS))), row(jnp.tile(k_norm_g, (1, N_KV_HEADS)))
    gate_b = row(jnp.pad(m_gate_b, ((0, 0), (0, GATE_PAD - N_GATES))))
    m_norm = row(m_norm_g)

    per_step = lambda w, n: (w, w.shape[1] // n, 0, n, None)
    xs = (x_prompt.reshape(rows_ctx, D_MODEL), x_sample.reshape(n_lat * t_lat, D_MODEL))
    kv_carried = states_carried = None
    for l in range(DEPTH):
        za, zm, fu, gt = _in_proj(xs, mod, norm1, w_in_t, rows_in, l)
        attn_c, *kv_carried = _attention(za, qg, kg, batch=n_ctx, t_new=t_ctx, latent=False, row0=0, layer=l,
                                         carried=kv_carried)
        hm_c, fo_c, *rest = _mlstm_fourier(
            zm, gt, fu, m_conv_w, gate_b, m_norm, batch=n_ctx, t=t_ctx, layer=l, row0=0, carried=states_carried,
            convert=((w_in_t, W_IN_ROWS, BG_FIRST, BG_CHUNKS, BG_LAST),)
            + tuple(per_step(w, n_ctx) for w in (w_proj_attn, w_proj_mlstm, w_proj_fourier, w_out)))
        states_carried, merge_w = rest[:3], rest[3:]

        c0t = jnp.swapaxes(state_C[:, l].astype(F32).reshape(n_lat, n_units, M_DK, M_DV), -1, -2)
        n0 = state_n[:, l].astype(F32).reshape(n_lat, n_units, 1, M_DK)
        caug0 = jnp.concatenate(
            [c0t, jnp.broadcast_to(n0, (n_lat, n_units, LANES - M_DV, M_DK))], axis=-2)
        m0 = jnp.broadcast_to(state_m[:, l].astype(F32).reshape(n_lat, n_units, 1),
                              (n_lat, n_units, LANES))
        (attn_l,) = _attention(za, qg, kg, batch=n_lat, t_new=t_lat, latent=True, row0=rows_ctx,
                               cache_k=ck, cache_v=cv, layer=l, rope_tabs=rope_tabs)
        hm_l, fo_l, w_ffn_in_b, w_ffn_out_b = _mlstm_fourier(
            zm, gt, fu, m_conv_w, gate_b, m_norm, batch=n_lat, t=t_lat, layer=l, row0=rows_ctx,
            caug0=caug0, m0=m0, convert=(per_step(w_ffn_in, n_lat), per_step(w_ffn_out, n_lat)))

        x1 = _merge(xs, mod, norm1, (attn_c, attn_l), (hm_c, hm_l), (fo_c, fo_l), merge_w, rows_merge, l)
        xs = tuple(_ffn(x1, mod, norm2, w_ffn_in_b, w_ffn_out_b, rows_ffn, l, split_out=l == DEPTH - 1))
    xp, xs = xs

    kv_out = lambda a: jnp.transpose(a.reshape(n_ctx, DEPTH, N_KV_HEADS, HEAD_DIM, t_ctx), (0, 1, 4, 2, 3))
    cs, ns, ms = states_carried
    return (xp.reshape(n_ctx, t_ctx, D_MODEL), xs.reshape(n_lat, t_lat, D_MODEL),
            kv_out(kv_carried[0]), kv_out(kv_carried[1]),
            cs.reshape(n_ctx, DEPTH, 2, M_HEADS, M_DK, M_DV), ns.reshape(n_ctx, DEPTH, 2, M_HEADS, M_DK),
            ms[..., 0].reshape(n_ctx, DEPTH, 2, M_HEADS))
```

```python
import functools

import numpy as np
import jax
import jax.numpy as jnp
from jax import lax
from jax.experimental import pallas as pl
from jax.experimental.pallas import tpu as pltpu

D_MODEL = 1024
DEPTH = 2
GRID_W = 64
HEAD_DIM = 64
N_Q_HEADS = 8
N_KV_HEADS = 4
ATTN_WIDTH = N_Q_HEADS * HEAD_DIM
KV_WIDTH = N_KV_HEADS * HEAD_DIM
ROPE_THETA = 10000.0
M_HEADS = 4
M_DK = 64
M_DV = 64
M_WIDTH = M_HEADS * M_DV
F_GROUPS = 4
F_GROUP_CH = 64
F_WIDTH = F_GROUPS * F_GROUP_CH
FF_HIDDEN = -(-8 * D_MODEL // (3 * 256)) * 256
EPS = 1e-6
N_GATES = 4 * M_HEADS
IN_SIZES = (ATTN_WIDTH, KV_WIDTH, KV_WIDTH, M_HEADS * M_DK, M_HEADS * M_DK, M_WIDTH, M_WIDTH,
            N_GATES, F_WIDTH, 3 * D_MODEL)
IN_OFFS = tuple(int(v) for v in np.cumsum((0,) + IN_SIZES))

LANES = 128
GATE_PAD = LANES
N_MOD = 6
MOD_ROWS = 16
M_CHUNK_K = 128
NEG_BIG = -1e30
LOG2_E = 1.4426950408889634
VMEM_LIMIT = 56 * 1024 * 1024
TM = 512
TM_IN = 1024
TQ = 1024
TN_MOD = 1024
W_IN_ROWS = 256

F32 = jnp.float32
BF16 = jnp.bfloat16


def _cparams(n_axes):
    return pltpu.CompilerParams(dimension_semantics=("arbitrary",) * n_axes,
                                vmem_limit_bytes=VMEM_LIMIT)


def _layer_spec(shape, layer):
    return pl.BlockSpec((None,) + tuple(shape), lambda *_: (layer,) + (0,) * len(shape),
                        pipeline_mode=pl.Buffered(1))


class _Rows:
    def __init__(self, rows_ctx, t_lat, n_stage=0, tm=TM):
        self.tm = tm
        self.n_ctx = rows_ctx // tm
        self.per_seq = t_lat // tm
        self.n_stage = n_stage

    def tile(self, i):
        return jnp.maximum(i - self.n_stage, 0)

    def both(self, i):
        return (self.tile(i), 0)

    def ctx(self, i):
        return (jnp.minimum(self.tile(i), self.n_ctx - 1), 0)

    def lat(self, i):
        return (jnp.maximum(self.tile(i) - self.n_ctx, 0), 0)

    def mod_spec(self, layer):
        def index(i):
            t = self.tile(i)
            return (layer, jnp.where(t < self.n_ctx, 0, 1 + (t - self.n_ctx) // self.per_seq), 0, 0)
        return pl.BlockSpec((None, None, N_MOD, D_MODEL), index)

    def spec(self, width, which):
        return pl.BlockSpec((self.tm, width), getattr(self, which))

    def is_ctx(self):
        return pl.program_id(0) - self.n_stage < self.n_ctx


def _pick(rows, ctx_ref, lat_ref):
    return jnp.where(rows.is_ctx(), ctx_ref[...], lat_ref[...])


def _staged_spec(shape, layer, n_chunks):
    rc, cols = shape[0] // n_chunks, shape[1]
    return pl.BlockSpec((None, rc, cols), lambda i: (layer, jnp.minimum(i, n_chunks - 1), 0))


def _stage(w_ref, scratch_ref, n_chunks):
    i = pl.program_id(0)
    rc = w_ref.shape[0]

    @pl.when(i < n_chunks)
    def _():
        scratch_ref[pl.ds(pl.multiple_of(i * rc, rc), rc), :] = w_ref[...].astype(BF16)


def _mm(a, b):
    return jnp.dot(a, b, preferred_element_type=F32)


def _mm_nt(a, b):
    return lax.dot_general(a, b, (((1,), (1,)), ((), ())), preferred_element_type=F32)


def _split3(x):
    hi = x.astype(BF16)
    r = x - hi.astype(F32)
    mid = r.astype(BF16)
    lo = (r - mid.astype(F32)).astype(BF16)
    return hi, mid, lo


def _mm_left_f32(x, m_bf16):
    hi, mid, lo = _split3(x)
    return _mm(hi, m_bf16) + _mm(mid, m_bf16) + _mm(lo, m_bf16)


def _sigmoid(x):
    return 1.0 / (1.0 + jnp.exp(-x))


def _silu(x):
    return x * _sigmoid(x)


def _log_sigmoid(x):
    return jnp.minimum(x, 0.0) - jnp.log(1.0 + jnp.exp(-jnp.abs(x)))


def _normmod(x, g, scale, shift):
    ms = jnp.mean(x * x, axis=-1, keepdims=True)
    return (x * lax.rsqrt(ms + EPS)) * g * (1.0 + scale) + shift


def _group_inv_rms(x, bd):
    x2 = x * x
    hi = x2.astype(BF16)
    lo = (x2 - hi.astype(F32)).astype(BF16)
    ss = _mm(hi, bd) + _mm(lo, bd)
    return lax.rsqrt(ss * (1.0 / HEAD_DIM) + EPS)


def _rope(x, cos, sin_signed):
    w = x.shape[1]
    lane = lax.broadcasted_iota(jnp.int32, x.shape, 1)
    up = pltpu.roll(x, w - 16, axis=1)
    dn = pltpu.roll(x, 16, axis=1)
    partner = jnp.where((lane & 31) < 16, up, dn)
    return x * cos + partner * sin_signed


def _mod_kernel(c_ref, w_ref, b_ref, o_ref):
    s = _silu(c_ref[...]).astype(BF16)
    o_ref[...] = _mm(s, w_ref[...].astype(BF16)) + b_ref[...]


def _modulation(cvec, w_ada, b_ada):
    tn = TN_MOD
    n_out = N_MOD * D_MODEL
    return pl.pallas_call(
        _mod_kernel,
        grid=(DEPTH, n_out // tn),
        in_specs=[
            pl.BlockSpec((MOD_ROWS, D_MODEL), lambda l, j: (0, 0)),
            pl.BlockSpec((None, D_MODEL, tn), lambda l, j: (l, 0, j)),
            pl.BlockSpec((None, 1, tn), lambda l, j: (l, 0, j)),
        ],
        out_specs=pl.BlockSpec((None, MOD_ROWS, tn), lambda l, j: (l, 0, j)),
        out_shape=jax.ShapeDtypeStruct((DEPTH, MOD_ROWS, n_out), F32),
        compiler_params=_cparams(2),
        name="modulation",
    )(cvec, w_ada, b_ada.reshape(DEPTH, 1, n_out))


GATE_OFF, FU_OFF, BG_OFF = IN_OFFS[7], IN_OFFS[8], IN_OFFS[9]
IN_COLS = IN_OFFS[-1]
MIX_CHUNKS = -(-BG_OFF // W_IN_ROWS)
BG_FIRST = BG_OFF // W_IN_ROWS
BG_CHUNKS = -(-IN_COLS // W_IN_ROWS) - BG_FIRST
BG_SKIP = BG_OFF - BG_FIRST * W_IN_ROWS
BG_LAST = IN_COLS - (BG_FIRST + BG_CHUNKS - 1) * W_IN_ROWS


def _in_kernel(*refs, rows, split_x):
    refs = list(refs)
    x_refs = [refs.pop(0) for _ in range(2 if split_x else 1)]
    mod_ref, g_ref, w_ref, za_ref, zm_ref, fu_ref, gt_ref, w_s = refs
    _stage(w_ref, w_s, MIX_CHUNKS)

    @pl.when(pl.program_id(0) >= MIX_CHUNKS)
    def _():
        x = _pick(rows, *x_refs) if split_x else x_refs[0][...]
        h = _normmod(x, g_ref[...], mod_ref[1:2, :], mod_ref[0:1, :])
        z = _mm_nt(h.astype(BF16), w_s[:BG_OFF, :])
        za_ref[...] = z[:, :D_MODEL]
        zm_ref[...] = z[:, D_MODEL:GATE_OFF]
        gt_ref[...] = z[:, GATE_OFF:GATE_OFF + GATE_PAD]
        fu_ref[...] = z[:, FU_OFF:FU_OFF + F_WIDTH].astype(BF16)


def _in_proj(xs, mod, g, w_in_t, rows, layer):
    split_x = len(xs) == 2
    n_rows = sum(x.shape[0] for x in xs)
    x_specs = [rows.spec(D_MODEL, "ctx"), rows.spec(D_MODEL, "lat")] if split_x else [rows.spec(D_MODEL, "both")]
    widths = (D_MODEL, D_MODEL, F_WIDTH, GATE_PAD)
    dtypes = (F32, F32, BF16, F32)
    return pl.pallas_call(
        functools.partial(_in_kernel, rows=rows, split_x=split_x),
        grid=(MIX_CHUNKS + n_rows // rows.tm,),
        in_specs=x_specs + [
            rows.mod_spec(layer),
            _layer_spec((1, D_MODEL), layer),
            _staged_spec((MIX_CHUNKS * W_IN_ROWS, D_MODEL), layer, MIX_CHUNKS),
        ],
        out_specs=[rows.spec(w, "both") for w in widths],
        out_shape=[jax.ShapeDtypeStruct((n_rows, w), d) for w, d in zip(widths, dtypes)],
        scratch_shapes=[pltpu.VMEM((MIX_CHUNKS * W_IN_ROWS, D_MODEL), BF16)],
        compiler_params=_cparams(1),
        name="in_proj",
    )(*xs, mod, g, w_in_t)


def _attn_kernel(*refs, tq, t_new, t_past, latent, fill_layer):
    if latent:
        (q_ref, k_ref, v_ref, ck_ref, cv_ref, cosq_ref, sinq_ref, cosk_ref, sink_ref,
         qg_ref, kg_ref, bdq_ref, bdk_ref, o_ref, ks_ref, vs_ref) = refs
    else:
        q_ref, k_ref, v_ref, qg_ref, kg_ref, bdq_ref, bdk_ref = refs[:7]
        o_ref, nk_ref, nv_ref, ks_ref, vs_ref = refs[-5:]

    @pl.when(pl.program_id(1) == 0)
    def _():
        k = k_ref[...]
        kn = k * _group_inv_rms(k, bdk_ref[...]) * kg_ref[...]
        v = v_ref[...]
        if latent:
            kn = _rope(kn, cosk_ref[...], sink_ref[...])
            ck = ck_ref[...].T
            cv = cv_ref[...].T
        else:
            if fill_layer is None:
                nk_ref[...] = kn.T
                nv_ref[...] = v.T
            else:
                for l2 in range(DEPTH):
                    nk_ref[l2] = kn.T if l2 == fill_layer else jnp.zeros((KV_WIDTH, t_new), F32)
                    nv_ref[l2] = v.T if l2 == fill_layer else jnp.zeros((KV_WIDTH, t_new), F32)
        for h in range(N_KV_HEADS):
            sl = slice(HEAD_DIM * h, HEAD_DIM * (h + 1))
            if latent:
                ks_ref[h, :t_past, :] = ck[:, sl].astype(BF16)
                vs_ref[h, :t_past, :HEAD_DIM] = cv[:, sl].astype(BF16)
            ks_ref[h, t_past:, :] = kn[:, sl].astype(BF16)
            vs_ref[h, t_past:, :HEAD_DIM] = v[:, sl].astype(BF16)
            vs_ref[h, :, HEAD_DIM:] = jnp.ones((t_past + t_new, LANES - HEAD_DIM), BF16)

    q = q_ref[...]
    qn = q * _group_inv_rms(q, bdq_ref[...]) * qg_ref[...]
    if latent:
        qn = _rope(qn, cosq_ref[...], sinq_ref[...])
    qn = qn * (HEAD_DIM ** -0.5 * LOG2_E)
    group = N_Q_HEADS // N_KV_HEADS
    outs = []

    def scores(h):
        qh = jnp.concatenate(
            [qn[:, HEAD_DIM * (group * h + g):HEAD_DIM * (group * h + g + 1)] for g in range(group)],
            axis=0).astype(BF16)
        return _mm_nt(qh, ks_ref[h])

    ahead = 1 if latent else N_KV_HEADS - 1
    s_q = [scores(h) for h in range(ahead)]
    for h in range(N_KV_HEADS):
        s = s_q.pop(0)
        if h + ahead < N_KV_HEADS:
            s_q.append(scores(h + ahead))
        m = jnp.max(s, axis=-1, keepdims=True)
        p = jnp.exp2(s - m).astype(BF16)
        od = _mm(p, vs_ref[h])
        o = (od / pltpu.roll(od, LANES - HEAD_DIM, axis=1))[:, :HEAD_DIM]
        outs.extend(o[tq * g:tq * (g + 1)] for g in range(group))
    o_ref[...] = jnp.concatenate(outs, axis=1).astype(BF16)


def _block_diag_ones(width):
    idx = np.arange(width) // HEAD_DIM
    return jnp.asarray(idx[:, None] == idx[None, :], dtype=BF16)


def _attention(za, qg, kg, *, batch, t_new, latent, row0, cache_k=None, cache_v=None, layer=0, rope_tabs=None,
               carried=None):
    tq = min(TQ, t_new)
    nq = t_new // tq
    t_past = cache_k.shape[3] if latent else 0
    t_keys = t_past + t_new
    kcol = ATTN_WIDTH // KV_WIDTH
    bdq = _block_diag_ones(ATTN_WIDTH)
    bdk = _block_diag_ones(KV_WIDTH)
    const = lambda shape: pl.BlockSpec(shape, lambda b, i: (0,) * len(shape))
    q0, s0 = row0 // tq, row0 // t_new
    in_specs = [
        pl.BlockSpec((tq, ATTN_WIDTH), lambda b, i: (q0 + b * nq + i, 0)),
        pl.BlockSpec((t_new, KV_WIDTH), lambda b, i: (s0 + b, kcol)),
        pl.BlockSpec((t_new, KV_WIDTH), lambda b, i: (s0 + b, kcol + 1)),
    ]
    args = [za, za, za]
    if latent:
        cos_t, sin_t = rope_tabs
        in_specs += [
            pl.BlockSpec((None, None, KV_WIDTH, t_past), lambda b, i: (b, layer, 0, 0)),
            pl.BlockSpec((None, None, KV_WIDTH, t_past), lambda b, i: (b, layer, 0, 0)),
            pl.BlockSpec((tq, ATTN_WIDTH), lambda b, i: (i, 0)),
            pl.BlockSpec((tq, ATTN_WIDTH), lambda b, i: (i, 0)),
            pl.BlockSpec((t_new, KV_WIDTH), lambda b, i: (0, 0)),
            pl.BlockSpec((t_new, KV_WIDTH), lambda b, i: (0, 0)),
        ]
        args += [cache_k, cache_v, cos_t, sin_t, cos_t, sin_t]
    in_specs += [_layer_spec((1, ATTN_WIDTH), layer), _layer_spec((1, KV_WIDTH), layer),
                 const((ATTN_WIDTH, ATTN_WIDTH)), const((KV_WIDTH, KV_WIDTH))]
    args += [qg, kg, bdq, bdk]
    out_specs = [pl.BlockSpec((tq, ATTN_WIDTH), lambda b, i: (b * nq + i, 0))]
    out_shape = [jax.ShapeDtypeStruct((batch * t_new, ATTN_WIDTH), BF16)]
    aliases = {}
    if not latent:
        out_shape += [jax.ShapeDtypeStruct((batch, DEPTH, KV_WIDTH, t_new), F32)] * 2
        if carried is None:
            out_specs += [pl.BlockSpec((None, DEPTH, KV_WIDTH, t_new), lambda b, i: (b, 0, 0, 0))] * 2
        else:
            out_specs += [pl.BlockSpec((None, None, KV_WIDTH, t_new), lambda b, i: (b, layer, 0, 0))] * 2
            aliases = {len(args) + j: 1 + j for j in range(2)}
            in_specs += [pl.BlockSpec(memory_space=pl.ANY)] * 2
            args += list(carried)
    return pl.pallas_call(
        functools.partial(_attn_kernel, tq=tq, t_new=t_new, t_past=t_past, latent=latent,
                          fill_layer=layer if not latent and carried is None else None),
        grid=(batch, nq),
        in_specs=in_specs,
        out_specs=out_specs,
        out_shape=out_shape,
        scratch_shapes=[pltpu.VMEM((N_KV_HEADS, t_keys, HEAD_DIM), BF16),
                        pltpu.VMEM((N_KV_HEADS, t_keys, LANES), BF16)],
        input_output_aliases=aliases,
        compiler_params=_cparams(2),
        name="attention_latent" if latent else "attention_context",
    )(*args)


def _running_max_sublanes(x, reverse):
    n = x.shape[0]
    row = lax.broadcasted_iota(jnp.int32, x.shape, 0)
    k = 1
    while k < n:
        if reverse:
            cand = jnp.where(row < n - k, pltpu.roll(x, n - k, axis=0), NEG_BIG)
        else:
            cand = jnp.where(row >= k, pltpu.roll(x, k, axis=0), NEG_BIG)
        x = jnp.maximum(x, cand)
        k *= 2
    return x


def _mlstm_kernel(*refs, t, chunk, has_state, emit_state, convert, n_carried, fill_layer):
    refs = list(refs)
    zm_ref, gt_ref, cw_ref, gb_ref, mg_ref, bd_ref, fu_ref, dft_c_ref, dft_ct_ref, dft_st_ref = refs[:10]
    refs = refs[10:]
    if has_state:
        c0_ref, n0_ref, m0_ref = refs[:3]
        refs = refs[3:]
    convert_in, refs = refs[:len(convert)], refs[len(convert):]
    refs = refs[n_carried:]
    o_ref, fo_ref = refs[:2]
    refs = refs[2:]
    if emit_state:
        cs_ref, ns_ref, ms_ref = refs[:3]
        refs = refs[3:]
    convert_out, refs = refs[:len(convert)], refs[len(convert):]
    hst_ref = refs[0]

    for (n_chunks, last_valid), src_ref, dst_ref in zip(convert, convert_in, convert_out):
        w = src_ref[...]
        if last_valid is not None:
            rc = w.shape[0]
            limit = jnp.where(pl.program_id(0) >= n_chunks - 1, last_valid, rc)
            w = jnp.where(lax.broadcasted_iota(jnp.int32, (rc, 1), 0) < limit, w, 0.0)
        dst_ref[...] = w.astype(BF16)

    width = M_HEADS * M_DK
    z = zm_ref[...]
    x = z[:, :2 * width]
    row = lax.broadcasted_iota(jnp.int32, (t, 1), 0)
    x_prev = jnp.where(row == 0, 0.0, pltpu.roll(x, 1, axis=0))
    x_next = jnp.where(row == t - 1, 0.0, pltpu.roll(x, t - 1, axis=0))
    cw = cw_ref[...]
    qk = _silu(x_prev * cw[0:1, :] + x * cw[1:2, :] + x_next * cw[2:3, :])
    q = qk[:, :width]
    k = qk[:, width:] * (M_DK ** -0.5)
    v = z[:, 2 * width:2 * width + M_WIDTH]
    om = z[:, 2 * width + M_WIDTH:]
    heads = [slice(M_DK * h, M_DK * (h + 1)) for h in range(M_HEADS)]
    q_rows = [q[:, hs].astype(BF16) for hs in heads]
    k_rows = [k[:, hs].astype(BF16) for hs in heads]
    qt = q.T.astype(BF16)
    vt = v.T
    ones_r = jnp.ones((LANES - M_DV, t), F32)
    vaug_t = [jnp.concatenate([vt[hs, :], ones_r], axis=0) for hs in heads]
    vaug_tb = [a.astype(BF16) for a in vaug_t]

    gates = gt_ref[...] + gb_ref[...]
    lane = lax.broadcasted_iota(jnp.int32, (1, GATE_PAD), 1)
    is_forget = (lane & M_HEADS) != 0
    gates = jnp.where(is_forget, _log_sigmoid(gates), gates)
    gates_t = gates.T[:N_GATES, :]

    ri = lax.broadcasted_iota(jnp.int32, (chunk, chunk), 0)
    ci = lax.broadcasted_iota(jnp.int32, (chunk, chunk), 1)
    row_le_col = ri <= ci
    row_ge_col = ri >= ci

    n_chunks = t // chunk
    g_all = jnp.concatenate([gates_t[:, chunk * c:chunk * (c + 1)] for c in range(n_chunks)], axis=0)
    i_all = pltpu.roll(g_all, M_HEADS, axis=0)
    rows = lax.broadcasted_iota(jnp.int32, (N_GATES * n_chunks, 1), 0)
    is_fwd_row = (rows & (2 * M_HEADS)) == 0
    cum_p = _mm_left_f32(g_all, row_le_col.astype(BF16))
    cum_s = _mm_left_f32(g_all, row_ge_col.astype(BF16))
    b_all = jnp.where(is_fwd_row, cum_p, cum_s)
    u_all = i_all - b_all
    u_cols = u_all.T
    cols = lax.broadcasted_iota(jnp.int32, (1, N_GATES * n_chunks), 1)
    pm_cols = jnp.where((cols & (2 * M_HEADS)) == 0, _running_max_sublanes(u_cols, reverse=False),
                        _running_max_sublanes(u_cols, reverse=True))
    pm_all = pm_cols.T
    u_cols2 = u_cols * LOG2_E
    pm_all2 = pm_all * LOG2_E
    pml_all = jnp.where(is_fwd_row, pm_all[:, chunk - 1:], pm_all[:, :1])
    bl_all = jnp.where(is_fwd_row, b_all[:, chunk - 1:], b_all[:, :1])
    wk_all = jnp.exp(u_all - pml_all)

    if has_state:
        m_state = [m0_ref[M_HEADS * d:M_HEADS * (d + 1), :] for d in range(2)]
        c_state = [[jnp.concatenate(
            [c0_ref[u].T, jnp.broadcast_to(n0_ref[u:u + 1, :], (LANES - M_DV, M_DK))], axis=0)
            for u in range(M_HEADS * d, M_HEADS * (d + 1))] for d in range(2)]
    else:
        m_state = [jnp.zeros((M_HEADS, 1), F32) for _ in range(2)]
        c_state = [[jnp.zeros((LANES, M_DK), F32) for _ in range(M_HEADS)] for _ in range(2)]

    steps = [(j if d == 0 else n_chunks - 1 - j, d) for j in range(n_chunks) for d in range(2)]

    def gate_rows(step):
        c, direction = step
        return N_GATES * c + (2 * direction + 1) * M_HEADS

    def key_query(step):
        rs = slice(chunk * step[0], chunk * (step[0] + 1))
        return [_mm_nt(k_rows[h][rs, :], q_rows[h][rs, :]) for h in range(M_HEADS)]

    def local_sums(step, st):
        c, direction = step
        rs = slice(chunk * c, chunk * (c + 1))
        r0 = gate_rows(step)
        valid = row_le_col if direction == 0 else row_ge_col
        s_loc, x_loc = [], []
        for h in range(M_HEADS):
            w = jnp.exp2(jnp.where(valid, u_cols2[:, r0 + h:r0 + h + 1] - pm_all2[r0 + h:r0 + h + 1, :], NEG_BIG))
            s_loc.append(_mm(vaug_tb[h][:, rs], (st[h] * w).astype(BF16)))
            x_loc.append(_mm((vaug_t[h][:, rs] * wk_all[r0 + h:r0 + h + 1, :]).astype(BF16), k_rows[h][rs, :]))
        return s_loc, x_loc

    written = set()
    n_steps = len(steps)
    st_q = [key_query(steps[i]) for i in range(min(2, n_steps))]
    loc_q = [local_sums(steps[0], st_q.pop(0))]
    for i, (c, direction) in enumerate(steps):
        if i + 1 < n_steps:
            loc_q.append(local_sums(steps[i + 1], st_q.pop(0)))
        if i + 2 < n_steps:
            st_q.append(key_query(steps[i + 2]))
        s_loc, x_loc = loc_q.pop(0)
        rs = slice(chunk * c, chunk * (c + 1))
        r0 = gate_rows((c, direction))
        b4, pm4, bl, pml = (a[r0:r0 + M_HEADS, :] for a in (b_all, pm_all, bl_all, pml_all))
        m = m_state[direction]
        mx = jnp.maximum(m, pm4)
        f_loc = jnp.exp(pm4 - mx)
        f_int = jnp.exp(m - mx)
        floor = jnp.exp(-(b4 + mx))
        m_new = bl + jnp.maximum(m, pml)
        decay = jnp.exp(bl + m - m_new)
        gain = jnp.exp(bl + pml - m_new)
        m_state[direction] = m_new
        inter = [_mm(c_state[direction][h].astype(BF16), qt[heads[h], rs]) for h in range(M_HEADS)]
        for h in range(M_HEADS):
            nd = f_loc[h:h + 1, :] * s_loc[h] + f_int[h:h + 1, :] * inter[h]
            ht = nd[:M_DV, :] / jnp.maximum(jnp.abs(nd[M_DV:, :]), floor[h:h + 1, :])
            c_state[direction][h] = decay[h:h + 1, :] * c_state[direction][h] + gain[h:h + 1, :] * x_loc[h]
            if c in written:
                hst_ref[heads[h], rs] += ht
            else:
                hst_ref[heads[h], rs] = ht
        written.add(c)

    if emit_state:
        if fill_layer is None:
            cs_out, ns_out, ms_out = cs_ref, ns_ref, ms_ref
        else:
            for l2 in range(DEPTH):
                if l2 != fill_layer:
                    cs_ref[l2] = jnp.zeros(cs_ref.shape[1:], F32)
                    ns_ref[l2] = jnp.zeros(ns_ref.shape[1:], F32)
                    ms_ref[l2] = jnp.zeros(ms_ref.shape[1:], F32)
            cs_out, ns_out, ms_out = cs_ref.at[fill_layer], ns_ref.at[fill_layer], ms_ref.at[fill_layer]
        for d in range(2):
            ms_out[M_HEADS * d:M_HEADS * (d + 1), :] = jnp.broadcast_to(m_state[d], (M_HEADS, LANES))
            for h in range(M_HEADS):
                u = M_HEADS * d + h
                cs_out[u] = c_state[d][h][:M_DV, :].T
                ns_out[u:u + 1, :] = c_state[d][h][M_DV:M_DV + 1, :]

    hsum = hst_ref[...].T
    hn = hsum * _group_inv_rms(hsum, bd_ref[...]) * mg_ref[...]
    o_ref[...] = (_sigmoid(om) * hn).astype(BF16)

    fa = _mm(fu_ref[...].astype(BF16), dft_c_ref[...])
    fo_ref[...] = (_mm(dft_ct_ref[...], fa[:, :F_WIDTH].astype(BF16))
                   - _mm(dft_st_ref[...], fa[:, F_WIDTH:].astype(BF16))).astype(BF16)


def _mlstm_fourier(zm, gt, fu, conv_w, gate_b, m_norm_g, *, batch, t, layer, row0, state0=None,
                   convert=(), carried=None):
    has_state = state0 is not None
    emit_state = not has_state
    chunk = min(M_CHUNK_K, t)
    n_units = 2 * M_HEADS
    const = lambda shape: pl.BlockSpec(shape, lambda b: (0,) * len(shape), pipeline_mode=pl.Buffered(1))
    s0 = row0 // t
    in_specs = [
        pl.BlockSpec((t, D_MODEL), lambda b: (s0 + b, 0)),
        pl.BlockSpec((t, GATE_PAD), lambda b: (s0 + b, 0)),
        _layer_spec((3, 2 * M_HEADS * M_DK), layer),
        _layer_spec((1, GATE_PAD), layer),
        _layer_spec((1, M_WIDTH), layer),
        const((M_WIDTH, M_WIDTH)),
        pl.BlockSpec((t, F_WIDTH), lambda b: (s0 + b, 0)),
        const((F_WIDTH, 2 * F_WIDTH)), const((t, t)), const((t, t)),
    ]
    args = [zm, gt, conv_w, gate_b, m_norm_g, _block_diag_ones(M_WIDTH), fu, *_dft_tables(t)]
    if has_state:
        in_specs += [pl.BlockSpec((None, None, n_units, M_DK, M_DV), lambda b: (b, layer, 0, 0, 0)),
                     pl.BlockSpec((None, None, n_units, M_DK), lambda b: (b, layer, 0, 0)),
                     pl.BlockSpec((None, None, n_units, 1), lambda b: (b, layer, 0, 0))]
        args += list(state0)
    out_specs = [pl.BlockSpec((t, M_WIDTH), lambda b: (b, 0)), pl.BlockSpec((t, F_WIDTH), lambda b: (b, 0))]
    out_shape = [jax.ShapeDtypeStruct((batch * t, M_WIDTH), BF16), jax.ShapeDtypeStruct((batch * t, F_WIDTH), BF16)]
    if emit_state:
        if carried is None:
            out_specs += [pl.BlockSpec((None, DEPTH, n_units, M_DK, M_DV), lambda b: (b, 0, 0, 0, 0)),
                          pl.BlockSpec((None, DEPTH, n_units, M_DK), lambda b: (b, 0, 0, 0)),
                          pl.BlockSpec((None, DEPTH, n_units, LANES), lambda b: (b, 0, 0, 0))]
        else:
            out_specs += [pl.BlockSpec((None, None, n_units, M_DK, M_DV), lambda b: (b, layer, 0, 0, 0)),
                          pl.BlockSpec((None, None, n_units, M_DK), lambda b: (b, layer, 0, 0)),
                          pl.BlockSpec((None, None, n_units, LANES), lambda b: (b, layer, 0, 0))]
        out_shape += [jax.ShapeDtypeStruct((batch, DEPTH, n_units, M_DK, M_DV), F32),
                      jax.ShapeDtypeStruct((batch, DEPTH, n_units, M_DK), F32),
                      jax.ShapeDtypeStruct((batch, DEPTH, n_units, LANES), F32)]
    aliases = {}
    for w, rc, first, n_chunks, _ in convert:
        assert n_chunks <= batch
        w_cols = w.shape[2]
        in_specs.append(pl.BlockSpec(
            (None, rc, w_cols), lambda b, first=first, n=n_chunks: (layer, first + jnp.minimum(b, n - 1), 0)))
        args.append(w)
        out_specs.append(pl.BlockSpec((rc, w_cols), lambda b, n=n_chunks: (jnp.minimum(b, n - 1), 0)))
        out_shape.append(jax.ShapeDtypeStruct((rc * n_chunks, w_cols), BF16))
    if carried is not None:
        aliases = {len(args) + j: 2 + j for j in range(3)}
        in_specs += [pl.BlockSpec(memory_space=pl.ANY)] * 3
        args += list(carried)
    return pl.pallas_call(
        functools.partial(_mlstm_kernel, t=t, chunk=chunk, has_state=has_state, emit_state=emit_state,
                          convert=tuple((n, last) for _, _, _, n, last in convert),
                          n_carried=0 if carried is None else 3,
                          fill_layer=layer if emit_state and carried is None else None),
        grid=(batch,),
        in_specs=in_specs,
        out_specs=out_specs,
        out_shape=out_shape,
        scratch_shapes=[pltpu.VMEM((M_WIDTH, t), F32)],
        input_output_aliases=aliases,
        compiler_params=_cparams(1),
        name="mlstm_fourier_latent" if has_state else "mlstm_fourier_context",
    )(*args)


def _dft_tables(t):
    kt = (np.arange(t)[:, None] * np.arange(t)[None, :]) % t
    ang_t = 2.0 * np.pi * kt.astype(np.float64) / t
    ct = np.cos(ang_t) / np.sqrt(t)
    st = np.sin(ang_t) / np.sqrt(t)
    c = F_GROUP_CH
    kc = (np.arange(c)[:, None] * np.arange(c)[None, :]) % c
    ang_c = 2.0 * np.pi * kc.astype(np.float64) / c
    eye = np.eye(F_GROUPS)
    cc = np.kron(eye, np.cos(ang_c) / np.sqrt(c))
    sc = np.kron(eye, np.sin(ang_c) / np.sqrt(c))
    cs = np.concatenate([cc, sc], axis=1)
    to_dev = lambda a: jnp.asarray(a.astype(np.float32)).astype(BF16)
    return to_dev(cs), to_dev(ct), to_dev(st)


def _merge_kernel(*refs, rows, split_x):
    refs = list(refs)
    x_refs = [refs.pop(0) for _ in range(2 if split_x else 1)]
    (mod_ref, g_ref, ac_ref, al_ref, hc_ref, hl_ref, fc_ref, fl_ref, wbg_ref,
     wpa_ref, wpm_ref, wpf_ref, wo_ref, o_ref) = refs
    x = _pick(rows, *x_refs) if split_x else x_refs[0][...]
    h = _normmod(x, g_ref[...], mod_ref[1:2, :], mod_ref[0:1, :]).astype(BF16)

    def branch(j, yc_ref, yl_ref, w_ref):
        gate = _sigmoid(_mm_nt(h, wbg_ref[BG_SKIP + D_MODEL * j:BG_SKIP + D_MODEL * (j + 1), :]))
        return gate * _mm(_pick(rows, yc_ref, yl_ref), w_ref[...])

    merged = (branch(0, ac_ref, al_ref, wpa_ref) + branch(1, hc_ref, hl_ref, wpm_ref)
              + branch(2, fc_ref, fl_ref, wpf_ref))
    o_ref[...] = x + mod_ref[2:3, :] * _mm(merged.astype(BF16), wo_ref[...])


def _merge(xs, mod, g, attn, hm, fo, weights, rows, layer):
    split_x = len(xs) == 2
    n_rows = sum(x.shape[0] for x in xs)
    x_specs = [rows.spec(D_MODEL, "ctx"), rows.spec(D_MODEL, "lat")] if split_x else [rows.spec(D_MODEL, "both")]
    pair = lambda width: [rows.spec(width, "ctx"), rows.spec(width, "lat")]
    resident = lambda w: pl.BlockSpec(w.shape, lambda i: (0, 0), pipeline_mode=pl.Buffered(1))
    return pl.pallas_call(
        functools.partial(_merge_kernel, rows=rows, split_x=split_x),
        grid=(n_rows // rows.tm,),
        in_specs=x_specs + [rows.mod_spec(layer), _layer_spec((1, D_MODEL), layer)]
        + pair(ATTN_WIDTH) + pair(M_WIDTH) + pair(F_WIDTH) + [resident(w) for w in weights],
        out_specs=rows.spec(D_MODEL, "both"),
        out_shape=jax.ShapeDtypeStruct((n_rows, D_MODEL), F32),
        compiler_params=_cparams(1),
        name="merge",
    )(*xs, mod, g, *attn, *hm, *fo, *weights)


def _ffn_kernel(*refs, rows, split_out):
    x_ref, mod_ref, g_ref, win_ref, wout_ref = refs[:5]
    out_refs = refs[5:]
    x = x_ref[...]
    h = _normmod(x, g_ref[...], mod_ref[4:5, :], mod_ref[3:4, :]).astype(BF16)
    u = _mm(h, win_ref[...])
    a = (_silu(u[:, :FF_HIDDEN]) * u[:, FF_HIDDEN:]).astype(BF16)
    y = x + mod_ref[5:6, :] * _mm(a, wout_ref[...])
    if split_out:
        @pl.when(rows.is_ctx())
        def _():
            out_refs[0][...] = y

        @pl.when(jnp.logical_not(rows.is_ctx()))
        def _():
            out_refs[1][...] = y
    else:
        out_refs[0][...] = y


def _ffn(x, mod, g, w_in, w_out, rows, layer, split_out):
    n_rows = x.shape[0]
    n_ctx_rows = rows.n_ctx * rows.tm
    resident = lambda w: pl.BlockSpec(w.shape, lambda i: (0, 0), pipeline_mode=pl.Buffered(1))
    if split_out:
        out_specs = [rows.spec(D_MODEL, "ctx"), rows.spec(D_MODEL, "lat")]
        out_shape = [jax.ShapeDtypeStruct((n_ctx_rows, D_MODEL), F32),
                     jax.ShapeDtypeStruct((n_rows - n_ctx_rows, D_MODEL), F32)]
    else:
        out_specs = [rows.spec(D_MODEL, "both")]
        out_shape = [jax.ShapeDtypeStruct((n_rows, D_MODEL), F32)]
    return pl.pallas_call(
        functools.partial(_ffn_kernel, rows=rows, split_out=split_out),
        grid=(n_rows // rows.tm,),
        in_specs=[rows.spec(D_MODEL, "both"), rows.mod_spec(layer), _layer_spec((1, D_MODEL), layer),
                  resident(w_in), resident(w_out)],
        out_specs=out_specs,
        out_shape=out_shape,
        compiler_params=_cparams(1),
        name="ffn",
    )(x, mod, g, w_in, w_out)


def _rope_tables(t):
    n = HEAD_DIM // 4
    inv = 1.0 / (ROPE_THETA ** (np.arange(n, dtype=np.float64) / n))
    pos = np.arange(t)
    ang_r = (pos // GRID_W)[:, None] * inv[None, :]
    ang_c = (pos % GRID_W)[:, None] * inv[None, :]
    cos = np.concatenate([np.cos(ang_r)] * 2 + [np.cos(ang_c)] * 2, axis=1)
    sin = np.concatenate([-np.sin(ang_r), np.sin(ang_r), -np.sin(ang_c), np.sin(ang_c)], axis=1)
    tile = lambda a: jnp.asarray(np.tile(a, (1, N_Q_HEADS)).astype(np.float32))
    return tile(cos), tile(sin)


def kernel(x_prompt, x_sample, cache_k, cache_v, state_C, state_n, state_m, c, c_ctx, w_ada, b_ada,
           norm1_g, w_in, q_norm_g, k_norm_g, m_conv_w, m_gate_b, m_norm_g, w_proj_attn,
           w_proj_mlstm, w_proj_fourier, w_out, norm2_g, w_ffn_in, w_ffn_out):
    n_ctx, t_ctx, _ = x_prompt.shape
    n_lat, t_lat, _ = x_sample.shape
    t_past = cache_k.shape[2]
    n_units = 2 * M_HEADS

    cvec = jnp.concatenate([c_ctx[None, :], c], axis=0)
    cvec = jnp.pad(cvec, ((0, MOD_ROWS - cvec.shape[0]), (0, 0)))
    mod = _modulation(cvec, w_ada, b_ada).reshape(DEPTH, MOD_ROWS, N_MOD, D_MODEL)

    rope_tabs = _rope_tables(t_lat)
    to_rows = lambda a: jnp.transpose(a, (0, 1, 3, 4, 2)).reshape(n_lat, DEPTH, KV_WIDTH, t_past)
    ck, cv = to_rows(cache_k), to_rows(cache_v)
    rows_ctx = n_ctx * t_ctx
    rows_in = _Rows(rows_ctx, t_lat, n_stage=MIX_CHUNKS, tm=TM_IN)
    rows_merge = _Rows(rows_ctx, t_lat)
    rows_ffn = _Rows(rows_ctx, t_lat)

    w_in_t = jnp.swapaxes(w_in, 1, 2)
    row = lambda a: a[:, None, :]
    norm1, norm2 = row(norm1_g), row(norm2_g)
    qg, kg = row(jnp.tile(q_norm_g, (1, N_Q_HEADS))), row(jnp.tile(k_norm_g, (1, N_KV_HEADS)))
    gate_b = row(jnp.pad(m_gate_b, ((0, 0), (0, GATE_PAD - N_GATES))))
    m_norm = row(m_norm_g)

    per_step = lambda w, n: (w, w.shape[1] // n, 0, n, None)
    xs = (x_prompt.reshape(rows_ctx, D_MODEL), x_sample.reshape(n_lat * t_lat, D_MODEL))
    kv_carried = states_carried = None
    state0 = (state_C.astype(F32).reshape(n_lat, DEPTH, n_units, M_DK, M_DV),
              state_n.astype(F32).reshape(n_lat, DEPTH, n_units, M_DK),
              state_m.astype(F32).reshape(n_lat, DEPTH, n_units, 1))
    for l in range(DEPTH):
        za, zm, fu, gt = _in_proj(xs, mod, norm1, w_in_t, rows_in, l)
        attn_c, *kv_carried = _attention(za, qg, kg, batch=n_ctx, t_new=t_ctx, latent=False, row0=0, layer=l,
                                         carried=kv_carried)
        hm_c, fo_c, *rest = _mlstm_fourier(
            zm, gt, fu, m_conv_w, gate_b, m_norm, batch=n_ctx, t=t_ctx, layer=l, row0=0, carried=states_carried,
            convert=((w_in_t, W_IN_ROWS, BG_FIRST, BG_CHUNKS, BG_LAST),)
            + tuple(per_step(w, n_ctx) for w in (w_proj_attn, w_proj_mlstm, w_proj_fourier, w_out)))
        states_carried, merge_w = rest[:3], rest[3:]

        (attn_l,) = _attention(za, qg, kg, batch=n_lat, t_new=t_lat, latent=True, row0=rows_ctx,
                               cache_k=ck, cache_v=cv, layer=l, rope_tabs=rope_tabs)
        hm_l, fo_l, w_ffn_in_b, w_ffn_out_b = _mlstm_fourier(
            zm, gt, fu, m_conv_w, gate_b, m_norm, batch=n_lat, t=t_lat, layer=l, row0=rows_ctx,
            state0=state0, convert=(per_step(w_ffn_in, n_lat), per_step(w_ffn_out, n_lat)))

        x1 = _merge(xs, mod, norm1, (attn_c, attn_l), (hm_c, hm_l), (fo_c, fo_l), merge_w, rows_merge, l)
        xs = tuple(_ffn(x1, mod, norm2, w_ffn_in_b, w_ffn_out_b, rows_ffn, l, split_out=l == DEPTH - 1))
    xp, xs = xs

    kv_out = lambda a: jnp.transpose(a.reshape(n_ctx, DEPTH, N_KV_HEADS, HEAD_DIM, t_ctx), (0, 1, 4, 2, 3))
    cs, ns, ms = states_carried
    return (xp.reshape(n_ctx, t_ctx, D_MODEL), xs.reshape(n_lat, t_lat, D_MODEL),
            kv_out(kv_carried[0]), kv_out(kv_carried[1]),
            cs.reshape(n_ctx, DEPTH, 2, M_HEADS, M_DK, M_DV), ns.reshape(n_ctx, DEPTH, 2, M_HEADS, M_DK),
            ms[..., 0].reshape(n_ctx, DEPTH, 2, M_HEADS))
```

```python
import functools

import numpy as np
import jax
import jax.numpy as jnp
from jax import lax
from jax.experimental import pallas as pl
from jax.experimental.pallas import tpu as pltpu

D_MODEL = 1024
DEPTH = 2
GRID_W = 64
HEAD_DIM = 64
N_Q_HEADS = 8
N_KV_HEADS = 4
ATTN_WIDTH = N_Q_HEADS * HEAD_DIM
KV_WIDTH = N_KV_HEADS * HEAD_DIM
ROPE_THETA = 10000.0
M_HEADS = 4
M_DK = 64
M_DV = 64
M_WIDTH = M_HEADS * M_DV
F_GROUPS = 4
F_GROUP_CH = 64
F_WIDTH = F_GROUPS * F_GROUP_CH
FF_HIDDEN = -(-8 * D_MODEL // (3 * 256)) * 256
EPS = 1e-6
N_GATES = 4 * M_HEADS
IN_SIZES = (ATTN_WIDTH, KV_WIDTH, KV_WIDTH, M_HEADS * M_DK, M_HEADS * M_DK, M_WIDTH, M_WIDTH,
            N_GATES, F_WIDTH, 3 * D_MODEL)
IN_OFFS = tuple(int(v) for v in np.cumsum((0,) + IN_SIZES))

LANES = 128
GATE_PAD = LANES
N_MOD = 6
MOD_ROWS = 16
M_CHUNK_K = 128
NEG_BIG = -1e30
LOG2_E = 1.4426950408889634
VMEM_LIMIT = 56 * 1024 * 1024
TM = 512
TM_IN = 1024
TQ = 1024
TN_MOD = 1024
W_IN_ROWS = 256

F32 = jnp.float32
BF16 = jnp.bfloat16


def _cparams(n_axes):
    return pltpu.CompilerParams(dimension_semantics=("arbitrary",) * n_axes,
                                vmem_limit_bytes=VMEM_LIMIT)


def _layer_spec(shape, layer):
    return pl.BlockSpec((None,) + tuple(shape), lambda *_: (layer,) + (0,) * len(shape),
                        pipeline_mode=pl.Buffered(1))


class _Rows:
    def __init__(self, rows_ctx, t_lat, n_stage=0, tm=TM):
        self.tm = tm
        self.n_ctx = rows_ctx // tm
        self.per_seq = t_lat // tm
        self.n_stage = n_stage

    def tile(self, i):
        return jnp.maximum(i - self.n_stage, 0)

    def both(self, i):
        return (self.tile(i), 0)

    def ctx(self, i):
        return (jnp.minimum(self.tile(i), self.n_ctx - 1), 0)

    def lat(self, i):
        return (jnp.maximum(self.tile(i) - self.n_ctx, 0), 0)

    def mod_spec(self, layer):
        def index(i):
            t = self.tile(i)
            return (layer, jnp.where(t < self.n_ctx, 0, 1 + (t - self.n_ctx) // self.per_seq), 0, 0)
        return pl.BlockSpec((None, None, N_MOD, D_MODEL), index)

    def spec(self, width, which):
        return pl.BlockSpec((self.tm, width), getattr(self, which))

    def is_ctx(self):
        return pl.program_id(0) - self.n_stage < self.n_ctx


def _pick(rows, ctx_ref, lat_ref):
    return jnp.where(rows.is_ctx(), ctx_ref[...], lat_ref[...])


def _staged_spec(shape, layer, n_chunks):
    rc, cols = shape[0] // n_chunks, shape[1]
    return pl.BlockSpec((None, rc, cols), lambda i: (layer, jnp.minimum(i, n_chunks - 1), 0))


def _stage(w_ref, scratch_ref, n_chunks):
    i = pl.program_id(0)
    rc = w_ref.shape[0]

    @pl.when(i < n_chunks)
    def _():
        scratch_ref[pl.ds(pl.multiple_of(i * rc, rc), rc), :] = w_ref[...].astype(BF16)


def _mm(a, b):
    return jnp.dot(a, b, preferred_element_type=F32)


def _mm_nt(a, b):
    return lax.dot_general(a, b, (((1,), (1,)), ((), ())), preferred_element_type=F32)


def _split3(x):
    hi = x.astype(BF16)
    r = x - hi.astype(F32)
    mid = r.astype(BF16)
    lo = (r - mid.astype(F32)).astype(BF16)
    return hi, mid, lo


def _mm_left_f32(x, m_bf16):
    hi, mid, lo = _split3(x)
    return _mm(hi, m_bf16) + _mm(mid, m_bf16) + _mm(lo, m_bf16)


def _sigmoid(x):
    return 0.5 * jnp.tanh(0.5 * x) + 0.5


def _silu(x):
    return x * _sigmoid(x)


def _log_sigmoid(x):
    return jnp.minimum(x, 0.0) - jnp.log(1.0 + jnp.exp(-jnp.abs(x)))


def _normmod(x, g, scale, shift):
    ms = jnp.mean(x * x, axis=-1, keepdims=True)
    return (x * lax.rsqrt(ms + EPS)) * g * (1.0 + scale) + shift


def _group_inv_rms(x, bd):
    x2 = x * x
    hi = x2.astype(BF16)
    lo = (x2 - hi.astype(F32)).astype(BF16)
    ss = _mm(hi, bd) + _mm(lo, bd)
    return lax.rsqrt(ss * (1.0 / HEAD_DIM) + EPS)


def _rope(x, cos, sin_signed):
    w = x.shape[1]
    lane = lax.broadcasted_iota(jnp.int32, x.shape, 1)
    up = pltpu.roll(x, w - 16, axis=1)
    dn = pltpu.roll(x, 16, axis=1)
    partner = jnp.where((lane & 31) < 16, up, dn)
    return x * cos + partner * sin_signed


def _mod_kernel(c_ref, w_ref, b_ref, o_ref):
    s = _silu(c_ref[...]).astype(BF16)
    o_ref[...] = _mm(s, w_ref[...].astype(BF16)) + b_ref[...]


def _modulation(cvec, w_ada, b_ada):
    tn = TN_MOD
    n_out = N_MOD * D_MODEL
    return pl.pallas_call(
        _mod_kernel,
        grid=(DEPTH, n_out // tn),
        in_specs=[
            pl.BlockSpec((MOD_ROWS, D_MODEL), lambda l, j: (0, 0)),
            pl.BlockSpec((None, D_MODEL, tn), lambda l, j: (l, 0, j)),
            pl.BlockSpec((None, 1, tn), lambda l, j: (l, 0, j)),
        ],
        out_specs=pl.BlockSpec((None, MOD_ROWS, tn), lambda l, j: (l, 0, j)),
        out_shape=jax.ShapeDtypeStruct((DEPTH, MOD_ROWS, n_out), F32),
        compiler_params=_cparams(2),
        name="modulation",
    )(cvec, w_ada, b_ada.reshape(DEPTH, 1, n_out))


GATE_OFF, FU_OFF, BG_OFF = IN_OFFS[7], IN_OFFS[8], IN_OFFS[9]
IN_COLS = IN_OFFS[-1]
MIX_CHUNKS = -(-BG_OFF // W_IN_ROWS)
BG_FIRST = BG_OFF // W_IN_ROWS
BG_CHUNKS = -(-IN_COLS // W_IN_ROWS) - BG_FIRST
BG_SKIP = BG_OFF - BG_FIRST * W_IN_ROWS
BG_LAST = IN_COLS - (BG_FIRST + BG_CHUNKS - 1) * W_IN_ROWS


def _in_kernel(*refs, rows, split_x):
    refs = list(refs)
    x_refs = [refs.pop(0) for _ in range(2 if split_x else 1)]
    mod_ref, g_ref, w_ref, za_ref, zm_ref, fu_ref, gt_ref, w_s = refs
    _stage(w_ref, w_s, MIX_CHUNKS)

    @pl.when(pl.program_id(0) >= MIX_CHUNKS)
    def _():
        x = _pick(rows, *x_refs) if split_x else x_refs[0][...]
        h = _normmod(x, g_ref[...], mod_ref[1:2, :], mod_ref[0:1, :])
        z = _mm_nt(h.astype(BF16), w_s[:BG_OFF, :])
        za_ref[...] = z[:, :D_MODEL]
        zm_ref[...] = z[:, D_MODEL:GATE_OFF]
        gt_ref[...] = z[:, GATE_OFF:GATE_OFF + GATE_PAD]
        fu_ref[...] = z[:, FU_OFF:FU_OFF + F_WIDTH].astype(BF16)


def _in_proj(xs, mod, g, w_in_t, rows, layer):
    split_x = len(xs) == 2
    n_rows = sum(x.shape[0] for x in xs)
    x_specs = [rows.spec(D_MODEL, "ctx"), rows.spec(D_MODEL, "lat")] if split_x else [rows.spec(D_MODEL, "both")]
    widths = (D_MODEL, D_MODEL, F_WIDTH, GATE_PAD)
    dtypes = (F32, F32, BF16, F32)
    return pl.pallas_call(
        functools.partial(_in_kernel, rows=rows, split_x=split_x),
        grid=(MIX_CHUNKS + n_rows // rows.tm,),
        in_specs=x_specs + [
            rows.mod_spec(layer),
            _layer_spec((1, D_MODEL), layer),
            _staged_spec((MIX_CHUNKS * W_IN_ROWS, D_MODEL), layer, MIX_CHUNKS),
        ],
        out_specs=[rows.spec(w, "both") for w in widths],
        out_shape=[jax.ShapeDtypeStruct((n_rows, w), d) for w, d in zip(widths, dtypes)],
        scratch_shapes=[pltpu.VMEM((MIX_CHUNKS * W_IN_ROWS, D_MODEL), BF16)],
        compiler_params=_cparams(1),
        name="in_proj",
    )(*xs, mod, g, w_in_t)


def _attn_kernel(*refs, tq, t_new, t_past, latent, fill_layer):
    if latent:
        (q_ref, k_ref, v_ref, ck_ref, cv_ref, cosq_ref, sinq_ref, cosk_ref, sink_ref,
         qg_ref, kg_ref, bdq_ref, bdk_ref, o_ref, ks_ref, vs_ref) = refs
    else:
        q_ref, k_ref, v_ref, qg_ref, kg_ref, bdq_ref, bdk_ref = refs[:7]
        o_ref, nk_ref, nv_ref, ks_ref, vs_ref = refs[-5:]

    @pl.when(pl.program_id(1) == 0)
    def _():
        k = k_ref[...]
        kn = k * _group_inv_rms(k, bdk_ref[...]) * kg_ref[...]
        v = v_ref[...]
        if latent:
            kn = _rope(kn, cosk_ref[...], sink_ref[...])
            ck = ck_ref[...].T
            cv = cv_ref[...].T
        else:
            if fill_layer is None:
                nk_ref[...] = kn.T
                nv_ref[...] = v.T
            else:
                for l2 in range(DEPTH):
                    nk_ref[l2] = kn.T if l2 == fill_layer else jnp.zeros((KV_WIDTH, t_new), F32)
                    nv_ref[l2] = v.T if l2 == fill_layer else jnp.zeros((KV_WIDTH, t_new), F32)
        for h in range(N_KV_HEADS):
            sl = slice(HEAD_DIM * h, HEAD_DIM * (h + 1))
            if latent:
                ks_ref[h, :t_past, :] = ck[:, sl].astype(BF16)
                vs_ref[h, :t_past, :HEAD_DIM] = cv[:, sl].astype(BF16)
            ks_ref[h, t_past:, :] = kn[:, sl].astype(BF16)
            vs_ref[h, t_past:, :HEAD_DIM] = v[:, sl].astype(BF16)
            vs_ref[h, :, HEAD_DIM:] = jnp.ones((t_past + t_new, LANES - HEAD_DIM), BF16)

    q = q_ref[...]
    qn = q * _group_inv_rms(q, bdq_ref[...]) * qg_ref[...]
    if latent:
        qn = _rope(qn, cosq_ref[...], sinq_ref[...])
    qn = qn * (HEAD_DIM ** -0.5 * LOG2_E)
    group = N_Q_HEADS // N_KV_HEADS
    outs = []

    def scores(h):
        qh = jnp.concatenate(
            [qn[:, HEAD_DIM * (group * h + g):HEAD_DIM * (group * h + g + 1)] for g in range(group)],
            axis=0).astype(BF16)
        return _mm_nt(qh, ks_ref[h])

    ahead = 1 if latent else N_KV_HEADS - 1
    s_q = [scores(h) for h in range(ahead)]
    for h in range(N_KV_HEADS):
        s = s_q.pop(0)
        if h + ahead < N_KV_HEADS:
            s_q.append(scores(h + ahead))
        m = jnp.max(s, axis=-1, keepdims=True)
        p = jnp.exp2(s - m).astype(BF16)
        od = _mm(p, vs_ref[h])
        o = (od / pltpu.roll(od, LANES - HEAD_DIM, axis=1))[:, :HEAD_DIM]
        outs.extend(o[tq * g:tq * (g + 1)] for g in range(group))
    o_ref[...] = jnp.concatenate(outs, axis=1).astype(BF16)


def _block_diag_ones(width):
    idx = np.arange(width) // HEAD_DIM
    return jnp.asarray(idx[:, None] == idx[None, :], dtype=BF16)


def _attention(za, qg, kg, *, batch, t_new, latent, row0, cache_k=None, cache_v=None, layer=0, rope_tabs=None,
               carried=None):
    tq = min(TQ, t_new)
    nq = t_new // tq
    t_past = cache_k.shape[3] if latent else 0
    t_keys = t_past + t_new
    kcol = ATTN_WIDTH // KV_WIDTH
    bdq = _block_diag_ones(ATTN_WIDTH)
    bdk = _block_diag_ones(KV_WIDTH)
    const = lambda shape: pl.BlockSpec(shape, lambda b, i: (0,) * len(shape))
    q0, s0 = row0 // tq, row0 // t_new
    in_specs = [
        pl.BlockSpec((tq, ATTN_WIDTH), lambda b, i: (q0 + b * nq + i, 0)),
        pl.BlockSpec((t_new, KV_WIDTH), lambda b, i: (s0 + b, kcol)),
        pl.BlockSpec((t_new, KV_WIDTH), lambda b, i: (s0 + b, kcol + 1)),
    ]
    args = [za, za, za]
    if latent:
        cos_t, sin_t = rope_tabs
        in_specs += [
            pl.BlockSpec((None, None, KV_WIDTH, t_past), lambda b, i: (b, layer, 0, 0)),
            pl.BlockSpec((None, None, KV_WIDTH, t_past), lambda b, i: (b, layer, 0, 0)),
            pl.BlockSpec((tq, ATTN_WIDTH), lambda b, i: (i, 0)),
            pl.BlockSpec((tq, ATTN_WIDTH), lambda b, i: (i, 0)),
            pl.BlockSpec((t_new, KV_WIDTH), lambda b, i: (0, 0)),
            pl.BlockSpec((t_new, KV_WIDTH), lambda b, i: (0, 0)),
        ]
        args += [cache_k, cache_v, cos_t, sin_t, cos_t, sin_t]
    in_specs += [_layer_spec((1, ATTN_WIDTH), layer), _layer_spec((1, KV_WIDTH), layer),
                 const((ATTN_WIDTH, ATTN_WIDTH)), const((KV_WIDTH, KV_WIDTH))]
    args += [qg, kg, bdq, bdk]
    out_specs = [pl.BlockSpec((tq, ATTN_WIDTH), lambda b, i: (b * nq + i, 0))]
    out_shape = [jax.ShapeDtypeStruct((batch * t_new, ATTN_WIDTH), BF16)]
    aliases = {}
    if not latent:
        out_shape += [jax.ShapeDtypeStruct((batch, DEPTH, KV_WIDTH, t_new), F32)] * 2
        if carried is None:
            out_specs += [pl.BlockSpec((None, DEPTH, KV_WIDTH, t_new), lambda b, i: (b, 0, 0, 0))] * 2
        else:
            out_specs += [pl.BlockSpec((None, None, KV_WIDTH, t_new), lambda b, i: (b, layer, 0, 0))] * 2
            aliases = {len(args) + j: 1 + j for j in range(2)}
            in_specs += [pl.BlockSpec(memory_space=pl.ANY)] * 2
            args += list(carried)
    return pl.pallas_call(
        functools.partial(_attn_kernel, tq=tq, t_new=t_new, t_past=t_past, latent=latent,
                          fill_layer=layer if not latent and carried is None else None),
        grid=(batch, nq),
        in_specs=in_specs,
        out_specs=out_specs,
        out_shape=out_shape,
        scratch_shapes=[pltpu.VMEM((N_KV_HEADS, t_keys, HEAD_DIM), BF16),
                        pltpu.VMEM((N_KV_HEADS, t_keys, LANES), BF16)],
        input_output_aliases=aliases,
        compiler_params=_cparams(2),
        name="attention_latent" if latent else "attention_context",
    )(*args)


def _running_max_sublanes(x, reverse):
    n = x.shape[0]
    row = lax.broadcasted_iota(jnp.int32, x.shape, 0)
    k = 1
    while k < n:
        if reverse:
            cand = jnp.where(row < n - k, pltpu.roll(x, n - k, axis=0), NEG_BIG)
        else:
            cand = jnp.where(row >= k, pltpu.roll(x, k, axis=0), NEG_BIG)
        x = jnp.maximum(x, cand)
        k *= 2
    return x


def _mlstm_kernel(*refs, t, chunk, has_state, emit_state, convert, n_carried, fill_layer):
    refs = list(refs)
    zm_ref, gt_ref, cw_ref, gb_ref, mg_ref, bd_ref, fu_ref, dft_c_ref, dft_ct_ref, dft_st_ref = refs[:10]
    refs = refs[10:]
    if has_state:
        c0_ref, n0_ref, m0_ref = refs[:3]
        refs = refs[3:]
    convert_in, refs = refs[:len(convert)], refs[len(convert):]
    refs = refs[n_carried:]
    o_ref, fo_ref = refs[:2]
    refs = refs[2:]
    if emit_state:
        cs_ref, ns_ref, ms_ref = refs[:3]
        refs = refs[3:]
    convert_out, refs = refs[:len(convert)], refs[len(convert):]
    hst_ref = refs[0]

    for (n_chunks, last_valid), src_ref, dst_ref in zip(convert, convert_in, convert_out):
        w = src_ref[...]
        if last_valid is not None:
            rc = w.shape[0]
            limit = jnp.where(pl.program_id(0) >= n_chunks - 1, last_valid, rc)
            w = jnp.where(lax.broadcasted_iota(jnp.int32, (rc, 1), 0) < limit, w, 0.0)
        dst_ref[...] = w.astype(BF16)

    width = M_HEADS * M_DK
    z = zm_ref[...]
    x = z[:, :2 * width]
    row = lax.broadcasted_iota(jnp.int32, (t, 1), 0)
    x_prev = jnp.where(row == 0, 0.0, pltpu.roll(x, 1, axis=0))
    x_next = jnp.where(row == t - 1, 0.0, pltpu.roll(x, t - 1, axis=0))
    cw = cw_ref[...]
    qk = _silu(x_prev * cw[0:1, :] + x * cw[1:2, :] + x_next * cw[2:3, :])
    q = qk[:, :width]
    k = qk[:, width:] * (M_DK ** -0.5)
    v = z[:, 2 * width:2 * width + M_WIDTH]
    om = z[:, 2 * width + M_WIDTH:]
    heads = [slice(M_DK * h, M_DK * (h + 1)) for h in range(M_HEADS)]
    q_rows = [q[:, hs].astype(BF16) for hs in heads]
    k_rows = [k[:, hs].astype(BF16) for hs in heads]
    qt = q.T.astype(BF16)
    vt = v.T
    ones_r = jnp.ones((LANES - M_DV, t), F32)
    vaug_t = [jnp.concatenate([vt[hs, :], ones_r], axis=0) for hs in heads]
    vaug_tb = [a.astype(BF16) for a in vaug_t]

    gates = gt_ref[...] + gb_ref[...]
    lane = lax.broadcasted_iota(jnp.int32, (1, GATE_PAD), 1)
    is_forget = (lane & M_HEADS) != 0
    gates = jnp.where(is_forget, _log_sigmoid(gates), gates)
    gates_t = gates.T[:N_GATES, :]

    ri = lax.broadcasted_iota(jnp.int32, (chunk, chunk), 0)
    ci = lax.broadcasted_iota(jnp.int32, (chunk, chunk), 1)
    row_le_col = ri <= ci
    row_ge_col = ri >= ci

    n_chunks = t // chunk
    g_all = jnp.concatenate([gates_t[:, chunk * c:chunk * (c + 1)] for c in range(n_chunks)], axis=0)
    i_all = pltpu.roll(g_all, M_HEADS, axis=0)
    rows = lax.broadcasted_iota(jnp.int32, (N_GATES * n_chunks, 1), 0)
    is_fwd_row = (rows & (2 * M_HEADS)) == 0
    cum_p = _mm_left_f32(g_all, row_le_col.astype(BF16))
    cum_s = _mm_left_f32(g_all, row_ge_col.astype(BF16))
    b_all = jnp.where(is_fwd_row, cum_p, cum_s)
    u_all = i_all - b_all
    u_cols = u_all.T
    cols = lax.broadcasted_iota(jnp.int32, (1, N_GATES * n_chunks), 1)
    pm_cols = jnp.where((cols & (2 * M_HEADS)) == 0, _running_max_sublanes(u_cols, reverse=False),
                        _running_max_sublanes(u_cols, reverse=True))
    pm_all = pm_cols.T
    u_cols2 = u_cols * LOG2_E
    pm_all2 = pm_all * LOG2_E
    pml_all = jnp.where(is_fwd_row, pm_all[:, chunk - 1:], pm_all[:, :1])
    bl_all = jnp.where(is_fwd_row, b_all[:, chunk - 1:], b_all[:, :1])
    wk_all = jnp.exp(u_all - pml_all)

    if has_state:
        m_state = [m0_ref[M_HEADS * d:M_HEADS * (d + 1), :] for d in range(2)]
        c_state = [[jnp.concatenate(
            [c0_ref[u].T, jnp.broadcast_to(n0_ref[u:u + 1, :], (LANES - M_DV, M_DK))], axis=0)
            for u in range(M_HEADS * d, M_HEADS * (d + 1))] for d in range(2)]
    else:
        m_state = [jnp.zeros((M_HEADS, 1), F32) for _ in range(2)]
        c_state = [[jnp.zeros((LANES, M_DK), F32) for _ in range(M_HEADS)] for _ in range(2)]

    steps = [(j if d == 0 else n_chunks - 1 - j, d) for j in range(n_chunks) for d in range(2)]

    def gate_rows(step):
        c, direction = step
        return N_GATES * c + (2 * direction + 1) * M_HEADS

    def key_query(step):
        rs = slice(chunk * step[0], chunk * (step[0] + 1))
        return [_mm_nt(k_rows[h][rs, :], q_rows[h][rs, :]) for h in range(M_HEADS)]

    def local_sums(step, st):
        c, direction = step
        rs = slice(chunk * c, chunk * (c + 1))
        r0 = gate_rows(step)
        valid = row_le_col if direction == 0 else row_ge_col
        s_loc, x_loc = [], []
        for h in range(M_HEADS):
            w = jnp.exp2(jnp.where(valid, u_cols2[:, r0 + h:r0 + h + 1] - pm_all2[r0 + h:r0 + h + 1, :], NEG_BIG))
            s_loc.append(_mm(vaug_tb[h][:, rs], (st[h] * w).astype(BF16)))
            x_loc.append(_mm((vaug_t[h][:, rs] * wk_all[r0 + h:r0 + h + 1, :]).astype(BF16), k_rows[h][rs, :]))
        return s_loc, x_loc

    written = set()
    n_steps = len(steps)
    st_q = [key_query(steps[i]) for i in range(min(2, n_steps))]
    loc_q = [local_sums(steps[0], st_q.pop(0))]
    for i, (c, direction) in enumerate(steps):
        if i + 1 < n_steps:
            loc_q.append(local_sums(steps[i + 1], st_q.pop(0)))
        if i + 2 < n_steps:
            st_q.append(key_query(steps[i + 2]))
        s_loc, x_loc = loc_q.pop(0)
        rs = slice(chunk * c, chunk * (c + 1))
        r0 = gate_rows((c, direction))
        b4, pm4, bl, pml = (a[r0:r0 + M_HEADS, :] for a in (b_all, pm_all, bl_all, pml_all))
        m = m_state[direction]
        mx = jnp.maximum(m, pm4)
        f_loc = jnp.exp(pm4 - mx)
        f_int = jnp.exp(m - mx)
        floor = jnp.exp(-(b4 + mx))
        m_new = bl + jnp.maximum(m, pml)
        decay = jnp.exp(bl + m - m_new)
        gain = jnp.exp(bl + pml - m_new)
        m_state[direction] = m_new
        inter = [_mm(c_state[direction][h].astype(BF16), qt[heads[h], rs]) for h in range(M_HEADS)]
        for h in range(M_HEADS):
            nd = f_loc[h:h + 1, :] * s_loc[h] + f_int[h:h + 1, :] * inter[h]
            ht = nd[:M_DV, :] / jnp.maximum(jnp.abs(nd[M_DV:, :]), floor[h:h + 1, :])
            c_state[direction][h] = decay[h:h + 1, :] * c_state[direction][h] + gain[h:h + 1, :] * x_loc[h]
            if c in written:
                hst_ref[heads[h], rs] += ht
            else:
                hst_ref[heads[h], rs] = ht
        written.add(c)

    if emit_state:
        if fill_layer is None:
            cs_out, ns_out, ms_out = cs_ref, ns_ref, ms_ref
        else:
            for l2 in range(DEPTH):
                if l2 != fill_layer:
                    cs_ref[l2] = jnp.zeros(cs_ref.shape[1:], F32)
                    ns_ref[l2] = jnp.zeros(ns_ref.shape[1:], F32)
                    ms_ref[l2] = jnp.zeros(ms_ref.shape[1:], F32)
            cs_out, ns_out, ms_out = cs_ref.at[fill_layer], ns_ref.at[fill_layer], ms_ref.at[fill_layer]
        for d in range(2):
            ms_out[M_HEADS * d:M_HEADS * (d + 1), :] = jnp.broadcast_to(m_state[d], (M_HEADS, LANES))
            for h in range(M_HEADS):
                u = M_HEADS * d + h
                cs_out[u] = c_state[d][h][:M_DV, :].T
                ns_out[u:u + 1, :] = c_state[d][h][M_DV:M_DV + 1, :]

    hsum = hst_ref[...].T
    hn = hsum * _group_inv_rms(hsum, bd_ref[...]) * mg_ref[...]
    o_ref[...] = (_sigmoid(om) * hn).astype(BF16)

    fa = _mm(fu_ref[...].astype(BF16), dft_c_ref[...])
    fo_ref[...] = (_mm(dft_ct_ref[...], fa[:, :F_WIDTH].astype(BF16))
                   - _mm(dft_st_ref[...], fa[:, F_WIDTH:].astype(BF16))).astype(BF16)


def _mlstm_fourier(zm, gt, fu, conv_w, gate_b, m_norm_g, *, batch, t, layer, row0, state0=None,
                   convert=(), carried=None):
    has_state = state0 is not None
    emit_state = not has_state
    chunk = min(M_CHUNK_K, t)
    n_units = 2 * M_HEADS
    const = lambda shape: pl.BlockSpec(shape, lambda b: (0,) * len(shape), pipeline_mode=pl.Buffered(1))
    s0 = row0 // t
    in_specs = [
        pl.BlockSpec((t, D_MODEL), lambda b: (s0 + b, 0)),
        pl.BlockSpec((t, GATE_PAD), lambda b: (s0 + b, 0)),
        _layer_spec((3, 2 * M_HEADS * M_DK), layer),
        _layer_spec((1, GATE_PAD), layer),
        _layer_spec((1, M_WIDTH), layer),
        const((M_WIDTH, M_WIDTH)),
        pl.BlockSpec((t, F_WIDTH), lambda b: (s0 + b, 0)),
        const((F_WIDTH, 2 * F_WIDTH)), const((t, t)), const((t, t)),
    ]
    args = [zm, gt, conv_w, gate_b, m_norm_g, _block_diag_ones(M_WIDTH), fu, *_dft_tables(t)]
    if has_state:
        in_specs += [pl.BlockSpec((None, None, n_units, M_DK, M_DV), lambda b: (b, layer, 0, 0, 0)),
                     pl.BlockSpec((None, None, n_units, M_DK), lambda b: (b, layer, 0, 0)),
                     pl.BlockSpec((None, None, n_units, 1), lambda b: (b, layer, 0, 0))]
        args += list(state0)
    out_specs = [pl.BlockSpec((t, M_WIDTH), lambda b: (b, 0)), pl.BlockSpec((t, F_WIDTH), lambda b: (b, 0))]
    out_shape = [jax.ShapeDtypeStruct((batch * t, M_WIDTH), BF16), jax.ShapeDtypeStruct((batch * t, F_WIDTH), BF16)]
    if emit_state:
        if carried is None:
            out_specs += [pl.BlockSpec((None, DEPTH, n_units, M_DK, M_DV), lambda b: (b, 0, 0, 0, 0)),
                          pl.BlockSpec((None, DEPTH, n_units, M_DK), lambda b: (b, 0, 0, 0)),
                          pl.BlockSpec((None, DEPTH, n_units, LANES), lambda b: (b, 0, 0, 0))]
        else:
            out_specs += [pl.BlockSpec((None, None, n_units, M_DK, M_DV), lambda b: (b, layer, 0, 0, 0)),
                          pl.BlockSpec((None, None, n_units, M_DK), lambda b: (b, layer, 0, 0)),
                          pl.BlockSpec((None, None, n_units, LANES), lambda b: (b, layer, 0, 0))]
        out_shape += [jax.ShapeDtypeStruct((batch, DEPTH, n_units, M_DK, M_DV), F32),
                      jax.ShapeDtypeStruct((batch, DEPTH, n_units, M_DK), F32),
                      jax.ShapeDtypeStruct((batch, DEPTH, n_units, LANES), F32)]
    aliases = {}
    for w, rc, first, n_chunks, _ in convert:
        assert n_chunks <= batch
        w_cols = w.shape[2]
        in_specs.append(pl.BlockSpec(
            (None, rc, w_cols), lambda b, first=first, n=n_chunks: (layer, first + jnp.minimum(b, n - 1), 0)))
        args.append(w)
        out_specs.append(pl.BlockSpec((rc, w_cols), lambda b, n=n_chunks: (jnp.minimum(b, n - 1), 0)))
        out_shape.append(jax.ShapeDtypeStruct((rc * n_chunks, w_cols), BF16))
    if carried is not None:
        aliases = {len(args) + j: 2 + j for j in range(3)}
        in_specs += [pl.BlockSpec(memory_space=pl.ANY)] * 3
        args += list(carried)
    return pl.pallas_call(
        functools.partial(_mlstm_kernel, t=t, chunk=chunk, has_state=has_state, emit_state=emit_state,
                          convert=tuple((n, last) for _, _, _, n, last in convert),
                          n_carried=0 if carried is None else 3,
                          fill_layer=layer if emit_state and carried is None else None),
        grid=(batch,),
        in_specs=in_specs,
        out_specs=out_specs,
        out_shape=out_shape,
        scratch_shapes=[pltpu.VMEM((M_WIDTH, t), F32)],
        input_output_aliases=aliases,
        compiler_params=_cparams(1),
        name="mlstm_fourier_latent" if has_state else "mlstm_fourier_context",
    )(*args)


def _dft_tables(t):
    kt = (np.arange(t)[:, None] * np.arange(t)[None, :]) % t
    ang_t = 2.0 * np.pi * kt.astype(np.float64) / t
    ct = np.cos(ang_t) / np.sqrt(t)
    st = np.sin(ang_t) / np.sqrt(t)
    c = F_GROUP_CH
    kc = (np.arange(c)[:, None] * np.arange(c)[None, :]) % c
    ang_c = 2.0 * np.pi * kc.astype(np.float64) / c
    eye = np.eye(F_GROUPS)
    cc = np.kron(eye, np.cos(ang_c) / np.sqrt(c))
    sc = np.kron(eye, np.sin(ang_c) / np.sqrt(c))
    cs = np.concatenate([cc, sc], axis=1)
    to_dev = lambda a: jnp.asarray(a.astype(np.float32)).astype(BF16)
    return to_dev(cs), to_dev(ct), to_dev(st)


def _merge_kernel(*refs, rows, split_x):
    refs = list(refs)
    x_refs = [refs.pop(0) for _ in range(2 if split_x else 1)]
    (mod_ref, g_ref, ac_ref, al_ref, hc_ref, hl_ref, fc_ref, fl_ref, wbg_ref,
     wpa_ref, wpm_ref, wpf_ref, wo_ref, o_ref) = refs
    x = _pick(rows, *x_refs) if split_x else x_refs[0][...]
    h = _normmod(x, g_ref[...], mod_ref[1:2, :], mod_ref[0:1, :]).astype(BF16)

    def branch(j, yc_ref, yl_ref, w_ref):
        gate = _sigmoid(_mm_nt(h, wbg_ref[BG_SKIP + D_MODEL * j:BG_SKIP + D_MODEL * (j + 1), :]))
        return gate * _mm(_pick(rows, yc_ref, yl_ref), w_ref[...])

    merged = (branch(0, ac_ref, al_ref, wpa_ref) + branch(1, hc_ref, hl_ref, wpm_ref)
              + branch(2, fc_ref, fl_ref, wpf_ref))
    o_ref[...] = x + mod_ref[2:3, :] * _mm(merged.astype(BF16), wo_ref[...])


def _merge(xs, mod, g, attn, hm, fo, weights, rows, layer):
    split_x = len(xs) == 2
    n_rows = sum(x.shape[0] for x in xs)
    x_specs = [rows.spec(D_MODEL, "ctx"), rows.spec(D_MODEL, "lat")] if split_x else [rows.spec(D_MODEL, "both")]
    pair = lambda width: [rows.spec(width, "ctx"), rows.spec(width, "lat")]
    resident = lambda w: pl.BlockSpec(w.shape, lambda i: (0, 0), pipeline_mode=pl.Buffered(1))
    return pl.pallas_call(
        functools.partial(_merge_kernel, rows=rows, split_x=split_x),
        grid=(n_rows // rows.tm,),
        in_specs=x_specs + [rows.mod_spec(layer), _layer_spec((1, D_MODEL), layer)]
        + pair(ATTN_WIDTH) + pair(M_WIDTH) + pair(F_WIDTH) + [resident(w) for w in weights],
        out_specs=rows.spec(D_MODEL, "both"),
        out_shape=jax.ShapeDtypeStruct((n_rows, D_MODEL), F32),
        compiler_params=_cparams(1),
        name="merge",
    )(*xs, mod, g, *attn, *hm, *fo, *weights)


def _ffn_kernel(*refs, rows, split_out):
    x_ref, mod_ref, g_ref, win_ref, wout_ref = refs[:5]
    out_refs = refs[5:]
    x = x_ref[...]
    h = _normmod(x, g_ref[...], mod_ref[4:5, :], mod_ref[3:4, :]).astype(BF16)
    u = _mm(h, win_ref[...])
    a = (_silu(u[:, :FF_HIDDEN]) * u[:, FF_HIDDEN:]).astype(BF16)
    y = x + mod_ref[5:6, :] * _mm(a, wout_ref[...])
    if split_out:
        @pl.when(rows.is_ctx())
        def _():
            out_refs[0][...] = y

        @pl.when(jnp.logical_not(rows.is_ctx()))
        def _():
            out_refs[1][...] = y
    else:
        out_refs[0][...] = y


def _ffn(x, mod, g, w_in, w_out, rows, layer, split_out):
    n_rows = x.shape[0]
    n_ctx_rows = rows.n_ctx * rows.tm
    resident = lambda w: pl.BlockSpec(w.shape, lambda i: (0, 0), pipeline_mode=pl.Buffered(1))
    if split_out:
        out_specs = [rows.spec(D_MODEL, "ctx"), rows.spec(D_MODEL, "lat")]
        out_shape = [jax.ShapeDtypeStruct((n_ctx_rows, D_MODEL), F32),
                     jax.ShapeDtypeStruct((n_rows - n_ctx_rows, D_MODEL), F32)]
    else:
        out_specs = [rows.spec(D_MODEL, "both")]
        out_shape = [jax.ShapeDtypeStruct((n_rows, D_MODEL), F32)]
    return pl.pallas_call(
        functools.partial(_ffn_kernel, rows=rows, split_out=split_out),
        grid=(n_rows // rows.tm,),
        in_specs=[rows.spec(D_MODEL, "both"), rows.mod_spec(layer), _layer_spec((1, D_MODEL), layer),
                  resident(w_in), resident(w_out)],
        out_specs=out_specs,
        out_shape=out_shape,
        compiler_params=_cparams(1),
        name="ffn",
    )(x, mod, g, w_in, w_out)


def _rope_tables(t):
    n = HEAD_DIM // 4
    inv = 1.0 / (ROPE_THETA ** (np.arange(n, dtype=np.float64) / n))
    pos = np.arange(t)
    ang_r = (pos // GRID_W)[:, None] * inv[None, :]
    ang_c = (pos % GRID_W)[:, None] * inv[None, :]
    cos = np.concatenate([np.cos(ang_r)] * 2 + [np.cos(ang_c)] * 2, axis=1)
    sin = np.concatenate([-np.sin(ang_r), np.sin(ang_r), -np.sin(ang_c), np.sin(ang_c)], axis=1)
    tile = lambda a: jnp.asarray(np.tile(a, (1, N_Q_HEADS)).astype(np.float32))
    return tile(cos), tile(sin)


def kernel(x_prompt, x_sample, cache_k, cache_v, state_C, state_n, state_m, c, c_ctx, w_ada, b_ada,
           norm1_g, w_in, q_norm_g, k_norm_g, m_conv_w, m_gate_b, m_norm_g, w_proj_attn,
           w_proj_mlstm, w_proj_fourier, w_out, norm2_g, w_ffn_in, w_ffn_out):
    n_ctx, t_ctx, _ = x_prompt.shape
    n_lat, t_lat, _ = x_sample.shape
    t_past = cache_k.shape[2]
    n_units = 2 * M_HEADS

    cvec = jnp.concatenate([c_ctx[None, :], c], axis=0)
    cvec = jnp.pad(cvec, ((0, MOD_ROWS - cvec.shape[0]), (0, 0)))
    mod = _modulation(cvec, w_ada, b_ada).reshape(DEPTH, MOD_ROWS, N_MOD, D_MODEL)

    rope_tabs = _rope_tables(t_lat)
    to_rows = lambda a: jnp.transpose(a, (0, 1, 3, 4, 2)).reshape(n_lat, DEPTH, KV_WIDTH, t_past)
    ck, cv = to_rows(cache_k), to_rows(cache_v)
    rows_ctx = n_ctx * t_ctx
    rows_in = _Rows(rows_ctx, t_lat, n_stage=MIX_CHUNKS, tm=TM_IN)
    rows_merge = _Rows(rows_ctx, t_lat)
    rows_ffn = _Rows(rows_ctx, t_lat)

    w_in_t = jnp.swapaxes(w_in, 1, 2)
    row = lambda a: a[:, None, :]
    norm1, norm2 = row(norm1_g), row(norm2_g)
    qg, kg = row(jnp.tile(q_norm_g, (1, N_Q_HEADS))), row(jnp.tile(k_norm_g, (1, N_KV_HEADS)))
    gate_b = row(jnp.pad(m_gate_b, ((0, 0), (0, GATE_PAD - N_GATES))))
    m_norm = row(m_norm_g)

    per_step = lambda w, n: (w, w.shape[1] // n, 0, n, None)
    xs = (x_prompt.reshape(rows_ctx, D_MODEL), x_sample.reshape(n_lat * t_lat, D_MODEL))
    kv_carried = states_carried = None
    state0 = (state_C.astype(F32).reshape(n_lat, DEPTH, n_units, M_DK, M_DV),
              state_n.astype(F32).reshape(n_lat, DEPTH, n_units, M_DK),
              state_m.astype(F32).reshape(n_lat, DEPTH, n_units, 1))
    for l in range(DEPTH):
        za, zm, fu, gt = _in_proj(xs, mod, norm1, w_in_t, rows_in, l)
        attn_c, *kv_carried = _attention(za, qg, kg, batch=n_ctx, t_new=t_ctx, latent=False, row0=0, layer=l,
                                         carried=kv_carried)
        hm_c, fo_c, *rest = _mlstm_fourier(
            zm, gt, fu, m_conv_w, gate_b, m_norm, batch=n_ctx, t=t_ctx, layer=l, row0=0, carried=states_carried,
            convert=((w_in_t, W_IN_ROWS, BG_FIRST, BG_CHUNKS, BG_LAST),)
            + tuple(per_step(w, n_ctx) for w in (w_proj_attn, w_proj_mlstm, w_proj_fourier, w_out)))
        states_carried, merge_w = rest[:3], rest[3:]

        (attn_l,) = _attention(za, qg, kg, batch=n_lat, t_new=t_lat, latent=True, row0=rows_ctx,
                               cache_k=ck, cache_v=cv, layer=l, rope_tabs=rope_tabs)
        hm_l, fo_l, w_ffn_in_b, w_ffn_out_b = _mlstm_fourier(
            zm, gt, fu, m_conv_w, gate_b, m_norm, batch=n_lat, t=t_lat, layer=l, row0=rows_ctx,
            state0=state0, convert=(per_step(w_ffn_in, n_lat), per_step(w_ffn_out, n_lat)))

        x1 = _merge(xs, mod, norm1, (attn_c, attn_l), (hm_c, hm_l), (fo_c, fo_l), merge_w, rows_merge, l)
        xs = tuple(_ffn(x1, mod, norm2, w_ffn_in_b, w_ffn_out_b, rows_ffn, l, split_out=l == DEPTH - 1))
    xp, xs = xs

    kv_out = lambda a: jnp.transpose(a.reshape(n_ctx, DEPTH, N_KV_HEADS, HEAD_DIM, t_ctx), (0, 1, 4, 2, 3))
    cs, ns, ms = states_carried
    return (xp.reshape(n_ctx, t_ctx, D_MODEL), xs.reshape(n_lat, t_lat, D_MODEL),
            kv_out(kv_carried[0]), kv_out(kv_carried[1]),
            cs.reshape(n_ctx, DEPTH, 2, M_HEADS, M_DK, M_DV), ns.reshape(n_ctx, DEPTH, 2, M_HEADS, M_DK),
            ms[..., 0].reshape(n_ctx, DEPTH, 2, M_HEADS))
```

```python
import functools

import numpy as np
import jax
import jax.numpy as jnp
from jax import lax
from jax.experimental import pallas as pl
from jax.experimental.pallas import tpu as pltpu

D_MODEL = 1024
DEPTH = 2
GRID_W = 64
HEAD_DIM = 64
N_Q_HEADS = 8
N_KV_HEADS = 4
ATTN_WIDTH = N_Q_HEADS * HEAD_DIM
KV_WIDTH = N_KV_HEADS * HEAD_DIM
ROPE_THETA = 10000.0
M_HEADS = 4
M_DK = 64
M_DV = 64
M_WIDTH = M_HEADS * M_DV
F_GROUPS = 4
F_GROUP_CH = 64
F_WIDTH = F_GROUPS * F_GROUP_CH
FF_HIDDEN = -(-8 * D_MODEL // (3 * 256)) * 256
EPS = 1e-6
N_GATES = 4 * M_HEADS
IN_SIZES = (ATTN_WIDTH, KV_WIDTH, KV_WIDTH, M_HEADS * M_DK, M_HEADS * M_DK, M_WIDTH, M_WIDTH,
            N_GATES, F_WIDTH, 3 * D_MODEL)
IN_OFFS = tuple(int(v) for v in np.cumsum((0,) + IN_SIZES))

LANES = 128
GATE_PAD = LANES
N_MOD = 6
MOD_ROWS = 16
M_CHUNK_K = 128
NEG_BIG = -1e30
LOG2_E = 1.4426950408889634
VMEM_LIMIT = 56 * 1024 * 1024
TM = 512
TM_IN = 1024
TQ = 1024
TN_MOD = 1024
W_IN_ROWS = 256

F32 = jnp.float32
BF16 = jnp.bfloat16


def _cparams(n_axes):
    return pltpu.CompilerParams(dimension_semantics=("arbitrary",) * n_axes,
                                vmem_limit_bytes=VMEM_LIMIT)


def _layer_spec(shape, layer):
    return pl.BlockSpec((None,) + tuple(shape), lambda *_: (layer,) + (0,) * len(shape),
                        pipeline_mode=pl.Buffered(1))


class _Rows:
    def __init__(self, rows_ctx, t_lat, n_stage=0, tm=TM):
        self.tm = tm
        self.n_ctx = rows_ctx // tm
        self.per_seq = t_lat // tm
        self.n_stage = n_stage

    def tile(self, i):
        return jnp.maximum(i - self.n_stage, 0)

    def both(self, i):
        return (self.tile(i), 0)

    def ctx(self, i):
        return (jnp.minimum(self.tile(i), self.n_ctx - 1), 0)

    def lat(self, i):
        return (jnp.maximum(self.tile(i) - self.n_ctx, 0), 0)

    def mod_spec(self, layer):
        def index(i):
            t = self.tile(i)
            return (layer, jnp.where(t < self.n_ctx, 0, 1 + (t - self.n_ctx) // self.per_seq), 0, 0)
        return pl.BlockSpec((None, None, N_MOD, D_MODEL), index)

    def spec(self, width, which):
        return pl.BlockSpec((self.tm, width), getattr(self, which))

    def is_ctx(self):
        return pl.program_id(0) - self.n_stage < self.n_ctx


def _pick(rows, ctx_ref, lat_ref):
    return jnp.where(rows.is_ctx(), ctx_ref[...], lat_ref[...])


def _staged_spec(shape, layer, n_chunks):
    rc, cols = shape[0] // n_chunks, shape[1]
    return pl.BlockSpec((None, rc, cols), lambda i: (layer, jnp.minimum(i, n_chunks - 1), 0))


def _stage(w_ref, scratch_ref, n_chunks):
    i = pl.program_id(0)
    rc = w_ref.shape[0]

    @pl.when(i < n_chunks)
    def _():
        scratch_ref[pl.ds(pl.multiple_of(i * rc, rc), rc), :] = w_ref[...].astype(BF16)


def _mm(a, b):
    return jnp.dot(a, b, preferred_element_type=F32)


def _mm_nt(a, b):
    return lax.dot_general(a, b, (((1,), (1,)), ((), ())), preferred_element_type=F32)


def _split3(x):
    hi = x.astype(BF16)
    r = x - hi.astype(F32)
    mid = r.astype(BF16)
    lo = (r - mid.astype(F32)).astype(BF16)
    return hi, mid, lo


def _mm_left_f32(x, m_bf16):
    hi, mid, lo = _split3(x)
    return _mm(hi, m_bf16) + _mm(mid, m_bf16) + _mm(lo, m_bf16)


def _sigmoid(x):
    return 0.5 * jnp.tanh(0.5 * x) + 0.5


def _silu(x):
    hx = 0.5 * x
    return hx * jnp.tanh(hx) + hx


def _log_sigmoid(x):
    return jnp.minimum(x, 0.0) - jnp.log(1.0 + jnp.exp(-jnp.abs(x)))


def _normmod(x, g, scale, shift):
    ms = jnp.mean(x * x, axis=-1, keepdims=True)
    return (x * lax.rsqrt(ms + EPS)) * g * (1.0 + scale) + shift


def _group_inv_rms(x, bd):
    x2 = x * x
    hi = x2.astype(BF16)
    lo = (x2 - hi.astype(F32)).astype(BF16)
    ss = _mm(hi, bd) + _mm(lo, bd)
    return lax.rsqrt(ss * (1.0 / HEAD_DIM) + EPS)


def _rope(x, cos, sin_signed):
    w = x.shape[1]
    lane = lax.broadcasted_iota(jnp.int32, x.shape, 1)
    up = pltpu.roll(x, w - 16, axis=1)
    dn = pltpu.roll(x, 16, axis=1)
    partner = jnp.where((lane & 31) < 16, up, dn)
    return x * cos + partner * sin_signed


def _mod_kernel(c_ref, w_ref, b_ref, o_ref):
    s = _silu(c_ref[...]).astype(BF16)
    o_ref[...] = _mm(s, w_ref[...].astype(BF16)) + b_ref[...]


def _modulation(cvec, w_ada, b_ada):
    tn = TN_MOD
    n_out = N_MOD * D_MODEL
    return pl.pallas_call(
        _mod_kernel,
        grid=(DEPTH, n_out // tn),
        in_specs=[
            pl.BlockSpec((MOD_ROWS, D_MODEL), lambda l, j: (0, 0)),
            pl.BlockSpec((None, D_MODEL, tn), lambda l, j: (l, 0, j)),
            pl.BlockSpec((None, 1, tn), lambda l, j: (l, 0, j)),
        ],
        out_specs=pl.BlockSpec((None, MOD_ROWS, tn), lambda l, j: (l, 0, j)),
        out_shape=jax.ShapeDtypeStruct((DEPTH, MOD_ROWS, n_out), F32),
        compiler_params=_cparams(2),
        name="modulation",
    )(cvec, w_ada, b_ada.reshape(DEPTH, 1, n_out))


GATE_OFF, FU_OFF, BG_OFF = IN_OFFS[7], IN_OFFS[8], IN_OFFS[9]
IN_COLS = IN_OFFS[-1]
MIX_CHUNKS = -(-BG_OFF // W_IN_ROWS)
BG_FIRST = BG_OFF // W_IN_ROWS
BG_CHUNKS = -(-IN_COLS // W_IN_ROWS) - BG_FIRST
BG_SKIP = BG_OFF - BG_FIRST * W_IN_ROWS
BG_LAST = IN_COLS - (BG_FIRST + BG_CHUNKS - 1) * W_IN_ROWS


def _in_kernel(*refs, rows, split_x):
    refs = list(refs)
    x_refs = [refs.pop(0) for _ in range(2 if split_x else 1)]
    mod_ref, g_ref, w_ref, za_ref, zm_ref, fu_ref, gt_ref, w_s = refs
    _stage(w_ref, w_s, MIX_CHUNKS)

    @pl.when(pl.program_id(0) >= MIX_CHUNKS)
    def _():
        x = _pick(rows, *x_refs) if split_x else x_refs[0][...]
        h = _normmod(x, g_ref[...], mod_ref[1:2, :], mod_ref[0:1, :])
        z = _mm_nt(h.astype(BF16), w_s[:BG_OFF, :])
        za_ref[...] = z[:, :D_MODEL]
        zm_ref[...] = z[:, D_MODEL:GATE_OFF]
        gt_ref[...] = z[:, GATE_OFF:GATE_OFF + GATE_PAD]
        fu_ref[...] = z[:, FU_OFF:FU_OFF + F_WIDTH].astype(BF16)


def _in_proj(xs, mod, g, w_in_t, rows, layer):
    split_x = len(xs) == 2
    n_rows = sum(x.shape[0] for x in xs)
    x_specs = [rows.spec(D_MODEL, "ctx"), rows.spec(D_MODEL, "lat")] if split_x else [rows.spec(D_MODEL, "both")]
    widths = (D_MODEL, D_MODEL, F_WIDTH, GATE_PAD)
    dtypes = (F32, F32, BF16, F32)
    return pl.pallas_call(
        functools.partial(_in_kernel, rows=rows, split_x=split_x),
        grid=(MIX_CHUNKS + n_rows // rows.tm,),
        in_specs=x_specs + [
            rows.mod_spec(layer),
            _layer_spec((1, D_MODEL), layer),
            _staged_spec((MIX_CHUNKS * W_IN_ROWS, D_MODEL), layer, MIX_CHUNKS),
        ],
        out_specs=[rows.spec(w, "both") for w in widths],
        out_shape=[jax.ShapeDtypeStruct((n_rows, w), d) for w, d in zip(widths, dtypes)],
        scratch_shapes=[pltpu.VMEM((MIX_CHUNKS * W_IN_ROWS, D_MODEL), BF16)],
        compiler_params=_cparams(1),
        name="in_proj",
    )(*xs, mod, g, w_in_t)


def _attn_kernel(*refs, tq, t_new, t_past, latent, fill_layer):
    if latent:
        (q_ref, k_ref, v_ref, ck_ref, cv_ref, cosq_ref, sinq_ref, cosk_ref, sink_ref,
         qg_ref, kg_ref, bdq_ref, bdk_ref, o_ref, ks_ref, vs_ref) = refs
    else:
        q_ref, k_ref, v_ref, qg_ref, kg_ref, bdq_ref, bdk_ref = refs[:7]
        o_ref, nk_ref, nv_ref, ks_ref, vs_ref = refs[-5:]

    @pl.when(pl.program_id(1) == 0)
    def _():
        k = k_ref[...]
        kn = k * _group_inv_rms(k, bdk_ref[...]) * kg_ref[...]
        v = v_ref[...]
        if latent:
            kn = _rope(kn, cosk_ref[...], sink_ref[...])
            ck = ck_ref[...].T
            cv = cv_ref[...].T
        else:
            if fill_layer is None:
                nk_ref[...] = kn.T
                nv_ref[...] = v.T
            else:
                for l2 in range(DEPTH):
                    nk_ref[l2] = kn.T if l2 == fill_layer else jnp.zeros((KV_WIDTH, t_new), F32)
                    nv_ref[l2] = v.T if l2 == fill_layer else jnp.zeros((KV_WIDTH, t_new), F32)
        for h in range(N_KV_HEADS):
            sl = slice(HEAD_DIM * h, HEAD_DIM * (h + 1))
            if latent:
                ks_ref[h, :t_past, :] = ck[:, sl].astype(BF16)
                vs_ref[h, :t_past, :HEAD_DIM] = cv[:, sl].astype(BF16)
            ks_ref[h, t_past:, :] = kn[:, sl].astype(BF16)
            vs_ref[h, t_past:, :HEAD_DIM] = v[:, sl].astype(BF16)
            vs_ref[h, :, HEAD_DIM:] = jnp.ones((t_past + t_new, LANES - HEAD_DIM), BF16)

    q = q_ref[...]
    qn = q * _group_inv_rms(q, bdq_ref[...]) * qg_ref[...]
    if latent:
        qn = _rope(qn, cosq_ref[...], sinq_ref[...])
    qn = qn * (HEAD_DIM ** -0.5 * LOG2_E)
    group = N_Q_HEADS // N_KV_HEADS
    outs = []

    def scores(h):
        qh = jnp.concatenate(
            [qn[:, HEAD_DIM * (group * h + g):HEAD_DIM * (group * h + g + 1)] for g in range(group)],
            axis=0).astype(BF16)
        return _mm_nt(qh, ks_ref[h])

    ahead = 1 if latent else N_KV_HEADS - 1
    s_q = [scores(h) for h in range(ahead)]
    for h in range(N_KV_HEADS):
        s = s_q.pop(0)
        if h + ahead < N_KV_HEADS:
            s_q.append(scores(h + ahead))
        m = jnp.max(s, axis=-1, keepdims=True)
        p = jnp.exp2(s - m).astype(BF16)
        od = _mm(p, vs_ref[h])
        o = (od / pltpu.roll(od, LANES - HEAD_DIM, axis=1))[:, :HEAD_DIM]
        outs.extend(o[tq * g:tq * (g + 1)] for g in range(group))
    o_ref[...] = jnp.concatenate(outs, axis=1).astype(BF16)


def _block_diag_ones(width):
    idx = np.arange(width) // HEAD_DIM
    return jnp.asarray(idx[:, None] == idx[None, :], dtype=BF16)


def _attention(za, qg, kg, *, batch, t_new, latent, row0, cache_k=None, cache_v=None, layer=0, rope_tabs=None,
               carried=None):
    tq = min(TQ, t_new)
    nq = t_new // tq
    t_past = cache_k.shape[3] if latent else 0
    t_keys = t_past + t_new
    kcol = ATTN_WIDTH // KV_WIDTH
    bdq = _block_diag_ones(ATTN_WIDTH)
    bdk = _block_diag_ones(KV_WIDTH)
    const = lambda shape: pl.BlockSpec(shape, lambda b, i: (0,) * len(shape))
    q0, s0 = row0 // tq, row0 // t_new
    in_specs = [
        pl.BlockSpec((tq, ATTN_WIDTH), lambda b, i: (q0 + b * nq + i, 0)),
        pl.BlockSpec((t_new, KV_WIDTH), lambda b, i: (s0 + b, kcol)),
        pl.BlockSpec((t_new, KV_WIDTH), lambda b, i: (s0 + b, kcol + 1)),
    ]
    args = [za, za, za]
    if latent:
        cos_t, sin_t = rope_tabs
        in_specs += [
            pl.BlockSpec((None, None, KV_WIDTH, t_past), lambda b, i: (b, layer, 0, 0)),
            pl.BlockSpec((None, None, KV_WIDTH, t_past), lambda b, i: (b, layer, 0, 0)),
            pl.BlockSpec((tq, ATTN_WIDTH), lambda b, i: (i, 0)),
            pl.BlockSpec((tq, ATTN_WIDTH), lambda b, i: (i, 0)),
            pl.BlockSpec((t_new, KV_WIDTH), lambda b, i: (0, 0)),
            pl.BlockSpec((t_new, KV_WIDTH), lambda b, i: (0, 0)),
        ]
        args += [cache_k, cache_v, cos_t, sin_t, cos_t, sin_t]
    in_specs += [_layer_spec((1, ATTN_WIDTH), layer), _layer_spec((1, KV_WIDTH), layer),
                 const((ATTN_WIDTH, ATTN_WIDTH)), const((KV_WIDTH, KV_WIDTH))]
    args += [qg, kg, bdq, bdk]
    out_specs = [pl.BlockSpec((tq, ATTN_WIDTH), lambda b, i: (b * nq + i, 0))]
    out_shape = [jax.ShapeDtypeStruct((batch * t_new, ATTN_WIDTH), BF16)]
    aliases = {}
    if not latent:
        out_shape += [jax.ShapeDtypeStruct((batch, DEPTH, KV_WIDTH, t_new), F32)] * 2
        if carried is None:
            out_specs += [pl.BlockSpec((None, DEPTH, KV_WIDTH, t_new), lambda b, i: (b, 0, 0, 0))] * 2
        else:
            out_specs += [pl.BlockSpec((None, None, KV_WIDTH, t_new), lambda b, i: (b, layer, 0, 0))] * 2
            aliases = {len(args) + j: 1 + j for j in range(2)}
            in_specs += [pl.BlockSpec(memory_space=pl.ANY)] * 2
            args += list(carried)
    return pl.pallas_call(
        functools.partial(_attn_kernel, tq=tq, t_new=t_new, t_past=t_past, latent=latent,
                          fill_layer=layer if not latent and carried is None else None),
        grid=(batch, nq),
        in_specs=in_specs,
        out_specs=out_specs,
        out_shape=out_shape,
        scratch_shapes=[pltpu.VMEM((N_KV_HEADS, t_keys, HEAD_DIM), BF16),
                        pltpu.VMEM((N_KV_HEADS, t_keys, LANES), BF16)],
        input_output_aliases=aliases,
        compiler_params=_cparams(2),
        name="attention_latent" if latent else "attention_context",
    )(*args)


def _running_max_sublanes(x, reverse):
    n = x.shape[0]
    row = lax.broadcasted_iota(jnp.int32, x.shape, 0)
    k = 1
    while k < n:
        if reverse:
            cand = jnp.where(row < n - k, pltpu.roll(x, n - k, axis=0), NEG_BIG)
        else:
            cand = jnp.where(row >= k, pltpu.roll(x, k, axis=0), NEG_BIG)
        x = jnp.maximum(x, cand)
        k *= 2
    return x


def _mlstm_kernel(*refs, t, chunk, has_state, emit_state, convert, n_carried, fill_layer):
    refs = list(refs)
    zm_ref, gt_ref, cw_ref, gb_ref, mg_ref, bd_ref, fu_ref, dft_c_ref, dft_ct_ref, dft_st_ref = refs[:10]
    refs = refs[10:]
    if has_state:
        c0_ref, n0_ref, m0_ref = refs[:3]
        refs = refs[3:]
    convert_in, refs = refs[:len(convert)], refs[len(convert):]
    refs = refs[n_carried:]
    o_ref, fo_ref = refs[:2]
    refs = refs[2:]
    if emit_state:
        cs_ref, ns_ref, ms_ref = refs[:3]
        refs = refs[3:]
    convert_out, refs = refs[:len(convert)], refs[len(convert):]
    hst_ref = refs[0]

    for (n_chunks, last_valid), src_ref, dst_ref in zip(convert, convert_in, convert_out):
        w = src_ref[...]
        if last_valid is not None:
            rc = w.shape[0]
            limit = jnp.where(pl.program_id(0) >= n_chunks - 1, last_valid, rc)
            w = jnp.where(lax.broadcasted_iota(jnp.int32, (rc, 1), 0) < limit, w, 0.0)
        dst_ref[...] = w.astype(BF16)

    width = M_HEADS * M_DK
    z = zm_ref[...]
    x = z[:, :2 * width]
    row = lax.broadcasted_iota(jnp.int32, (t, 1), 0)
    x_prev = jnp.where(row == 0, 0.0, pltpu.roll(x, 1, axis=0))
    x_next = jnp.where(row == t - 1, 0.0, pltpu.roll(x, t - 1, axis=0))
    cw = cw_ref[...]
    qk = _silu(x_prev * cw[0:1, :] + x * cw[1:2, :] + x_next * cw[2:3, :])
    q = qk[:, :width]
    k = qk[:, width:] * (M_DK ** -0.5)
    v = z[:, 2 * width:2 * width + M_WIDTH]
    om = z[:, 2 * width + M_WIDTH:]
    heads = [slice(M_DK * h, M_DK * (h + 1)) for h in range(M_HEADS)]
    q_rows = [q[:, hs].astype(BF16) for hs in heads]
    k_rows = [k[:, hs].astype(BF16) for hs in heads]
    qt = q.T.astype(BF16)
    vt = v.T
    ones_r = jnp.ones((LANES - M_DV, t), F32)
    vaug_t = [jnp.concatenate([vt[hs, :], ones_r], axis=0) for hs in heads]
    vaug_tb = [a.astype(BF16) for a in vaug_t]

    gates = gt_ref[...] + gb_ref[...]
    lane = lax.broadcasted_iota(jnp.int32, (1, GATE_PAD), 1)
    is_forget = (lane & M_HEADS) != 0
    gates = jnp.where(is_forget, _log_sigmoid(gates), gates)
    gates_t = gates.T[:N_GATES, :]

    ri = lax.broadcasted_iota(jnp.int32, (chunk, chunk), 0)
    ci = lax.broadcasted_iota(jnp.int32, (chunk, chunk), 1)
    row_le_col = ri <= ci
    row_ge_col = ri >= ci

    n_chunks = t // chunk
    g_all = jnp.concatenate([gates_t[:, chunk * c:chunk * (c + 1)] for c in range(n_chunks)], axis=0)
    i_all = pltpu.roll(g_all, M_HEADS, axis=0)
    rows = lax.broadcasted_iota(jnp.int32, (N_GATES * n_chunks, 1), 0)
    is_fwd_row = (rows & (2 * M_HEADS)) == 0
    cum_p = _mm_left_f32(g_all, row_le_col.astype(BF16))
    cum_s = _mm_left_f32(g_all, row_ge_col.astype(BF16))
    b_all = jnp.where(is_fwd_row, cum_p, cum_s)
    u_all = i_all - b_all
    u_cols = u_all.T
    cols = lax.broadcasted_iota(jnp.int32, (1, N_GATES * n_chunks), 1)
    pm_cols = jnp.where((cols & (2 * M_HEADS)) == 0, _running_max_sublanes(u_cols, reverse=False),
                        _running_max_sublanes(u_cols, reverse=True))
    pm_all = pm_cols.T
    u_cols2 = u_cols * LOG2_E
    pm_all2 = pm_all * LOG2_E
    pml_all = jnp.where(is_fwd_row, pm_all[:, chunk - 1:], pm_all[:, :1])
    bl_all = jnp.where(is_fwd_row, b_all[:, chunk - 1:], b_all[:, :1])
    wk_all = jnp.exp(u_all - pml_all)

    if has_state:
        m_state = [m0_ref[M_HEADS * d:M_HEADS * (d + 1), :] for d in range(2)]
        c_state = [[jnp.concatenate(
            [c0_ref[u].T, jnp.broadcast_to(n0_ref[u:u + 1, :], (LANES - M_DV, M_DK))], axis=0)
            for u in range(M_HEADS * d, M_HEADS * (d + 1))] for d in range(2)]
    else:
        m_state = [jnp.zeros((M_HEADS, 1), F32) for _ in range(2)]
        c_state = [[jnp.zeros((LANES, M_DK), F32) for _ in range(M_HEADS)] for _ in range(2)]

    steps = [(j if d == 0 else n_chunks - 1 - j, d) for j in range(n_chunks) for d in range(2)]

    def gate_rows(step):
        c, direction = step
        return N_GATES * c + (2 * direction + 1) * M_HEADS

    def key_query(step):
        rs = slice(chunk * step[0], chunk * (step[0] + 1))
        return [_mm_nt(k_rows[h][rs, :], q_rows[h][rs, :]) for h in range(M_HEADS)]

    def local_sums(step, st):
        c, direction = step
        rs = slice(chunk * c, chunk * (c + 1))
        r0 = gate_rows(step)
        valid = row_le_col if direction == 0 else row_ge_col
        s_loc, x_loc = [], []
        for h in range(M_HEADS):
            w = jnp.exp2(jnp.where(valid, u_cols2[:, r0 + h:r0 + h + 1] - pm_all2[r0 + h:r0 + h + 1, :], NEG_BIG))
            s_loc.append(_mm(vaug_tb[h][:, rs], (st[h] * w).astype(BF16)))
            x_loc.append(_mm((vaug_t[h][:, rs] * wk_all[r0 + h:r0 + h + 1, :]).astype(BF16), k_rows[h][rs, :]))
        return s_loc, x_loc

    written = set()
    n_steps = len(steps)
    st_q = [key_query(steps[i]) for i in range(min(2, n_steps))]
    loc_q = [local_sums(steps[0], st_q.pop(0))]
    for i, (c, direction) in enumerate(steps):
        if i + 1 < n_steps:
            loc_q.append(local_sums(steps[i + 1], st_q.pop(0)))
        if i + 2 < n_steps:
            st_q.append(key_query(steps[i + 2]))
        s_loc, x_loc = loc_q.pop(0)
        rs = slice(chunk * c, chunk * (c + 1))
        r0 = gate_rows((c, direction))
        b4, pm4, bl, pml = (a[r0:r0 + M_HEADS, :] for a in (b_all, pm_all, bl_all, pml_all))
        m = m_state[direction]
        mx = jnp.maximum(m, pm4)
        f_loc = jnp.exp(pm4 - mx)
        f_int = jnp.exp(m - mx)
        floor = jnp.exp(-(b4 + mx))
        m_new = bl + jnp.maximum(m, pml)
        decay = jnp.exp(bl + m - m_new)
        gain = jnp.exp(bl + pml - m_new)
        m_state[direction] = m_new
        inter = [_mm(c_state[direction][h].astype(BF16), qt[heads[h], rs]) for h in range(M_HEADS)]
        for h in range(M_HEADS):
            nd = f_loc[h:h + 1, :] * s_loc[h] + f_int[h:h + 1, :] * inter[h]
            ht = nd[:M_DV, :] / jnp.maximum(jnp.abs(nd[M_DV:, :]), floor[h:h + 1, :])
            c_state[direction][h] = decay[h:h + 1, :] * c_state[direction][h] + gain[h:h + 1, :] * x_loc[h]
            if c in written:
                hst_ref[heads[h], rs] += ht
            else:
                hst_ref[heads[h], rs] = ht
        written.add(c)

    if emit_state:
        if fill_layer is None:
            cs_out, ns_out, ms_out = cs_ref, ns_ref, ms_ref
        else:
            for l2 in range(DEPTH):
                if l2 != fill_layer:
                    cs_ref[l2] = jnp.zeros(cs_ref.shape[1:], F32)
                    ns_ref[l2] = jnp.zeros(ns_ref.shape[1:], F32)
                    ms_ref[l2] = jnp.zeros(ms_ref.shape[1:], F32)
            cs_out, ns_out, ms_out = cs_ref.at[fill_layer], ns_ref.at[fill_layer], ms_ref.at[fill_layer]
        for d in range(2):
            ms_out[M_HEADS * d:M_HEADS * (d + 1), :] = jnp.broadcast_to(m_state[d], (M_HEADS, LANES))
            for h in range(M_HEADS):
                u = M_HEADS * d + h
                cs_out[u] = c_state[d][h][:M_DV, :].T
                ns_out[u:u + 1, :] = c_state[d][h][M_DV:M_DV + 1, :]

    hsum = hst_ref[...].T
    hn = hsum * _group_inv_rms(hsum, bd_ref[...]) * mg_ref[...]
    o_ref[...] = (_sigmoid(om) * hn).astype(BF16)

    fa = _mm(fu_ref[...].astype(BF16), dft_c_ref[...])
    fo_ref[...] = (_mm(dft_ct_ref[...], fa[:, :F_WIDTH].astype(BF16))
                   - _mm(dft_st_ref[...], fa[:, F_WIDTH:].astype(BF16))).astype(BF16)


def _mlstm_fourier(zm, gt, fu, conv_w, gate_b, m_norm_g, *, batch, t, layer, row0, state0=None,
                   convert=(), carried=None):
    has_state = state0 is not None
    emit_state = not has_state
    chunk = min(M_CHUNK_K, t)
    n_units = 2 * M_HEADS
    const = lambda shape: pl.BlockSpec(shape, lambda b: (0,) * len(shape), pipeline_mode=pl.Buffered(1))
    s0 = row0 // t
    in_specs = [
        pl.BlockSpec((t, D_MODEL), lambda b: (s0 + b, 0)),
        pl.BlockSpec((t, GATE_PAD), lambda b: (s0 + b, 0)),
        _layer_spec((3, 2 * M_HEADS * M_DK), layer),
        _layer_spec((1, GATE_PAD), layer),
        _layer_spec((1, M_WIDTH), layer),
        const((M_WIDTH, M_WIDTH)),
        pl.BlockSpec((t, F_WIDTH), lambda b: (s0 + b, 0)),
        const((F_WIDTH, 2 * F_WIDTH)), const((t, t)), const((t, t)),
    ]
    args = [zm, gt, conv_w, gate_b, m_norm_g, _block_diag_ones(M_WIDTH), fu, *_dft_tables(t)]
    if has_state:
        in_specs += [pl.BlockSpec((None, None, n_units, M_DK, M_DV), lambda b: (b, layer, 0, 0, 0)),
                     pl.BlockSpec((None, None, n_units, M_DK), lambda b: (b, layer, 0, 0)),
                     pl.BlockSpec((None, None, n_units, 1), lambda b: (b, layer, 0, 0))]
        args += list(state0)
    out_specs = [pl.BlockSpec((t, M_WIDTH), lambda b: (b, 0)), pl.BlockSpec((t, F_WIDTH), lambda b: (b, 0))]
    out_shape = [jax.ShapeDtypeStruct((batch * t, M_WIDTH), BF16), jax.ShapeDtypeStruct((batch * t, F_WIDTH), BF16)]
    if emit_state:
        if carried is None:
            out_specs += [pl.BlockSpec((None, DEPTH, n_units, M_DK, M_DV), lambda b: (b, 0, 0, 0, 0)),
                          pl.BlockSpec((None, DEPTH, n_units, M_DK), lambda b: (b, 0, 0, 0)),
                          pl.BlockSpec((None, DEPTH, n_units, LANES), lambda b: (b, 0, 0, 0))]
        else:
            out_specs += [pl.BlockSpec((None, None, n_units, M_DK, M_DV), lambda b: (b, layer, 0, 0, 0)),
                          pl.BlockSpec((None, None, n_units, M_DK), lambda b: (b, layer, 0, 0)),
                          pl.BlockSpec((None, None, n_units, LANES), lambda b: (b, layer, 0, 0))]
        out_shape += [jax.ShapeDtypeStruct((batch, DEPTH, n_units, M_DK, M_DV), F32),
                      jax.ShapeDtypeStruct((batch, DEPTH, n_units, M_DK), F32),
                      jax.ShapeDtypeStruct((batch, DEPTH, n_units, LANES), F32)]
    aliases = {}
    for w, rc, first, n_chunks, _ in convert:
        assert n_chunks <= batch
        w_cols = w.shape[2]
        in_specs.append(pl.BlockSpec(
            (None, rc, w_cols), lambda b, first=first, n=n_chunks: (layer, first + jnp.minimum(b, n - 1), 0)))
        args.append(w)
        out_specs.append(pl.BlockSpec((rc, w_cols), lambda b, n=n_chunks: (jnp.minimum(b, n - 1), 0)))
        out_shape.append(jax.ShapeDtypeStruct((rc * n_chunks, w_cols), BF16))
    if carried is not None:
        aliases = {len(args) + j: 2 + j for j in range(3)}
        in_specs += [pl.BlockSpec(memory_space=pl.ANY)] * 3
        args += list(carried)
    return pl.pallas_call(
        functools.partial(_mlstm_kernel, t=t, chunk=chunk, has_state=has_state, emit_state=emit_state,
                          convert=tuple((n, last) for _, _, _, n, last in convert),
                          n_carried=0 if carried is None else 3,
                          fill_layer=layer if emit_state and carried is None else None),
        grid=(batch,),
        in_specs=in_specs,
        out_specs=out_specs,
        out_shape=out_shape,
        scratch_shapes=[pltpu.VMEM((M_WIDTH, t), F32)],
        input_output_aliases=aliases,
        compiler_params=_cparams(1),
        name="mlstm_fourier_latent" if has_state else "mlstm_fourier_context",
    )(*args)


def _dft_tables(t):
    kt = (np.arange(t)[:, None] * np.arange(t)[None, :]) % t
    ang_t = 2.0 * np.pi * kt.astype(np.float64) / t
    ct = np.cos(ang_t) / np.sqrt(t)
    st = np.sin(ang_t) / np.sqrt(t)
    c = F_GROUP_CH
    kc = (np.arange(c)[:, None] * np.arange(c)[None, :]) % c
    ang_c = 2.0 * np.pi * kc.astype(np.float64) / c
    eye = np.eye(F_GROUPS)
    cc = np.kron(eye, np.cos(ang_c) / np.sqrt(c))
    sc = np.kron(eye, np.sin(ang_c) / np.sqrt(c))
    cs = np.concatenate([cc, sc], axis=1)
    to_dev = lambda a: jnp.asarray(a.astype(np.float32)).astype(BF16)
    return to_dev(cs), to_dev(ct), to_dev(st)


def _merge_kernel(*refs, rows, split_x):
    refs = list(refs)
    x_refs = [refs.pop(0) for _ in range(2 if split_x else 1)]
    (mod_ref, g_ref, ac_ref, al_ref, hc_ref, hl_ref, fc_ref, fl_ref, wbg_ref,
     wpa_ref, wpm_ref, wpf_ref, wo_ref, o_ref) = refs
    x = _pick(rows, *x_refs) if split_x else x_refs[0][...]
    h = _normmod(x, g_ref[...], mod_ref[1:2, :], mod_ref[0:1, :]).astype(BF16)

    def branch(j, yc_ref, yl_ref, w_ref):
        gate = _sigmoid(_mm_nt(h, wbg_ref[BG_SKIP + D_MODEL * j:BG_SKIP + D_MODEL * (j + 1), :]))
        return gate * _mm(_pick(rows, yc_ref, yl_ref), w_ref[...])

    merged = (branch(0, ac_ref, al_ref, wpa_ref) + branch(1, hc_ref, hl_ref, wpm_ref)
              + branch(2, fc_ref, fl_ref, wpf_ref))
    o_ref[...] = x + mod_ref[2:3, :] * _mm(merged.astype(BF16), wo_ref[...])


def _merge(xs, mod, g, attn, hm, fo, weights, rows, layer):
    split_x = len(xs) == 2
    n_rows = sum(x.shape[0] for x in xs)
    x_specs = [rows.spec(D_MODEL, "ctx"), rows.spec(D_MODEL, "lat")] if split_x else [rows.spec(D_MODEL, "both")]
    pair = lambda width: [rows.spec(width, "ctx"), rows.spec(width, "lat")]
    resident = lambda w: pl.BlockSpec(w.shape, lambda i: (0, 0), pipeline_mode=pl.Buffered(1))
    return pl.pallas_call(
        functools.partial(_merge_kernel, rows=rows, split_x=split_x),
        grid=(n_rows // rows.tm,),
        in_specs=x_specs + [rows.mod_spec(layer), _layer_spec((1, D_MODEL), layer)]
        + pair(ATTN_WIDTH) + pair(M_WIDTH) + pair(F_WIDTH) + [resident(w) for w in weights],
        out_specs=rows.spec(D_MODEL, "both"),
        out_shape=jax.ShapeDtypeStruct((n_rows, D_MODEL), F32),
        compiler_params=_cparams(1),
        name="merge",
    )(*xs, mod, g, *attn, *hm, *fo, *weights)


def _ffn_kernel(*refs, rows, split_out):
    x_ref, mod_ref, g_ref, win_ref, wout_ref = refs[:5]
    out_refs = refs[5:]
    x = x_ref[...]
    h = _normmod(x, g_ref[...], mod_ref[4:5, :], mod_ref[3:4, :]).astype(BF16)
    u = _mm(h, win_ref[...])
    a = (_silu(u[:, :FF_HIDDEN]) * u[:, FF_HIDDEN:]).astype(BF16)
    y = x + mod_ref[5:6, :] * _mm(a, wout_ref[...])
    if split_out:
        @pl.when(rows.is_ctx())
        def _():
            out_refs[0][...] = y

        @pl.when(jnp.logical_not(rows.is_ctx()))
        def _():
            out_refs[1][...] = y
    else:
        out_refs[0][...] = y


def _ffn(x, mod, g, w_in, w_out, rows, layer, split_out):
    n_rows = x.shape[0]
    n_ctx_rows = rows.n_ctx * rows.tm
    resident = lambda w: pl.BlockSpec(w.shape, lambda i: (0, 0), pipeline_mode=pl.Buffered(1))
    if split_out:
        out_specs = [rows.spec(D_MODEL, "ctx"), rows.spec(D_MODEL, "lat")]
        out_shape = [jax.ShapeDtypeStruct((n_ctx_rows, D_MODEL), F32),
                     jax.ShapeDtypeStruct((n_rows - n_ctx_rows, D_MODEL), F32)]
    else:
        out_specs = [rows.spec(D_MODEL, "both")]
        out_shape = [jax.ShapeDtypeStruct((n_rows, D_MODEL), F32)]
    return pl.pallas_call(
        functools.partial(_ffn_kernel, rows=rows, split_out=split_out),
        grid=(n_rows // rows.tm,),
        in_specs=[rows.spec(D_MODEL, "both"), rows.mod_spec(layer), _layer_spec((1, D_MODEL), layer),
                  resident(w_in), resident(w_out)],
        out_specs=out_specs,
        out_shape=out_shape,
        compiler_params=_cparams(1),
        name="ffn",
    )(x, mod, g, w_in, w_out)


def _rope_tables(t):
    n = HEAD_DIM // 4
    inv = 1.0 / (ROPE_THETA ** (np.arange(n, dtype=np.float64) / n))
    pos = np.arange(t)
    ang_r = (pos // GRID_W)[:, None] * inv[None, :]
    ang_c = (pos % GRID_W)[:, None] * inv[None, :]
    cos = np.concatenate([np.cos(ang_r)] * 2 + [np.cos(ang_c)] * 2, axis=1)
    sin = np.concatenate([-np.sin(ang_r), np.sin(ang_r), -np.sin(ang_c), np.sin(ang_c)], axis=1)
    tile = lambda a: jnp.asarray(np.tile(a, (1, N_Q_HEADS)).astype(np.float32))
    return tile(cos), tile(sin)


def kernel(x_prompt, x_sample, cache_k, cache_v, state_C, state_n, state_m, c, c_ctx, w_ada, b_ada,
           norm1_g, w_in, q_norm_g, k_norm_g, m_conv_w, m_gate_b, m_norm_g, w_proj_attn,
           w_proj_mlstm, w_proj_fourier, w_out, norm2_g, w_ffn_in, w_ffn_out):
    n_ctx, t_ctx, _ = x_prompt.shape
    n_lat, t_lat, _ = x_sample.shape
    t_past = cache_k.shape[2]
    n_units = 2 * M_HEADS

    cvec = jnp.concatenate([c_ctx[None, :], c], axis=0)
    cvec = jnp.pad(cvec, ((0, MOD_ROWS - cvec.shape[0]), (0, 0)))
    mod = _modulation(cvec, w_ada, b_ada).reshape(DEPTH, MOD_ROWS, N_MOD, D_MODEL)

    rope_tabs = _rope_tables(t_lat)
    to_rows = lambda a: jnp.transpose(a, (0, 1, 3, 4, 2)).reshape(n_lat, DEPTH, KV_WIDTH, t_past)
    ck, cv = to_rows(cache_k), to_rows(cache_v)
    rows_ctx = n_ctx * t_ctx
    rows_in = _Rows(rows_ctx, t_lat, n_stage=MIX_CHUNKS, tm=TM_IN)
    rows_merge = _Rows(rows_ctx, t_lat)
    rows_ffn = _Rows(rows_ctx, t_lat)

    w_in_t = jnp.swapaxes(w_in, 1, 2)
    row = lambda a: a[:, None, :]
    norm1, norm2 = row(norm1_g), row(norm2_g)
    qg, kg = row(jnp.tile(q_norm_g, (1, N_Q_HEADS))), row(jnp.tile(k_norm_g, (1, N_KV_HEADS)))
    gate_b = row(jnp.pad(m_gate_b, ((0, 0), (0, GATE_PAD - N_GATES))))
    m_norm = row(m_norm_g)

    per_step = lambda w, n: (w, w.shape[1] // n, 0, n, None)
    xs = (x_prompt.reshape(rows_ctx, D_MODEL), x_sample.reshape(n_lat * t_lat, D_MODEL))
    kv_carried = states_carried = None
    state0 = (state_C.astype(F32).reshape(n_lat, DEPTH, n_units, M_DK, M_DV),
              state_n.astype(F32).reshape(n_lat, DEPTH, n_units, M_DK),
              state_m.astype(F32).reshape(n_lat, DEPTH, n_units, 1))
    for l in range(DEPTH):
        za, zm, fu, gt = _in_proj(xs, mod, norm1, w_in_t, rows_in, l)
        attn_c, *kv_carried = _attention(za, qg, kg, batch=n_ctx, t_new=t_ctx, latent=False, row0=0, layer=l,
                                         carried=kv_carried)
        hm_c, fo_c, *rest = _mlstm_fourier(
            zm, gt, fu, m_conv_w, gate_b, m_norm, batch=n_ctx, t=t_ctx, layer=l, row0=0, carried=states_carried,
            convert=((w_in_t, W_IN_ROWS, BG_FIRST, BG_CHUNKS, BG_LAST),)
            + tuple(per_step(w, n_ctx) for w in (w_proj_attn, w_proj_mlstm, w_proj_fourier, w_out)))
        states_carried, merge_w = rest[:3], rest[3:]

        (attn_l,) = _attention(za, qg, kg, batch=n_lat, t_new=t_lat, latent=True, row0=rows_ctx,
                               cache_k=ck, cache_v=cv, layer=l, rope_tabs=rope_tabs)
        hm_l, fo_l, w_ffn_in_b, w_ffn_out_b = _mlstm_fourier(
            zm, gt, fu, m_conv_w, gate_b, m_norm, batch=n_lat, t=t_lat, layer=l, row0=rows_ctx,
            state0=state0, convert=(per_step(w_ffn_in, n_lat), per_step(w_ffn_out, n_lat)))

        x1 = _merge(xs, mod, norm1, (attn_c, attn_l), (hm_c, hm_l), (fo_c, fo_l), merge_w, rows_merge, l)
        xs = tuple(_ffn(x1, mod, norm2, w_ffn_in_b, w_ffn_out_b, rows_ffn, l, split_out=l == DEPTH - 1))
    xp, xs = xs

    kv_out = lambda a: jnp.transpose(a.reshape(n_ctx, DEPTH, N_KV_HEADS, HEAD_DIM, t_ctx), (0, 1, 4, 2, 3))
    cs, ns, ms = states_carried
    return (xp.reshape(n_ctx, t_ctx, D_MODEL), xs.reshape(n_lat, t_lat, D_MODEL),
            kv_out(kv_carried[0]), kv_out(kv_carried[1]),
            cs.reshape(n_ctx, DEPTH, 2, M_HEADS, M_DK, M_DV), ns.reshape(n_ctx, DEPTH, 2, M_HEADS, M_DK),
            ms[..., 0].reshape(n_ctx, DEPTH, 2, M_HEADS))
```

```python
import functools

import numpy as np
import jax
import jax.numpy as jnp
from jax import lax
from jax.experimental import pallas as pl
from jax.experimental.pallas import tpu as pltpu

D_MODEL = 1024
DEPTH = 2
GRID_W = 64
HEAD_DIM = 64
N_Q_HEADS = 8
N_KV_HEADS = 4
ATTN_WIDTH = N_Q_HEADS * HEAD_DIM
KV_WIDTH = N_KV_HEADS * HEAD_DIM
ROPE_THETA = 10000.0
M_HEADS = 4
M_DK = 64
M_DV = 64
M_WIDTH = M_HEADS * M_DV
F_GROUPS = 4
F_GROUP_CH = 64
F_WIDTH = F_GROUPS * F_GROUP_CH
FF_HIDDEN = -(-8 * D_MODEL // (3 * 256)) * 256
EPS = 1e-6
N_GATES = 4 * M_HEADS
IN_SIZES = (ATTN_WIDTH, KV_WIDTH, KV_WIDTH, M_HEADS * M_DK, M_HEADS * M_DK, M_WIDTH, M_WIDTH,
            N_GATES, F_WIDTH, 3 * D_MODEL)
IN_OFFS = tuple(int(v) for v in np.cumsum((0,) + IN_SIZES))

LANES = 128
GATE_PAD = LANES
N_MOD = 6
MOD_ROWS = 16
M_CHUNK_K = 128
NEG_BIG = -1e30
LOG2_E = 1.4426950408889634
VMEM_LIMIT = 56 * 1024 * 1024
TM = 512
TM_IN = 1024
TQ = 1024
TN_MOD = 1024
FF_SPANS = ((0, 1536), (1536, FF_HIDDEN))
W_IN_ROWS = 256

F32 = jnp.float32
BF16 = jnp.bfloat16


def _cparams(n_axes):
    return pltpu.CompilerParams(dimension_semantics=("arbitrary",) * n_axes,
                                vmem_limit_bytes=VMEM_LIMIT)


def _layer_spec(shape, layer):
    return pl.BlockSpec((None,) + tuple(shape), lambda *_: (layer,) + (0,) * len(shape),
                        pipeline_mode=pl.Buffered(1))


class _Rows:
    def __init__(self, rows_ctx, t_lat, n_stage=0, tm=TM):
        self.tm = tm
        self.n_ctx = rows_ctx // tm
        self.per_seq = t_lat // tm
        self.n_stage = n_stage

    def tile(self, i):
        return jnp.maximum(i - self.n_stage, 0)

    def both(self, i):
        return (self.tile(i), 0)

    def ctx(self, i):
        return (jnp.minimum(self.tile(i), self.n_ctx - 1), 0)

    def lat(self, i):
        return (jnp.maximum(self.tile(i) - self.n_ctx, 0), 0)

    def mod_spec(self, layer):
        def index(i):
            t = self.tile(i)
            return (layer, jnp.where(t < self.n_ctx, 0, 1 + (t - self.n_ctx) // self.per_seq), 0, 0)
        return pl.BlockSpec((None, None, N_MOD, D_MODEL), index)

    def spec(self, width, which):
        return pl.BlockSpec((self.tm, width), getattr(self, which))

    def is_ctx(self):
        return pl.program_id(0) - self.n_stage < self.n_ctx


def _pick(rows, ctx_ref, lat_ref):
    return jnp.where(rows.is_ctx(), ctx_ref[...], lat_ref[...])


def _staged_spec(shape, layer, n_chunks):
    rc, cols = shape[0] // n_chunks, shape[1]
    return pl.BlockSpec((None, rc, cols), lambda i: (layer, jnp.minimum(i, n_chunks - 1), 0))


def _stage(w_ref, scratch_ref, n_chunks):
    i = pl.program_id(0)
    rc = w_ref.shape[0]

    @pl.when(i < n_chunks)
    def _():
        scratch_ref[pl.ds(pl.multiple_of(i * rc, rc), rc), :] = w_ref[...].astype(BF16)


def _mm(a, b):
    return jnp.dot(a, b, preferred_element_type=F32)


def _mm_nt(a, b):
    return lax.dot_general(a, b, (((1,), (1,)), ((), ())), preferred_element_type=F32)


def _split3(x):
    hi = x.astype(BF16)
    r = x - hi.astype(F32)
    mid = r.astype(BF16)
    lo = (r - mid.astype(F32)).astype(BF16)
    return hi, mid, lo


def _mm_left_f32(x, m_bf16):
    hi, mid, lo = _split3(x)
    return _mm(hi, m_bf16) + _mm(mid, m_bf16) + _mm(lo, m_bf16)


def _sigmoid(x):
    return 0.5 * jnp.tanh(0.5 * x) + 0.5


def _silu(x):
    hx = 0.5 * x
    return hx * jnp.tanh(hx) + hx


def _log_sigmoid(x):
    return jnp.minimum(x, 0.0) - jnp.log(1.0 + jnp.exp(-jnp.abs(x)))


def _normmod(x, g, scale, shift):
    ms = jnp.mean(x * x, axis=-1, keepdims=True)
    return (x * lax.rsqrt(ms + EPS)) * g * (1.0 + scale) + shift


def _group_inv_rms(x, bd):
    x2 = x * x
    hi = x2.astype(BF16)
    lo = (x2 - hi.astype(F32)).astype(BF16)
    ss = _mm(hi, bd) + _mm(lo, bd)
    return lax.rsqrt(ss * (1.0 / HEAD_DIM) + EPS)


def _rope(x, cos, sin_signed):
    w = x.shape[1]
    lane = lax.broadcasted_iota(jnp.int32, x.shape, 1)
    up = pltpu.roll(x, w - 16, axis=1)
    dn = pltpu.roll(x, 16, axis=1)
    partner = jnp.where((lane & 31) < 16, up, dn)
    return x * cos + partner * sin_signed


def _mod_kernel(c_ref, w_ref, b_ref, o_ref):
    s = _silu(c_ref[...]).astype(BF16)
    o_ref[...] = _mm(s, w_ref[...].astype(BF16)) + b_ref[...]


def _modulation(cvec, w_ada, b_ada):
    tn = TN_MOD
    n_out = N_MOD * D_MODEL
    return pl.pallas_call(
        _mod_kernel,
        grid=(DEPTH, n_out // tn),
        in_specs=[
            pl.BlockSpec((MOD_ROWS, D_MODEL), lambda l, j: (0, 0)),
            pl.BlockSpec((None, D_MODEL, tn), lambda l, j: (l, 0, j)),
            pl.BlockSpec((None, 1, tn), lambda l, j: (l, 0, j)),
        ],
        out_specs=pl.BlockSpec((None, MOD_ROWS, tn), lambda l, j: (l, 0, j)),
        out_shape=jax.ShapeDtypeStruct((DEPTH, MOD_ROWS, n_out), F32),
        compiler_params=_cparams(2),
        name="modulation",
    )(cvec, w_ada, b_ada.reshape(DEPTH, 1, n_out))


GATE_OFF, FU_OFF, BG_OFF = IN_OFFS[7], IN_OFFS[8], IN_OFFS[9]
IN_COLS = IN_OFFS[-1]
MIX_CHUNKS = -(-BG_OFF // W_IN_ROWS)
BG_FIRST = BG_OFF // W_IN_ROWS
BG_CHUNKS = -(-IN_COLS // W_IN_ROWS) - BG_FIRST
BG_SKIP = BG_OFF - BG_FIRST * W_IN_ROWS
BG_LAST = IN_COLS - (BG_FIRST + BG_CHUNKS - 1) * W_IN_ROWS


def _in_kernel(*refs, rows, split_x):
    refs = list(refs)
    x_refs = [refs.pop(0) for _ in range(2 if split_x else 1)]
    mod_ref, g_ref, w_ref, za_ref, zm_ref, fu_ref, gt_ref, w_s = refs
    _stage(w_ref, w_s, MIX_CHUNKS)

    @pl.when(pl.program_id(0) >= MIX_CHUNKS)
    def _():
        x = _pick(rows, *x_refs) if split_x else x_refs[0][...]
        h = _normmod(x, g_ref[...], mod_ref[1:2, :], mod_ref[0:1, :])
        z = _mm_nt(h.astype(BF16), w_s[:BG_OFF, :])
        za_ref[...] = z[:, :D_MODEL]
        zm_ref[...] = z[:, D_MODEL:GATE_OFF]
        gt_ref[...] = z[:, GATE_OFF:GATE_OFF + GATE_PAD]
        fu_ref[...] = z[:, FU_OFF:FU_OFF + F_WIDTH].astype(BF16)


def _in_proj(xs, mod, g, w_in_t, rows, layer):
    split_x = len(xs) == 2
    n_rows = sum(x.shape[0] for x in xs)
    x_specs = [rows.spec(D_MODEL, "ctx"), rows.spec(D_MODEL, "lat")] if split_x else [rows.spec(D_MODEL, "both")]
    widths = (D_MODEL, D_MODEL, F_WIDTH, GATE_PAD)
    dtypes = (F32, F32, BF16, F32)
    return pl.pallas_call(
        functools.partial(_in_kernel, rows=rows, split_x=split_x),
        grid=(MIX_CHUNKS + n_rows // rows.tm,),
        in_specs=x_specs + [
            rows.mod_spec(layer),
            _layer_spec((1, D_MODEL), layer),
            _staged_spec((MIX_CHUNKS * W_IN_ROWS, D_MODEL), layer, MIX_CHUNKS),
        ],
        out_specs=[rows.spec(w, "both") for w in widths],
        out_shape=[jax.ShapeDtypeStruct((n_rows, w), d) for w, d in zip(widths, dtypes)],
        scratch_shapes=[pltpu.VMEM((MIX_CHUNKS * W_IN_ROWS, D_MODEL), BF16)],
        compiler_params=_cparams(1),
        name="in_proj",
    )(*xs, mod, g, w_in_t)


def _attn_kernel(*refs, tq, t_new, t_past, latent, fill_layer):
    if latent:
        (q_ref, k_ref, v_ref, ck_ref, cv_ref, cosq_ref, sinq_ref, cosk_ref, sink_ref,
         qg_ref, kg_ref, bdq_ref, bdk_ref, o_ref, ks_ref, vs_ref) = refs
    else:
        q_ref, k_ref, v_ref, qg_ref, kg_ref, bdq_ref, bdk_ref = refs[:7]
        o_ref, nk_ref, nv_ref, ks_ref, vs_ref = refs[-5:]

    @pl.when(pl.program_id(1) == 0)
    def _():
        k = k_ref[...]
        kn = k * _group_inv_rms(k, bdk_ref[...]) * kg_ref[...]
        v = v_ref[...]
        if latent:
            kn = _rope(kn, cosk_ref[...], sink_ref[...])
            ck = ck_ref[...].T
            cv = cv_ref[...].T
        else:
            if fill_layer is None:
                nk_ref[...] = kn.T
                nv_ref[...] = v.T
            else:
                for l2 in range(DEPTH):
                    nk_ref[l2] = kn.T if l2 == fill_layer else jnp.zeros((KV_WIDTH, t_new), F32)
                    nv_ref[l2] = v.T if l2 == fill_layer else jnp.zeros((KV_WIDTH, t_new), F32)
        for h in range(N_KV_HEADS):
            sl = slice(HEAD_DIM * h, HEAD_DIM * (h + 1))
            if latent:
                ks_ref[h, :t_past, :] = ck[:, sl].astype(BF16)
                vs_ref[h, :t_past, :HEAD_DIM] = cv[:, sl].astype(BF16)
            ks_ref[h, t_past:, :] = kn[:, sl].astype(BF16)
            vs_ref[h, t_past:, :HEAD_DIM] = v[:, sl].astype(BF16)
            vs_ref[h, :, HEAD_DIM:] = jnp.ones((t_past + t_new, LANES - HEAD_DIM), BF16)

    q = q_ref[...]
    qn = q * _group_inv_rms(q, bdq_ref[...]) * qg_ref[...]
    if latent:
        qn = _rope(qn, cosq_ref[...], sinq_ref[...])
    qn = qn * (HEAD_DIM ** -0.5 * LOG2_E)
    group = N_Q_HEADS // N_KV_HEADS
    outs = []

    def scores(h):
        qh = jnp.concatenate(
            [qn[:, HEAD_DIM * (group * h + g):HEAD_DIM * (group * h + g + 1)] for g in range(group)],
            axis=0).astype(BF16)
        return _mm_nt(qh, ks_ref[h])

    ahead = 1 if latent else N_KV_HEADS - 1
    s_q = [scores(h) for h in range(ahead)]
    for h in range(N_KV_HEADS):
        s = s_q.pop(0)
        if h + ahead < N_KV_HEADS:
            s_q.append(scores(h + ahead))
        m = jnp.max(s, axis=-1, keepdims=True)
        p = jnp.exp2(s - m).astype(BF16)
        od = _mm(p, vs_ref[h])
        o = (od / pltpu.roll(od, LANES - HEAD_DIM, axis=1))[:, :HEAD_DIM]
        outs.extend(o[tq * g:tq * (g + 1)] for g in range(group))
    o_ref[...] = jnp.concatenate(outs, axis=1).astype(BF16)


def _block_diag_ones(width):
    idx = np.arange(width) // HEAD_DIM
    return jnp.asarray(idx[:, None] == idx[None, :], dtype=BF16)


def _attention(za, qg, kg, *, batch, t_new, latent, row0, cache_k=None, cache_v=None, layer=0, rope_tabs=None,
               carried=None):
    tq = min(TQ, t_new)
    nq = t_new // tq
    t_past = cache_k.shape[3] if latent else 0
    t_keys = t_past + t_new
    kcol = ATTN_WIDTH // KV_WIDTH
    bdq = _block_diag_ones(ATTN_WIDTH)
    bdk = _block_diag_ones(KV_WIDTH)
    const = lambda shape: pl.BlockSpec(shape, lambda b, i: (0,) * len(shape))
    q0, s0 = row0 // tq, row0 // t_new
    in_specs = [
        pl.BlockSpec((tq, ATTN_WIDTH), lambda b, i: (q0 + b * nq + i, 0)),
        pl.BlockSpec((t_new, KV_WIDTH), lambda b, i: (s0 + b, kcol)),
        pl.BlockSpec((t_new, KV_WIDTH), lambda b, i: (s0 + b, kcol + 1)),
    ]
    args = [za, za, za]
    if latent:
        cos_t, sin_t = rope_tabs
        in_specs += [
            pl.BlockSpec((None, None, KV_WIDTH, t_past), lambda b, i: (b, layer, 0, 0)),
            pl.BlockSpec((None, None, KV_WIDTH, t_past), lambda b, i: (b, layer, 0, 0)),
            pl.BlockSpec((tq, ATTN_WIDTH), lambda b, i: (i, 0)),
            pl.BlockSpec((tq, ATTN_WIDTH), lambda b, i: (i, 0)),
            pl.BlockSpec((t_new, KV_WIDTH), lambda b, i: (0, 0)),
            pl.BlockSpec((t_new, KV_WIDTH), lambda b, i: (0, 0)),
        ]
        args += [cache_k, cache_v, cos_t, sin_t, cos_t, sin_t]
    in_specs += [_layer_spec((1, ATTN_WIDTH), layer), _layer_spec((1, KV_WIDTH), layer),
                 const((ATTN_WIDTH, ATTN_WIDTH)), const((KV_WIDTH, KV_WIDTH))]
    args += [qg, kg, bdq, bdk]
    out_specs = [pl.BlockSpec((tq, ATTN_WIDTH), lambda b, i: (b * nq + i, 0))]
    out_shape = [jax.ShapeDtypeStruct((batch * t_new, ATTN_WIDTH), BF16)]
    aliases = {}
    if not latent:
        out_shape += [jax.ShapeDtypeStruct((batch, DEPTH, KV_WIDTH, t_new), F32)] * 2
        if carried is None:
            out_specs += [pl.BlockSpec((None, DEPTH, KV_WIDTH, t_new), lambda b, i: (b, 0, 0, 0))] * 2
        else:
            out_specs += [pl.BlockSpec((None, None, KV_WIDTH, t_new), lambda b, i: (b, layer, 0, 0))] * 2
            aliases = {len(args) + j: 1 + j for j in range(2)}
            in_specs += [pl.BlockSpec(memory_space=pl.ANY)] * 2
            args += list(carried)
    return pl.pallas_call(
        functools.partial(_attn_kernel, tq=tq, t_new=t_new, t_past=t_past, latent=latent,
                          fill_layer=layer if not latent and carried is None else None),
        grid=(batch, nq),
        in_specs=in_specs,
        out_specs=out_specs,
        out_shape=out_shape,
        scratch_shapes=[pltpu.VMEM((N_KV_HEADS, t_keys, HEAD_DIM), BF16),
                        pltpu.VMEM((N_KV_HEADS, t_keys, LANES), BF16)],
        input_output_aliases=aliases,
        compiler_params=_cparams(2),
        name="attention_latent" if latent else "attention_context",
    )(*args)


def _running_max_sublanes(x, reverse):
    n = x.shape[0]
    row = lax.broadcasted_iota(jnp.int32, x.shape, 0)
    k = 1
    while k < n:
        if reverse:
            cand = jnp.where(row < n - k, pltpu.roll(x, n - k, axis=0), NEG_BIG)
        else:
            cand = jnp.where(row >= k, pltpu.roll(x, k, axis=0), NEG_BIG)
        x = jnp.maximum(x, cand)
        k *= 2
    return x


def _mlstm_kernel(*refs, t, chunk, has_state, emit_state, convert, n_carried, fill_layer):
    refs = list(refs)
    zm_ref, gt_ref, cw_ref, gb_ref, mg_ref, bd_ref, fu_ref, dft_c_ref, dft_ct_ref, dft_st_ref = refs[:10]
    refs = refs[10:]
    if has_state:
        c0_ref, n0_ref, m0_ref = refs[:3]
        refs = refs[3:]
    convert_in, refs = refs[:len(convert)], refs[len(convert):]
    refs = refs[n_carried:]
    o_ref, fo_ref = refs[:2]
    refs = refs[2:]
    if emit_state:
        cs_ref, ns_ref, ms_ref = refs[:3]
        refs = refs[3:]
    convert_out, refs = refs[:len(convert)], refs[len(convert):]
    hst_ref = refs[0]

    for (n_chunks, last_valid), src_ref, dst_ref in zip(convert, convert_in, convert_out):
        w = src_ref[...]
        if last_valid is not None:
            rc = w.shape[0]
            limit = jnp.where(pl.program_id(0) >= n_chunks - 1, last_valid, rc)
            w = jnp.where(lax.broadcasted_iota(jnp.int32, (rc, 1), 0) < limit, w, 0.0)
        dst_ref[...] = w.astype(BF16)

    width = M_HEADS * M_DK
    z = zm_ref[...]
    x = z[:, :2 * width]
    row = lax.broadcasted_iota(jnp.int32, (t, 1), 0)
    x_prev = jnp.where(row == 0, 0.0, pltpu.roll(x, 1, axis=0))
    x_next = jnp.where(row == t - 1, 0.0, pltpu.roll(x, t - 1, axis=0))
    cw = cw_ref[...]
    qk = _silu(x_prev * cw[0:1, :] + x * cw[1:2, :] + x_next * cw[2:3, :])
    q = qk[:, :width]
    k = qk[:, width:] * (M_DK ** -0.5)
    v = z[:, 2 * width:2 * width + M_WIDTH]
    om = z[:, 2 * width + M_WIDTH:]
    heads = [slice(M_DK * h, M_DK * (h + 1)) for h in range(M_HEADS)]
    q_rows = [q[:, hs].astype(BF16) for hs in heads]
    k_rows = [k[:, hs].astype(BF16) for hs in heads]
    qt = q.T.astype(BF16)
    vt = v.T
    ones_r = jnp.ones((LANES - M_DV, t), F32)
    vaug_t = [jnp.concatenate([vt[hs, :], ones_r], axis=0) for hs in heads]
    vaug_tb = [a.astype(BF16) for a in vaug_t]

    gates = gt_ref[...] + gb_ref[...]
    lane = lax.broadcasted_iota(jnp.int32, (1, GATE_PAD), 1)
    is_forget = (lane & M_HEADS) != 0
    gates = jnp.where(is_forget, _log_sigmoid(gates), gates)
    gates_t = gates.T[:N_GATES, :]

    ri = lax.broadcasted_iota(jnp.int32, (chunk, chunk), 0)
    ci = lax.broadcasted_iota(jnp.int32, (chunk, chunk), 1)
    row_le_col = ri <= ci
    row_ge_col = ri >= ci

    n_chunks = t // chunk
    g_all = jnp.concatenate([gates_t[:, chunk * c:chunk * (c + 1)] for c in range(n_chunks)], axis=0)
    i_all = pltpu.roll(g_all, M_HEADS, axis=0)
    rows = lax.broadcasted_iota(jnp.int32, (N_GATES * n_chunks, 1), 0)
    is_fwd_row = (rows & (2 * M_HEADS)) == 0
    cum_p = _mm_left_f32(g_all, row_le_col.astype(BF16))
    cum_s = _mm_left_f32(g_all, row_ge_col.astype(BF16))
    b_all = jnp.where(is_fwd_row, cum_p, cum_s)
    u_all = i_all - b_all
    u_cols = u_all.T
    cols = lax.broadcasted_iota(jnp.int32, (1, N_GATES * n_chunks), 1)
    pm_cols = jnp.where((cols & (2 * M_HEADS)) == 0, _running_max_sublanes(u_cols, reverse=False),
                        _running_max_sublanes(u_cols, reverse=True))
    pm_all = pm_cols.T
    u_cols2 = u_cols * LOG2_E
    pm_all2 = pm_all * LOG2_E
    pml_all = jnp.where(is_fwd_row, pm_all[:, chunk - 1:], pm_all[:, :1])
    bl_all = jnp.where(is_fwd_row, b_all[:, chunk - 1:], b_all[:, :1])
    wk_all = jnp.exp(u_all - pml_all)

    if has_state:
        m_state = [m0_ref[M_HEADS * d:M_HEADS * (d + 1), :] for d in range(2)]
        c_state = [[jnp.concatenate(
            [c0_ref[u].T, jnp.broadcast_to(n0_ref[u:u + 1, :], (LANES - M_DV, M_DK))], axis=0)
            for u in range(M_HEADS * d, M_HEADS * (d + 1))] for d in range(2)]
    else:
        m_state = [jnp.zeros((M_HEADS, 1), F32) for _ in range(2)]
        c_state = [[jnp.zeros((LANES, M_DK), F32) for _ in range(M_HEADS)] for _ in range(2)]

    steps = [(j if d == 0 else n_chunks - 1 - j, d) for j in range(n_chunks) for d in range(2)]

    def gate_rows(step):
        c, direction = step
        return N_GATES * c + (2 * direction + 1) * M_HEADS

    def key_query(step):
        rs = slice(chunk * step[0], chunk * (step[0] + 1))
        return [_mm_nt(k_rows[h][rs, :], q_rows[h][rs, :]) for h in range(M_HEADS)]

    def local_sums(step, st):
        c, direction = step
        rs = slice(chunk * c, chunk * (c + 1))
        r0 = gate_rows(step)
        valid = row_le_col if direction == 0 else row_ge_col
        s_loc, x_loc = [], []
        for h in range(M_HEADS):
            w = jnp.exp2(jnp.where(valid, u_cols2[:, r0 + h:r0 + h + 1] - pm_all2[r0 + h:r0 + h + 1, :], NEG_BIG))
            s_loc.append(_mm(vaug_tb[h][:, rs], (st[h] * w).astype(BF16)))
            x_loc.append(_mm((vaug_t[h][:, rs] * wk_all[r0 + h:r0 + h + 1, :]).astype(BF16), k_rows[h][rs, :]))
        return s_loc, x_loc

    written = set()
    n_steps = len(steps)
    st_q = [key_query(steps[i]) for i in range(min(2, n_steps))]
    loc_q = [local_sums(steps[0], st_q.pop(0))]
    for i, (c, direction) in enumerate(steps):
        if i + 1 < n_steps:
            loc_q.append(local_sums(steps[i + 1], st_q.pop(0)))
        if i + 2 < n_steps:
            st_q.append(key_query(steps[i + 2]))
        s_loc, x_loc = loc_q.pop(0)
        rs = slice(chunk * c, chunk * (c + 1))
        r0 = gate_rows((c, direction))
        b4, pm4, bl, pml = (a[r0:r0 + M_HEADS, :] for a in (b_all, pm_all, bl_all, pml_all))
        m = m_state[direction]
        mx = jnp.maximum(m, pm4)
        f_loc = jnp.exp(pm4 - mx)
        f_int = jnp.exp(m - mx)
        floor = jnp.exp(-(b4 + mx))
        m_new = bl + jnp.maximum(m, pml)
        decay = jnp.exp(bl + m - m_new)
        gain = jnp.exp(bl + pml - m_new)
        m_state[direction] = m_new
        inter = [_mm(c_state[direction][h].astype(BF16), qt[heads[h], rs]) for h in range(M_HEADS)]
        for h in range(M_HEADS):
            nd = f_loc[h:h + 1, :] * s_loc[h] + f_int[h:h + 1, :] * inter[h]
            ht = nd[:M_DV, :] / jnp.maximum(jnp.abs(nd[M_DV:, :]), floor[h:h + 1, :])
            c_state[direction][h] = decay[h:h + 1, :] * c_state[direction][h] + gain[h:h + 1, :] * x_loc[h]
            if c in written:
                hst_ref[heads[h], rs] += ht
            else:
                hst_ref[heads[h], rs] = ht
        written.add(c)

    if emit_state:
        if fill_layer is None:
            cs_out, ns_out, ms_out = cs_ref, ns_ref, ms_ref
        else:
            for l2 in range(DEPTH):
                if l2 != fill_layer:
                    cs_ref[l2] = jnp.zeros(cs_ref.shape[1:], F32)
                    ns_ref[l2] = jnp.zeros(ns_ref.shape[1:], F32)
                    ms_ref[l2] = jnp.zeros(ms_ref.shape[1:], F32)
            cs_out, ns_out, ms_out = cs_ref.at[fill_layer], ns_ref.at[fill_layer], ms_ref.at[fill_layer]
        for d in range(2):
            ms_out[M_HEADS * d:M_HEADS * (d + 1), :] = jnp.broadcast_to(m_state[d], (M_HEADS, LANES))
            for h in range(M_HEADS):
                u = M_HEADS * d + h
                cs_out[u] = c_state[d][h][:M_DV, :].T
                ns_out[u:u + 1, :] = c_state[d][h][M_DV:M_DV + 1, :]

    hsum = hst_ref[...].T
    hn = hsum * _group_inv_rms(hsum, bd_ref[...]) * mg_ref[...]
    o_ref[...] = (_sigmoid(om) * hn).astype(BF16)

    fa = _mm(fu_ref[...].astype(BF16), dft_c_ref[...])
    fo_ref[...] = (_mm(dft_ct_ref[...], fa[:, :F_WIDTH].astype(BF16))
                   - _mm(dft_st_ref[...], fa[:, F_WIDTH:].astype(BF16))).astype(BF16)


def _mlstm_fourier(zm, gt, fu, conv_w, gate_b, m_norm_g, *, batch, t, layer, row0, state0=None,
                   convert=(), carried=None):
    has_state = state0 is not None
    emit_state = not has_state
    chunk = min(M_CHUNK_K, t)
    n_units = 2 * M_HEADS
    const = lambda shape: pl.BlockSpec(shape, lambda b: (0,) * len(shape), pipeline_mode=pl.Buffered(1))
    s0 = row0 // t
    in_specs = [
        pl.BlockSpec((t, D_MODEL), lambda b: (s0 + b, 0)),
        pl.BlockSpec((t, GATE_PAD), lambda b: (s0 + b, 0)),
        _layer_spec((3, 2 * M_HEADS * M_DK), layer),
        _layer_spec((1, GATE_PAD), layer),
        _layer_spec((1, M_WIDTH), layer),
        const((M_WIDTH, M_WIDTH)),
        pl.BlockSpec((t, F_WIDTH), lambda b: (s0 + b, 0)),
        const((F_WIDTH, 2 * F_WIDTH)), const((t, t)), const((t, t)),
    ]
    args = [zm, gt, conv_w, gate_b, m_norm_g, _block_diag_ones(M_WIDTH), fu, *_dft_tables(t)]
    if has_state:
        in_specs += [pl.BlockSpec((None, None, n_units, M_DK, M_DV), lambda b: (b, layer, 0, 0, 0)),
                     pl.BlockSpec((None, None, n_units, M_DK), lambda b: (b, layer, 0, 0)),
                     pl.BlockSpec((None, None, n_units, 1), lambda b: (b, layer, 0, 0))]
        args += list(state0)
    out_specs = [pl.BlockSpec((t, M_WIDTH), lambda b: (b, 0)), pl.BlockSpec((t, F_WIDTH), lambda b: (b, 0))]
    out_shape = [jax.ShapeDtypeStruct((batch * t, M_WIDTH), BF16), jax.ShapeDtypeStruct((batch * t, F_WIDTH), BF16)]
    if emit_state:
        if carried is None:
            out_specs += [pl.BlockSpec((None, DEPTH, n_units, M_DK, M_DV), lambda b: (b, 0, 0, 0, 0)),
                          pl.BlockSpec((None, DEPTH, n_units, M_DK), lambda b: (b, 0, 0, 0)),
                          pl.BlockSpec((None, DEPTH, n_units, LANES), lambda b: (b, 0, 0, 0))]
        else:
            out_specs += [pl.BlockSpec((None, None, n_units, M_DK, M_DV), lambda b: (b, layer, 0, 0, 0)),
                          pl.BlockSpec((None, None, n_units, M_DK), lambda b: (b, layer, 0, 0)),
                          pl.BlockSpec((None, None, n_units, LANES), lambda b: (b, layer, 0, 0))]
        out_shape += [jax.ShapeDtypeStruct((batch, DEPTH, n_units, M_DK, M_DV), F32),
                      jax.ShapeDtypeStruct((batch, DEPTH, n_units, M_DK), F32),
                      jax.ShapeDtypeStruct((batch, DEPTH, n_units, LANES), F32)]
    aliases = {}
    for w, rc, first, n_chunks, _ in convert:
        assert n_chunks <= batch
        w_cols = w.shape[2]
        in_specs.append(pl.BlockSpec(
            (None, rc, w_cols), lambda b, first=first, n=n_chunks: (layer, first + jnp.minimum(b, n - 1), 0)))
        args.append(w)
        out_specs.append(pl.BlockSpec((rc, w_cols), lambda b, n=n_chunks: (jnp.minimum(b, n - 1), 0)))
        out_shape.append(jax.ShapeDtypeStruct((rc * n_chunks, w_cols), BF16))
    if carried is not None:
        aliases = {len(args) + j: 2 + j for j in range(3)}
        in_specs += [pl.BlockSpec(memory_space=pl.ANY)] * 3
        args += list(carried)
    return pl.pallas_call(
        functools.partial(_mlstm_kernel, t=t, chunk=chunk, has_state=has_state, emit_state=emit_state,
                          convert=tuple((n, last) for _, _, _, n, last in convert),
                          n_carried=0 if carried is None else 3,
                          fill_layer=layer if emit_state and carried is None else None),
        grid=(batch,),
        in_specs=in_specs,
        out_specs=out_specs,
        out_shape=out_shape,
        scratch_shapes=[pltpu.VMEM((M_WIDTH, t), F32)],
        input_output_aliases=aliases,
        compiler_params=_cparams(1),
        name="mlstm_fourier_latent" if has_state else "mlstm_fourier_context",
    )(*args)


def _dft_tables(t):
    kt = (np.arange(t)[:, None] * np.arange(t)[None, :]) % t
    ang_t = 2.0 * np.pi * kt.astype(np.float64) / t
    ct = np.cos(ang_t) / np.sqrt(t)
    st = np.sin(ang_t) / np.sqrt(t)
    c = F_GROUP_CH
    kc = (np.arange(c)[:, None] * np.arange(c)[None, :]) % c
    ang_c = 2.0 * np.pi * kc.astype(np.float64) / c
    eye = np.eye(F_GROUPS)
    cc = np.kron(eye, np.cos(ang_c) / np.sqrt(c))
    sc = np.kron(eye, np.sin(ang_c) / np.sqrt(c))
    cs = np.concatenate([cc, sc], axis=1)
    to_dev = lambda a: jnp.asarray(a.astype(np.float32)).astype(BF16)
    return to_dev(cs), to_dev(ct), to_dev(st)


def _merge_kernel(*refs, rows, split_x):
    refs = list(refs)
    x_refs = [refs.pop(0) for _ in range(2 if split_x else 1)]
    (mod_ref, g_ref, ac_ref, al_ref, hc_ref, hl_ref, fc_ref, fl_ref, wbg_ref,
     wpa_ref, wpm_ref, wpf_ref, wo_ref, o_ref) = refs
    x = _pick(rows, *x_refs) if split_x else x_refs[0][...]
    h = _normmod(x, g_ref[...], mod_ref[1:2, :], mod_ref[0:1, :]).astype(BF16)

    def branch(j, yc_ref, yl_ref, w_ref):
        gate = _sigmoid(_mm_nt(h, wbg_ref[BG_SKIP + D_MODEL * j:BG_SKIP + D_MODEL * (j + 1), :]))
        return gate * _mm(_pick(rows, yc_ref, yl_ref), w_ref[...])

    merged = (branch(0, ac_ref, al_ref, wpa_ref) + branch(1, hc_ref, hl_ref, wpm_ref)
              + branch(2, fc_ref, fl_ref, wpf_ref))
    o_ref[...] = x + mod_ref[2:3, :] * _mm(merged.astype(BF16), wo_ref[...])


def _merge(xs, mod, g, attn, hm, fo, weights, rows, layer):
    split_x = len(xs) == 2
    n_rows = sum(x.shape[0] for x in xs)
    x_specs = [rows.spec(D_MODEL, "ctx"), rows.spec(D_MODEL, "lat")] if split_x else [rows.spec(D_MODEL, "both")]
    pair = lambda width: [rows.spec(width, "ctx"), rows.spec(width, "lat")]
    resident = lambda w: pl.BlockSpec(w.shape, lambda i: (0, 0), pipeline_mode=pl.Buffered(1))
    return pl.pallas_call(
        functools.partial(_merge_kernel, rows=rows, split_x=split_x),
        grid=(n_rows // rows.tm,),
        in_specs=x_specs + [rows.mod_spec(layer), _layer_spec((1, D_MODEL), layer)]
        + pair(ATTN_WIDTH) + pair(M_WIDTH) + pair(F_WIDTH) + [resident(w) for w in weights],
        out_specs=rows.spec(D_MODEL, "both"),
        out_shape=jax.ShapeDtypeStruct((n_rows, D_MODEL), F32),
        compiler_params=_cparams(1),
        name="merge",
    )(*xs, mod, g, *attn, *hm, *fo, *weights)


def _ffn_kernel(*refs, rows, split_out):
    x_ref, mod_ref, g_ref, win_ref, wout_ref = refs[:5]
    out_refs = refs[5:]
    x = x_ref[...]
    h = _normmod(x, g_ref[...], mod_ref[4:5, :], mod_ref[3:4, :]).astype(BF16)
    acc = None
    for lo, hi in FF_SPANS:
        fg = _mm(h, win_ref[:, lo:hi])
        fv = _mm(h, win_ref[:, FF_HIDDEN + lo:FF_HIDDEN + hi])
        part = _mm((_silu(fg) * fv).astype(BF16), wout_ref[lo:hi, :])
        acc = part if acc is None else acc + part
    y = x + mod_ref[5:6, :] * acc
    if split_out:
        @pl.when(rows.is_ctx())
        def _():
            out_refs[0][...] = y

        @pl.when(jnp.logical_not(rows.is_ctx()))
        def _():
            out_refs[1][...] = y
    else:
        out_refs[0][...] = y


def _ffn(x, mod, g, w_in, w_out, rows, layer, split_out):
    n_rows = x.shape[0]
    n_ctx_rows = rows.n_ctx * rows.tm
    resident = lambda w: pl.BlockSpec(w.shape, lambda i: (0, 0), pipeline_mode=pl.Buffered(1))
    if split_out:
        out_specs = [rows.spec(D_MODEL, "ctx"), rows.spec(D_MODEL, "lat")]
        out_shape = [jax.ShapeDtypeStruct((n_ctx_rows, D_MODEL), F32),
                     jax.ShapeDtypeStruct((n_rows - n_ctx_rows, D_MODEL), F32)]
    else:
        out_specs = [rows.spec(D_MODEL, "both")]
        out_shape = [jax.ShapeDtypeStruct((n_rows, D_MODEL), F32)]
    return pl.pallas_call(
        functools.partial(_ffn_kernel, rows=rows, split_out=split_out),
        grid=(n_rows // rows.tm,),
        in_specs=[rows.spec(D_MODEL, "both"), rows.mod_spec(layer), _layer_spec((1, D_MODEL), layer),
                  resident(w_in), resident(w_out)],
        out_specs=out_specs,
        out_shape=out_shape,
        compiler_params=_cparams(1),
        name="ffn",
    )(x, mod, g, w_in, w_out)


def _rope_tables(t):
    n = HEAD_DIM // 4
    inv = 1.0 / (ROPE_THETA ** (np.arange(n, dtype=np.float64) / n))
    pos = np.arange(t)
    ang_r = (pos // GRID_W)[:, None] * inv[None, :]
    ang_c = (pos % GRID_W)[:, None] * inv[None, :]
    cos = np.concatenate([np.cos(ang_r)] * 2 + [np.cos(ang_c)] * 2, axis=1)
    sin = np.concatenate([-np.sin(ang_r), np.sin(ang_r), -np.sin(ang_c), np.sin(ang_c)], axis=1)
    tile = lambda a: jnp.asarray(np.tile(a, (1, N_Q_HEADS)).astype(np.float32))
    return tile(cos), tile(sin)


def kernel(x_prompt, x_sample, cache_k, cache_v, state_C, state_n, state_m, c, c_ctx, w_ada, b_ada,
           norm1_g, w_in, q_norm_g, k_norm_g, m_conv_w, m_gate_b, m_norm_g, w_proj_attn,
           w_proj_mlstm, w_proj_fourier, w_out, norm2_g, w_ffn_in, w_ffn_out):
    n_ctx, t_ctx, _ = x_prompt.shape
    n_lat, t_lat, _ = x_sample.shape
    t_past = cache_k.shape[2]
    n_units = 2 * M_HEADS

    cvec = jnp.concatenate([c_ctx[None, :], c], axis=0)
    cvec = jnp.pad(cvec, ((0, MOD_ROWS - cvec.shape[0]), (0, 0)))
    mod = _modulation(cvec, w_ada, b_ada).reshape(DEPTH, MOD_ROWS, N_MOD, D_MODEL)

    rope_tabs = _rope_tables(t_lat)
    to_rows = lambda a: jnp.transpose(a, (0, 1, 3, 4, 2)).reshape(n_lat, DEPTH, KV_WIDTH, t_past)
    ck, cv = to_rows(cache_k), to_rows(cache_v)
    rows_ctx = n_ctx * t_ctx
    rows_in = _Rows(rows_ctx, t_lat, n_stage=MIX_CHUNKS, tm=TM_IN)
    rows_merge = _Rows(rows_ctx, t_lat)
    rows_ffn = _Rows(rows_ctx, t_lat)

    w_in_t = jnp.swapaxes(w_in, 1, 2)
    row = lambda a: a[:, None, :]
    norm1, norm2 = row(norm1_g), row(norm2_g)
    qg, kg = row(jnp.tile(q_norm_g, (1, N_Q_HEADS))), row(jnp.tile(k_norm_g, (1, N_KV_HEADS)))
    gate_b = row(jnp.pad(m_gate_b, ((0, 0), (0, GATE_PAD - N_GATES))))
    m_norm = row(m_norm_g)

    per_step = lambda w, n: (w, w.shape[1] // n, 0, n, None)
    xs = (x_prompt.reshape(rows_ctx, D_MODEL), x_sample.reshape(n_lat * t_lat, D_MODEL))
    kv_carried = states_carried = None
    state0 = (state_C.astype(F32).reshape(n_lat, DEPTH, n_units, M_DK, M_DV),
              state_n.astype(F32).reshape(n_lat, DEPTH, n_units, M_DK),
              state_m.astype(F32).reshape(n_lat, DEPTH, n_units, 1))
    for l in range(DEPTH):
        za, zm, fu, gt = _in_proj(xs, mod, norm1, w_in_t, rows_in, l)
        attn_c, *kv_carried = _attention(za, qg, kg, batch=n_ctx, t_new=t_ctx, latent=False, row0=0, layer=l,
                                         carried=kv_carried)
        hm_c, fo_c, *rest = _mlstm_fourier(
            zm, gt, fu, m_conv_w, gate_b, m_norm, batch=n_ctx, t=t_ctx, layer=l, row0=0, carried=states_carried,
            convert=((w_in_t, W_IN_ROWS, BG_FIRST, BG_CHUNKS, BG_LAST),)
            + tuple(per_step(w, n_ctx) for w in (w_proj_attn, w_proj_mlstm, w_proj_fourier, w_out)))
        states_carried, merge_w = rest[:3], rest[3:]

        (attn_l,) = _attention(za, qg, kg, batch=n_lat, t_new=t_lat, latent=True, row0=rows_ctx,
                               cache_k=ck, cache_v=cv, layer=l, rope_tabs=rope_tabs)
        hm_l, fo_l, w_ffn_in_b, w_ffn_out_b = _mlstm_fourier(
            zm, gt, fu, m_conv_w, gate_b, m_norm, batch=n_lat, t=t_lat, layer=l, row0=rows_ctx,
            state0=state0, convert=(per_step(w_ffn_in, n_lat), per_step(w_ffn_out, n_lat)))

        x1 = _merge(xs, mod, norm1, (attn_c, attn_l), (hm_c, hm_l), (fo_c, fo_l), merge_w, rows_merge, l)
        xs = tuple(_ffn(x1, mod, norm2, w_ffn_in_b, w_ffn_out_b, rows_ffn, l, split_out=l == DEPTH - 1))
    xp, xs = xs

    kv_out = lambda a: jnp.transpose(a.reshape(n_ctx, DEPTH, N_KV_HEADS, HEAD_DIM, t_ctx), (0, 1, 4, 2, 3))
    cs, ns, ms = states_carried
    return (xp.reshape(n_ctx, t_ctx, D_MODEL), xs.reshape(n_lat, t_lat, D_MODEL),
            kv_out(kv_carried[0]), kv_out(kv_carried[1]),
            cs.reshape(n_ctx, DEPTH, 2, M_HEADS, M_DK, M_DV), ns.reshape(n_ctx, DEPTH, 2, M_HEADS, M_DK),
            ms[..., 0].reshape(n_ctx, DEPTH, 2, M_HEADS))
```
